```python
import math
import jax, jax.numpy as jnp
from jax import lax
import numpy as np

D_MODEL = 1024
BATCH = 16
SEQ = 256
DEPTH = 4
DEC_BATCH = 2
DEC_SEQ = 1024
PAST_LEN = 512

GRID_W = 64
N_EVEN = (DEPTH + 1) // 2
N_ODD = DEPTH // 2
N_HEADS_A = D_MODEL // 256
HEAD_DIM_A = 64
VAL_DIM_A = 2 * HEAD_DIM_A
MIX_A = N_HEADS_A * VAL_DIM_A
QK_W = N_HEADS_A * 2 * HEAD_DIM_A
SG_GROUPS = 4
SG_CHUNK = 128
SG_DIM = (D_MODEL - MIX_A) // SG_GROUPS
MIX_B = SG_GROUPS * SG_DIM
IN_W = 2 * QK_W + MIX_A + 2 * MIX_B
POOL_WINDOWS = (2, 4, 8, 16)
POOL_DIM = D_MODEL // len(POOL_WINDOWS)
N_EXPERTS = 16
N_EXPERT_GROUPS = 4
EXPERTS_PER_GROUP = N_EXPERTS // N_EXPERT_GROUPS
TOP_K = 2
D_FF_EXPERT = D_MODEL // 2
ROPE_BASE = 10000.0
Q_BLOCK = 128
N_MOD = 6
EPS = 1e-6

kernel_name = 'hybrid_diffusion_prefix_trunk_step'


def rms_norm(x, g):
    xf = x.astype(jnp.float32)
    y = xf * lax.rsqrt(jnp.mean(xf * xf, axis=-1, keepdims=True) + EPS)
    return (y * g.astype(jnp.float32)).astype(x.dtype)


def lambda_init(layer_idx):
    return 0.8 - 0.6 * math.exp(-0.3 * layer_idx)


def rope_2d(x):
    S = x.shape[1]
    rows = S // GRID_W
    row = jnp.repeat(jnp.arange(rows), GRID_W)
    col = jnp.tile(jnp.arange(GRID_W), rows)
    half = HEAD_DIM_A // 2
    nf = half // 2
    inv = ROPE_BASE ** (-jnp.arange(nf, dtype=jnp.float32) / nf)

    def rot(xa, pos):
        ang = pos.astype(jnp.float32)[:, None] * inv[None, :]
        cos = jnp.cos(ang)[:, None, None, :]
        sin = jnp.sin(ang)[:, None, None, :]
        x1, x2 = xa[..., :nf], xa[..., nf:]
        return jnp.concatenate([x1 * cos - x2 * sin, x2 * cos + x1 * sin], axis=-1)

    xf = x.astype(jnp.float32)
    out = jnp.concatenate([rot(xf[..., :half], row), rot(xf[..., half:], col)], axis=-1)
    return out.astype(x.dtype)


def diff_attention(q, k, v, lam):
    B, Sq = q.shape[0], q.shape[1]
    nq = Sq // Q_BLOCK
    qb = jnp.moveaxis(q.reshape(B, nq, Q_BLOCK, N_HEADS_A, 2, HEAD_DIM_A), 1, 0)
    scale = HEAD_DIM_A ** -0.5

    def block(qi):
        s = jnp.einsum('bqhcd,bkhcd->bhcqk', qi, k, preferred_element_type=jnp.float32) * scale
        p = jax.nn.softmax(s, axis=-1)
        a = (p[:, :, 0] - lam * p[:, :, 1]).astype(v.dtype)
        return jnp.einsum('bhqk,bkhe->bqhe', a, v)

    o = lax.map(block, qb)
    return jnp.moveaxis(o, 0, 1).reshape(B, Sq, N_HEADS_A, VAL_DIM_A)


def spatial_gate(u, v, sg_w, sg_b):
    B, S, _ = u.shape
    nc = S // SG_CHUNK
    u = jax.nn.gelu(u).reshape(B, nc, SG_CHUNK, SG_GROUPS, SG_DIM)
    v = jax.nn.gelu(v).reshape(B, nc, SG_CHUNK, SG_GROUPS, SG_DIM)
    mixed = jnp.einsum('gpq,bnqgc->bnpgc', sg_w, v) + sg_b.T[None, None, :, :, None]
    return (u * mixed).reshape(B, S, MIX_B)


def pool_mixer(h, pool_w, pool_scale):
    B, S, D = h.shape
    hf = h.astype(jnp.float32)
    cs = jnp.concatenate([jnp.zeros((B, 1, D), jnp.float32), jnp.cumsum(hf, axis=1)], axis=1)
    t = jnp.arange(S)
    outs = []
    for g, w in enumerate(POOL_WINDOWS):
        lo = jnp.clip(t - w // 2, 0, S)
        hi = jnp.clip(t - w // 2 + w, 0, S)
        csg = cs[:, :, g * POOL_DIM:(g + 1) * POOL_DIM]
        mean = (csg[:, hi] - csg[:, lo]) / (hi - lo).astype(jnp.float32)[None, :, None]
        d = (mean - hf[:, :, g * POOL_DIM:(g + 1) * POOL_DIM]).astype(h.dtype)
        outs.append(d @ pool_w[g])
    return jnp.concatenate(outs, axis=-1) * pool_scale


def moe(h, router_w, router_b, w_gate, w_up, w_down):
    B, S, D = h.shape
    t = h.reshape(B * S, D)
    scores = jax.nn.sigmoid(jnp.einsum('td,de->te', t, router_w, preferred_element_type=jnp.float32))
    sel = scores + router_b.astype(jnp.float32)
    grp_score = lax.top_k(sel.reshape(-1, N_EXPERT_GROUPS, EXPERTS_PER_GROUP), TOP_K)[0].sum(-1)
    g_idx = jnp.argmax(grp_score, axis=-1)
    in_grp = (jnp.arange(N_EXPERTS) // EXPERTS_PER_GROUP)[None, :] == g_idx[:, None]
    _, e_idx = lax.top_k(jnp.where(in_grp, sel, -jnp.inf), TOP_K)
    wts = jnp.take_along_axis(scores, e_idx, axis=-1)
    wts = wts / jnp.sum(wts, axis=-1, keepdims=True)
    gates = jnp.sum(jax.nn.one_hot(e_idx, N_EXPERTS, dtype=jnp.float32) * wts[..., None], axis=1)
    g = jnp.einsum('td,edf->tef', t, w_gate)
    u = jnp.einsum('td,edf->tef', t, w_up)
    a = jax.nn.silu(g) * u * gates[:, :, None].astype(t.dtype)
    return jnp.einsum('tef,efd->td', a, w_down).reshape(B, S, D)


def trunk(x, cond, ctx_k, ctx_v, norm1_g, norm2_g, w_ada, b_ada, w_in, w_out, q_norm_g, k_norm_g,
          lam_q1, lam_k1, lam_q2, lam_k2, subln_g, sg_w, sg_b, pool_w, pool_scale,
          router_w, router_b, w_gate, w_up, w_down):
    latent = ctx_k is not None
    B, S, _ = x.shape
    ks, vs = [], []
    for l in range(DEPTH):
        mod = jax.nn.silu(cond) @ w_ada[l] + b_ada[l]
        sh1, sc1, g1, sh2, sc2, g2 = [m[:, None, :] for m in jnp.split(mod, N_MOD, axis=-1)]
        h = rms_norm(x, norm1_g[l]) * (1 + sc1) + sh1
        if l % 2 == 0:
            e = l // 2
            proj = h @ w_in[e]
            q = proj[..., :QK_W].reshape(B, S, N_HEADS_A, 2, HEAD_DIM_A)
            k = proj[..., QK_W:2 * QK_W].reshape(B, S, N_HEADS_A, 2, HEAD_DIM_A)
            v = proj[..., 2 * QK_W:2 * QK_W + MIX_A].reshape(B, S, N_HEADS_A, VAL_DIM_A)
            gu = proj[..., 2 * QK_W + MIX_A:2 * QK_W + MIX_A + MIX_B]
            gv = proj[..., 2 * QK_W + MIX_A + MIX_B:]
            q = rms_norm(q, q_norm_g[e])
            k = rms_norm(k, k_norm_g[e])
            if latent:
                qa = rope_2d(q)
                keys = jnp.concatenate([ctx_k[:, e], rope_2d(k)], axis=1)
                vals = jnp.concatenate([ctx_v[:, e], v], axis=1)
            else:
                qa, keys, vals = q, k, v
                ks.append(k)
                vs.append(v)
            lam0 = lambda_init(l)
            lam = (jnp.exp(jnp.sum(lam_q1[e].astype(jnp.float32) * lam_k1[e].astype(jnp.float32)))
                   - jnp.exp(jnp.sum(lam_q2[e].astype(jnp.float32) * lam_k2[e].astype(jnp.float32))) + lam0)
            att = rms_norm(diff_attention(qa, keys, vals, lam), subln_g[e]) * (1.0 - lam0)
            sgo = spatial_gate(gu, gv, sg_w[e], sg_b[e])
            out = jnp.concatenate([att.reshape(B, S, MIX_A), sgo], axis=-1) @ w_out[e]
        else:
            o = l // 2
            out = pool_mixer(h, pool_w[o], pool_scale[o])
        x = x + g1 * out
        h = rms_norm(x, norm2_g[l]) * (1 + sc2) + sh2
        x = x + g2 * moe(h, router_w, router_b, w_gate[l], w_up[l], w_down[l])
    return x, ks, vs


def setup_inputs(seed: int = 0) -> dict:
    key = jax.random.key(seed)
    k = jax.random.split(key, 32)
    f32 = jnp.float32

    def nrm(kk, shape, s):
        return jax.random.normal(kk, shape, f32) * s

    D = D_MODEL
    return {
        'x_prompt': nrm(k[0], (BATCH, SEQ, D), 1.0),
        'x_sample': nrm(k[1], (DEC_BATCH, DEC_SEQ, D), 1.0),
        'cache_k': nrm(k[2], (DEC_BATCH, N_EVEN, PAST_LEN, N_HEADS_A, 2, HEAD_DIM_A), 1.0),
        'cache_v': nrm(k[3], (DEC_BATCH, N_EVEN, PAST_LEN, N_HEADS_A, VAL_DIM_A), 1.0),
        'c': nrm(k[4], (DEC_BATCH, D), 1.0),
        'c_ctx': nrm(k[5], (D,), 1.0),
        'norm1_g': 1.0 + nrm(k[6], (DEPTH, D), 0.02),
        'norm2_g': 1.0 + nrm(k[7], (DEPTH, D), 0.02),
        'w_ada': nrm(k[8], (DEPTH, D, N_MOD * D), 0.5 * D ** -0.5),
        'b_ada': nrm(k[9], (DEPTH, N_MOD * D), 0.01),
        'w_in': nrm(k[10], (N_EVEN, D, IN_W), D ** -0.5),
        'w_out': nrm(k[11], (N_EVEN, MIX_A + MIX_B, D), (MIX_A + MIX_B) ** -0.5),
        'q_norm_g': 1.0 + nrm(k[12], (N_EVEN, HEAD_DIM_A), 0.02),
        'k_norm_g': 1.0 + nrm(k[13], (N_EVEN, HEAD_DIM_A), 0.02),
        'lam_q1': nrm(k[14], (N_EVEN, HEAD_DIM_A), 0.1),
        'lam_k1': nrm(k[15], (N_EVEN, HEAD_DIM_A), 0.1),
        'lam_q2': nrm(k[16], (N_EVEN, HEAD_DIM_A), 0.1),
        'lam_k2': nrm(k[17], (N_EVEN, HEAD_DIM_A), 0.1),
        'subln_g': 1.0 + nrm(k[18], (N_EVEN, VAL_DIM_A), 0.02),
        'sg_w': nrm(k[19], (N_EVEN, SG_GROUPS, SG_CHUNK, SG_CHUNK), 0.5 * SG_CHUNK ** -0.5),
        'sg_b': 1.0 + nrm(k[20], (N_EVEN, SG_GROUPS, SG_CHUNK), 0.02),
        'pool_w': nrm(k[21], (N_ODD, len(POOL_WINDOWS), POOL_DIM, POOL_DIM), POOL_DIM ** -0.5),
        'pool_scale': 1.0 + nrm(k[22], (N_ODD, D), 0.02),
        'router_w': nrm(k[23], (D, N_EXPERTS), D ** -0.5),
        'router_b': nrm(k[24], (N_EXPERTS,), 0.01),
        'w_gate': nrm(k[25], (DEPTH, N_EXPERTS, D, D_FF_EXPERT), D ** -0.5),
        'w_up': nrm(k[26], (DEPTH, N_EXPERTS, D, D_FF_EXPERT), D ** -0.5),
        'w_down': nrm(k[27], (DEPTH, N_EXPERTS, D_FF_EXPERT, D), D_FF_EXPERT ** -0.5),
    }


def reference(x_prompt, x_sample, cache_k, cache_v, c, c_ctx, norm1_g, norm2_g, w_ada, b_ada,
              w_in, w_out, q_norm_g, k_norm_g, lam_q1, lam_k1, lam_q2, lam_k2, subln_g,
              sg_w, sg_b, pool_w, pool_scale, router_w, router_b, w_gate, w_up, w_down):
    y_prompt, ks, vs = trunk(x_prompt, c_ctx[None, :], None, None, norm1_g, norm2_g, w_ada, b_ada,
                             w_in, w_out, q_norm_g, k_norm_g, lam_q1, lam_k1, lam_q2, lam_k2,
                             subln_g, sg_w, sg_b, pool_w, pool_scale, router_w, router_b,
                             w_gate, w_up, w_down)
    state_k = jnp.stack(ks, axis=1)
    state_v = jnp.stack(vs, axis=1)
    y_sample, _, _ = trunk(x_sample, c, cache_k, cache_v, norm1_g, norm2_g, w_ada, b_ada,
                           w_in, w_out, q_norm_g, k_norm_g, lam_q1, lam_k1, lam_q2, lam_k2,
                           subln_g, sg_w, sg_b, pool_w, pool_scale, router_w, router_b,
                           w_gate, w_up, w_down)
    return (y_prompt, y_sample, state_k, state_v)
```

```python
import functools
import math

import jax
import jax.numpy as jnp
import numpy as np
from jax import lax
from jax.experimental import pallas as pl
from jax.experimental.pallas import tpu as pltpu

F32 = jnp.float32
BF16 = jnp.bfloat16
I32 = jnp.int32

D = 1024
N_CTX_B, CTX_S = 16, 256
N_LAT_B, LAT_S = 2, 1024
PAST = 512
DEPTH = 4
T_CTX = N_CTX_B * CTX_S
T_LAT = N_LAT_B * LAT_S
T = T_CTX + T_LAT
GRID_W = 64
N_HEADS = 4
HEAD_DIM = 64
VAL_DIM = 128
QK_W = 512
MIX_A = 512
MIX_B = 512
IN_W = 2560
SG_GROUPS = 4
SG_CHUNK = 128
POOL_WINDOWS = (2, 4, 8, 16)
POOL_DIM = 256
POOL_HALO = 8
N_EXPERTS = 16
N_GROUPS = 4
PER_GROUP = 4
D_FF = 512
N_MOD = 6
N_COND = 1 + N_LAT_B
EPS = 1e-6
ROPE_BASE = 10000.0

TM = 256
N_TILES = T // TM
N_CTX_TILES = T_CTX // TM
LAT_TILES_PER_SEQ = LAT_S // TM
EXPERT_CAP = T
CAP_TILES = EXPERT_CAP // TM
MOE_TILES = 2 * T // TM + N_EXPERTS
MOD_TN = 1024
MIB = 1024 * 1024

NT_DIMS = (((1,), (1,)), ((), ()))


def _cparams(sem, vmem_mib):
    return pltpu.CompilerParams(dimension_semantics=sem, vmem_limit_bytes=vmem_mib * MIB)


def _sigmoid(x):
    return 1.0 / (1.0 + jnp.exp(-x))


def _gelu_tanh(x):
    c = math.sqrt(2.0 / math.pi)
    return x * (0.5 * (1.0 + jnp.tanh(c * (x + 0.044715 * (x * x * x)))))


def _modnorm(x, g, scale, shift):
    ms = jnp.mean(x * x, axis=-1, keepdims=True)
    return (x * lax.rsqrt(ms + EPS) * g) * (1.0 + scale) + shift


def _cond_of_tile(i):
    return jnp.where(i < N_CTX_TILES, 0, 1 + (i - N_CTX_TILES) // LAT_TILES_PER_SEQ)


def _mod_kernel(cb_ref, w_ref, b_ref, o_ref, sb_ref):
    @pl.when((pl.program_id(0) == 0) & (pl.program_id(1) == 0))
    def _():
        cb = cb_ref[...]
        sb_ref[...] = cb * _sigmoid(cb)

    for j in range(N_COND):
        cols = []
        for n0 in range(0, MOD_TN, 128):
            cols.append(jnp.sum(w_ref[0, :, n0:n0 + 128] * sb_ref[j], axis=0, keepdims=True))
        o_ref[0, j:j + 1, :] = jnp.concatenate(cols, axis=1) + b_ref[0]


def _modulation(cond, w_ada, b_ada):
    cb = jnp.broadcast_to(cond[:, :, None], (N_COND, D, 128))
    out = pl.pallas_call(
        _mod_kernel,
        grid=(DEPTH, N_MOD * D // MOD_TN),
        in_specs=[
            pl.BlockSpec((N_COND, D, 128), lambda l, n: (0, 0, 0)),
            pl.BlockSpec((1, D, MOD_TN), lambda l, n: (l, 0, n)),
            pl.BlockSpec((1, 1, MOD_TN), lambda l, n: (l, 0, n)),
        ],
        out_specs=pl.BlockSpec((1, N_COND, MOD_TN), lambda l, n: (l, 0, n)),
        out_shape=jax.ShapeDtypeStruct((DEPTH, N_COND, N_MOD * D), F32),
        scratch_shapes=[pltpu.VMEM((N_COND, D, 128), F32)],
        compiler_params=_cparams(("arbitrary", "arbitrary"), 24),
        name="adaln_mod",
    )(cb, w_ada, b_ada.reshape(DEPTH, 1, N_MOD * D))
    return out.reshape(DEPTH, N_COND, N_MOD, D)


def _front_kernel(latent, *refs):
    if latent:
        (x_ref, m_ref, n1_ref, w_ref, qg_ref, kg_ref, seg_ref, sgw_ref, sgb_ref, cos_ref, sin_ref,
         q_o, k_o, v_o, sg_o, wbf) = refs
    else:
        (x_ref, m_ref, n1_ref, w_ref, qg_ref, kg_ref, seg_ref, sgw_ref, sgb_ref, _sk_in, _sv_in,
         q_o, k_o, v_o, sg_o, sk_o, sv_o, wbf) = refs

    @pl.when(pl.program_id(0) == 0)
    def _():
        for c0 in range(0, IN_W, 512):
            wbf[:, c0:c0 + 512] = w_ref[0, :, c0:c0 + 512].astype(BF16)

    h = _modnorm(x_ref[...], n1_ref[0], m_ref[0, 0, 1:2, :], m_ref[0, 0, 0:1, :])
    proj = jnp.dot(h.astype(BF16), wbf[...], preferred_element_type=F32)

    def qk_norm(z, g):
        ms = jnp.dot((z * z).astype(BF16), seg_ref[...], preferred_element_type=F32)
        return z * lax.rsqrt(ms + EPS) * g

    q = qk_norm(proj[:, 0:QK_W], qg_ref[...])
    k = qk_norm(proj[:, QK_W:2 * QK_W], kg_ref[...])
    v = proj[:, 2 * QK_W:2 * QK_W + MIX_A]

    if latent:
        lane = lax.broadcasted_iota(I32, (1, QK_W), 1)
        first = (lane % 32) < 16

        def rope(z):
            partner = jnp.where(first, pltpu.roll(z, QK_W - 16, 1), pltpu.roll(z, 16, 1))
            return z * cos_ref[...] + partner * sin_ref[...]

        q = rope(q)
        k = rope(k)
    else:
        sk_o[0, 0] = k
        sv_o[0, 0] = v

    q_o[...] = (q * (HEAD_DIM ** -0.5)).astype(BF16)
    k_o[...] = k.astype(BF16)
    v_o[...] = v.astype(BF16)

    gu0 = 2 * QK_W + MIX_A
    gv0 = gu0 + MIX_B
    for g in range(SG_GROUPS):
        wg = sgw_ref[0, g].astype(BF16)
        for n in range(TM // SG_CHUNK):
            rows = slice(n * SG_CHUNK, (n + 1) * SG_CHUNK)
            cu = slice(gu0 + g * 128, gu0 + (g + 1) * 128)
            cv = slice(gv0 + g * 128, gv0 + (g + 1) * 128)
            gv = _gelu_tanh(proj[rows, cv]).astype(BF16)
            mixed = jnp.dot(wg, gv, preferred_element_type=F32) + sgb_ref[0, g]
            sg_o[rows, g * 128:(g + 1) * 128] = (_gelu_tanh(proj[rows, cu]) * mixed).astype(BF16)


def _front(latent, l, x, mod, norm1_g, w_in, qg, kg, seg, sg_w, sgb, rope_tabs, states):
    e = l // 2
    n_tiles = (T_LAT if latent else T_CTX) // TM
    off = N_CTX_TILES if latent else 0
    rows = T_LAT if latent else T_CTX
    if latent:
        cond_map = lambda i: (l, 1 + i // LAT_TILES_PER_SEQ, 0, 0)
    else:
        cond_map = lambda i: (l, 0, 0, 0)
    in_specs = [
        pl.BlockSpec((TM, D), lambda i: (i + off, 0)),
        pl.BlockSpec((1, 1, N_MOD, D), cond_map),
        pl.BlockSpec((1, 1, D), lambda i: (l, 0, 0)),
        pl.BlockSpec((1, D, IN_W), lambda i: (e, 0, 0)),
        pl.BlockSpec((1, QK_W), lambda i: (0, 0)),
        pl.BlockSpec((1, QK_W), lambda i: (0, 0)),
        pl.BlockSpec((QK_W, QK_W), lambda i: (0, 0)),
        pl.BlockSpec((1, SG_GROUPS, SG_CHUNK, SG_CHUNK), lambda i: (e, 0, 0, 0)),
        pl.BlockSpec((1, SG_GROUPS, SG_CHUNK, 128), lambda i: (e, 0, 0, 0)),
    ]
    args = [x, mod, norm1_g.reshape(DEPTH, 1, D), w_in, qg, kg, seg, sg_w, sgb]
    tok = lambda: pl.BlockSpec((TM, QK_W), lambda i: (i, 0))
    out_specs = [tok(), tok(), tok(), tok()]
    out_shape = [jax.ShapeDtypeStruct((rows, QK_W), BF16)] * 4
    aliases = {}
    if latent:
        cos_t, sin_t = rope_tabs
        in_specs += [pl.BlockSpec((TM, QK_W), lambda i: (i % LAT_TILES_PER_SEQ, 0))] * 2
        args += [cos_t, sin_t]
    else:
        state_k, state_v = states
        in_specs += [pl.BlockSpec(memory_space=pl.ANY)] * 2
        args += [state_k, state_v]
        st = lambda: pl.BlockSpec((1, 1, CTX_S, QK_W), lambda i: (i, e, 0, 0))
        out_specs += [st(), st()]
        out_shape += [jax.ShapeDtypeStruct((N_CTX_B, DEPTH // 2, CTX_S, QK_W), F32)] * 2
        aliases = {9: 4, 10: 5}
    return pl.pallas_call(
        functools.partial(_front_kernel, latent),
        grid=(n_tiles,),
        in_specs=in_specs,
        out_specs=out_specs,
        out_shape=out_shape,
        scratch_shapes=[pltpu.VMEM((D, IN_W), BF16)],
        input_output_aliases=aliases,
        compiler_params=_cparams(("arbitrary",), 48),
        name="front_lat" if latent else "front_ctx",
    )(*args)


def _attn_kernel(has_cache, lam0, *refs):
    if has_cache:
        (q_ref, k_ref, v_ref, ck_ref, cv_ref, lq1, lk1, lq2, lk2, sub_ref, o_ref) = refs
    else:
        (q_ref, k_ref, v_ref, lq1, lk1, lq2, lk2, sub_ref, o_ref) = refs

    lane = lax.broadcasted_iota(I32, (1, VAL_DIM), 1)
    qf = q_ref[...].astype(F32)
    qc = [jnp.where(lane < HEAD_DIM, qf, 0.0).astype(BF16),
          jnp.where(lane >= HEAD_DIM, qf, 0.0).astype(BF16)]
    keys = [k_ref[...]]
    vals = [v_ref[...]]
    if has_cache:
        keys.append(ck_ref[0, 0].astype(BF16))
        vals.append(cv_ref[0, 0].astype(BF16))

    lam = (jnp.exp(jnp.sum(lq1[0] * lk1[0], axis=-1, keepdims=True))
           - jnp.exp(jnp.sum(lq2[0] * lk2[0], axis=-1, keepdims=True)) + lam0)

    probs = []
    for c in range(2):
        s = [lax.dot_general(qc[c], kk, NT_DIMS, preferred_element_type=F32) for kk in keys]
        m = s[0].max(axis=-1, keepdims=True)
        for sx in s[1:]:
            m = jnp.maximum(m, sx.max(axis=-1, keepdims=True))
        p = [jnp.exp(sx - m) for sx in s]
        den = p[0].sum(axis=-1, keepdims=True)
        for px in p[1:]:
            den = den + px.sum(axis=-1, keepdims=True)
        probs.append((p, den))

    r0 = 1.0 / probs[0][1]
    r1 = lam / probs[1][1]
    o = None
    for idx in range(len(keys)):
        a = (probs[0][0][idx] * r0 - probs[1][0][idx] * r1).astype(BF16)
        t = jnp.dot(a, vals[idx], preferred_element_type=F32)
        o = t if o is None else o + t
    ms = jnp.mean(o * o, axis=-1, keepdims=True)
    o_ref[...] = ((o * lax.rsqrt(ms + EPS) * sub_ref[0]) * (1.0 - lam0)).astype(BF16)


def _attention(has_cache, l, q, k, v, cache_k, cache_v, lam_q1, lam_k1, lam_q2, lam_k2, subln_g):
    e = l // 2
    lam0 = 0.8 - 0.6 * math.exp(-0.3 * l)
    n_b, s_len = (N_LAT_B, LAT_S) if has_cache else (N_CTX_B, CTX_S)
    nq = s_len // TM
    kv = lambda: pl.BlockSpec((s_len, VAL_DIM), lambda b, h, qi: (b, h))
    par = lambda: pl.BlockSpec((1, 1, HEAD_DIM), lambda b, h, qi: (e, 0, 0))
    in_specs = [pl.BlockSpec((TM, VAL_DIM), lambda b, h, qi: (b * nq + qi, h)), kv(), kv()]
    args = [q, k, v]
    if has_cache:
        cs = lambda: pl.BlockSpec((1, 1, PAST, VAL_DIM), lambda b, h, qi: (b, e, 0, h))
        in_specs += [cs(), cs()]
        args += [cache_k.reshape(N_LAT_B, DEPTH // 2, PAST, QK_W),
                 cache_v.reshape(N_LAT_B, DEPTH // 2, PAST, MIX_A)]
    in_specs += [par(), par(), par(), par(),
                 pl.BlockSpec((1, 1, VAL_DIM), lambda b, h, qi: (e, 0, 0))]
    r3 = lambda a: a.reshape(DEPTH // 2, 1, -1)
    args += [r3(lam_q1), r3(lam_k1), r3(lam_q2), r3(lam_k2), r3(subln_g)]
    return pl.pallas_call(
        functools.partial(_attn_kernel, has_cache, lam0),
        grid=(n_b, N_HEADS, nq),
        in_specs=in_specs,
        out_specs=pl.BlockSpec((TM, VAL_DIM), lambda b, h, qi: (b * nq + qi, h)),
        out_shape=jax.ShapeDtypeStruct((n_b * s_len, MIX_A), BF16),
        compiler_params=_cparams(("arbitrary", "arbitrary", "arbitrary"), 32),
        name="attn_lat" if has_cache else "attn_ctx",
    )(*args)


def _route_tail(x_new, m_ref, n2_ref, rwt_ref, rbb_ref, tri_ref, h2_o, ri_o, rf_o, cnt_o, carry):
    i = pl.program_id(0)

    @pl.when(i == 0)
    def _():
        carry[...] = jnp.zeros_like(carry)

    h2 = _modnorm(x_new, n2_ref[0], m_ref[0, 0, 4:5, :], m_ref[0, 0, 3:4, :])
    h2_o[...] = h2

    hh = h2.astype(BF16)
    hl = (h2 - hh.astype(F32)).astype(BF16)
    rw = rwt_ref[...]
    wh = rw.astype(BF16)
    wl = (rw - wh.astype(F32)).astype(BF16)
    dg = lambda a, b: lax.dot_general(a, b, NT_DIMS, preferred_element_type=F32)
    logits = dg(wh, hh) + dg(wh, hl) + dg(wl, hh)
    score = _sigmoid(logits)
    sel = score + rbb_ref[...]

    neg_inf = jnp.full((1, TM), -jnp.inf, F32)
    grp = []
    for g in range(N_GROUPS):
        s_rows = [sel[PER_GROUP * g + j:PER_GROUP * g + j + 1, :] for j in range(PER_GROUP)]
        c_rows = [score[PER_GROUP * g + j:PER_GROUP * g + j + 1, :] for j in range(PER_GROUP)]
        f_idx = jnp.zeros((1, TM), I32)
        f_val, f_sc = s_rows[0], c_rows[0]
        for j in range(1, PER_GROUP):
            better = s_rows[j] > f_val
            f_idx = jnp.where(better, j, f_idx)
            f_val = jnp.where(better, s_rows[j], f_val)
            f_sc = jnp.where(better, c_rows[j], f_sc)
        s_idx = jnp.zeros((1, TM), I32)
        s_val, s_sc = neg_inf, jnp.zeros((1, TM), F32)
        for j in range(PER_GROUP):
            better = (f_idx != j) & (s_rows[j] > s_val)
            s_idx = jnp.where(better, j, s_idx)
            s_val = jnp.where(better, s_rows[j], s_val)
            s_sc = jnp.where(better, c_rows[j], s_sc)
        grp.append((f_val + s_val, PER_GROUP * g + f_idx, PER_GROUP * g + s_idx, f_sc, s_sc))

    best, e0, e1, c0, c1 = grp[0]
    for g in range(1, N_GROUPS):
        better = grp[g][0] > best
        best = jnp.where(better, grp[g][0], best)
        e0 = jnp.where(better, grp[g][1], e0)
        e1 = jnp.where(better, grp[g][2], e1)
        c0 = jnp.where(better, grp[g][3], c0)
        c1 = jnp.where(better, grp[g][4], c1)
    den = c0 + c1
    w0 = c0 / den
    w1 = c1 / den

    sub = lax.broadcasted_iota(I32, (N_EXPERTS, TM), 0)
    hit0 = sub == jnp.broadcast_to(e0, (N_EXPERTS, TM))
    hit1 = sub == jnp.broadcast_to(e1, (N_EXPERTS, TM))
    onehot = jnp.where(hit0 | hit1, 1.0, 0.0)
    prefix = jnp.dot(onehot.astype(BF16), tri_ref[...], preferred_element_type=F32)
    rank = carry[:, 0:1] + prefix
    rank0 = jnp.sum(jnp.where(hit0, rank, 0.0), axis=0, keepdims=True).astype(I32)
    rank1 = jnp.sum(jnp.where(hit1, rank, 0.0), axis=0, keepdims=True).astype(I32)
    carry[...] = carry[...] + jnp.sum(onehot, axis=1, keepdims=True)
    cnt_o[...] = carry[...].astype(I32)

    pos0 = e0 * EXPERT_CAP + rank0
    pos1 = e1 * EXPERT_CAP + rank1
    sub8 = lax.broadcasted_iota(I32, (8, TM), 0)
    bc = lambda r: jnp.broadcast_to(r, (8, TM))
    ri_o[...] = jnp.where(sub8 == 0, bc(pos0), jnp.where(sub8 == 1, bc(pos1),
                          jnp.where(sub8 == 2, bc(e0), jnp.where(sub8 == 3, bc(e1), 0))))
    rf_o[...] = jnp.where(sub8 == 0, bc(w0), jnp.where(sub8 == 1, bc(w1), 0.0))


def _tail_specs(l):
    in_specs = [
        pl.BlockSpec((1, 1, D), lambda i: (l, 0, 0)),
        pl.BlockSpec((N_EXPERTS, D), lambda i: (0, 0)),
        pl.BlockSpec((N_EXPERTS, TM), lambda i: (0, 0)),
        pl.BlockSpec((TM, TM), lambda i: (0, 0)),
    ]
    out_specs = [
        pl.BlockSpec((TM, D), lambda i: (i, 0)),
        pl.BlockSpec((8, TM), lambda i: (0, i)),
        pl.BlockSpec((8, TM), lambda i: (0, i)),
        pl.BlockSpec((N_EXPERTS, 128), lambda i: (0, 0)),
    ]
    out_shape = [
        jax.ShapeDtypeStruct((T, D), F32),
        jax.ShapeDtypeStruct((8, T), I32),
        jax.ShapeDtypeStruct((8, T), F32),
        jax.ShapeDtypeStruct((N_EXPERTS, 128), I32),
    ]
    scratch = [pltpu.VMEM((N_EXPERTS, 128), F32)]
    return in_specs, out_specs, out_shape, scratch


def _out_kernel(x_ref, m_ref, ac_ref, al_ref, sc_ref, sl_ref, w_ref,
                n2_ref, rwt_ref, rbb_ref, tri_ref,
                x_o, h2_o, ri_o, rf_o, cnt_o, wbf, carry):
    i = pl.program_id(0)

    @pl.when(i == 0)
    def _():
        wbf[...] = w_ref[0].astype(BF16)

    is_ctx = i < N_CTX_TILES
    att = jnp.where(is_ctx, ac_ref[...], al_ref[...])
    sgo = jnp.where(is_ctx, sc_ref[...], sl_ref[...])
    out = (jnp.dot(att, wbf[0:MIX_A, :], preferred_element_type=F32)
           + jnp.dot(sgo, wbf[MIX_A:MIX_A + MIX_B, :], preferred_element_type=F32))
    x_new = x_ref[...] + m_ref[0, 0, 2:3, :] * out
    x_o[...] = x_new
    _route_tail(x_new, m_ref, n2_ref, rwt_ref, rbb_ref, tri_ref, h2_o, ri_o, rf_o, cnt_o, carry)


def _out_proj(l, x, mod, att_c, att_l, sg_c, sg_l, w_out, norm2_g, rwt, rbb, tri):
    e = l // 2
    t_in, t_out, t_shape, t_scratch = _tail_specs(l)
    ctx = lambda: pl.BlockSpec((TM, MIX_A), lambda i: (jnp.minimum(i, N_CTX_TILES - 1), 0))
    lat = lambda: pl.BlockSpec((TM, MIX_A), lambda i: (jnp.maximum(i - N_CTX_TILES, 0), 0))
    in_specs = [
        pl.BlockSpec((TM, D), lambda i: (i, 0)),
        pl.BlockSpec((1, 1, N_MOD, D), lambda i: (l, _cond_of_tile(i), 0, 0)),
        ctx(), lat(), ctx(), lat(),
        pl.BlockSpec((1, D, D), lambda i: (e, 0, 0)),
    ] + t_in
    return pl.pallas_call(
        _out_kernel,
        grid=(N_TILES,),
        in_specs=in_specs,
        out_specs=[pl.BlockSpec((TM, D), lambda i: (i, 0))] + t_out,
        out_shape=[jax.ShapeDtypeStruct((T, D), F32)] + t_shape,
        scratch_shapes=[pltpu.VMEM((D, D), BF16)] + t_scratch,
        input_output_aliases={0: 0},
        compiler_params=_cparams(("arbitrary",), 40),
        name="out_proj_route",
    )(x, mod, att_c, att_l, sg_c, sg_l, w_out, norm2_g.reshape(DEPTH, 1, D), rwt, rbb, tri)


def _pool_kernel(x_ref, xp_ref, xn_ref, m_ref, n1_ref, pw_ref, ps_ref,
                 n2_ref, rwt_ref, rbb_ref, tri_ref,
                 x_o, h2_o, ri_o, rf_o, cnt_o, carry):
    i = pl.program_id(0)
    is_lat = i >= N_CTX_TILES
    ti = jnp.where(is_lat, (i - N_CTX_TILES) % LAT_TILES_PER_SEQ, 0)
    last_ti = jnp.where(is_lat, LAT_TILES_PER_SEQ - 1, 0)
    seq_len = jnp.where(is_lat, LAT_S, CTX_S)

    g1n, sc1, sh1 = n1_ref[0], m_ref[0, 0, 1:2, :], m_ref[0, 0, 0:1, :]
    x = x_ref[...]
    h = _modnorm(x, g1n, sc1, sh1)
    hp = jnp.where(ti == 0, 0.0, _modnorm(xp_ref[...], g1n, sc1, sh1))
    hn = jnp.where(ti == last_ti, 0.0, _modnorm(xn_ref[...], g1n, sc1, sh1))
    hcat = jnp.concatenate([hp, h, hn], axis=0)
    n_rows = TM + 2 * POOL_HALO

    t_pos = ti * TM + lax.broadcasted_iota(I32, (TM, 1), 0)
    outs = []
    for g, w in enumerate(POOL_WINDOWS):
        cols = slice(g * POOL_DIM, (g + 1) * POOL_DIM)
        acc = hcat[:, cols]
        acc = acc + pltpu.roll(acc, 1, 0)
        step = 1
        while 2 * step < w:
            acc = pltpu.roll(acc, step, 0) + pltpu.roll(acc, n_rows - step, 0)
            step *= 2
        lo = jnp.maximum(t_pos - w // 2, 0)
        hi = jnp.minimum(t_pos - w // 2 + w, seq_len)
        mean = acc[POOL_HALO:POOL_HALO + TM, :] / (hi - lo).astype(F32)
        dlt = (mean - h[:, cols]).astype(BF16)
        outs.append(jnp.dot(dlt, pw_ref[0, g].astype(BF16), preferred_element_type=F32))
    out = jnp.concatenate(outs, axis=1) * ps_ref[0]
    x_new = x + m_ref[0, 0, 2:3, :] * out
    x_o[...] = x_new
    _route_tail(x_new, m_ref, n2_ref, rwt_ref, rbb_ref, tri_ref, h2_o, ri_o, rf_o, cnt_o, carry)


def _pool_mixer(l, x, mod, norm1_g, pool_w, pool_scale, norm2_g, rwt, rbb, tri):
    o = l // 2
    t_in, t_out, t_shape, t_scratch = _tail_specs(l)
    halo_blocks = TM // POOL_HALO
    in_specs = [
        pl.BlockSpec((TM, D), lambda i: (i, 0)),
        pl.BlockSpec((POOL_HALO, D), lambda i: (jnp.maximum(i * halo_blocks - 1, 0), 0)),
        pl.BlockSpec((POOL_HALO, D), lambda i: (jnp.minimum((i + 1) * halo_blocks, T // POOL_HALO - 1), 0)),
        pl.BlockSpec((1, 1, N_MOD, D), lambda i: (l, _cond_of_tile(i), 0, 0)),
        pl.BlockSpec((1, 1, D), lambda i: (l, 0, 0)),
        pl.BlockSpec((1, len(POOL_WINDOWS), POOL_DIM, POOL_DIM), lambda i: (o, 0, 0, 0)),
        pl.BlockSpec((1, 1, D), lambda i: (o, 0, 0)),
    ] + t_in
    return pl.pallas_call(
        _pool_kernel,
        grid=(N_TILES,),
        in_specs=in_specs,
        out_specs=[pl.BlockSpec((TM, D), lambda i: (i, 0))] + t_out,
        out_shape=[jax.ShapeDtypeStruct((T, D), F32)] + t_shape,
        scratch_shapes=t_scratch,
        compiler_params=_cparams(("arbitrary",), 32),
        name="pool_route",
    )(x, x, x, mod, norm1_g.reshape(DEPTH, 1, D), pool_w, pool_scale.reshape(DEPTH // 2, 1, D),
      norm2_g.reshape(DEPTH, 1, D), rwt, rbb, tri)


def _row_copies(i, pos_ref, make_copy):
    def issue(r, c):
        for j in range(2):
            make_copy(r, j, pos_ref[(i * TM + r) * 2 + j]).start()
        return c

    lax.fori_loop(0, TM, issue, 0, unroll=8)

    def drain(r, c):
        for j in range(2):
            make_copy(0, j, 0).wait()
        return c

    lax.fori_loop(0, TM, drain, 0, unroll=8)


def _dispatch_kernel(pos_ref, h_ref, xs_ref, sem):
    def make_copy(r, j, p):
        return pltpu.make_async_copy(h_ref.at[pl.ds(r, 1)], xs_ref.at[pl.ds(p, 1)], sem)

    _row_copies(pl.program_id(0), pos_ref, make_copy)


def _dispatch(pos, h2):
    return pl.pallas_call(
        _dispatch_kernel,
        grid_spec=pltpu.PrefetchScalarGridSpec(
            num_scalar_prefetch=1,
            grid=(N_TILES,),
            in_specs=[pl.BlockSpec((TM, D), lambda i, pos: (i, 0))],
            out_specs=pl.BlockSpec(memory_space=pl.ANY),
            scratch_shapes=[pltpu.SemaphoreType.DMA],
        ),
        out_shape=jax.ShapeDtypeStruct((N_EXPERTS * EXPERT_CAP, D), F32),
        compiler_params=_cparams(("arbitrary",), 16),
        name="moe_dispatch",
    )(pos, h2)


def _moe_kernel(rb_ref, e_ref, nv_ref, xs_ref, wg_ref, wu_ref, wd_ref, y_ref, wgb, wub, wdb):
    i = pl.program_id(0)
    n_valid = nv_ref[i]

    @pl.when(n_valid > 0)
    def _():
        changed = (i == 0) | (e_ref[i] != e_ref[jnp.maximum(i - 1, 0)])

        @pl.when(changed)
        def _():
            wgb[...] = wg_ref[0, 0].astype(BF16)
            wub[...] = wu_ref[0, 0].astype(BF16)
            wdb[...] = wd_ref[0, 0].astype(BF16)

        row = lax.broadcasted_iota(I32, (TM, 1), 0)
        x = jnp.where(row < n_valid, xs_ref[...], 0.0).astype(BF16)
        gate = jnp.dot(x, wgb[...], preferred_element_type=F32)
        up = jnp.dot(x, wub[...], preferred_element_type=F32)
        act = (gate * _sigmoid(gate) * up).astype(BF16)
        y_ref[...] = jnp.dot(act, wdb[...], preferred_element_type=F32)


def _experts(l, tile_rb, tile_e, tile_nv, xs, w_gate, w_up, w_down):
    return pl.pallas_call(
        _moe_kernel,
        grid_spec=pltpu.PrefetchScalarGridSpec(
            num_scalar_prefetch=3,
            grid=(MOE_TILES,),
            in_specs=[
                pl.BlockSpec((TM, D), lambda i, rb, e, nv: (rb[i], 0)),
                pl.BlockSpec((1, 1, D, D_FF), lambda i, rb, e, nv: (l, e[i], 0, 0)),
                pl.BlockSpec((1, 1, D, D_FF), lambda i, rb, e, nv: (l, e[i], 0, 0)),
                pl.BlockSpec((1, 1, D_FF, D), lambda i, rb, e, nv: (l, e[i], 0, 0)),
            ],
            out_specs=pl.BlockSpec((TM, D), lambda i, rb, e, nv: (rb[i], 0)),
            scratch_shapes=[pltpu.VMEM((D, D_FF), BF16), pltpu.VMEM((D, D_FF), BF16),
                            pltpu.VMEM((D_FF, D), BF16)],
        ),
        out_shape=jax.ShapeDtypeStruct((N_EXPERTS * EXPERT_CAP, D), F32),
        compiler_params=_cparams(("arbitrary",), 40),
        name="moe_experts",
    )(tile_rb, tile_e, tile_nv, xs, w_gate, w_up, w_down)


def _combine_kernel(pos_ref, x_ref, m_ref, w_ref, y_ref, o_ref, ybuf, sem):
    def make_copy(r, j, p):
        return pltpu.make_async_copy(y_ref.at[pl.ds(p, 1)], ybuf.at[j, pl.ds(r, 1)], sem)

    _row_copies(pl.program_id(0), pos_ref, make_copy)
    moe = w_ref[:, 0:1] * ybuf[0] + w_ref[:, 1:2] * ybuf[1]
    o_ref[...] = x_ref[...] + m_ref[0, 0, 5:6, :] * moe


def _combine(l, pos, x, mod, wcol, y):
    return pl.pallas_call(
        _combine_kernel,
        grid_spec=pltpu.PrefetchScalarGridSpec(
            num_scalar_prefetch=1,
            grid=(N_TILES,),
            in_specs=[
                pl.BlockSpec((TM, D), lambda i, pos: (i, 0)),
                pl.BlockSpec((1, 1, N_MOD, D), lambda i, pos: (l, _cond_of_tile(i), 0, 0)),
                pl.BlockSpec((TM, 8), lambda i, pos: (i, 0)),
                pl.BlockSpec(memory_space=pl.ANY),
            ],
            out_specs=pl.BlockSpec((TM, D), lambda i, pos: (i, 0)),
            scratch_shapes=[pltpu.VMEM((2, TM, D), F32), pltpu.SemaphoreType.DMA],
        ),
        out_shape=jax.ShapeDtypeStruct((T, D), F32),
        input_output_aliases={1: 0},
        compiler_params=_cparams(("arbitrary",), 16),
        name="moe_combine",
    )(pos, x, mod, wcol, y)


def _tile_map(cnt):
    cnt = cnt[:, 0]
    n_tiles = (cnt + TM - 1) // TM
    cum = jnp.cumsum(n_tiles)
    total = cum[-1]
    i = jnp.arange(MOE_TILES, dtype=I32)
    ii = jnp.minimum(i, total - 1)
    e = jnp.sum((ii[:, None] >= cum[None, :]).astype(I32), axis=1)
    onehot = (e[:, None] == jnp.arange(N_EXPERTS, dtype=I32)[None, :]).astype(I32)
    j = ii - jnp.sum(onehot * (cum - n_tiles)[None, :], axis=1)
    rows = jnp.sum(onehot * cnt[None, :], axis=1) - j * TM
    n_valid = jnp.where(i < total, jnp.clip(rows, 0, TM), 0)
    return (e * CAP_TILES + j).astype(I32), e.astype(I32), n_valid.astype(I32)


def _moe(l, x, h2, ri, rf, cnt, mod, w_gate, w_up, w_down):
    pos = ri[0:2].T.reshape(2 * T)
    wcol = rf.T
    tile_rb, tile_e, tile_nv = _tile_map(cnt)
    xs = _dispatch(pos, h2)
    y = _experts(l, tile_rb, tile_e, tile_nv, xs, w_gate, w_up, w_down)
    return _combine(l, pos, x, mod, wcol, y)


def _rope_tables():
    t = jnp.arange(LAT_S)
    row = (t // GRID_W).astype(F32)
    col = (t % GRID_W).astype(F32)
    nf = HEAD_DIM // 4
    inv = ROPE_BASE ** (-jnp.arange(nf, dtype=F32) / nf)
    ar = row[:, None] * inv[None, :]
    ac = col[:, None] * inv[None, :]
    cos = jnp.concatenate([jnp.cos(ar), jnp.cos(ar), jnp.cos(ac), jnp.cos(ac)], axis=1)
    sin = jnp.concatenate([-jnp.sin(ar), jnp.sin(ar), -jnp.sin(ac), jnp.sin(ac)], axis=1)
    reps = QK_W // HEAD_DIM
    return jnp.tile(cos, (1, reps)), jnp.tile(sin, (1, reps))


def kernel(x_prompt, x_sample, cache_k, cache_v, c, c_ctx, norm1_g, norm2_g, w_ada, b_ada, w_in, w_out,
           q_norm_g, k_norm_g, lam_q1, lam_k1, lam_q2, lam_k2, subln_g, sg_w, sg_b, pool_w, pool_scale,
           router_w, router_b, w_gate, w_up, w_down):
    x = jnp.concatenate([x_prompt.reshape(T_CTX, D), x_sample.reshape(T_LAT, D)], axis=0)
    cond = jnp.concatenate([c_ctx[None, :], c], axis=0)
    mod = _modulation(cond, w_ada, b_ada)

    seg_id = np.arange(QK_W) // HEAD_DIM
    seg = jnp.asarray((seg_id[:, None] == seg_id[None, :]) / HEAD_DIM, dtype=BF16)
    tri = jnp.asarray(np.arange(TM)[:, None] < np.arange(TM)[None, :], dtype=BF16)
    rope_tabs = _rope_tables()
    rwt = router_w.T
    rbb = jnp.broadcast_to(router_b[:, None], (N_EXPERTS, TM))
    state_k = jnp.zeros((1,), F32)
    state_v = jnp.zeros((1,), F32)

    for l in range(DEPTH):
        if l % 2 == 0:
            e = l // 2
            qg = jnp.tile(q_norm_g[e], QK_W // HEAD_DIM)[None, :]
            kg = jnp.tile(k_norm_g[e], QK_W // HEAD_DIM)[None, :]
            sgb = jnp.broadcast_to(sg_b[:, :, :, None], (DEPTH // 2, SG_GROUPS, SG_CHUNK, 128))
            common = (l, x, mod, norm1_g, w_in, qg, kg, seg, sg_w, sgb)
            if e == 0:
                state_k = jnp.zeros((N_CTX_B, DEPTH // 2, CTX_S, QK_W), F32)
                state_v = jnp.zeros((N_CTX_B, DEPTH // 2, CTX_S, MIX_A), F32)
            q_c, k_c, v_c, sg_c, state_k, state_v = _front(False, *common, None, (state_k, state_v))
            q_l, k_l, v_l, sg_l = _front(True, *common, rope_tabs, None)
            lam = (lam_q1, lam_k1, lam_q2, lam_k2, subln_g)
            att_c = _attention(False, l, q_c, k_c, v_c, None, None, *lam)
            att_l = _attention(True, l, q_l, k_l, v_l, cache_k, cache_v, *lam)
            x, h2, ri, rf, cnt = _out_proj(l, x, mod, att_c, att_l, sg_c, sg_l, w_out, norm2_g, rwt, rbb, tri)
        else:
            x, h2, ri, rf, cnt = _pool_mixer(l, x, mod, norm1_g, pool_w, pool_scale, norm2_g, rwt, rbb, tri)
        x = _moe(l, x, h2, ri, rf, cnt, mod, w_gate, w_up, w_down)

    y_prompt = x[:T_CTX].reshape(N_CTX_B, CTX_S, D)
    y_sample = x[T_CTX:].reshape(N_LAT_B, LAT_S, D)
    state_k = state_k.reshape(N_CTX_B, DEPTH // 2, CTX_S, N_HEADS, 2, HEAD_DIM)
    state_v = state_v.reshape(N_CTX_B, DEPTH // 2, CTX_S, N_HEADS, VAL_DIM)
    return y_prompt, y_sample, state_k, state_v
```

```python
import functools
import math

import jax
import jax.numpy as jnp
import numpy as np
from jax import lax
from jax.experimental import pallas as pl
from jax.experimental.pallas import tpu as pltpu

F32 = jnp.float32
BF16 = jnp.bfloat16
I32 = jnp.int32

D = 1024
N_CTX_B, CTX_S = 16, 256
N_LAT_B, LAT_S = 2, 1024
PAST = 512
DEPTH = 4
T_CTX = N_CTX_B * CTX_S
T_LAT = N_LAT_B * LAT_S
T = T_CTX + T_LAT
GRID_W = 64
N_HEADS = 4
HEAD_DIM = 64
VAL_DIM = 128
QK_W = 512
MIX_A = 512
MIX_B = 512
IN_W = 2560
SG_GROUPS = 4
SG_CHUNK = 128
POOL_WINDOWS = (2, 4, 8, 16)
POOL_DIM = 256
POOL_HALO = 8
N_EXPERTS = 16
N_GROUPS = 4
PER_GROUP = 4
D_FF = 512
N_MOD = 6
N_COND = 1 + N_LAT_B
EPS = 1e-6
ROPE_BASE = 10000.0

TM = 256
N_TILES = T // TM
N_CTX_TILES = T_CTX // TM
LAT_TILES_PER_SEQ = LAT_S // TM
EXPERT_CAP = T
CAP_TILES = EXPERT_CAP // TM
MOE_TILES = 2 * T // TM + N_EXPERTS
MOD_TN = 1024
MIB = 1024 * 1024

NT_DIMS = (((1,), (1,)), ((), ()))


def _cparams(sem, vmem_mib):
    return pltpu.CompilerParams(dimension_semantics=sem, vmem_limit_bytes=vmem_mib * MIB)


def _hbm(x):
    try:
        return pltpu.with_memory_space_constraint(x, pltpu.HBM)
    except ValueError:
        return x


def _hbm_out(shape, dtype):
    return pltpu.HBM(shape, dtype)


def _sigmoid(x):
    return 1.0 / (1.0 + jnp.exp(-x))


def _gelu_tanh(x):
    c = math.sqrt(2.0 / math.pi)
    return x * (0.5 * (1.0 + jnp.tanh(c * (x + 0.044715 * (x * x * x)))))


def _modnorm(x, g, scale, shift):
    ms = jnp.mean(x * x, axis=-1, keepdims=True)
    return (x * lax.rsqrt(ms + EPS) * g) * (1.0 + scale) + shift


def _cond_of_tile(i):
    return jnp.where(i < N_CTX_TILES, 0, 1 + (i - N_CTX_TILES) // LAT_TILES_PER_SEQ)


def _mod_kernel(cb_ref, w_ref, b_ref, o_ref, sb_ref):
    @pl.when((pl.program_id(0) == 0) & (pl.program_id(1) == 0))
    def _():
        cb = cb_ref[...]
        sb_ref[...] = cb * _sigmoid(cb)

    for j in range(N_COND):
        cols = []
        for n0 in range(0, MOD_TN, 128):
            cols.append(jnp.sum(w_ref[0, :, n0:n0 + 128] * sb_ref[j], axis=0, keepdims=True))
        o_ref[0, j:j + 1, :] = jnp.concatenate(cols, axis=1) + b_ref[0]


def _modulation(cond, w_ada, b_ada):
    cb = jnp.broadcast_to(cond[:, :, None], (N_COND, D, 128))
    out = pl.pallas_call(
        _mod_kernel,
        grid=(DEPTH, N_MOD * D // MOD_TN),
        in_specs=[
            pl.BlockSpec((N_COND, D, 128), lambda l, n: (0, 0, 0)),
            pl.BlockSpec((1, D, MOD_TN), lambda l, n: (l, 0, n)),
            pl.BlockSpec((1, 1, MOD_TN), lambda l, n: (l, 0, n)),
        ],
        out_specs=pl.BlockSpec((1, N_COND, MOD_TN), lambda l, n: (l, 0, n)),
        out_shape=jax.ShapeDtypeStruct((DEPTH, N_COND, N_MOD * D), F32),
        scratch_shapes=[pltpu.VMEM((N_COND, D, 128), F32)],
        compiler_params=_cparams(("arbitrary", "arbitrary"), 24),
        name="adaln_mod",
    )(cb, w_ada, b_ada.reshape(DEPTH, 1, N_MOD * D))
    return out.reshape(DEPTH, N_COND, N_MOD, D)


def _front_kernel(latent, *refs):
    if latent:
        (x_ref, m_ref, n1_ref, w_ref, qg_ref, kg_ref, seg_ref, sgw_ref, sgb_ref, cos_ref, sin_ref,
         q_o, k_o, v_o, sg_o, wbf) = refs
    else:
        (x_ref, m_ref, n1_ref, w_ref, qg_ref, kg_ref, seg_ref, sgw_ref, sgb_ref, _sk_in, _sv_in,
         q_o, k_o, v_o, sg_o, sk_o, sv_o, wbf) = refs

    @pl.when(pl.program_id(0) == 0)
    def _():
        for c0 in range(0, IN_W, 512):
            wbf[:, c0:c0 + 512] = w_ref[0, :, c0:c0 + 512].astype(BF16)

    h = _modnorm(x_ref[...], n1_ref[0], m_ref[0, 0, 1:2, :], m_ref[0, 0, 0:1, :])
    proj = jnp.dot(h.astype(BF16), wbf[...], preferred_element_type=F32)

    def qk_norm(z, g):
        ms = jnp.dot((z * z).astype(BF16), seg_ref[...], preferred_element_type=F32)
        return z * lax.rsqrt(ms + EPS) * g

    q = qk_norm(proj[:, 0:QK_W], qg_ref[...])
    k = qk_norm(proj[:, QK_W:2 * QK_W], kg_ref[...])
    v = proj[:, 2 * QK_W:2 * QK_W + MIX_A]

    if latent:
        lane = lax.broadcasted_iota(I32, (1, QK_W), 1)
        first = (lane % 32) < 16

        def rope(z):
            partner = jnp.where(first, pltpu.roll(z, QK_W - 16, 1), pltpu.roll(z, 16, 1))
            return z * cos_ref[...] + partner * sin_ref[...]

        q = rope(q)
        k = rope(k)
    else:
        sk_o[0, 0] = k
        sv_o[0, 0] = v

    q_o[...] = (q * (HEAD_DIM ** -0.5)).astype(BF16)
    k_o[...] = k.astype(BF16)
    v_o[...] = v.astype(BF16)

    gu0 = 2 * QK_W + MIX_A
    gv0 = gu0 + MIX_B
    for g in range(SG_GROUPS):
        wg = sgw_ref[0, g].astype(BF16)
        for n in range(TM // SG_CHUNK):
            rows = slice(n * SG_CHUNK, (n + 1) * SG_CHUNK)
            cu = slice(gu0 + g * 128, gu0 + (g + 1) * 128)
            cv = slice(gv0 + g * 128, gv0 + (g + 1) * 128)
            gv = _gelu_tanh(proj[rows, cv]).astype(BF16)
            mixed = jnp.dot(wg, gv, preferred_element_type=F32) + sgb_ref[0, g]
            sg_o[rows, g * 128:(g + 1) * 128] = (_gelu_tanh(proj[rows, cu]) * mixed).astype(BF16)


def _front(latent, l, x, mod, norm1_g, w_in, qg, kg, seg, sg_w, sgb, rope_tabs, states):
    e = l // 2
    n_tiles = (T_LAT if latent else T_CTX) // TM
    off = N_CTX_TILES if latent else 0
    rows = T_LAT if latent else T_CTX
    if latent:
        cond_map = lambda i: (l, 1 + i // LAT_TILES_PER_SEQ, 0, 0)
    else:
        cond_map = lambda i: (l, 0, 0, 0)
    in_specs = [
        pl.BlockSpec((TM, D), lambda i: (i + off, 0)),
        pl.BlockSpec((1, 1, N_MOD, D), cond_map),
        pl.BlockSpec((1, 1, D), lambda i: (l, 0, 0)),
        pl.BlockSpec((1, D, IN_W), lambda i: (e, 0, 0)),
        pl.BlockSpec((1, QK_W), lambda i: (0, 0)),
        pl.BlockSpec((1, QK_W), lambda i: (0, 0)),
        pl.BlockSpec((QK_W, QK_W), lambda i: (0, 0)),
        pl.BlockSpec((1, SG_GROUPS, SG_CHUNK, SG_CHUNK), lambda i: (e, 0, 0, 0)),
        pl.BlockSpec((1, SG_GROUPS, SG_CHUNK, 128), lambda i: (e, 0, 0, 0)),
    ]
    args = [_hbm(x), mod, norm1_g.reshape(DEPTH, 1, D), w_in, qg, kg, seg, sg_w, sgb]
    tok = lambda: pl.BlockSpec((TM, QK_W), lambda i: (i, 0))
    out_specs = [tok(), tok(), tok(), tok()]
    out_shape = [_hbm_out((rows, QK_W), BF16)] * 4
    aliases = {}
    if latent:
        cos_t, sin_t = rope_tabs
        in_specs += [pl.BlockSpec((TM, QK_W), lambda i: (i % LAT_TILES_PER_SEQ, 0))] * 2
        args += [cos_t, sin_t]
    else:
        state_k, state_v = states
        in_specs += [pl.BlockSpec(memory_space=pl.ANY)] * 2
        args += [state_k, state_v]
        st = lambda: pl.BlockSpec((1, 1, CTX_S, QK_W), lambda i: (i, e, 0, 0))
        out_specs += [st(), st()]
        out_shape += [_hbm_out((N_CTX_B, DEPTH // 2, CTX_S, QK_W), F32)] * 2
        aliases = {9: 4, 10: 5}
    return pl.pallas_call(
        functools.partial(_front_kernel, latent),
        grid=(n_tiles,),
        in_specs=in_specs,
        out_specs=out_specs,
        out_shape=out_shape,
        scratch_shapes=[pltpu.VMEM((D, IN_W), BF16)],
        input_output_aliases=aliases,
        compiler_params=_cparams(("arbitrary",), 48),
        name="front_lat" if latent else "front_ctx",
    )(*args)


def _attn_kernel(has_cache, heads, lam0, *refs):
    if has_cache:
        (q_ref, k_ref, v_ref, ck_ref, cv_ref, lq1, lk1, lq2, lk2, sub_ref, o_ref) = refs
    else:
        (q_ref, k_ref, v_ref, lq1, lk1, lq2, lk2, sub_ref, o_ref) = refs

    lane = lax.broadcasted_iota(I32, (1, VAL_DIM), 1)
    lam = (jnp.exp(jnp.sum(lq1[0] * lk1[0], axis=-1, keepdims=True))
           - jnp.exp(jnp.sum(lq2[0] * lk2[0], axis=-1, keepdims=True)) + lam0)

    for h in range(heads):
        hs = slice(h * VAL_DIM, (h + 1) * VAL_DIM)
        qf = q_ref[:, hs].astype(F32)
        qc = [jnp.where(lane < HEAD_DIM, qf, 0.0).astype(BF16),
              jnp.where(lane >= HEAD_DIM, qf, 0.0).astype(BF16)]
        keys = [k_ref[:, hs]]
        vals = [v_ref[:, hs]]
        if has_cache:
            keys.append(ck_ref[0, 0].astype(BF16))
            vals.append(cv_ref[0, 0].astype(BF16))

        probs = []
        for c in range(2):
            s = [lax.dot_general(qc[c], kk, NT_DIMS, preferred_element_type=F32) for kk in keys]
            m = s[0].max(axis=-1, keepdims=True)
            for sx in s[1:]:
                m = jnp.maximum(m, sx.max(axis=-1, keepdims=True))
            p = [jnp.exp(sx - m) for sx in s]
            den = p[0].sum(axis=-1, keepdims=True)
            for px in p[1:]:
                den = den + px.sum(axis=-1, keepdims=True)
            probs.append((p, den))

        r0 = 1.0 / probs[0][1]
        r1 = lam / probs[1][1]
        o = None
        for idx in range(len(keys)):
            a = (probs[0][0][idx] * r0 - probs[1][0][idx] * r1).astype(BF16)
            t = jnp.dot(a, vals[idx], preferred_element_type=F32)
            o = t if o is None else o + t
        ms = jnp.mean(o * o, axis=-1, keepdims=True)
        o_ref[:, hs] = ((o * lax.rsqrt(ms + EPS) * sub_ref[0]) * (1.0 - lam0)).astype(BF16)


def _attention(has_cache, l, q, k, v, cache_k, cache_v, lam_q1, lam_k1, lam_q2, lam_k2, subln_g):
    e = l // 2
    lam0 = 0.8 - 0.6 * math.exp(-0.3 * l)
    n_b, s_len = (N_LAT_B, LAT_S) if has_cache else (N_CTX_B, CTX_S)
    heads = 1 if has_cache else N_HEADS
    width = heads * VAL_DIM
    nq = s_len // TM
    kv = lambda: pl.BlockSpec((s_len, width), lambda b, h, qi: (b, h))
    par = lambda: pl.BlockSpec((1, 1, HEAD_DIM), lambda b, h, qi: (e, 0, 0))
    in_specs = [pl.BlockSpec((TM, width), lambda b, h, qi: (b * nq + qi, h)), kv(), kv()]
    args = [_hbm(q), _hbm(k), _hbm(v)]
    if has_cache:
        cs = lambda: pl.BlockSpec((1, 1, PAST, VAL_DIM), lambda b, h, qi: (b, e, 0, h))
        in_specs += [cs(), cs()]
        args += [cache_k.reshape(N_LAT_B, DEPTH // 2, PAST, QK_W),
                 cache_v.reshape(N_LAT_B, DEPTH // 2, PAST, MIX_A)]
    in_specs += [par(), par(), par(), par(),
                 pl.BlockSpec((1, 1, VAL_DIM), lambda b, h, qi: (e, 0, 0))]
    r3 = lambda a: a.reshape(DEPTH // 2, 1, -1)
    args += [r3(lam_q1), r3(lam_k1), r3(lam_q2), r3(lam_k2), r3(subln_g)]
    return pl.pallas_call(
        functools.partial(_attn_kernel, has_cache, heads, lam0),
        grid=(n_b, N_HEADS // heads, nq),
        in_specs=in_specs,
        out_specs=pl.BlockSpec((TM, width), lambda b, h, qi: (b * nq + qi, h)),
        out_shape=_hbm_out((n_b * s_len, MIX_A), BF16),
        compiler_params=_cparams(("arbitrary", "arbitrary", "arbitrary"), 32),
        name="attn_lat" if has_cache else "attn_ctx",
    )(*args)


def _route_tail(x_new, m_ref, n2_ref, rwt_ref, rbb_ref, tri_ref, h2_o, ri_o, rf_o, cnt_o, carry):
    i = pl.program_id(0)

    @pl.when(i == 0)
    def _():
        carry[...] = jnp.zeros_like(carry)

    h2 = _modnorm(x_new, n2_ref[0], m_ref[0, 0, 4:5, :], m_ref[0, 0, 3:4, :])
    h2_o[...] = h2

    hh = h2.astype(BF16)
    hl = (h2 - hh.astype(F32)).astype(BF16)
    rw = rwt_ref[...]
    wh = rw.astype(BF16)
    wl = (rw - wh.astype(F32)).astype(BF16)
    dg = lambda a, b: lax.dot_general(a, b, NT_DIMS, preferred_element_type=F32)
    logits = dg(wh, hh) + dg(wh, hl) + dg(wl, hh)
    score = _sigmoid(logits)
    sel = score + rbb_ref[...]

    neg_inf = jnp.full((1, TM), -jnp.inf, F32)
    grp = []
    for g in range(N_GROUPS):
        s_rows = [sel[PER_GROUP * g + j:PER_GROUP * g + j + 1, :] for j in range(PER_GROUP)]
        c_rows = [score[PER_GROUP * g + j:PER_GROUP * g + j + 1, :] for j in range(PER_GROUP)]
        f_idx = jnp.zeros((1, TM), I32)
        f_val, f_sc = s_rows[0], c_rows[0]
        for j in range(1, PER_GROUP):
            better = s_rows[j] > f_val
            f_idx = jnp.where(better, j, f_idx)
            f_val = jnp.where(better, s_rows[j], f_val)
            f_sc = jnp.where(better, c_rows[j], f_sc)
        s_idx = jnp.zeros((1, TM), I32)
        s_val, s_sc = neg_inf, jnp.zeros((1, TM), F32)
        for j in range(PER_GROUP):
            better = (f_idx != j) & (s_rows[j] > s_val)
            s_idx = jnp.where(better, j, s_idx)
            s_val = jnp.where(better, s_rows[j], s_val)
            s_sc = jnp.where(better, c_rows[j], s_sc)
        grp.append((f_val + s_val, PER_GROUP * g + f_idx, PER_GROUP * g + s_idx, f_sc, s_sc))

    best, e0, e1, c0, c1 = grp[0]
    for g in range(1, N_GROUPS):
        better = grp[g][0] > best
        best = jnp.where(better, grp[g][0], best)
        e0 = jnp.where(better, grp[g][1], e0)
        e1 = jnp.where(better, grp[g][2], e1)
        c0 = jnp.where(better, grp[g][3], c0)
        c1 = jnp.where(better, grp[g][4], c1)
    den = c0 + c1
    w0 = c0 / den
    w1 = c1 / den

    sub = lax.broadcasted_iota(I32, (N_EXPERTS, TM), 0)
    hit0 = sub == jnp.broadcast_to(e0, (N_EXPERTS, TM))
    hit1 = sub == jnp.broadcast_to(e1, (N_EXPERTS, TM))
    onehot = jnp.where(hit0 | hit1, 1.0, 0.0)
    prefix = jnp.dot(onehot.astype(BF16), tri_ref[...], preferred_element_type=F32)
    rank = carry[:, 0:1] + prefix
    rank0 = jnp.sum(jnp.where(hit0, rank, 0.0), axis=0, keepdims=True).astype(I32)
    rank1 = jnp.sum(jnp.where(hit1, rank, 0.0), axis=0, keepdims=True).astype(I32)
    carry[...] = carry[...] + jnp.sum(onehot, axis=1, keepdims=True)
    cnt_o[...] = carry[...].astype(I32)

    pos0 = e0 * EXPERT_CAP + rank0
    pos1 = e1 * EXPERT_CAP + rank1
    sub8 = lax.broadcasted_iota(I32, (8, TM), 0)
    bc = lambda r: jnp.broadcast_to(r, (8, TM))
    ri_o[...] = jnp.where(sub8 == 0, bc(pos0), jnp.where(sub8 == 1, bc(pos1),
                          jnp.where(sub8 == 2, bc(e0), jnp.where(sub8 == 3, bc(e1), 0))))
    rf_o[...] = jnp.where(sub8 == 0, bc(w0), jnp.where(sub8 == 1, bc(w1), 0.0))


def _tail_specs(l):
    in_specs = [
        pl.BlockSpec((1, 1, D), lambda i: (l, 0, 0)),
        pl.BlockSpec((N_EXPERTS, D), lambda i: (0, 0)),
        pl.BlockSpec((N_EXPERTS, TM), lambda i: (0, 0)),
        pl.BlockSpec((TM, TM), lambda i: (0, 0)),
    ]
    out_specs = [
        pl.BlockSpec((TM, D), lambda i: (i, 0)),
        pl.BlockSpec((8, TM), lambda i: (0, i)),
        pl.BlockSpec((8, TM), lambda i: (0, i)),
        pl.BlockSpec((N_EXPERTS, 128), lambda i: (0, 0)),
    ]
    out_shape = [
        _hbm_out((T, D), F32),
        jax.ShapeDtypeStruct((8, T), I32),
        jax.ShapeDtypeStruct((8, T), F32),
        jax.ShapeDtypeStruct((N_EXPERTS, 128), I32),
    ]
    scratch = [pltpu.VMEM((N_EXPERTS, 128), F32)]
    return in_specs, out_specs, out_shape, scratch


def _out_kernel(x_ref, m_ref, ac_ref, al_ref, sc_ref, sl_ref, w_ref,
                n2_ref, rwt_ref, rbb_ref, tri_ref,
                x_o, h2_o, ri_o, rf_o, cnt_o, wbf, carry):
    i = pl.program_id(0)

    @pl.when(i == 0)
    def _():
        wbf[...] = w_ref[0].astype(BF16)

    is_ctx = i < N_CTX_TILES
    att = jnp.where(is_ctx, ac_ref[...], al_ref[...])
    sgo = jnp.where(is_ctx, sc_ref[...], sl_ref[...])
    out = (jnp.dot(att, wbf[0:MIX_A, :], preferred_element_type=F32)
           + jnp.dot(sgo, wbf[MIX_A:MIX_A + MIX_B, :], preferred_element_type=F32))
    x_new = x_ref[...] + m_ref[0, 0, 2:3, :] * out
    x_o[...] = x_new
    _route_tail(x_new, m_ref, n2_ref, rwt_ref, rbb_ref, tri_ref, h2_o, ri_o, rf_o, cnt_o, carry)


def _out_proj(l, x, mod, att_c, att_l, sg_c, sg_l, w_out, norm2_g, rwt, rbb, tri):
    e = l // 2
    t_in, t_out, t_shape, t_scratch = _tail_specs(l)
    ctx = lambda: pl.BlockSpec((TM, MIX_A), lambda i: (jnp.minimum(i, N_CTX_TILES - 1), 0))
    lat = lambda: pl.BlockSpec((TM, MIX_A), lambda i: (jnp.maximum(i - N_CTX_TILES, 0), 0))
    in_specs = [
        pl.BlockSpec((TM, D), lambda i: (i, 0)),
        pl.BlockSpec((1, 1, N_MOD, D), lambda i: (l, _cond_of_tile(i), 0, 0)),
        ctx(), lat(), ctx(), lat(),
        pl.BlockSpec((1, D, D), lambda i: (e, 0, 0)),
    ] + t_in
    return pl.pallas_call(
        _out_kernel,
        grid=(N_TILES,),
        in_specs=in_specs,
        out_specs=[pl.BlockSpec((TM, D), lambda i: (i, 0))] + t_out,
        out_shape=[_hbm_out((T, D), F32)] + t_shape,
        scratch_shapes=[pltpu.VMEM((D, D), BF16)] + t_scratch,
        input_output_aliases={0: 0},
        compiler_params=_cparams(("arbitrary",), 40),
        name="out_proj_route",
    )(_hbm(x), mod, _hbm(att_c), _hbm(att_l), _hbm(sg_c), _hbm(sg_l), w_out,
      norm2_g.reshape(DEPTH, 1, D), rwt, rbb, tri)


def _pool_kernel(x_ref, xp_ref, xn_ref, m_ref, n1_ref, pw_ref, ps_ref,
                 n2_ref, rwt_ref, rbb_ref, tri_ref,
                 x_o, h2_o, ri_o, rf_o, cnt_o, carry):
    i = pl.program_id(0)
    is_lat = i >= N_CTX_TILES
    ti = jnp.where(is_lat, (i - N_CTX_TILES) % LAT_TILES_PER_SEQ, 0)
    last_ti = jnp.where(is_lat, LAT_TILES_PER_SEQ - 1, 0)
    seq_len = jnp.where(is_lat, LAT_S, CTX_S)

    g1n, sc1, sh1 = n1_ref[0], m_ref[0, 0, 1:2, :], m_ref[0, 0, 0:1, :]
    x = x_ref[...]
    h = _modnorm(x, g1n, sc1, sh1)
    hp = jnp.where(ti == 0, 0.0, _modnorm(xp_ref[...], g1n, sc1, sh1))
    hn = jnp.where(ti == last_ti, 0.0, _modnorm(xn_ref[...], g1n, sc1, sh1))
    hcat = jnp.concatenate([hp, h, hn], axis=0)
    n_rows = TM + 2 * POOL_HALO

    t_pos = ti * TM + lax.broadcasted_iota(I32, (TM, 1), 0)
    outs = []
    for g, w in enumerate(POOL_WINDOWS):
        cols = slice(g * POOL_DIM, (g + 1) * POOL_DIM)
        acc = hcat[:, cols]
        acc = acc + pltpu.roll(acc, 1, 0)
        step = 1
        while 2 * step < w:
            acc = pltpu.roll(acc, step, 0) + pltpu.roll(acc, n_rows - step, 0)
            step *= 2
        lo = jnp.maximum(t_pos - w // 2, 0)
        hi = jnp.minimum(t_pos - w // 2 + w, seq_len)
        mean = acc[POOL_HALO:POOL_HALO + TM, :] / (hi - lo).astype(F32)
        dlt = (mean - h[:, cols]).astype(BF16)
        outs.append(jnp.dot(dlt, pw_ref[0, g].astype(BF16), preferred_element_type=F32))
    out = jnp.concatenate(outs, axis=1) * ps_ref[0]
    x_new = x + m_ref[0, 0, 2:3, :] * out
    x_o[...] = x_new
    _route_tail(x_new, m_ref, n2_ref, rwt_ref, rbb_ref, tri_ref, h2_o, ri_o, rf_o, cnt_o, carry)


def _pool_mixer(l, x, mod, norm1_g, pool_w, pool_scale, norm2_g, rwt, rbb, tri):
    o = l // 2
    t_in, t_out, t_shape, t_scratch = _tail_specs(l)
    halo_blocks = TM // POOL_HALO
    in_specs = [
        pl.BlockSpec((TM, D), lambda i: (i, 0)),
        pl.BlockSpec((POOL_HALO, D), lambda i: (jnp.maximum(i * halo_blocks - 1, 0), 0)),
        pl.BlockSpec((POOL_HALO, D), lambda i: (jnp.minimum((i + 1) * halo_blocks, T // POOL_HALO - 1), 0)),
        pl.BlockSpec((1, 1, N_MOD, D), lambda i: (l, _cond_of_tile(i), 0, 0)),
        pl.BlockSpec((1, 1, D), lambda i: (l, 0, 0)),
        pl.BlockSpec((1, len(POOL_WINDOWS), POOL_DIM, POOL_DIM), lambda i: (o, 0, 0, 0)),
        pl.BlockSpec((1, 1, D), lambda i: (o, 0, 0)),
    ] + t_in
    return pl.pallas_call(
        _pool_kernel,
        grid=(N_TILES,),
        in_specs=in_specs,
        out_specs=[pl.BlockSpec((TM, D), lambda i: (i, 0))] + t_out,
        out_shape=[_hbm_out((T, D), F32)] + t_shape,
        scratch_shapes=t_scratch,
        compiler_params=_cparams(("arbitrary",), 32),
        name="pool_route",
    )(_hbm(x), _hbm(x), _hbm(x), mod, norm1_g.reshape(DEPTH, 1, D), pool_w,
      pool_scale.reshape(DEPTH // 2, 1, D),
      norm2_g.reshape(DEPTH, 1, D), rwt, rbb, tri)


def _row_copies(i, pos_ref, make_copy):
    def issue(r, c):
        for j in range(2):
            make_copy(r, j, pos_ref[(i * TM + r) * 2 + j]).start()
        return c

    lax.fori_loop(0, TM, issue, 0, unroll=8)

    def drain(r, c):
        for j in range(2):
            make_copy(0, j, 0).wait()
        return c

    lax.fori_loop(0, TM, drain, 0, unroll=8)


def _dispatch_kernel(pos_ref, h_ref, xs_ref, sem):
    def make_copy(r, j, p):
        return pltpu.make_async_copy(h_ref.at[pl.ds(r, 1)], xs_ref.at[pl.ds(p, 1)], sem)

    _row_copies(pl.program_id(0), pos_ref, make_copy)


def _dispatch(pos, h2):
    return pl.pallas_call(
        _dispatch_kernel,
        grid_spec=pltpu.PrefetchScalarGridSpec(
            num_scalar_prefetch=1,
            grid=(N_TILES,),
            in_specs=[pl.BlockSpec((TM, D), lambda i, pos: (i, 0))],
            out_specs=pl.BlockSpec(memory_space=pl.ANY),
            scratch_shapes=[pltpu.SemaphoreType.DMA],
        ),
        out_shape=_hbm_out((N_EXPERTS * EXPERT_CAP, D), F32),
        compiler_params=_cparams(("arbitrary",), 16),
        name="moe_dispatch",
    )(pos, _hbm(h2))


def _moe_kernel(l, rb_ref, e_ref, nv_ref, first_ref, ord_ref, nxt_ref, xs_ref, wg_hbm, wu_hbm, wd_hbm, y_ref,
                wgf, wuf, wdf, wgb, wub, wdb, sems):
    i = pl.program_id(0)
    n_valid = nv_ref[i]

    def weight_copies(e, slot):
        return (pltpu.make_async_copy(wg_hbm.at[l, e], wgf.at[slot], sems.at[slot, 0]),
                pltpu.make_async_copy(wu_hbm.at[l, e], wuf.at[slot], sems.at[slot, 1]),
                pltpu.make_async_copy(wd_hbm.at[l, e], wdf.at[slot], sems.at[slot, 2]))

    @pl.when(first_ref[i] == 1)
    def _():
        slot = ord_ref[i] % 2

        @pl.when(ord_ref[i] == 0)
        def _():
            for cp in weight_copies(e_ref[i], slot):
                cp.start()

        for cp in weight_copies(e_ref[i], slot):
            cp.wait()

        @pl.when(nxt_ref[i] >= 0)
        def _():
            for cp in weight_copies(nxt_ref[i], 1 - slot):
                cp.start()

        wgb[...] = wgf[slot].astype(BF16)
        wub[...] = wuf[slot].astype(BF16)
        wdb[...] = wdf[slot].astype(BF16)

    @pl.when(n_valid > 0)
    def _():
        row = lax.broadcasted_iota(I32, (TM, 1), 0)
        x = jnp.where(row < n_valid, xs_ref[...], 0.0).astype(BF16)
        gate = jnp.dot(x, wgb[...], preferred_element_type=F32)
        up = jnp.dot(x, wub[...], preferred_element_type=F32)
        act = (gate * _sigmoid(gate) * up).astype(BF16)
        y_ref[...] = jnp.dot(act, wdb[...], preferred_element_type=F32)


def _experts(l, tile_map, xs, w_gate, w_up, w_down):
    n_pre = len(tile_map)
    blk = lambda i, rb, *_: (rb[i], 0)
    return pl.pallas_call(
        functools.partial(_moe_kernel, l),
        grid_spec=pltpu.PrefetchScalarGridSpec(
            num_scalar_prefetch=n_pre,
            grid=(MOE_TILES,),
            in_specs=[
                pl.BlockSpec((TM, D), blk),
                pl.BlockSpec(memory_space=pl.ANY),
                pl.BlockSpec(memory_space=pl.ANY),
                pl.BlockSpec(memory_space=pl.ANY),
            ],
            out_specs=pl.BlockSpec((TM, D), blk),
            scratch_shapes=[pltpu.VMEM((2, D, D_FF), F32), pltpu.VMEM((2, D, D_FF), F32),
                            pltpu.VMEM((2, D_FF, D), F32),
                            pltpu.VMEM((D, D_FF), BF16), pltpu.VMEM((D, D_FF), BF16),
                            pltpu.VMEM((D_FF, D), BF16),
                            pltpu.SemaphoreType.DMA((2, 3))],
        ),
        out_shape=_hbm_out((N_EXPERTS * EXPERT_CAP, D), F32),
        compiler_params=_cparams(("arbitrary",), 40),
        name="moe_experts",
    )(*tile_map, _hbm(xs), w_gate, w_up, w_down)


def _combine_kernel(pos_ref, x_ref, m_ref, w_ref, y_ref, o_ref, ybuf, sem):
    def make_copy(r, j, p):
        return pltpu.make_async_copy(y_ref.at[pl.ds(p, 1)], ybuf.at[j, pl.ds(r, 1)], sem)

    _row_copies(pl.program_id(0), pos_ref, make_copy)
    moe = w_ref[:, 0:1] * ybuf[0] + w_ref[:, 1:2] * ybuf[1]
    o_ref[...] = x_ref[...] + m_ref[0, 0, 5:6, :] * moe


def _combine(l, pos, x, mod, wcol, y):
    return pl.pallas_call(
        _combine_kernel,
        grid_spec=pltpu.PrefetchScalarGridSpec(
            num_scalar_prefetch=1,
            grid=(N_TILES,),
            in_specs=[
                pl.BlockSpec((TM, D), lambda i, pos: (i, 0)),
                pl.BlockSpec((1, 1, N_MOD, D), lambda i, pos: (l, _cond_of_tile(i), 0, 0)),
                pl.BlockSpec((TM, 8), lambda i, pos: (i, 0)),
                pl.BlockSpec(memory_space=pl.ANY),
            ],
            out_specs=pl.BlockSpec((TM, D), lambda i, pos: (i, 0)),
            scratch_shapes=[pltpu.VMEM((2, TM, D), F32), pltpu.SemaphoreType.DMA],
        ),
        out_shape=_hbm_out((T, D), F32),
        input_output_aliases={1: 0},
        compiler_params=_cparams(("arbitrary",), 16),
        name="moe_combine",
    )(pos, _hbm(x), mod, wcol, _hbm(y))


def _tile_map(cnt):
    cnt = cnt[:, 0]
    ids = jnp.arange(N_EXPERTS, dtype=I32)
    n_tiles = (cnt + TM - 1) // TM
    cum = jnp.cumsum(n_tiles)
    total = cum[-1]
    i = jnp.arange(MOE_TILES, dtype=I32)
    ii = jnp.minimum(i, total - 1)
    e = jnp.sum((ii[:, None] >= cum[None, :]).astype(I32), axis=1)
    onehot = (e[:, None] == ids[None, :]).astype(I32)
    pick = lambda v: jnp.sum(onehot * v[None, :], axis=1)
    j = ii - pick(cum - n_tiles)
    active = i < total
    n_valid = jnp.where(active, jnp.clip(pick(cnt) - j * TM, 0, TM), 0)
    first = (active & (j == 0)).astype(I32)
    nonempty = n_tiles > 0
    ordinal = jnp.cumsum(nonempty.astype(I32)) - 1
    later = jnp.where(nonempty[None, :] & (ids[None, :] > ids[:, None]), ids[None, :], N_EXPERTS)
    nxt = jnp.min(later, axis=1)
    nxt = jnp.where(nxt == N_EXPERTS, -1, nxt)
    as_i32 = lambda v: v.astype(I32)
    return tuple(map(as_i32, (e * CAP_TILES + j, e, n_valid, first, pick(ordinal), pick(nxt))))


def _moe(l, x, h2, ri, rf, cnt, mod, w_gate, w_up, w_down):
    pos = ri[0:2].T.reshape(2 * T)
    wcol = rf.T
    xs = _dispatch(pos, h2)
    y = _experts(l, _tile_map(cnt), xs, w_gate, w_up, w_down)
    return _combine(l, pos, x, mod, wcol, y)


def _rope_tables():
    t = jnp.arange(LAT_S)
    row = (t // GRID_W).astype(F32)
    col = (t % GRID_W).astype(F32)
    nf = HEAD_DIM // 4
    inv = ROPE_BASE ** (-jnp.arange(nf, dtype=F32) / nf)
    ar = row[:, None] * inv[None, :]
    ac = col[:, None] * inv[None, :]
    cos = jnp.concatenate([jnp.cos(ar), jnp.cos(ar), jnp.cos(ac), jnp.cos(ac)], axis=1)
    sin = jnp.concatenate([-jnp.sin(ar), jnp.sin(ar), -jnp.sin(ac), jnp.sin(ac)], axis=1)
    reps = QK_W // HEAD_DIM
    return jnp.tile(cos, (1, reps)), jnp.tile(sin, (1, reps))


def kernel(x_prompt, x_sample, cache_k, cache_v, c, c_ctx, norm1_g, norm2_g, w_ada, b_ada, w_in, w_out,
           q_norm_g, k_norm_g, lam_q1, lam_k1, lam_q2, lam_k2, subln_g, sg_w, sg_b, pool_w, pool_scale,
           router_w, router_b, w_gate, w_up, w_down):
    x = jnp.concatenate([x_prompt.reshape(T_CTX, D), x_sample.reshape(T_LAT, D)], axis=0)
    cond = jnp.concatenate([c_ctx[None, :], c], axis=0)
    mod = _modulation(cond, w_ada, b_ada)

    seg_id = np.arange(QK_W) // HEAD_DIM
    seg = jnp.asarray((seg_id[:, None] == seg_id[None, :]) / HEAD_DIM, dtype=BF16)
    tri = jnp.asarray(np.arange(TM)[:, None] < np.arange(TM)[None, :], dtype=BF16)
    rope_tabs = _rope_tables()
    rwt = router_w.T
    rbb = jnp.broadcast_to(router_b[:, None], (N_EXPERTS, TM))
    state_k = jnp.zeros((1,), F32)
    state_v = jnp.zeros((1,), F32)

    for l in range(DEPTH):
        if l % 2 == 0:
            e = l // 2
            qg = jnp.tile(q_norm_g[e], QK_W // HEAD_DIM)[None, :]
            kg = jnp.tile(k_norm_g[e], QK_W // HEAD_DIM)[None, :]
            sgb = jnp.broadcast_to(sg_b[:, :, :, None], (DEPTH // 2, SG_GROUPS, SG_CHUNK, 128))
            common = (l, x, mod, norm1_g, w_in, qg, kg, seg, sg_w, sgb)
            if e == 0:
                state_k = jnp.zeros((N_CTX_B, DEPTH // 2, CTX_S, QK_W), F32)
                state_v = jnp.zeros((N_CTX_B, DEPTH // 2, CTX_S, MIX_A), F32)
            q_c, k_c, v_c, sg_c, state_k, state_v = _front(False, *common, None, (state_k, state_v))
            q_l, k_l, v_l, sg_l = _front(True, *common, rope_tabs, None)
            lam = (lam_q1, lam_k1, lam_q2, lam_k2, subln_g)
            att_c = _attention(False, l, q_c, k_c, v_c, None, None, *lam)
            att_l = _attention(True, l, q_l, k_l, v_l, cache_k, cache_v, *lam)
            x, h2, ri, rf, cnt = _out_proj(l, x, mod, att_c, att_l, sg_c, sg_l, w_out, norm2_g, rwt, rbb, tri)
        else:
            x, h2, ri, rf, cnt = _pool_mixer(l, x, mod, norm1_g, pool_w, pool_scale, norm2_g, rwt, rbb, tri)
        x = _moe(l, x, h2, ri, rf, cnt, mod, w_gate, w_up, w_down)

    y_prompt = x[:T_CTX].reshape(N_CTX_B, CTX_S, D)
    y_sample = x[T_CTX:].reshape(N_LAT_B, LAT_S, D)
    state_k = state_k.reshape(N_CTX_B, DEPTH // 2, CTX_S, N_HEADS, 2, HEAD_DIM)
    state_v = state_v.reshape(N_CTX_B, DEPTH // 2, CTX_S, N_HEADS, VAL_DIM)
    return y_prompt, y_sample, state_k, state_v
```

```python
import functools
import math

import jax
import jax.numpy as jnp
import numpy as np
from jax import lax
from jax.experimental import pallas as pl
from jax.experimental.pallas import tpu as pltpu

F32 = jnp.float32
BF16 = jnp.bfloat16
I32 = jnp.int32

D = 1024
N_CTX_B, CTX_S = 16, 256
N_LAT_B, LAT_S = 2, 1024
PAST = 512
DEPTH = 4
T_CTX = N_CTX_B * CTX_S
T_LAT = N_LAT_B * LAT_S
T = T_CTX + T_LAT
GRID_W = 64
N_HEADS = 4
HEAD_DIM = 64
VAL_DIM = 128
QK_W = 512
MIX_A = 512
MIX_B = 512
IN_W = 2560
SG_GROUPS = 4
SG_CHUNK = 128
POOL_WINDOWS = (2, 4, 8, 16)
POOL_DIM = 256
POOL_HALO = 8
N_EXPERTS = 16
N_GROUPS = 4
PER_GROUP = 4
D_FF = 512
N_MOD = 6
N_COND = 1 + N_LAT_B
EPS = 1e-6
ROPE_BASE = 10000.0

TM = 256
N_TILES = T // TM
N_CTX_TILES = T_CTX // TM
LAT_TILES_PER_SEQ = LAT_S // TM
ROW_CHUNK = 16
PAD_PER_RUN = ROW_CHUNK - 1
SORTED_ROWS = -(-(2 * TM + N_EXPERTS * PAD_PER_RUN) // 128) * 128
CAP_TILES = -(-(T + N_TILES * PAD_PER_RUN) // TM)
EXPERT_CAP = CAP_TILES * TM
MOE_TILES = (2 * T + N_EXPERTS * N_TILES * PAD_PER_RUN) // TM + N_EXPERTS
MOD_TN = 1024
MIB = 1024 * 1024

NT_DIMS = (((1,), (1,)), ((), ()))


def _cparams(sem, vmem_mib):
    return pltpu.CompilerParams(dimension_semantics=sem, vmem_limit_bytes=vmem_mib * MIB)


def _hbm(x):
    try:
        return pltpu.with_memory_space_constraint(x, pltpu.HBM)
    except ValueError:
        return x


def _hbm_out(shape, dtype):
    return pltpu.HBM(shape, dtype)


def _sigmoid(x):
    return 1.0 / (1.0 + jnp.exp(-x))


def _gelu_tanh(x):
    c = math.sqrt(2.0 / math.pi)
    return x * (0.5 * (1.0 + jnp.tanh(c * (x + 0.044715 * (x * x * x)))))


def _modnorm(x, g, scale, shift):
    ms = jnp.mean(x * x, axis=-1, keepdims=True)
    return (x * lax.rsqrt(ms + EPS) * g) * (1.0 + scale) + shift


def _cond_of_tile(i):
    return jnp.where(i < N_CTX_TILES, 0, 1 + (i - N_CTX_TILES) // LAT_TILES_PER_SEQ)


def _mod_kernel(cb_ref, w_ref, b_ref, o_ref, sb_ref):
    @pl.when((pl.program_id(0) == 0) & (pl.program_id(1) == 0))
    def _():
        cb = cb_ref[...]
        sb_ref[...] = cb * _sigmoid(cb)

    for j in range(N_COND):
        cols = []
        for n0 in range(0, MOD_TN, 128):
            cols.append(jnp.sum(w_ref[0, :, n0:n0 + 128] * sb_ref[j], axis=0, keepdims=True))
        o_ref[0, j:j + 1, :] = jnp.concatenate(cols, axis=1) + b_ref[0]


def _modulation(cond, w_ada, b_ada):
    cb = jnp.broadcast_to(cond[:, :, None], (N_COND, D, 128))
    out = pl.pallas_call(
        _mod_kernel,
        grid=(DEPTH, N_MOD * D // MOD_TN),
        in_specs=[
            pl.BlockSpec((N_COND, D, 128), lambda l, n: (0, 0, 0)),
            pl.BlockSpec((1, D, MOD_TN), lambda l, n: (l, 0, n)),
            pl.BlockSpec((1, 1, MOD_TN), lambda l, n: (l, 0, n)),
        ],
        out_specs=pl.BlockSpec((1, N_COND, MOD_TN), lambda l, n: (l, 0, n)),
        out_shape=jax.ShapeDtypeStruct((DEPTH, N_COND, N_MOD * D), F32),
        scratch_shapes=[pltpu.VMEM((N_COND, D, 128), F32)],
        compiler_params=_cparams(("arbitrary", "arbitrary"), 24),
        name="adaln_mod",
    )(cb, w_ada, b_ada.reshape(DEPTH, 1, N_MOD * D))
    return out.reshape(DEPTH, N_COND, N_MOD, D)


def _front_kernel(latent, n_state_in, *refs):
    (x_ref, m_ref, n1_ref, w_ref, qg_ref, kg_ref, seg_ref, sgw_ref, sgb_ref), refs = refs[:9], refs[9:]
    if latent:
        cos_ref, sin_ref, q_o, k_o, v_o, sg_o, wbf = refs
    else:
        q_o, k_o, v_o, sg_o, sk_o, sv_o, wbf = refs[n_state_in:]

    @pl.when(pl.program_id(0) == 0)
    def _():
        for c0 in range(0, IN_W, 512):
            wbf[:, c0:c0 + 512] = w_ref[0, :, c0:c0 + 512].astype(BF16)

    h = _modnorm(x_ref[...], n1_ref[0], m_ref[0, 0, 1:2, :], m_ref[0, 0, 0:1, :])
    proj = jnp.dot(h.astype(BF16), wbf[...], preferred_element_type=F32)

    def qk_norm(z, g):
        ms = jnp.dot((z * z).astype(BF16), seg_ref[...], preferred_element_type=F32)
        return z * lax.rsqrt(ms + EPS) * g

    q = qk_norm(proj[:, 0:QK_W], qg_ref[...])
    k = qk_norm(proj[:, QK_W:2 * QK_W], kg_ref[...])
    v = proj[:, 2 * QK_W:2 * QK_W + MIX_A]

    if latent:
        lane = lax.broadcasted_iota(I32, (1, QK_W), 1)
        first = (lane % 32) < 16

        def rope(z):
            partner = jnp.where(first, pltpu.roll(z, QK_W - 16, 1), pltpu.roll(z, 16, 1))
            return z * cos_ref[...] + partner * sin_ref[...]

        q = rope(q)
        k = rope(k)
    else:
        sk_o[0, 0] = k
        sv_o[0, 0] = v

    q_o[...] = (q * (HEAD_DIM ** -0.5)).astype(BF16)
    k_o[...] = k.astype(BF16)
    v_o[...] = v.astype(BF16)

    gu0 = 2 * QK_W + MIX_A
    gv0 = gu0 + MIX_B
    for g in range(SG_GROUPS):
        wg = sgw_ref[0, g].astype(BF16)
        for n in range(TM // SG_CHUNK):
            rows = slice(n * SG_CHUNK, (n + 1) * SG_CHUNK)
            cu = slice(gu0 + g * 128, gu0 + (g + 1) * 128)
            cv = slice(gv0 + g * 128, gv0 + (g + 1) * 128)
            gv = _gelu_tanh(proj[rows, cv]).astype(BF16)
            mixed = jnp.dot(wg, gv, preferred_element_type=F32) + sgb_ref[0, g]
            sg_o[rows, g * 128:(g + 1) * 128] = (_gelu_tanh(proj[rows, cu]) * mixed).astype(BF16)


def _front(latent, l, x, mod, norm1_g, w_in, qg, kg, seg, sg_w, sgb, rope_tabs, states):
    e = l // 2
    n_tiles = (T_LAT if latent else T_CTX) // TM
    off = N_CTX_TILES if latent else 0
    rows = T_LAT if latent else T_CTX
    if latent:
        cond_map = lambda i: (l, 1 + i // LAT_TILES_PER_SEQ, 0, 0)
    else:
        cond_map = lambda i: (l, 0, 0, 0)
    in_specs = [
        pl.BlockSpec((TM, D), lambda i: (i + off, 0)),
        pl.BlockSpec((1, 1, N_MOD, D), cond_map),
        pl.BlockSpec((1, 1, D), lambda i: (l, 0, 0)),
        pl.BlockSpec((1, D, IN_W), lambda i: (e, 0, 0)),
        pl.BlockSpec((1, QK_W), lambda i: (0, 0)),
        pl.BlockSpec((1, QK_W), lambda i: (0, 0)),
        pl.BlockSpec((QK_W, QK_W), lambda i: (0, 0)),
        pl.BlockSpec((1, SG_GROUPS, SG_CHUNK, SG_CHUNK), lambda i: (e, 0, 0, 0)),
        pl.BlockSpec((1, SG_GROUPS, SG_CHUNK, 128), lambda i: (e, 0, 0, 0)),
    ]
    args = [_hbm(x), mod, norm1_g.reshape(DEPTH, 1, D), w_in, qg, kg, seg, sg_w, sgb]
    tok = lambda: pl.BlockSpec((TM, QK_W), lambda i: (i, 0))
    out_specs = [tok(), tok(), tok(), tok()]
    out_shape = [_hbm_out((rows, QK_W), BF16)] * 4
    aliases = {}
    if latent:
        cos_t, sin_t = rope_tabs
        in_specs += [pl.BlockSpec((TM, QK_W), lambda i: (i % LAT_TILES_PER_SEQ, 0))] * 2
        args += [cos_t, sin_t]
    else:
        if states:
            in_specs += [pl.BlockSpec(memory_space=pl.ANY)] * 2
            args += list(states)
            aliases = {9: 4, 10: 5}
        st = lambda: pl.BlockSpec((1, 1, CTX_S, QK_W), lambda i: (i, e, 0, 0))
        out_specs += [st(), st()]
        out_shape += [_hbm_out((N_CTX_B, DEPTH // 2, CTX_S, QK_W), F32)] * 2
    return pl.pallas_call(
        functools.partial(_front_kernel, latent, len(aliases)),
        grid=(n_tiles,),
        in_specs=in_specs,
        out_specs=out_specs,
        out_shape=out_shape,
        scratch_shapes=[pltpu.VMEM((D, IN_W), BF16)],
        input_output_aliases=aliases,
        compiler_params=_cparams(("arbitrary",), 48),
        name="front_lat" if latent else "front_ctx",
    )(*args)


def _attn_kernel(has_cache, heads, lam0, *refs):
    if has_cache:
        (q_ref, k_ref, v_ref, ck_ref, cv_ref, lq1, lk1, lq2, lk2, sub_ref, o_ref) = refs
    else:
        (q_ref, k_ref, v_ref, lq1, lk1, lq2, lk2, sub_ref, o_ref) = refs

    lane = lax.broadcasted_iota(I32, (1, VAL_DIM), 1)
    lam = (jnp.exp(jnp.sum(lq1[0] * lk1[0], axis=-1, keepdims=True))
           - jnp.exp(jnp.sum(lq2[0] * lk2[0], axis=-1, keepdims=True)) + lam0)

    for h in range(heads):
        hs = slice(h * VAL_DIM, (h + 1) * VAL_DIM)
        qf = q_ref[:, hs].astype(F32)
        qc = [jnp.where(lane < HEAD_DIM, qf, 0.0).astype(BF16),
              jnp.where(lane >= HEAD_DIM, qf, 0.0).astype(BF16)]
        keys = [k_ref[:, hs]]
        vals = [v_ref[:, hs]]
        if has_cache:
            keys.append(ck_ref[0, 0].astype(BF16))
            vals.append(cv_ref[0, 0].astype(BF16))

        probs = []
        for c in range(2):
            s = [lax.dot_general(qc[c], kk, NT_DIMS, preferred_element_type=F32) for kk in keys]
            m = s[0].max(axis=-1, keepdims=True)
            for sx in s[1:]:
                m = jnp.maximum(m, sx.max(axis=-1, keepdims=True))
            p = [jnp.exp(sx - m) for sx in s]
            den = p[0].sum(axis=-1, keepdims=True)
            for px in p[1:]:
                den = den + px.sum(axis=-1, keepdims=True)
            probs.append((p, den))

        r0 = 1.0 / probs[0][1]
        r1 = lam / probs[1][1]
        o = None
        for idx in range(len(keys)):
            a = (probs[0][0][idx] * r0 - probs[1][0][idx] * r1).astype(BF16)
            t = jnp.dot(a, vals[idx], preferred_element_type=F32)
            o = t if o is None else o + t
        ms = jnp.mean(o * o, axis=-1, keepdims=True)
        o_ref[:, hs] = ((o * lax.rsqrt(ms + EPS) * sub_ref[0]) * (1.0 - lam0)).astype(BF16)


def _attention(has_cache, l, q, k, v, cache_k, cache_v, lam_q1, lam_k1, lam_q2, lam_k2, subln_g):
    e = l // 2
    lam0 = 0.8 - 0.6 * math.exp(-0.3 * l)
    n_b, s_len = (N_LAT_B, LAT_S) if has_cache else (N_CTX_B, CTX_S)
    heads = 1 if has_cache else N_HEADS
    width = heads * VAL_DIM
    nq = s_len // TM
    kv = lambda: pl.BlockSpec((s_len, width), lambda b, h, qi: (b, h))
    par = lambda: pl.BlockSpec((1, 1, HEAD_DIM), lambda b, h, qi: (e, 0, 0))
    in_specs = [pl.BlockSpec((TM, width), lambda b, h, qi: (b * nq + qi, h)), kv(), kv()]
    args = [_hbm(q), _hbm(k), _hbm(v)]
    if has_cache:
        cs = lambda: pl.BlockSpec((1, 1, PAST, VAL_DIM), lambda b, h, qi: (b, e, 0, h))
        in_specs += [cs(), cs()]
        args += [cache_k.reshape(N_LAT_B, DEPTH // 2, PAST, QK_W),
                 cache_v.reshape(N_LAT_B, DEPTH // 2, PAST, MIX_A)]
    in_specs += [par(), par(), par(), par(),
                 pl.BlockSpec((1, 1, VAL_DIM), lambda b, h, qi: (e, 0, 0))]
    r3 = lambda a: a.reshape(DEPTH // 2, 1, -1)
    args += [r3(lam_q1), r3(lam_k1), r3(lam_q2), r3(lam_k2), r3(subln_g)]
    return pl.pallas_call(
        functools.partial(_attn_kernel, has_cache, heads, lam0),
        grid=(n_b, N_HEADS // heads, nq),
        in_specs=in_specs,
        out_specs=pl.BlockSpec((TM, width), lambda b, h, qi: (b * nq + qi, h)),
        out_shape=_hbm_out((n_b * s_len, MIX_A), BF16),
        compiler_params=_cparams(("arbitrary", "arbitrary", "arbitrary"), 32),
        name="attn_lat" if has_cache else "attn_ctx",
    )(*args)


def _route_tail(x_new, m_ref, n2_ref, rwt_ref, rbb_ref, tri_ref,
                h2_o, lp_o, rf_o, nch_o, loff_o, gb_o, cnt_o, carry):
    i = pl.program_id(0)

    @pl.when(i == 0)
    def _():
        carry[...] = jnp.zeros_like(carry)

    h2 = _modnorm(x_new, n2_ref[0], m_ref[0, 0, 4:5, :], m_ref[0, 0, 3:4, :])

    hh = h2.astype(BF16)
    h2_o[...] = hh
    hl = (h2 - hh.astype(F32)).astype(BF16)
    rw = rwt_ref[...]
    wh = rw.astype(BF16)
    wl = (rw - wh.astype(F32)).astype(BF16)
    dg = lambda a, b: lax.dot_general(a, b, NT_DIMS, preferred_element_type=F32)
    logits = dg(wh, hh) + dg(wh, hl) + dg(wl, hh)
    score = _sigmoid(logits)
    sel = score + rbb_ref[...]

    neg_inf = jnp.full((1, TM), -jnp.inf, F32)
    grp = []
    for g in range(N_GROUPS):
        s_rows = [sel[PER_GROUP * g + j:PER_GROUP * g + j + 1, :] for j in range(PER_GROUP)]
        c_rows = [score[PER_GROUP * g + j:PER_GROUP * g + j + 1, :] for j in range(PER_GROUP)]
        f_idx = jnp.zeros((1, TM), I32)
        f_val, f_sc = s_rows[0], c_rows[0]
        for j in range(1, PER_GROUP):
            better = s_rows[j] > f_val
            f_idx = jnp.where(better, j, f_idx)
            f_val = jnp.where(better, s_rows[j], f_val)
            f_sc = jnp.where(better, c_rows[j], f_sc)
        s_idx = jnp.zeros((1, TM), I32)
        s_val, s_sc = neg_inf, jnp.zeros((1, TM), F32)
        for j in range(PER_GROUP):
            better = (f_idx != j) & (s_rows[j] > s_val)
            s_idx = jnp.where(better, j, s_idx)
            s_val = jnp.where(better, s_rows[j], s_val)
            s_sc = jnp.where(better, c_rows[j], s_sc)
        grp.append((f_val + s_val, PER_GROUP * g + f_idx, PER_GROUP * g + s_idx, f_sc, s_sc))

    best, e0, e1, c0, c1 = grp[0]
    for g in range(1, N_GROUPS):
        better = grp[g][0] > best
        best = jnp.where(better, grp[g][0], best)
        e0 = jnp.where(better, grp[g][1], e0)
        e1 = jnp.where(better, grp[g][2], e1)
        c0 = jnp.where(better, grp[g][3], c0)
        c1 = jnp.where(better, grp[g][4], c1)
    den = c0 + c1
    w0 = c0 / den
    w1 = c1 / den

    sub = lax.broadcasted_iota(I32, (N_EXPERTS, TM), 0)
    hit0 = sub == jnp.broadcast_to(e0, (N_EXPERTS, TM))
    hit1 = sub == jnp.broadcast_to(e1, (N_EXPERTS, TM))
    onehot = jnp.where(hit0 | hit1, 1.0, 0.0)
    prefix = jnp.dot(onehot.astype(BF16), tri_ref[...], preferred_element_type=F32)
    count = jnp.sum(onehot, axis=1, keepdims=True)
    padded = jnp.floor((count + (ROW_CHUNK - 1)) * (1.0 / ROW_CHUNK)) * ROW_CHUNK

    sub1 = lax.broadcasted_iota(I32, (N_EXPERTS, 1), 0)
    run = jnp.zeros((1, 1), F32)
    local_off = jnp.zeros((N_EXPERTS, 1), F32)
    for e in range(N_EXPERTS):
        local_off = jnp.where(sub1 == e, run, local_off)
        run = run + padded[e:e + 1, :]
    place = prefix + local_off
    lpos0 = jnp.sum(jnp.where(hit0, place, 0.0), axis=0, keepdims=True).astype(I32)
    lpos1 = jnp.sum(jnp.where(hit1, place, 0.0), axis=0, keepdims=True).astype(I32)

    lanes = lambda col: jnp.broadcast_to(col, (N_EXPERTS, 128)).astype(I32)
    nch_o[0] = lanes(padded * (1.0 / ROW_CHUNK))
    loff_o[0] = lanes(local_off)
    gb_o[0] = carry[...].astype(I32)
    carry[...] = carry[...] + padded
    cnt_o[...] = carry[...].astype(I32)

    sub8 = lax.broadcasted_iota(I32, (8, TM), 0)
    bc = lambda r: jnp.broadcast_to(r, (8, TM))
    lp_o[...] = jnp.where(sub8 == 0, bc(lpos0), jnp.where(sub8 == 1, bc(lpos1), 0))
    rf_o[...] = jnp.where(sub8 == 0, bc(w0), jnp.where(sub8 == 1, bc(w1), 0.0))


def _tail_specs(l):
    in_specs = [
        pl.BlockSpec((1, 1, D), lambda i: (l, 0, 0)),
        pl.BlockSpec((N_EXPERTS, D), lambda i: (0, 0)),
        pl.BlockSpec((N_EXPERTS, TM), lambda i: (0, 0)),
        pl.BlockSpec((TM, TM), lambda i: (0, 0)),
    ]
    per_tile = lambda: pl.BlockSpec((1, N_EXPERTS, 128), lambda i: (i, 0, 0))
    out_specs = [
        pl.BlockSpec((TM, D), lambda i: (i, 0)),
        pl.BlockSpec((8, TM), lambda i: (0, i)),
        pl.BlockSpec((8, TM), lambda i: (0, i)),
        per_tile(), per_tile(), per_tile(),
        pl.BlockSpec((N_EXPERTS, 128), lambda i: (0, 0)),
    ]
    out_shape = [
        _hbm_out((T, D), BF16),
        jax.ShapeDtypeStruct((8, T), I32),
        jax.ShapeDtypeStruct((8, T), F32),
        jax.ShapeDtypeStruct((N_TILES, N_EXPERTS, 128), I32),
        jax.ShapeDtypeStruct((N_TILES, N_EXPERTS, 128), I32),
        jax.ShapeDtypeStruct((N_TILES, N_EXPERTS, 128), I32),
        jax.ShapeDtypeStruct((N_EXPERTS, 128), I32),
    ]
    scratch = [pltpu.VMEM((N_EXPERTS, 128), F32)]
    return in_specs, out_specs, out_shape, scratch


def _out_kernel(x_ref, m_ref, ac_ref, al_ref, sc_ref, sl_ref, w_ref,
                n2_ref, rwt_ref, rbb_ref, tri_ref,
                x_o, *tail_refs):
    *tail_outs, wbf, carry = tail_refs
    i = pl.program_id(0)

    @pl.when(i == 0)
    def _():
        wbf[...] = w_ref[0].astype(BF16)

    is_ctx = i < N_CTX_TILES
    att = jnp.where(is_ctx, ac_ref[...], al_ref[...])
    sgo = jnp.where(is_ctx, sc_ref[...], sl_ref[...])
    out = (jnp.dot(att, wbf[0:MIX_A, :], preferred_element_type=F32)
           + jnp.dot(sgo, wbf[MIX_A:MIX_A + MIX_B, :], preferred_element_type=F32))
    x_new = x_ref[...] + m_ref[0, 0, 2:3, :] * out
    x_o[...] = x_new
    _route_tail(x_new, m_ref, n2_ref, rwt_ref, rbb_ref, tri_ref, *tail_outs, carry)


def _out_proj(l, x, mod, att_c, att_l, sg_c, sg_l, w_out, norm2_g, rwt, rbb, tri):
    e = l // 2
    t_in, t_out, t_shape, t_scratch = _tail_specs(l)
    ctx = lambda: pl.BlockSpec((TM, MIX_A), lambda i: (jnp.minimum(i, N_CTX_TILES - 1), 0))
    lat = lambda: pl.BlockSpec((TM, MIX_A), lambda i: (jnp.maximum(i - N_CTX_TILES, 0), 0))
    in_specs = [
        pl.BlockSpec((TM, D), lambda i: (i, 0)),
        pl.BlockSpec((1, 1, N_MOD, D), lambda i: (l, _cond_of_tile(i), 0, 0)),
        ctx(), lat(), ctx(), lat(),
        pl.BlockSpec((1, D, D), lambda i: (e, 0, 0)),
    ] + t_in
    return pl.pallas_call(
        _out_kernel,
        grid=(N_TILES,),
        in_specs=in_specs,
        out_specs=[pl.BlockSpec((TM, D), lambda i: (i, 0))] + t_out,
        out_shape=[_hbm_out((T, D), F32)] + t_shape,
        scratch_shapes=[pltpu.VMEM((D, D), BF16)] + t_scratch,
        input_output_aliases={0: 0},
        compiler_params=_cparams(("arbitrary",), 40),
        name="out_proj_route",
    )(_hbm(x), mod, _hbm(att_c), _hbm(att_l), _hbm(sg_c), _hbm(sg_l), w_out,
      norm2_g.reshape(DEPTH, 1, D), rwt, rbb, tri)


def _pool_kernel(x_ref, xp_ref, xn_ref, m_ref, n1_ref, pw_ref, ps_ref,
                 n2_ref, rwt_ref, rbb_ref, tri_ref,
                 x_o, *tail_refs):
    *tail_outs, carry = tail_refs
    i = pl.program_id(0)
    is_lat = i >= N_CTX_TILES
    ti = jnp.where(is_lat, (i - N_CTX_TILES) % LAT_TILES_PER_SEQ, 0)
    last_ti = jnp.where(is_lat, LAT_TILES_PER_SEQ - 1, 0)
    seq_len = jnp.where(is_lat, LAT_S, CTX_S)

    g1n, sc1, sh1 = n1_ref[0], m_ref[0, 0, 1:2, :], m_ref[0, 0, 0:1, :]
    x = x_ref[...]
    h = _modnorm(x, g1n, sc1, sh1)
    hp = jnp.where(ti == 0, 0.0, _modnorm(xp_ref[...], g1n, sc1, sh1))
    hn = jnp.where(ti == last_ti, 0.0, _modnorm(xn_ref[...], g1n, sc1, sh1))
    hcat = jnp.concatenate([hp, h, hn], axis=0)
    n_rows = TM + 2 * POOL_HALO

    t_pos = ti * TM + lax.broadcasted_iota(I32, (TM, 1), 0)
    outs = []
    for g, w in enumerate(POOL_WINDOWS):
        cols = slice(g * POOL_DIM, (g + 1) * POOL_DIM)
        acc = hcat[:, cols]
        acc = acc + pltpu.roll(acc, 1, 0)
        step = 1
        while 2 * step < w:
            acc = pltpu.roll(acc, step, 0) + pltpu.roll(acc, n_rows - step, 0)
            step *= 2
        lo = jnp.maximum(t_pos - w // 2, 0)
        hi = jnp.minimum(t_pos - w // 2 + w, seq_len)
        mean = acc[POOL_HALO:POOL_HALO + TM, :] / (hi - lo).astype(F32)
        dlt = (mean - h[:, cols]).astype(BF16)
        outs.append(jnp.dot(dlt, pw_ref[0, g].astype(BF16), preferred_element_type=F32))
    out = jnp.concatenate(outs, axis=1) * ps_ref[0]
    x_new = x + m_ref[0, 0, 2:3, :] * out
    x_o[...] = x_new
    _route_tail(x_new, m_ref, n2_ref, rwt_ref, rbb_ref, tri_ref, *tail_outs, carry)


def _pool_mixer(l, x, mod, norm1_g, pool_w, pool_scale, norm2_g, rwt, rbb, tri):
    o = l // 2
    t_in, t_out, t_shape, t_scratch = _tail_specs(l)
    halo_blocks = TM // POOL_HALO
    in_specs = [
        pl.BlockSpec((TM, D), lambda i: (i, 0)),
        pl.BlockSpec((POOL_HALO, D), lambda i: (jnp.maximum(i * halo_blocks - 1, 0), 0)),
        pl.BlockSpec((POOL_HALO, D), lambda i: (jnp.minimum((i + 1) * halo_blocks, T // POOL_HALO - 1), 0)),
        pl.BlockSpec((1, 1, N_MOD, D), lambda i: (l, _cond_of_tile(i), 0, 0)),
        pl.BlockSpec((1, 1, D), lambda i: (l, 0, 0)),
        pl.BlockSpec((1, len(POOL_WINDOWS), POOL_DIM, POOL_DIM), lambda i: (o, 0, 0, 0)),
        pl.BlockSpec((1, 1, D), lambda i: (o, 0, 0)),
    ] + t_in
    return pl.pallas_call(
        _pool_kernel,
        grid=(N_TILES,),
        in_specs=in_specs,
        out_specs=[pl.BlockSpec((TM, D), lambda i: (i, 0))] + t_out,
        out_shape=[_hbm_out((T, D), F32)] + t_shape,
        scratch_shapes=t_scratch,
        compiler_params=_cparams(("arbitrary",), 32),
        name="pool_route",
    )(_hbm(x), _hbm(x), _hbm(x), mod, norm1_g.reshape(DEPTH, 1, D), pool_w,
      pool_scale.reshape(DEPTH // 2, 1, D),
      norm2_g.reshape(DEPTH, 1, D), rwt, rbb, tri)


def _run_copies(i, nch_ref, loff_ref, gb_ref, make_copy):
    total = 0
    for e in range(N_EXPERTS):
        n = nch_ref[i * N_EXPERTS + e]
        lo = loff_ref[i * N_EXPERTS + e]
        gb = gb_ref[i * N_EXPERTS + e] + e * EXPERT_CAP

        def issue(c, carry, lo=lo, gb=gb):
            make_copy(pl.multiple_of(lo + c * ROW_CHUNK, ROW_CHUNK),
                      pl.multiple_of(gb + c * ROW_CHUNK, ROW_CHUNK)).start()
            return carry

        lax.fori_loop(0, n, issue, 0)
        total = total + n

    def drain(c, carry):
        make_copy(0, 0).wait()
        return carry

    lax.fori_loop(0, total, drain, 0)


def _dispatch_kernel(nch_ref, loff_ref, gb_ref, h_ref, lp_ref, xs_ref, sbuf, sem):
    r = lax.broadcasted_iota(I32, (SORTED_ROWS, TM), 0)
    hit = (r == lp_ref[0:1, :]) | (r == lp_ref[1:2, :])
    perm = jnp.where(hit, 1.0, 0.0).astype(BF16)
    sbuf[...] = jnp.dot(perm, h_ref[...], preferred_element_type=F32).astype(BF16)

    def make_copy(local_row, global_row):
        return pltpu.make_async_copy(sbuf.at[pl.ds(local_row, ROW_CHUNK)],
                                     xs_ref.at[pl.ds(global_row, ROW_CHUNK)], sem)

    _run_copies(pl.program_id(0), nch_ref, loff_ref, gb_ref, make_copy)


def _dispatch(layout, h2, lp):
    return pl.pallas_call(
        _dispatch_kernel,
        grid_spec=pltpu.PrefetchScalarGridSpec(
            num_scalar_prefetch=3,
            grid=(N_TILES,),
            in_specs=[pl.BlockSpec((TM, D), lambda i, *_: (i, 0)),
                      pl.BlockSpec((8, TM), lambda i, *_: (0, i))],
            out_specs=pl.BlockSpec(memory_space=pl.ANY),
            scratch_shapes=[pltpu.VMEM((SORTED_ROWS, D), BF16), pltpu.SemaphoreType.DMA],
        ),
        out_shape=_hbm_out((N_EXPERTS * EXPERT_CAP, D), BF16),
        compiler_params=_cparams(("arbitrary",), 16),
        name="moe_dispatch",
    )(*layout, _hbm(h2), lp)


def _moe_kernel(l, rb_ref, e_ref, nv_ref, first_ref, ord_ref, nxt_ref, xs_ref, wg_hbm, wu_hbm, wd_hbm, y_ref,
                wgf, wuf, wdf, wgb, wub, wdb, sems):
    i = pl.program_id(0)
    n_valid = nv_ref[i]

    def weight_copies(e, slot):
        return (pltpu.make_async_copy(wg_hbm.at[l, e], wgf.at[slot], sems.at[slot, 0]),
                pltpu.make_async_copy(wu_hbm.at[l, e], wuf.at[slot], sems.at[slot, 1]),
                pltpu.make_async_copy(wd_hbm.at[l, e], wdf.at[slot], sems.at[slot, 2]))

    @pl.when(first_ref[i] == 1)
    def _():
        slot = ord_ref[i] % 2

        @pl.when(ord_ref[i] == 0)
        def _():
            for cp in weight_copies(e_ref[i], slot):
                cp.start()

        for cp in weight_copies(e_ref[i], slot):
            cp.wait()

        @pl.when(nxt_ref[i] >= 0)
        def _():
            for cp in weight_copies(nxt_ref[i], 1 - slot):
                cp.start()

        wgb[...] = wgf[slot].astype(BF16)
        wub[...] = wuf[slot].astype(BF16)
        wdb[...] = wdf[slot].astype(BF16)

    @pl.when(n_valid > 0)
    def _():
        row = lax.broadcasted_iota(I32, (TM, 1), 0)
        x = jnp.where(row < n_valid, xs_ref[...], jnp.zeros((), BF16))
        gate = jnp.dot(x, wgb[...], preferred_element_type=F32)
        up = jnp.dot(x, wub[...], preferred_element_type=F32)
        act = (gate * _sigmoid(gate) * up).astype(BF16)
        y_ref[...] = jnp.dot(act, wdb[...], preferred_element_type=F32).astype(BF16)


def _experts(l, tile_map, xs, w_gate, w_up, w_down):
    n_pre = len(tile_map)
    blk = lambda i, rb, *_: (rb[i], 0)
    return pl.pallas_call(
        functools.partial(_moe_kernel, l),
        grid_spec=pltpu.PrefetchScalarGridSpec(
            num_scalar_prefetch=n_pre,
            grid=(MOE_TILES,),
            in_specs=[
                pl.BlockSpec((TM, D), blk),
                pl.BlockSpec(memory_space=pl.ANY),
                pl.BlockSpec(memory_space=pl.ANY),
                pl.BlockSpec(memory_space=pl.ANY),
            ],
            out_specs=pl.BlockSpec((TM, D), blk),
            scratch_shapes=[pltpu.VMEM((2, D, D_FF), F32), pltpu.VMEM((2, D, D_FF), F32),
                            pltpu.VMEM((2, D_FF, D), F32),
                            pltpu.VMEM((D, D_FF), BF16), pltpu.VMEM((D, D_FF), BF16),
                            pltpu.VMEM((D_FF, D), BF16),
                            pltpu.SemaphoreType.DMA((2, 3))],
        ),
        out_shape=_hbm_out((N_EXPERTS * EXPERT_CAP, D), BF16),
        compiler_params=_cparams(("arbitrary",), 40),
        name="moe_experts",
    )(*tile_map, _hbm(xs), w_gate, w_up, w_down)


def _combine_kernel(nch_ref, loff_ref, gb_ref, x_ref, m_ref, lpt_ref, w_ref, y_ref, o_ref, ybuf, sem):
    i = pl.program_id(0)

    @pl.when(i == 0)
    def _():
        ybuf[...] = jnp.zeros_like(ybuf)

    def make_copy(local_row, global_row):
        return pltpu.make_async_copy(y_ref.at[pl.ds(global_row, ROW_CHUNK)],
                                     ybuf.at[pl.ds(local_row, ROW_CHUNK)], sem)

    _run_copies(i, nch_ref, loff_ref, gb_ref, make_copy)

    col = lax.broadcasted_iota(I32, (TM, SORTED_ROWS), 1)
    pick0 = jnp.where(col == lpt_ref[:, 0:1], 1.0, 0.0).astype(BF16)
    pick1 = jnp.where(col == lpt_ref[:, 1:2], 1.0, 0.0).astype(BF16)
    y0 = jnp.dot(pick0, ybuf[...], preferred_element_type=F32)
    y1 = jnp.dot(pick1, ybuf[...], preferred_element_type=F32)
    moe = w_ref[:, 0:1] * y0 + w_ref[:, 1:2] * y1
    o_ref[...] = x_ref[...] + m_ref[0, 0, 5:6, :] * moe


def _combine(l, layout, x, mod, lpt, wcol, y):
    return pl.pallas_call(
        _combine_kernel,
        grid_spec=pltpu.PrefetchScalarGridSpec(
            num_scalar_prefetch=3,
            grid=(N_TILES,),
            in_specs=[
                pl.BlockSpec((TM, D), lambda i, *_: (i, 0)),
                pl.BlockSpec((1, 1, N_MOD, D), lambda i, *_: (l, _cond_of_tile(i), 0, 0)),
                pl.BlockSpec((TM, 8), lambda i, *_: (i, 0)),
                pl.BlockSpec((TM, 8), lambda i, *_: (i, 0)),
                pl.BlockSpec(memory_space=pl.ANY),
            ],
            out_specs=pl.BlockSpec((TM, D), lambda i, *_: (i, 0)),
            scratch_shapes=[pltpu.VMEM((SORTED_ROWS, D), BF16), pltpu.SemaphoreType.DMA],
        ),
        out_shape=_hbm_out((T, D), F32),
        input_output_aliases={3: 0},
        compiler_params=_cparams(("arbitrary",), 24),
        name="moe_combine",
    )(*layout, _hbm(x), mod, lpt, wcol, _hbm(y))


def _tile_map(cnt):
    cnt = cnt[:, 0]
    ids = jnp.arange(N_EXPERTS, dtype=I32)
    n_tiles = (cnt + TM - 1) // TM
    cum = jnp.cumsum(n_tiles)
    total = cum[-1]
    i = jnp.arange(MOE_TILES, dtype=I32)
    ii = jnp.minimum(i, total - 1)
    e = jnp.sum((ii[:, None] >= cum[None, :]).astype(I32), axis=1)
    onehot = (e[:, None] == ids[None, :]).astype(I32)
    pick = lambda v: jnp.sum(onehot * v[None, :], axis=1)
    j = ii - pick(cum - n_tiles)
    active = i < total
    n_valid = jnp.where(active, jnp.clip(pick(cnt) - j * TM, 0, TM), 0)
    first = (active & (j == 0)).astype(I32)
    nonempty = n_tiles > 0
    ordinal = jnp.cumsum(nonempty.astype(I32)) - 1
    later = jnp.where(nonempty[None, :] & (ids[None, :] > ids[:, None]), ids[None, :], N_EXPERTS)
    nxt = jnp.min(later, axis=1)
    nxt = jnp.where(nxt == N_EXPERTS, -1, nxt)
    as_i32 = lambda v: v.astype(I32)
    return tuple(map(as_i32, (e * CAP_TILES + j, e, n_valid, first, pick(ordinal), pick(nxt))))


def _moe(l, x, tail, mod, w_gate, w_up, w_down):
    h2, lp, rf, nch, loff, gb, cnt = tail
    layout = tuple(a[:, :, 0].reshape(N_TILES * N_EXPERTS) for a in (nch, loff, gb))
    xs = _dispatch(layout, h2, lp)
    y = _experts(l, _tile_map(cnt), xs, w_gate, w_up, w_down)
    return _combine(l, layout, x, mod, lp.T, rf.T, y)


def _rope_tables():
    t = jnp.arange(LAT_S)
    row = (t // GRID_W).astype(F32)
    col = (t % GRID_W).astype(F32)
    nf = HEAD_DIM // 4
    inv = ROPE_BASE ** (-jnp.arange(nf, dtype=F32) / nf)
    ar = row[:, None] * inv[None, :]
    ac = col[:, None] * inv[None, :]
    cos = jnp.concatenate([jnp.cos(ar), jnp.cos(ar), jnp.cos(ac), jnp.cos(ac)], axis=1)
    sin = jnp.concatenate([-jnp.sin(ar), jnp.sin(ar), -jnp.sin(ac), jnp.sin(ac)], axis=1)
    reps = QK_W // HEAD_DIM
    return jnp.tile(cos, (1, reps)), jnp.tile(sin, (1, reps))


def kernel(x_prompt, x_sample, cache_k, cache_v, c, c_ctx, norm1_g, norm2_g, w_ada, b_ada, w_in, w_out,
           q_norm_g, k_norm_g, lam_q1, lam_k1, lam_q2, lam_k2, subln_g, sg_w, sg_b, pool_w, pool_scale,
           router_w, router_b, w_gate, w_up, w_down):
    x = jnp.concatenate([x_prompt.reshape(T_CTX, D), x_sample.reshape(T_LAT, D)], axis=0)
    cond = jnp.concatenate([c_ctx[None, :], c], axis=0)
    mod = _modulation(cond, w_ada, b_ada)

    seg_id = np.arange(QK_W) // HEAD_DIM
    seg = jnp.asarray((seg_id[:, None] == seg_id[None, :]) / HEAD_DIM, dtype=BF16)
    tri = jnp.asarray(np.arange(TM)[:, None] < np.arange(TM)[None, :], dtype=BF16)
    rope_tabs = _rope_tables()
    rwt = router_w.T
    rbb = jnp.broadcast_to(router_b[:, None], (N_EXPERTS, TM))
    states = None

    for l in range(DEPTH):
        if l % 2 == 0:
            e = l // 2
            qg = jnp.tile(q_norm_g[e], QK_W // HEAD_DIM)[None, :]
            kg = jnp.tile(k_norm_g[e], QK_W // HEAD_DIM)[None, :]
            sgb = jnp.broadcast_to(sg_b[:, :, :, None], (DEPTH // 2, SG_GROUPS, SG_CHUNK, 128))
            common = (l, x, mod, norm1_g, w_in, qg, kg, seg, sg_w, sgb)
            q_c, k_c, v_c, sg_c, *states = _front(False, *common, None, states)
            q_l, k_l, v_l, sg_l = _front(True, *common, rope_tabs, None)
            lam = (lam_q1, lam_k1, lam_q2, lam_k2, subln_g)
            att_c = _attention(False, l, q_c, k_c, v_c, None, None, *lam)
            att_l = _attention(True, l, q_l, k_l, v_l, cache_k, cache_v, *lam)
            x, *tail = _out_proj(l, x, mod, att_c, att_l, sg_c, sg_l, w_out, norm2_g, rwt, rbb, tri)
        else:
            x, *tail = _pool_mixer(l, x, mod, norm1_g, pool_w, pool_scale, norm2_g, rwt, rbb, tri)
        x = _moe(l, x, tail, mod, w_gate, w_up, w_down)
    state_k, state_v = states

    y_prompt = x[:T_CTX].reshape(N_CTX_B, CTX_S, D)
    y_sample = x[T_CTX:].reshape(N_LAT_B, LAT_S, D)
    state_k = state_k.reshape(N_CTX_B, DEPTH // 2, CTX_S, N_HEADS, 2, HEAD_DIM)
    state_v = state_v.reshape(N_CTX_B, DEPTH // 2, CTX_S, N_HEADS, VAL_DIM)
    return y_prompt, y_sample, state_k, state_v
```

```python
import functools
import math

import jax
import jax.numpy as jnp
import numpy as np
from jax import lax
from jax.experimental import pallas as pl
from jax.experimental.pallas import tpu as pltpu

F32 = jnp.float32
BF16 = jnp.bfloat16
I32 = jnp.int32

D = 1024
N_CTX_B, CTX_S = 16, 256
N_LAT_B, LAT_S = 2, 1024
PAST = 512
DEPTH = 4
T_CTX = N_CTX_B * CTX_S
T_LAT = N_LAT_B * LAT_S
T = T_CTX + T_LAT
GRID_W = 64
N_HEADS = 4
HEAD_DIM = 64
VAL_DIM = 128
QK_W = 512
MIX_A = 512
MIX_B = 512
IN_W = 2560
SG_GROUPS = 4
SG_CHUNK = 128
POOL_WINDOWS = (2, 4, 8, 16)
POOL_DIM = 256
POOL_HALO = 8
N_EXPERTS = 16
N_GROUPS = 4
PER_GROUP = 4
D_FF = 512
N_MOD = 6
N_COND = 1 + N_LAT_B
EPS = 1e-6
ROPE_BASE = 10000.0

TM = 256
N_TILES = T // TM
N_CTX_TILES = T_CTX // TM
LAT_TILES_PER_SEQ = LAT_S // TM
ROW_CHUNK = 16
PAD_PER_RUN = ROW_CHUNK - 1
SORTED_ROWS = -(-(2 * TM + N_EXPERTS * PAD_PER_RUN) // 128) * 128
TE = 512
TF = 512
TO = 512
CAP_TILES = -(-(T + N_TILES * PAD_PER_RUN) // TE)
EXPERT_CAP = CAP_TILES * TE
MOE_TILES = (2 * T + N_EXPERTS * N_TILES * PAD_PER_RUN) // TE + N_EXPERTS
MOD_TN = 1024
MIB = 1024 * 1024

NT_DIMS = (((1,), (1,)), ((), ()))


def _cparams(sem, vmem_mib):
    return pltpu.CompilerParams(dimension_semantics=sem, vmem_limit_bytes=vmem_mib * MIB)


def _hbm(x):
    try:
        return pltpu.with_memory_space_constraint(x, pltpu.HBM)
    except ValueError:
        return x


def _hbm_out(shape, dtype):
    return pltpu.HBM(shape, dtype)


def _sigmoid(x):
    return 1.0 / (1.0 + jnp.exp(-x))


def _gelu_tanh(x):
    c = math.sqrt(2.0 / math.pi)
    return x * (0.5 * (1.0 + jnp.tanh(c * (x + 0.044715 * (x * x * x)))))


def _modnorm(x, g, scale, shift):
    ms = jnp.mean(x * x, axis=-1, keepdims=True)
    return (x * lax.rsqrt(ms + EPS) * g) * (1.0 + scale) + shift


def _cond_of_tile(i):
    return jnp.where(i < N_CTX_TILES, 0, 1 + (i - N_CTX_TILES) // LAT_TILES_PER_SEQ)


def _mod_kernel(cb_ref, w_ref, b_ref, o_ref, sb_ref):
    @pl.when((pl.program_id(0) == 0) & (pl.program_id(1) == 0))
    def _():
        cb = cb_ref[...]
        sb_ref[...] = cb * _sigmoid(cb)

    for j in range(N_COND):
        cols = []
        for n0 in range(0, MOD_TN, 128):
            cols.append(jnp.sum(w_ref[0, :, n0:n0 + 128] * sb_ref[j], axis=0, keepdims=True))
        o_ref[0, j:j + 1, :] = jnp.concatenate(cols, axis=1) + b_ref[0]


def _modulation(cond, w_ada, b_ada):
    cb = jnp.broadcast_to(cond[:, :, None], (N_COND, D, 128))
    out = pl.pallas_call(
        _mod_kernel,
        grid=(DEPTH, N_MOD * D // MOD_TN),
        in_specs=[
            pl.BlockSpec((N_COND, D, 128), lambda l, n: (0, 0, 0)),
            pl.BlockSpec((1, D, MOD_TN), lambda l, n: (l, 0, n)),
            pl.BlockSpec((1, 1, MOD_TN), lambda l, n: (l, 0, n)),
        ],
        out_specs=pl.BlockSpec((1, N_COND, MOD_TN), lambda l, n: (l, 0, n)),
        out_shape=jax.ShapeDtypeStruct((DEPTH, N_COND, N_MOD * D), F32),
        scratch_shapes=[pltpu.VMEM((N_COND, D, 128), F32)],
        compiler_params=_cparams(("arbitrary", "arbitrary"), 24),
        name="adaln_mod",
    )(cb, w_ada, b_ada.reshape(DEPTH, 1, N_MOD * D))
    return out.reshape(DEPTH, N_COND, N_MOD, D)


def _front_kernel(latent, n_state_in, *refs):
    (x_ref, m_ref, n1_ref, w_ref, qg_ref, kg_ref, seg_ref, sgw_ref, sgb_ref), refs = refs[:9], refs[9:]
    if latent:
        cos_ref, sin_ref, q_o, k_o, v_o, sg_o, wbf = refs
    else:
        q_o, k_o, v_o, sg_o, sk_o, sv_o, wbf = refs[n_state_in:]

    @pl.when(pl.program_id(0) == 0)
    def _():
        for c0 in range(0, IN_W, 512):
            wbf[:, c0:c0 + 512] = w_ref[0, :, c0:c0 + 512].astype(BF16)

    h = _modnorm(x_ref[...], n1_ref[0], m_ref[0, 0, 1:2, :], m_ref[0, 0, 0:1, :])
    proj = jnp.dot(h.astype(BF16), wbf[...], preferred_element_type=F32)

    def qk_norm(z, g):
        ms = jnp.dot((z * z).astype(BF16), seg_ref[...], preferred_element_type=F32)
        return z * lax.rsqrt(ms + EPS) * g

    q = qk_norm(proj[:, 0:QK_W], qg_ref[...])
    k = qk_norm(proj[:, QK_W:2 * QK_W], kg_ref[...])
    v = proj[:, 2 * QK_W:2 * QK_W + MIX_A]

    if latent:
        lane = lax.broadcasted_iota(I32, (1, QK_W), 1)
        first = (lane % 32) < 16

        def rope(z):
            partner = jnp.where(first, pltpu.roll(z, QK_W - 16, 1), pltpu.roll(z, 16, 1))
            return z * cos_ref[...] + partner * sin_ref[...]

        q = rope(q)
        k = rope(k)
    else:
        for b in range(TF // CTX_S):
            sk_o[b, 0] = k[b * CTX_S:(b + 1) * CTX_S, :]
            sv_o[b, 0] = v[b * CTX_S:(b + 1) * CTX_S, :]

    q_o[...] = (q * (HEAD_DIM ** -0.5)).astype(BF16)
    k_o[...] = k.astype(BF16)
    v_o[...] = v.astype(BF16)

    gu0 = 2 * QK_W + MIX_A
    gv0 = gu0 + MIX_B
    for g in range(SG_GROUPS):
        wg = sgw_ref[0, g].astype(BF16)
        for n in range(TF // SG_CHUNK):
            rows = slice(n * SG_CHUNK, (n + 1) * SG_CHUNK)
            cu = slice(gu0 + g * 128, gu0 + (g + 1) * 128)
            cv = slice(gv0 + g * 128, gv0 + (g + 1) * 128)
            gv = _gelu_tanh(proj[rows, cv]).astype(BF16)
            mixed = jnp.dot(wg, gv, preferred_element_type=F32) + sgb_ref[0, g]
            sg_o[rows, g * 128:(g + 1) * 128] = (_gelu_tanh(proj[rows, cu]) * mixed).astype(BF16)


def _front(latent, l, x, mod, norm1_g, w_in, qg, kg, seg, sg_w, sgb, rope_tabs, states):
    e = l // 2
    rows = T_LAT if latent else T_CTX
    n_tiles = rows // TF
    off = T_CTX // TF if latent else 0
    tiles_per_seq = LAT_S // TF
    if latent:
        cond_map = lambda i: (l, 1 + i // tiles_per_seq, 0, 0)
    else:
        cond_map = lambda i: (l, 0, 0, 0)
    in_specs = [
        pl.BlockSpec((TF, D), lambda i: (i + off, 0)),
        pl.BlockSpec((1, 1, N_MOD, D), cond_map),
        pl.BlockSpec((1, 1, D), lambda i: (l, 0, 0)),
        pl.BlockSpec((1, D, IN_W), lambda i: (e, 0, 0)),
        pl.BlockSpec((1, QK_W), lambda i: (0, 0)),
        pl.BlockSpec((1, QK_W), lambda i: (0, 0)),
        pl.BlockSpec((QK_W, QK_W), lambda i: (0, 0)),
        pl.BlockSpec((1, SG_GROUPS, SG_CHUNK, SG_CHUNK), lambda i: (e, 0, 0, 0)),
        pl.BlockSpec((1, SG_GROUPS, SG_CHUNK, 128), lambda i: (e, 0, 0, 0)),
    ]
    args = [_hbm(x), mod, norm1_g.reshape(DEPTH, 1, D), w_in, qg, kg, seg, sg_w, sgb]
    tok = lambda: pl.BlockSpec((TF, QK_W), lambda i: (i, 0))
    out_specs = [tok(), tok(), tok(), tok()]
    out_shape = [_hbm_out((rows, QK_W), BF16)] * 4
    aliases = {}
    if latent:
        cos_t, sin_t = rope_tabs
        in_specs += [pl.BlockSpec((TF, QK_W), lambda i: (i % tiles_per_seq, 0))] * 2
        args += [cos_t, sin_t]
    else:
        if states:
            in_specs += [pl.BlockSpec(memory_space=pl.ANY)] * 2
            args += list(states)
            aliases = {9: 4, 10: 5}
        st = lambda: pl.BlockSpec((TF // CTX_S, 1, CTX_S, QK_W), lambda i: (i, e, 0, 0))
        out_specs += [st(), st()]
        out_shape += [_hbm_out((N_CTX_B, DEPTH // 2, CTX_S, QK_W), F32)] * 2
    return pl.pallas_call(
        functools.partial(_front_kernel, latent, len(aliases)),
        grid=(n_tiles,),
        in_specs=in_specs,
        out_specs=out_specs,
        out_shape=out_shape,
        scratch_shapes=[pltpu.VMEM((D, IN_W), BF16)],
        input_output_aliases=aliases,
        compiler_params=_cparams(("arbitrary",), 48),
        name="front_lat" if latent else "front_ctx",
    )(*args)


def _attn_kernel(has_cache, heads, lam0, *refs):
    if has_cache:
        (q_ref, k_ref, v_ref, ck_ref, cv_ref, lq1, lk1, lq2, lk2, sub_ref, o_ref) = refs
    else:
        (q_ref, k_ref, v_ref, lq1, lk1, lq2, lk2, sub_ref, o_ref) = refs

    lane = lax.broadcasted_iota(I32, (1, VAL_DIM), 1)
    lam = (jnp.exp(jnp.sum(lq1[0] * lk1[0], axis=-1, keepdims=True))
           - jnp.exp(jnp.sum(lq2[0] * lk2[0], axis=-1, keepdims=True)) + lam0)

    for h in range(heads):
        hs = slice(h * VAL_DIM, (h + 1) * VAL_DIM)
        qf = q_ref[:, hs].astype(F32)
        qc = [jnp.where(lane < HEAD_DIM, qf, 0.0).astype(BF16),
              jnp.where(lane >= HEAD_DIM, qf, 0.0).astype(BF16)]
        keys = [k_ref[:, hs]]
        vals = [v_ref[:, hs]]
        if has_cache:
            keys.append(ck_ref[0, 0].astype(BF16))
            vals.append(cv_ref[0, 0].astype(BF16))

        probs = []
        for c in range(2):
            s = [lax.dot_general(qc[c], kk, NT_DIMS, preferred_element_type=F32) for kk in keys]
            m = s[0].max(axis=-1, keepdims=True)
            for sx in s[1:]:
                m = jnp.maximum(m, sx.max(axis=-1, keepdims=True))
            p = [jnp.exp(sx - m) for sx in s]
            den = p[0].sum(axis=-1, keepdims=True)
            for px in p[1:]:
                den = den + px.sum(axis=-1, keepdims=True)
            probs.append((p, den))

        r0 = 1.0 / probs[0][1]
        r1 = lam / probs[1][1]
        o = None
        for idx in range(len(keys)):
            a = (probs[0][0][idx] * r0 - probs[1][0][idx] * r1).astype(BF16)
            t = jnp.dot(a, vals[idx], preferred_element_type=F32)
            o = t if o is None else o + t
        ms = jnp.mean(o * o, axis=-1, keepdims=True)
        o_ref[:, hs] = ((o * lax.rsqrt(ms + EPS) * sub_ref[0]) * (1.0 - lam0)).astype(BF16)


def _attention(has_cache, l, q, k, v, cache_k, cache_v, lam_q1, lam_k1, lam_q2, lam_k2, subln_g):
    e = l // 2
    lam0 = 0.8 - 0.6 * math.exp(-0.3 * l)
    n_b, s_len = (N_LAT_B, LAT_S) if has_cache else (N_CTX_B, CTX_S)
    heads = 1 if has_cache else N_HEADS
    width = heads * VAL_DIM
    nq = s_len // TM
    kv = lambda: pl.BlockSpec((s_len, width), lambda b, h, qi: (b, h))
    par = lambda: pl.BlockSpec((1, 1, HEAD_DIM), lambda b, h, qi: (e, 0, 0))
    in_specs = [pl.BlockSpec((TM, width), lambda b, h, qi: (b * nq + qi, h)), kv(), kv()]
    args = [_hbm(q), _hbm(k), _hbm(v)]
    if has_cache:
        cs = lambda: pl.BlockSpec((1, 1, PAST, VAL_DIM), lambda b, h, qi: (b, e, 0, h))
        in_specs += [cs(), cs()]
        args += [cache_k.reshape(N_LAT_B, DEPTH // 2, PAST, QK_W),
                 cache_v.reshape(N_LAT_B, DEPTH // 2, PAST, MIX_A)]
    in_specs += [par(), par(), par(), par(),
                 pl.BlockSpec((1, 1, VAL_DIM), lambda b, h, qi: (e, 0, 0))]
    r3 = lambda a: a.reshape(DEPTH // 2, 1, -1)
    args += [r3(lam_q1), r3(lam_k1), r3(lam_q2), r3(lam_k2), r3(subln_g)]
    return pl.pallas_call(
        functools.partial(_attn_kernel, has_cache, heads, lam0),
        grid=(n_b, N_HEADS // heads, nq),
        in_specs=in_specs,
        out_specs=pl.BlockSpec((TM, width), lambda b, h, qi: (b * nq + qi, h)),
        out_shape=_hbm_out((n_b * s_len, MIX_A), BF16),
        compiler_params=_cparams(("arbitrary", "arbitrary", "arbitrary"), 32),
        name="attn_lat" if has_cache else "attn_ctx",
    )(*args)


def _route_tail(sub_tile, x_new, m_ref, n2_ref, rwt_ref, rbb_ref, tri_ref,
                h2_o, lp_o, rf_o, nch_o, loff_o, gb_o, cnt_o, carry):
    tok = slice(sub_tile * TM, (sub_tile + 1) * TM)
    if sub_tile == 0:
        @pl.when(pl.program_id(0) == 0)
        def _():
            carry[...] = jnp.zeros_like(carry)

    h2 = _modnorm(x_new, n2_ref[0], m_ref[0, 0, 4:5, :], m_ref[0, 0, 3:4, :])

    hh = h2.astype(BF16)
    h2_o[tok, :] = hh
    hl = (h2 - hh.astype(F32)).astype(BF16)
    rw = rwt_ref[...]
    wh = rw.astype(BF16)
    wl = (rw - wh.astype(F32)).astype(BF16)
    dg = lambda a, b: lax.dot_general(a, b, NT_DIMS, preferred_element_type=F32)
    logits = dg(wh, hh) + dg(wh, hl) + dg(wl, hh)
    score = _sigmoid(logits)
    sel = score + rbb_ref[...]

    neg_inf = jnp.full((1, TM), -jnp.inf, F32)
    grp = []
    for g in range(N_GROUPS):
        s_rows = [sel[PER_GROUP * g + j:PER_GROUP * g + j + 1, :] for j in range(PER_GROUP)]
        c_rows = [score[PER_GROUP * g + j:PER_GROUP * g + j + 1, :] for j in range(PER_GROUP)]
        f_idx = jnp.zeros((1, TM), I32)
        f_val, f_sc = s_rows[0], c_rows[0]
        for j in range(1, PER_GROUP):
            better = s_rows[j] > f_val
            f_idx = jnp.where(better, j, f_idx)
            f_val = jnp.where(better, s_rows[j], f_val)
            f_sc = jnp.where(better, c_rows[j], f_sc)
        s_idx = jnp.zeros((1, TM), I32)
        s_val, s_sc = neg_inf, jnp.zeros((1, TM), F32)
        for j in range(PER_GROUP):
            better = (f_idx != j) & (s_rows[j] > s_val)
            s_idx = jnp.where(better, j, s_idx)
            s_val = jnp.where(better, s_rows[j], s_val)
            s_sc = jnp.where(better, c_rows[j], s_sc)
        grp.append((f_val + s_val, PER_GROUP * g + f_idx, PER_GROUP * g + s_idx, f_sc, s_sc))

    best, e0, e1, c0, c1 = grp[0]
    for g in range(1, N_GROUPS):
        better = grp[g][0] > best
        best = jnp.where(better, grp[g][0], best)
        e0 = jnp.where(better, grp[g][1], e0)
        e1 = jnp.where(better, grp[g][2], e1)
        c0 = jnp.where(better, grp[g][3], c0)
        c1 = jnp.where(better, grp[g][4], c1)
    den = c0 + c1
    w0 = c0 / den
    w1 = c1 / den

    sub = lax.broadcasted_iota(I32, (N_EXPERTS, TM), 0)
    hit0 = sub == jnp.broadcast_to(e0, (N_EXPERTS, TM))
    hit1 = sub == jnp.broadcast_to(e1, (N_EXPERTS, TM))
    onehot = jnp.where(hit0 | hit1, 1.0, 0.0)
    prefix = jnp.dot(onehot.astype(BF16), tri_ref[...], preferred_element_type=F32)
    count = jnp.sum(onehot, axis=1, keepdims=True)
    padded = jnp.floor((count + (ROW_CHUNK - 1)) * (1.0 / ROW_CHUNK)) * ROW_CHUNK

    sub1 = lax.broadcasted_iota(I32, (N_EXPERTS, 1), 0)
    run = jnp.zeros((1, 1), F32)
    local_off = jnp.zeros((N_EXPERTS, 1), F32)
    for e in range(N_EXPERTS):
        local_off = jnp.where(sub1 == e, run, local_off)
        run = run + padded[e:e + 1, :]
    place = prefix + local_off
    lpos0 = jnp.sum(jnp.where(hit0, place, 0.0), axis=0, keepdims=True).astype(I32)
    lpos1 = jnp.sum(jnp.where(hit1, place, 0.0), axis=0, keepdims=True).astype(I32)

    lanes = lambda col: jnp.broadcast_to(col, (N_EXPERTS, 128)).astype(I32)
    nch_o[sub_tile] = lanes(padded * (1.0 / ROW_CHUNK))
    loff_o[sub_tile] = lanes(local_off)
    gb_o[sub_tile] = carry[...].astype(I32)
    carry[...] = carry[...] + padded
    cnt_o[...] = carry[...].astype(I32)

    sub8 = lax.broadcasted_iota(I32, (8, TM), 0)
    bc = lambda r: jnp.broadcast_to(r, (8, TM))
    lp_o[:, tok] = jnp.where(sub8 == 0, bc(lpos0), jnp.where(sub8 == 1, bc(lpos1), 0))
    rf_o[:, tok] = jnp.where(sub8 == 0, bc(w0), jnp.where(sub8 == 1, bc(w1), 0.0))


def _tail_specs(l, sub_tiles):
    rows = sub_tiles * TM
    in_specs = [
        pl.BlockSpec((1, 1, D), lambda i: (l, 0, 0)),
        pl.BlockSpec((N_EXPERTS, D), lambda i: (0, 0)),
        pl.BlockSpec((N_EXPERTS, TM), lambda i: (0, 0)),
        pl.BlockSpec((TM, TM), lambda i: (0, 0)),
    ]
    per_tile = lambda: pl.BlockSpec((sub_tiles, N_EXPERTS, 128), lambda i: (i, 0, 0))
    out_specs = [
        pl.BlockSpec((rows, D), lambda i: (i, 0)),
        pl.BlockSpec((8, rows), lambda i: (0, i)),
        pl.BlockSpec((8, rows), lambda i: (0, i)),
        per_tile(), per_tile(), per_tile(),
        pl.BlockSpec((N_EXPERTS, 128), lambda i: (0, 0)),
    ]
    out_shape = [
        _hbm_out((T, D), BF16),
        jax.ShapeDtypeStruct((8, T), I32),
        jax.ShapeDtypeStruct((8, T), F32),
        jax.ShapeDtypeStruct((N_TILES, N_EXPERTS, 128), I32),
        jax.ShapeDtypeStruct((N_TILES, N_EXPERTS, 128), I32),
        jax.ShapeDtypeStruct((N_TILES, N_EXPERTS, 128), I32),
        jax.ShapeDtypeStruct((N_EXPERTS, 128), I32),
    ]
    scratch = [pltpu.VMEM((N_EXPERTS, 128), F32)]
    return in_specs, out_specs, out_shape, scratch


def _out_kernel(x_ref, m_ref, ac_ref, al_ref, sc_ref, sl_ref, w_ref,
                n2_ref, rwt_ref, rbb_ref, tri_ref,
                x_o, *tail_refs):
    *tail_outs, wbf, carry = tail_refs
    i = pl.program_id(0)

    @pl.when(i == 0)
    def _():
        wbf[...] = w_ref[0].astype(BF16)

    is_ctx = i < T_CTX // TO
    att = jnp.where(is_ctx, ac_ref[...], al_ref[...])
    sgo = jnp.where(is_ctx, sc_ref[...], sl_ref[...])
    out = (jnp.dot(att, wbf[0:MIX_A, :], preferred_element_type=F32)
           + jnp.dot(sgo, wbf[MIX_A:MIX_A + MIX_B, :], preferred_element_type=F32))
    x_new = x_ref[...] + m_ref[0, 0, 2:3, :] * out
    x_o[...] = x_new
    for sub_tile in range(TO // TM):
        _route_tail(sub_tile, x_new[sub_tile * TM:(sub_tile + 1) * TM, :],
                    m_ref, n2_ref, rwt_ref, rbb_ref, tri_ref, *tail_outs, carry)


def _out_proj(l, x, mod, att_c, att_l, sg_c, sg_l, w_out, norm2_g, rwt, rbb, tri):
    e = l // 2
    t_in, t_out, t_shape, t_scratch = _tail_specs(l, TO // TM)
    n_ctx = T_CTX // TO
    ctx = lambda: pl.BlockSpec((TO, MIX_A), lambda i: (jnp.minimum(i, n_ctx - 1), 0))
    lat = lambda: pl.BlockSpec((TO, MIX_A), lambda i: (jnp.maximum(i - n_ctx, 0), 0))
    in_specs = [
        pl.BlockSpec((TO, D), lambda i: (i, 0)),
        pl.BlockSpec((1, 1, N_MOD, D), lambda i: (l, _cond_of_tile(i * (TO // TM)), 0, 0)),
        ctx(), lat(), ctx(), lat(),
        pl.BlockSpec((1, D, D), lambda i: (e, 0, 0)),
    ] + t_in
    return pl.pallas_call(
        _out_kernel,
        grid=(T // TO,),
        in_specs=in_specs,
        out_specs=[pl.BlockSpec((TO, D), lambda i: (i, 0))] + t_out,
        out_shape=[_hbm_out((T, D), F32)] + t_shape,
        scratch_shapes=[pltpu.VMEM((D, D), BF16)] + t_scratch,
        input_output_aliases={0: 0},
        compiler_params=_cparams(("arbitrary",), 40),
        name="out_proj_route",
    )(_hbm(x), mod, _hbm(att_c), _hbm(att_l), _hbm(sg_c), _hbm(sg_l), w_out,
      norm2_g.reshape(DEPTH, 1, D), rwt, rbb, tri)


def _pool_kernel(x_ref, xp_ref, xn_ref, m_ref, n1_ref, pw_ref, ps_ref,
                 n2_ref, rwt_ref, rbb_ref, tri_ref,
                 x_o, *tail_refs):
    *tail_outs, carry = tail_refs
    i = pl.program_id(0)
    is_lat = i >= N_CTX_TILES
    ti = jnp.where(is_lat, (i - N_CTX_TILES) % LAT_TILES_PER_SEQ, 0)
    last_ti = jnp.where(is_lat, LAT_TILES_PER_SEQ - 1, 0)
    seq_len = jnp.where(is_lat, LAT_S, CTX_S)

    g1n, sc1, sh1 = n1_ref[0], m_ref[0, 0, 1:2, :], m_ref[0, 0, 0:1, :]
    x = x_ref[...]
    h = _modnorm(x, g1n, sc1, sh1)
    hp = jnp.where(ti == 0, 0.0, _modnorm(xp_ref[...], g1n, sc1, sh1))
    hn = jnp.where(ti == last_ti, 0.0, _modnorm(xn_ref[...], g1n, sc1, sh1))
    hcat = jnp.concatenate([hp, h, hn], axis=0)
    n_rows = TM + 2 * POOL_HALO

    t_pos = ti * TM + lax.broadcasted_iota(I32, (TM, 1), 0)
    outs = []
    for g, w in enumerate(POOL_WINDOWS):
        cols = slice(g * POOL_DIM, (g + 1) * POOL_DIM)
        acc = hcat[:, cols]
        acc = acc + pltpu.roll(acc, 1, 0)
        step = 1
        while 2 * step < w:
            acc = pltpu.roll(acc, step, 0) + pltpu.roll(acc, n_rows - step, 0)
            step *= 2
        lo = jnp.maximum(t_pos - w // 2, 0)
        hi = jnp.minimum(t_pos - w // 2 + w, seq_len)
        mean = acc[POOL_HALO:POOL_HALO + TM, :] / (hi - lo).astype(F32)
        dlt = (mean - h[:, cols]).astype(BF16)
        outs.append(jnp.dot(dlt, pw_ref[0, g].astype(BF16), preferred_element_type=F32))
    out = jnp.concatenate(outs, axis=1) * ps_ref[0]
    x_new = x + m_ref[0, 0, 2:3, :] * out
    x_o[...] = x_new
    _route_tail(0, x_new, m_ref, n2_ref, rwt_ref, rbb_ref, tri_ref, *tail_outs, carry)


def _pool_mixer(l, x, mod, norm1_g, pool_w, pool_scale, norm2_g, rwt, rbb, tri):
    o = l // 2
    t_in, t_out, t_shape, t_scratch = _tail_specs(l, 1)
    halo_blocks = TM // POOL_HALO
    in_specs = [
        pl.BlockSpec((TM, D), lambda i: (i, 0)),
        pl.BlockSpec((POOL_HALO, D), lambda i: (jnp.maximum(i * halo_blocks - 1, 0), 0)),
        pl.BlockSpec((POOL_HALO, D), lambda i: (jnp.minimum((i + 1) * halo_blocks, T // POOL_HALO - 1), 0)),
        pl.BlockSpec((1, 1, N_MOD, D), lambda i: (l, _cond_of_tile(i), 0, 0)),
        pl.BlockSpec((1, 1, D), lambda i: (l, 0, 0)),
        pl.BlockSpec((1, len(POOL_WINDOWS), POOL_DIM, POOL_DIM), lambda i: (o, 0, 0, 0)),
        pl.BlockSpec((1, 1, D), lambda i: (o, 0, 0)),
    ] + t_in
    return pl.pallas_call(
        _pool_kernel,
        grid=(N_TILES,),
        in_specs=in_specs,
        out_specs=[pl.BlockSpec((TM, D), lambda i: (i, 0))] + t_out,
        out_shape=[_hbm_out((T, D), F32)] + t_shape,
        scratch_shapes=t_scratch,
        compiler_params=_cparams(("arbitrary",), 32),
        name="pool_route",
    )(_hbm(x), _hbm(x), _hbm(x), mod, norm1_g.reshape(DEPTH, 1, D), pool_w,
      pool_scale.reshape(DEPTH // 2, 1, D),
      norm2_g.reshape(DEPTH, 1, D), rwt, rbb, tri)


def _start_run_copies(tile, nch_ref, loff_ref, gb_ref, make_copy):
    for e in range(N_EXPERTS):
        lo = loff_ref[tile * N_EXPERTS + e]
        gb = gb_ref[tile * N_EXPERTS + e] + e * EXPERT_CAP

        def issue(c, carry, lo=lo, gb=gb):
            make_copy(pl.multiple_of(lo + c * ROW_CHUNK, ROW_CHUNK),
                      pl.multiple_of(gb + c * ROW_CHUNK, ROW_CHUNK)).start()
            return carry

        lax.fori_loop(0, nch_ref[tile * N_EXPERTS + e], issue, 0)


def _wait_run_copies(tile, nch_ref, make_copy):
    total = 0
    for e in range(N_EXPERTS):
        total = total + nch_ref[tile * N_EXPERTS + e]

    def drain(c, carry):
        make_copy(0, 0).wait()
        return carry

    lax.fori_loop(0, total, drain, 0)


def _dispatch_kernel(nch_ref, loff_ref, gb_ref, h_ref, lp_ref, xs_ref, sbuf, sems):
    i = pl.program_id(0)
    slot = i % 2
    r = lax.broadcasted_iota(I32, (SORTED_ROWS, TM), 0)
    hit = (r == lp_ref[0:1, :]) | (r == lp_ref[1:2, :])
    perm = jnp.where(hit, 1.0, 0.0).astype(BF16)
    sbuf[slot] = jnp.dot(perm, h_ref[...], preferred_element_type=F32).astype(BF16)

    def copies_of(s):
        def make_copy(local_row, global_row):
            return pltpu.make_async_copy(sbuf.at[s, pl.ds(local_row, ROW_CHUNK)],
                                         xs_ref.at[pl.ds(global_row, ROW_CHUNK)], sems.at[s])
        return make_copy

    @pl.when(i > 0)
    def _():
        _wait_run_copies(i - 1, nch_ref, copies_of(1 - slot))

    _start_run_copies(i, nch_ref, loff_ref, gb_ref, copies_of(slot))

    @pl.when(i == N_TILES - 1)
    def _():
        _wait_run_copies(i, nch_ref, copies_of(slot))


def _dispatch(layout, h2, lp):
    return pl.pallas_call(
        _dispatch_kernel,
        grid_spec=pltpu.PrefetchScalarGridSpec(
            num_scalar_prefetch=3,
            grid=(N_TILES,),
            in_specs=[pl.BlockSpec((TM, D), lambda i, *_: (i, 0)),
                      pl.BlockSpec((8, TM), lambda i, *_: (0, i))],
            out_specs=pl.BlockSpec(memory_space=pl.ANY),
            scratch_shapes=[pltpu.VMEM((2, SORTED_ROWS, D), BF16), pltpu.SemaphoreType.DMA((2,))],
        ),
        out_shape=_hbm_out((N_EXPERTS * EXPERT_CAP, D), BF16),
        compiler_params=_cparams(("arbitrary",), 16),
        name="moe_dispatch",
    )(*layout, _hbm(h2), lp)


def _moe_kernel(l, rb_ref, e_ref, nv_ref, first_ref, ord_ref, nxt_ref, xs_ref, wg_hbm, wu_hbm, wd_hbm, y_ref,
                wgf, wuf, wdf, wgb, wub, wdb, sems):
    i = pl.program_id(0)
    n_valid = nv_ref[i]

    def weight_copies(e, slot):
        return (pltpu.make_async_copy(wg_hbm.at[l, e], wgf.at[slot], sems.at[slot, 0]),
                pltpu.make_async_copy(wu_hbm.at[l, e], wuf.at[slot], sems.at[slot, 1]),
                pltpu.make_async_copy(wd_hbm.at[l, e], wdf.at[slot], sems.at[slot, 2]))

    @pl.when(first_ref[i] == 1)
    def _():
        slot = ord_ref[i] % 2

        @pl.when(ord_ref[i] == 0)
        def _():
            for cp in weight_copies(e_ref[i], slot):
                cp.start()

        for cp in weight_copies(e_ref[i], slot):
            cp.wait()

        @pl.when(nxt_ref[i] >= 0)
        def _():
            for cp in weight_copies(nxt_ref[i], 1 - slot):
                cp.start()

        wgb[...] = wgf[slot].astype(BF16)
        wub[...] = wuf[slot].astype(BF16)
        wdb[...] = wdf[slot].astype(BF16)

    @pl.when(n_valid > 0)
    def _():
        row = lax.broadcasted_iota(I32, (TE, 1), 0)
        x = jnp.where(row < n_valid, xs_ref[...], jnp.zeros((), BF16))
        gate = jnp.dot(x, wgb[...], preferred_element_type=F32)
        up = jnp.dot(x, wub[...], preferred_element_type=F32)
        act = (gate * _sigmoid(gate) * up).astype(BF16)
        y_ref[...] = jnp.dot(act, wdb[...], preferred_element_type=F32).astype(BF16)


def _experts(l, tile_map, xs, w_gate, w_up, w_down):
    n_pre = len(tile_map)
    blk = lambda i, rb, *_: (rb[i], 0)
    return pl.pallas_call(
        functools.partial(_moe_kernel, l),
        grid_spec=pltpu.PrefetchScalarGridSpec(
            num_scalar_prefetch=n_pre,
            grid=(MOE_TILES,),
            in_specs=[
                pl.BlockSpec((TE, D), blk),
                pl.BlockSpec(memory_space=pl.ANY),
                pl.BlockSpec(memory_space=pl.ANY),
                pl.BlockSpec(memory_space=pl.ANY),
            ],
            out_specs=pl.BlockSpec((TE, D), blk),
            scratch_shapes=[pltpu.VMEM((2, D, D_FF), F32), pltpu.VMEM((2, D, D_FF), F32),
                            pltpu.VMEM((2, D_FF, D), F32),
                            pltpu.VMEM((D, D_FF), BF16), pltpu.VMEM((D, D_FF), BF16),
                            pltpu.VMEM((D_FF, D), BF16),
                            pltpu.SemaphoreType.DMA((2, 3))],
        ),
        out_shape=_hbm_out((N_EXPERTS * EXPERT_CAP, D), BF16),
        compiler_params=_cparams(("arbitrary",), 40),
        name="moe_experts",
    )(*tile_map, _hbm(xs), w_gate, w_up, w_down)


def _combine_kernel(nch_ref, loff_ref, gb_ref, x_ref, m_ref, lpt_ref, w_ref, y_ref, o_ref, ybuf, sems):
    i = pl.program_id(0)
    slot = i % 2

    def copies_of(s):
        def make_copy(local_row, global_row):
            return pltpu.make_async_copy(y_ref.at[pl.ds(global_row, ROW_CHUNK)],
                                         ybuf.at[s, pl.ds(local_row, ROW_CHUNK)], sems.at[s])
        return make_copy

    @pl.when(i == 0)
    def _():
        ybuf[...] = jnp.zeros_like(ybuf)
        _start_run_copies(i, nch_ref, loff_ref, gb_ref, copies_of(slot))

    @pl.when(i + 1 < N_TILES)
    def _():
        _start_run_copies(i + 1, nch_ref, loff_ref, gb_ref, copies_of(1 - slot))

    _wait_run_copies(i, nch_ref, copies_of(slot))

    col = lax.broadcasted_iota(I32, (TM, SORTED_ROWS), 1)
    pick0 = jnp.where(col == lpt_ref[:, 0:1], 1.0, 0.0).astype(BF16)
    pick1 = jnp.where(col == lpt_ref[:, 1:2], 1.0, 0.0).astype(BF16)
    y0 = jnp.dot(pick0, ybuf[slot], preferred_element_type=F32)
    y1 = jnp.dot(pick1, ybuf[slot], preferred_element_type=F32)
    moe = w_ref[:, 0:1] * y0 + w_ref[:, 1:2] * y1
    o_ref[...] = x_ref[...] + m_ref[0, 0, 5:6, :] * moe


def _combine(l, layout, x, mod, lpt, wcol, y):
    return pl.pallas_call(
        _combine_kernel,
        grid_spec=pltpu.PrefetchScalarGridSpec(
            num_scalar_prefetch=3,
            grid=(N_TILES,),
            in_specs=[
                pl.BlockSpec((TM, D), lambda i, *_: (i, 0)),
                pl.BlockSpec((1, 1, N_MOD, D), lambda i, *_: (l, _cond_of_tile(i), 0, 0)),
                pl.BlockSpec((TM, 8), lambda i, *_: (i, 0)),
                pl.BlockSpec((TM, 8), lambda i, *_: (i, 0)),
                pl.BlockSpec(memory_space=pl.ANY),
            ],
            out_specs=pl.BlockSpec((TM, D), lambda i, *_: (i, 0)),
            scratch_shapes=[pltpu.VMEM((2, SORTED_ROWS, D), BF16), pltpu.SemaphoreType.DMA((2,))],
        ),
        out_shape=_hbm_out((T, D), F32),
        input_output_aliases={3: 0},
        compiler_params=_cparams(("arbitrary",), 24),
        name="moe_combine",
    )(*layout, _hbm(x), mod, lpt, wcol, _hbm(y))


def _tile_map(cnt):
    cnt = cnt[:, 0]
    ids = jnp.arange(N_EXPERTS, dtype=I32)
    n_tiles = (cnt + TE - 1) // TE
    cum = jnp.cumsum(n_tiles)
    total = cum[-1]
    i = jnp.arange(MOE_TILES, dtype=I32)
    ii = jnp.minimum(i, total - 1)
    e = jnp.sum((ii[:, None] >= cum[None, :]).astype(I32), axis=1)
    onehot = (e[:, None] == ids[None, :]).astype(I32)
    pick = lambda v: jnp.sum(onehot * v[None, :], axis=1)
    j = ii - pick(cum - n_tiles)
    active = i < total
    n_valid = jnp.where(active, jnp.clip(pick(cnt) - j * TE, 0, TE), 0)
    first = (active & (j == 0)).astype(I32)
    nonempty = n_tiles > 0
    ordinal = jnp.cumsum(nonempty.astype(I32)) - 1
    later = jnp.where(nonempty[None, :] & (ids[None, :] > ids[:, None]), ids[None, :], N_EXPERTS)
    nxt = jnp.min(later, axis=1)
    nxt = jnp.where(nxt == N_EXPERTS, -1, nxt)
    as_i32 = lambda v: v.astype(I32)
    return tuple(map(as_i32, (e * CAP_TILES + j, e, n_valid, first, pick(ordinal), pick(nxt))))


def _moe(l, x, tail, mod, w_gate, w_up, w_down):
    h2, lp, rf, nch, loff, gb, cnt = tail
    layout = tuple(a[:, :, 0].reshape(N_TILES * N_EXPERTS) for a in (nch, loff, gb))
    xs = _dispatch(layout, h2, lp)
    y = _experts(l, _tile_map(cnt), xs, w_gate, w_up, w_down)
    return _combine(l, layout, x, mod, lp.T, rf.T, y)


def _rope_tables():
    t = jnp.arange(LAT_S)
    row = (t // GRID_W).astype(F32)
    col = (t % GRID_W).astype(F32)
    nf = HEAD_DIM // 4
    inv = ROPE_BASE ** (-jnp.arange(nf, dtype=F32) / nf)
    ar = row[:, None] * inv[None, :]
    ac = col[:, None] * inv[None, :]
    cos = jnp.concatenate([jnp.cos(ar), jnp.cos(ar), jnp.cos(ac), jnp.cos(ac)], axis=1)
    sin = jnp.concatenate([-jnp.sin(ar), jnp.sin(ar), -jnp.sin(ac), jnp.sin(ac)], axis=1)
    reps = QK_W // HEAD_DIM
    return jnp.tile(cos, (1, reps)), jnp.tile(sin, (1, reps))


def kernel(x_prompt, x_sample, cache_k, cache_v, c, c_ctx, norm1_g, norm2_g, w_ada, b_ada, w_in, w_out,
           q_norm_g, k_norm_g, lam_q1, lam_k1, lam_q2, lam_k2, subln_g, sg_w, sg_b, pool_w, pool_scale,
           router_w, router_b, w_gate, w_up, w_down):
    x = jnp.concatenate([x_prompt.reshape(T_CTX, D), x_sample.reshape(T_LAT, D)], axis=0)
    cond = jnp.concatenate([c_ctx[None, :], c], axis=0)
    mod = _modulation(cond, w_ada, b_ada)

    seg_id = np.arange(QK_W) // HEAD_DIM
    seg = jnp.asarray((seg_id[:, None] == seg_id[None, :]) / HEAD_DIM, dtype=BF16)
    tri = jnp.asarray(np.arange(TM)[:, None] < np.arange(TM)[None, :], dtype=BF16)
    rope_tabs = _rope_tables()
    rwt = router_w.T
    rbb = jnp.broadcast_to(router_b[:, None], (N_EXPERTS, TM))
    states = None

    for l in range(DEPTH):
        if l % 2 == 0:
            e = l // 2
            qg = jnp.tile(q_norm_g[e], QK_W // HEAD_DIM)[None, :]
            kg = jnp.tile(k_norm_g[e], QK_W // HEAD_DIM)[None, :]
            sgb = jnp.broadcast_to(sg_b[:, :, :, None], (DEPTH // 2, SG_GROUPS, SG_CHUNK, 128))
            common = (l, x, mod, norm1_g, w_in, qg, kg, seg, sg_w, sgb)
            q_c, k_c, v_c, sg_c, *states = _front(False, *common, None, states)
            q_l, k_l, v_l, sg_l = _front(True, *common, rope_tabs, None)
            lam = (lam_q1, lam_k1, lam_q2, lam_k2, subln_g)
            att_c = _attention(False, l, q_c, k_c, v_c, None, None, *lam)
            att_l = _attention(True, l, q_l, k_l, v_l, cache_k, cache_v, *lam)
            x, *tail = _out_proj(l, x, mod, att_c, att_l, sg_c, sg_l, w_out, norm2_g, rwt, rbb, tri)
        else:
            x, *tail = _pool_mixer(l, x, mod, norm1_g, pool_w, pool_scale, norm2_g, rwt, rbb, tri)
        x = _moe(l, x, tail, mod, w_gate, w_up, w_down)
    state_k, state_v = states

    y_prompt = x[:T_CTX].reshape(N_CTX_B, CTX_S, D)
    y_sample = x[T_CTX:].reshape(N_LAT_B, LAT_S, D)
    state_k = state_k.reshape(N_CTX_B, DEPTH // 2, CTX_S, N_HEADS, 2, HEAD_DIM)
    state_v = state_v.reshape(N_CTX_B, DEPTH // 2, CTX_S, N_HEADS, VAL_DIM)
    return y_prompt, y_sample, state_k, state_v
```

```python
import functools
import math

import jax
import jax.numpy as jnp
import numpy as np
from jax import lax
from jax.experimental import pallas as pl
from jax.experimental.pallas import tpu as pltpu

F32 = jnp.float32
BF16 = jnp.bfloat16
I32 = jnp.int32

D = 1024
N_CTX_B, CTX_S = 16, 256
N_LAT_B, LAT_S = 2, 1024
PAST = 512
DEPTH = 4
T_CTX = N_CTX_B * CTX_S
T_LAT = N_LAT_B * LAT_S
T = T_CTX + T_LAT
GRID_W = 64
N_HEADS = 4
HEAD_DIM = 64
VAL_DIM = 128
QK_W = 512
MIX_A = 512
MIX_B = 512
IN_W = 2560
SG_GROUPS = 4
SG_CHUNK = 128
POOL_WINDOWS = (2, 4, 8, 16)
POOL_DIM = 256
POOL_HALO = 8
N_EXPERTS = 16
N_GROUPS = 4
PER_GROUP = 4
D_FF = 512
N_MOD = 6
N_COND = 1 + N_LAT_B
EPS = 1e-6
ROPE_BASE = 10000.0

TM = 256
N_TILES = T // TM
N_CTX_TILES = T_CTX // TM
LAT_TILES_PER_SEQ = LAT_S // TM
ROW_CHUNK = 16
PAD_PER_RUN = ROW_CHUNK - 1
SORTED_ROWS = -(-(2 * TM + N_EXPERTS * PAD_PER_RUN) // 128) * 128
TE = 512
TF = 512
TO = 512
CAP_TILES = -(-(T + N_TILES * PAD_PER_RUN) // TE)
EXPERT_CAP = CAP_TILES * TE
MOE_TILES = (2 * T + N_EXPERTS * N_TILES * PAD_PER_RUN) // TE + N_EXPERTS
MOD_TN = 1024
MOD_STREAMS = 4
MIB = 1024 * 1024

NT_DIMS = (((1,), (1,)), ((), ()))


def _cparams(sem, vmem_mib):
    return pltpu.CompilerParams(dimension_semantics=sem, vmem_limit_bytes=vmem_mib * MIB)


def _hbm(x):
    try:
        return pltpu.with_memory_space_constraint(x, pltpu.HBM)
    except ValueError:
        return x


def _hbm_out(shape, dtype):
    return pltpu.HBM(shape, dtype)


def _sigmoid(x):
    return 1.0 / (1.0 + jnp.exp(-x))


def _gelu_tanh(x):
    c = math.sqrt(2.0 / math.pi)
    return x * (0.5 * (1.0 + jnp.tanh(c * (x + 0.044715 * (x * x * x)))))


def _modnorm(x, g, scale, shift):
    ms = jnp.mean(x * x, axis=-1, keepdims=True)
    return (x * lax.rsqrt(ms + EPS) * g) * (1.0 + scale) + shift


def _cond_of_tile(i):
    return jnp.where(i < N_CTX_TILES, 0, 1 + (i - N_CTX_TILES) // LAT_TILES_PER_SEQ)


def _mod_kernel(cb_ref, *refs):
    w_refs, (b_ref, o_ref, sb_ref) = refs[:MOD_STREAMS], refs[MOD_STREAMS:]

    @pl.when((pl.program_id(0) == 0) & (pl.program_id(1) == 0))
    def _():
        cb = cb_ref[...]
        sb_ref[...] = cb * _sigmoid(cb)

    slab = D // MOD_STREAMS
    for j in range(N_COND):
        cols = []
        for n0 in range(0, MOD_TN, 128):
            acc = None
            for s, w_ref in enumerate(w_refs):
                part = jnp.sum(w_ref[0, :, n0:n0 + 128] * sb_ref[j, s * slab:(s + 1) * slab, :],
                               axis=0, keepdims=True)
                acc = part if acc is None else acc + part
            cols.append(acc)
        o_ref[0, j:j + 1, :] = jnp.concatenate(cols, axis=1) + b_ref[0]


def _modulation(cond, w_ada, b_ada):
    cb = jnp.broadcast_to(cond[:, :, None], (N_COND, D, 128))
    slab_spec = lambda s: pl.BlockSpec((1, D // MOD_STREAMS, MOD_TN), lambda l, n: (l, s, n))
    out = pl.pallas_call(
        _mod_kernel,
        grid=(DEPTH, N_MOD * D // MOD_TN),
        in_specs=[pl.BlockSpec((N_COND, D, 128), lambda l, n: (0, 0, 0))]
        + [slab_spec(s) for s in range(MOD_STREAMS)]
        + [pl.BlockSpec((1, 1, MOD_TN), lambda l, n: (l, 0, n))],
        out_specs=pl.BlockSpec((1, N_COND, MOD_TN), lambda l, n: (l, 0, n)),
        out_shape=jax.ShapeDtypeStruct((DEPTH, N_COND, N_MOD * D), F32),
        scratch_shapes=[pltpu.VMEM((N_COND, D, 128), F32)],
        compiler_params=_cparams(("arbitrary", "arbitrary"), 24),
        name="adaln_mod",
    )(cb, *([w_ada] * MOD_STREAMS), b_ada.reshape(DEPTH, 1, N_MOD * D))
    return out.reshape(DEPTH, N_COND, N_MOD, D)


def _front_kernel(latent, n_state_in, *refs):
    (x_ref, m_ref, n1_ref, w_ref, qg_ref, kg_ref, seg_ref, sgw_ref, sgb_ref), refs = refs[:9], refs[9:]
    if latent:
        cos_ref, sin_ref, q_o, k_o, v_o, sg_o, wbf = refs
    else:
        q_o, k_o, v_o, sg_o, sk_o, sv_o, wbf = refs[n_state_in:]

    @pl.when(pl.program_id(0) == 0)
    def _():
        for c0 in range(0, IN_W, 512):
            wbf[:, c0:c0 + 512] = w_ref[0, :, c0:c0 + 512].astype(BF16)

    h = _modnorm(x_ref[...], n1_ref[0], m_ref[0, 0, 1:2, :], m_ref[0, 0, 0:1, :])
    proj = jnp.dot(h.astype(BF16), wbf[...], preferred_element_type=F32)

    def qk_norm(z, g):
        ms = jnp.dot((z * z).astype(BF16), seg_ref[...], preferred_element_type=F32)
        return z * lax.rsqrt(ms + EPS) * g

    q = qk_norm(proj[:, 0:QK_W], qg_ref[...])
    k = qk_norm(proj[:, QK_W:2 * QK_W], kg_ref[...])
    v = proj[:, 2 * QK_W:2 * QK_W + MIX_A]

    if latent:
        lane = lax.broadcasted_iota(I32, (1, QK_W), 1)
        first = (lane % 32) < 16

        def rope(z):
            partner = jnp.where(first, pltpu.roll(z, QK_W - 16, 1), pltpu.roll(z, 16, 1))
            return z * cos_ref[...] + partner * sin_ref[...]

        q = rope(q)
        k = rope(k)
    else:
        for b in range(TF // CTX_S):
            sk_o[b, 0] = k[b * CTX_S:(b + 1) * CTX_S, :]
            sv_o[b, 0] = v[b * CTX_S:(b + 1) * CTX_S, :]

    q_o[...] = (q * (HEAD_DIM ** -0.5)).astype(BF16)
    k_o[...] = k.astype(BF16)
    v_o[...] = v.astype(BF16)

    gu0 = 2 * QK_W + MIX_A
    gv0 = gu0 + MIX_B
    for g in range(SG_GROUPS):
        wg = sgw_ref[0, g].astype(BF16)
        for n in range(TF // SG_CHUNK):
            rows = slice(n * SG_CHUNK, (n + 1) * SG_CHUNK)
            cu = slice(gu0 + g * 128, gu0 + (g + 1) * 128)
            cv = slice(gv0 + g * 128, gv0 + (g + 1) * 128)
            gv = _gelu_tanh(proj[rows, cv]).astype(BF16)
            mixed = jnp.dot(wg, gv, preferred_element_type=F32) + sgb_ref[0, g]
            sg_o[rows, g * 128:(g + 1) * 128] = (_gelu_tanh(proj[rows, cu]) * mixed).astype(BF16)


def _front(latent, l, x, mod, norm1_g, w_in, qg, kg, seg, sg_w, sgb, rope_tabs, states):
    e = l // 2
    rows = T_LAT if latent else T_CTX
    n_tiles = rows // TF
    off = T_CTX // TF if (latent and x.shape[0] == T) else 0
    tiles_per_seq = LAT_S // TF
    if latent:
        cond_map = lambda i: (l, 1 + i // tiles_per_seq, 0, 0)
    else:
        cond_map = lambda i: (l, 0, 0, 0)
    in_specs = [
        pl.BlockSpec((TF, D), lambda i: (i + off, 0)),
        pl.BlockSpec((1, 1, N_MOD, D), cond_map),
        pl.BlockSpec((1, 1, D), lambda i: (l, 0, 0)),
        pl.BlockSpec((1, D, IN_W), lambda i: (e, 0, 0)),
        pl.BlockSpec((1, QK_W), lambda i: (0, 0)),
        pl.BlockSpec((1, QK_W), lambda i: (0, 0)),
        pl.BlockSpec((QK_W, QK_W), lambda i: (0, 0)),
        pl.BlockSpec((1, SG_GROUPS, SG_CHUNK, SG_CHUNK), lambda i: (e, 0, 0, 0)),
        pl.BlockSpec((1, SG_GROUPS, SG_CHUNK, 128), lambda i: (e, 0, 0, 0)),
    ]
    args = [_hbm(x), mod, norm1_g.reshape(DEPTH, 1, D), w_in, qg, kg, seg, sg_w, sgb]
    tok = lambda: pl.BlockSpec((TF, QK_W), lambda i: (i, 0))
    out_specs = [tok(), tok(), tok(), tok()]
    out_shape = [_hbm_out((rows, QK_W), BF16)] * 4
    aliases = {}
    if latent:
        cos_t, sin_t = rope_tabs
        in_specs += [pl.BlockSpec((TF, QK_W), lambda i: (i % tiles_per_seq, 0))] * 2
        args += [cos_t, sin_t]
    else:
        if states:
            in_specs += [pl.BlockSpec(memory_space=pl.ANY)] * 2
            args += list(states)
            aliases = {9: 4, 10: 5}
        st = lambda: pl.BlockSpec((TF // CTX_S, 1, CTX_S, QK_W), lambda i: (i, e, 0, 0))
        out_specs += [st(), st()]
        out_shape += [_hbm_out((N_CTX_B, DEPTH // 2, CTX_S, QK_W), F32)] * 2
    return pl.pallas_call(
        functools.partial(_front_kernel, latent, len(aliases)),
        grid=(n_tiles,),
        in_specs=in_specs,
        out_specs=out_specs,
        out_shape=out_shape,
        scratch_shapes=[pltpu.VMEM((D, IN_W), BF16)],
        input_output_aliases=aliases,
        compiler_params=_cparams(("arbitrary",), 48),
        name="front_lat" if latent else "front_ctx",
    )(*args)


def _attn_kernel(has_cache, heads, lam0, *refs):
    if has_cache:
        (q_ref, k_ref, v_ref, ck_ref, cv_ref, lq1, lk1, lq2, lk2, sub_ref, o_ref) = refs
    else:
        (q_ref, k_ref, v_ref, lq1, lk1, lq2, lk2, sub_ref, o_ref) = refs

    lane = lax.broadcasted_iota(I32, (1, VAL_DIM), 1)
    lam = (jnp.exp(jnp.sum(lq1[0] * lk1[0], axis=-1, keepdims=True))
           - jnp.exp(jnp.sum(lq2[0] * lk2[0], axis=-1, keepdims=True)) + lam0)

    for h in range(heads):
        hs = slice(h * VAL_DIM, (h + 1) * VAL_DIM)
        qf = q_ref[:, hs].astype(F32)
        qc = [jnp.where(lane < HEAD_DIM, qf, 0.0).astype(BF16),
              jnp.where(lane >= HEAD_DIM, qf, 0.0).astype(BF16)]
        keys = [k_ref[:, hs]]
        vals = [v_ref[:, hs]]
        if has_cache:
            keys.append(ck_ref[0, 0].astype(BF16))
            vals.append(cv_ref[0, 0].astype(BF16))

        probs = []
        for c in range(2):
            s = [lax.dot_general(qc[c], kk, NT_DIMS, preferred_element_type=F32) for kk in keys]
            m = s[0].max(axis=-1, keepdims=True)
            for sx in s[1:]:
                m = jnp.maximum(m, sx.max(axis=-1, keepdims=True))
            p = [jnp.exp(sx - m) for sx in s]
            den = p[0].sum(axis=-1, keepdims=True)
            for px in p[1:]:
                den = den + px.sum(axis=-1, keepdims=True)
            probs.append((p, den))

        r0 = 1.0 / probs[0][1]
        r1 = lam / probs[1][1]
        o = None
        for idx in range(len(keys)):
            a = (probs[0][0][idx] * r0 - probs[1][0][idx] * r1).astype(BF16)
            t = jnp.dot(a, vals[idx], preferred_element_type=F32)
            o = t if o is None else o + t
        ms = jnp.mean(o * o, axis=-1, keepdims=True)
        o_ref[:, hs] = ((o * lax.rsqrt(ms + EPS) * sub_ref[0]) * (1.0 - lam0)).astype(BF16)


def _attention(has_cache, l, q, k, v, cache_k, cache_v, lam_q1, lam_k1, lam_q2, lam_k2, subln_g):
    e = l // 2
    lam0 = 0.8 - 0.6 * math.exp(-0.3 * l)
    n_b, s_len = (N_LAT_B, LAT_S) if has_cache else (N_CTX_B, CTX_S)
    heads = 1 if has_cache else N_HEADS
    width = heads * VAL_DIM
    nq = s_len // TM
    kv = lambda: pl.BlockSpec((s_len, width), lambda b, h, qi: (b, h))
    par = lambda: pl.BlockSpec((1, 1, HEAD_DIM), lambda b, h, qi: (e, 0, 0))
    in_specs = [pl.BlockSpec((TM, width), lambda b, h, qi: (b * nq + qi, h)), kv(), kv()]
    args = [_hbm(q), _hbm(k), _hbm(v)]
    if has_cache:
        cs = lambda: pl.BlockSpec((1, 1, PAST, VAL_DIM), lambda b, h, qi: (b, e, 0, h))
        in_specs += [cs(), cs()]
        args += [cache_k.reshape(N_LAT_B, DEPTH // 2, PAST, QK_W),
                 cache_v.reshape(N_LAT_B, DEPTH // 2, PAST, MIX_A)]
    in_specs += [par(), par(), par(), par(),
                 pl.BlockSpec((1, 1, VAL_DIM), lambda b, h, qi: (e, 0, 0))]
    r3 = lambda a: a.reshape(DEPTH // 2, 1, -1)
    args += [r3(lam_q1), r3(lam_k1), r3(lam_q2), r3(lam_k2), r3(subln_g)]
    return pl.pallas_call(
        functools.partial(_attn_kernel, has_cache, heads, lam0),
        grid=(n_b, N_HEADS // heads, nq),
        in_specs=in_specs,
        out_specs=pl.BlockSpec((TM, width), lambda b, h, qi: (b * nq + qi, h)),
        out_shape=_hbm_out((n_b * s_len, MIX_A), BF16),
        compiler_params=_cparams(("arbitrary", "arbitrary", "arbitrary"), 32),
        name="attn_lat" if has_cache else "attn_ctx",
    )(*args)


def _route_tail(sub_tile, x_new, m_ref, n2_ref, rwt_ref, rbb_ref, tri_ref,
                h2_o, lp_o, rf_o, nch_o, loff_o, gb_o, cnt_o, carry):
    tok = slice(sub_tile * TM, (sub_tile + 1) * TM)
    if sub_tile == 0:
        @pl.when(pl.program_id(0) == 0)
        def _():
            carry[...] = jnp.zeros_like(carry)

    h2 = _modnorm(x_new, n2_ref[0], m_ref[0, 0, 4:5, :], m_ref[0, 0, 3:4, :])

    hh = h2.astype(BF16)
    h2_o[tok, :] = hh
    hl = (h2 - hh.astype(F32)).astype(BF16)
    rw = rwt_ref[...]
    wh = rw.astype(BF16)
    wl = (rw - wh.astype(F32)).astype(BF16)
    dg = lambda a, b: lax.dot_general(a, b, NT_DIMS, preferred_element_type=F32)
    logits = dg(wh, hh) + dg(wh, hl) + dg(wl, hh)
    score = _sigmoid(logits)
    sel = score + rbb_ref[...]

    neg_inf = jnp.full((1, TM), -jnp.inf, F32)
    grp = []
    for g in range(N_GROUPS):
        s_rows = [sel[PER_GROUP * g + j:PER_GROUP * g + j + 1, :] for j in range(PER_GROUP)]
        c_rows = [score[PER_GROUP * g + j:PER_GROUP * g + j + 1, :] for j in range(PER_GROUP)]
        f_idx = jnp.zeros((1, TM), I32)
        f_val, f_sc = s_rows[0], c_rows[0]
        for j in range(1, PER_GROUP):
            better = s_rows[j] > f_val
            f_idx = jnp.where(better, j, f_idx)
            f_val = jnp.where(better, s_rows[j], f_val)
            f_sc = jnp.where(better, c_rows[j], f_sc)
        s_idx = jnp.zeros((1, TM), I32)
        s_val, s_sc = neg_inf, jnp.zeros((1, TM), F32)
        for j in range(PER_GROUP):
            better = (f_idx != j) & (s_rows[j] > s_val)
            s_idx = jnp.where(better, j, s_idx)
            s_val = jnp.where(better, s_rows[j], s_val)
            s_sc = jnp.where(better, c_rows[j], s_sc)
        grp.append((f_val + s_val, PER_GROUP * g + f_idx, PER_GROUP * g + s_idx, f_sc, s_sc))

    best, e0, e1, c0, c1 = grp[0]
    for g in range(1, N_GROUPS):
        better = grp[g][0] > best
        best = jnp.where(better, grp[g][0], best)
        e0 = jnp.where(better, grp[g][1], e0)
        e1 = jnp.where(better, grp[g][2], e1)
        c0 = jnp.where(better, grp[g][3], c0)
        c1 = jnp.where(better, grp[g][4], c1)
    den = c0 + c1
    w0 = c0 / den
    w1 = c1 / den

    sub = lax.broadcasted_iota(I32, (N_EXPERTS, TM), 0)
    hit0 = sub == jnp.broadcast_to(e0, (N_EXPERTS, TM))
    hit1 = sub == jnp.broadcast_to(e1, (N_EXPERTS, TM))
    onehot = jnp.where(hit0 | hit1, 1.0, 0.0)
    prefix = jnp.dot(onehot.astype(BF16), tri_ref[...], preferred_element_type=F32)
    count = jnp.sum(onehot, axis=1, keepdims=True)
    padded = jnp.floor((count + (ROW_CHUNK - 1)) * (1.0 / ROW_CHUNK)) * ROW_CHUNK

    sub1 = lax.broadcasted_iota(I32, (N_EXPERTS, 1), 0)
    run = jnp.zeros((1, 1), F32)
    local_off = jnp.zeros((N_EXPERTS, 1), F32)
    for e in range(N_EXPERTS):
        local_off = jnp.where(sub1 == e, run, local_off)
        run = run + padded[e:e + 1, :]
    place = prefix + local_off
    lpos0 = jnp.sum(jnp.where(hit0, place, 0.0), axis=0, keepdims=True).astype(I32)
    lpos1 = jnp.sum(jnp.where(hit1, place, 0.0), axis=0, keepdims=True).astype(I32)

    lanes = lambda col: jnp.broadcast_to(col, (N_EXPERTS, 128)).astype(I32)
    nch_o[sub_tile] = lanes(padded * (1.0 / ROW_CHUNK))
    loff_o[sub_tile] = lanes(local_off)
    gb_o[sub_tile] = carry[...].astype(I32)
    carry[...] = carry[...] + padded
    cnt_o[...] = carry[...].astype(I32)

    sub8 = lax.broadcasted_iota(I32, (8, TM), 0)
    bc = lambda r: jnp.broadcast_to(r, (8, TM))
    lp_o[:, tok] = jnp.where(sub8 == 0, bc(lpos0), jnp.where(sub8 == 1, bc(lpos1), 0))
    rf_o[:, tok] = jnp.where(sub8 == 0, bc(w0), jnp.where(sub8 == 1, bc(w1), 0.0))


def _tail_specs(l, sub_tiles):
    rows = sub_tiles * TM
    in_specs = [
        pl.BlockSpec((1, 1, D), lambda i: (l, 0, 0)),
        pl.BlockSpec((N_EXPERTS, D), lambda i: (0, 0)),
        pl.BlockSpec((N_EXPERTS, TM), lambda i: (0, 0)),
        pl.BlockSpec((TM, TM), lambda i: (0, 0)),
    ]
    per_tile = lambda: pl.BlockSpec((sub_tiles, N_EXPERTS, 128), lambda i: (i, 0, 0))
    out_specs = [
        pl.BlockSpec((rows, D), lambda i: (i, 0)),
        pl.BlockSpec((8, rows), lambda i: (0, i)),
        pl.BlockSpec((8, rows), lambda i: (0, i)),
        per_tile(), per_tile(), per_tile(),
        pl.BlockSpec((N_EXPERTS, 128), lambda i: (0, 0)),
    ]
    out_shape = [
        _hbm_out((T, D), BF16),
        jax.ShapeDtypeStruct((8, T), I32),
        jax.ShapeDtypeStruct((8, T), F32),
        jax.ShapeDtypeStruct((N_TILES, N_EXPERTS, 128), I32),
        jax.ShapeDtypeStruct((N_TILES, N_EXPERTS, 128), I32),
        jax.ShapeDtypeStruct((N_TILES, N_EXPERTS, 128), I32),
        jax.ShapeDtypeStruct((N_EXPERTS, 128), I32),
    ]
    scratch = [pltpu.VMEM((N_EXPERTS, 128), F32)]
    return in_specs, out_specs, out_shape, scratch


def _out_kernel(split_x, *refs):
    x_refs, refs = refs[:1 + split_x], refs[1 + split_x:]
    (m_ref, ac_ref, al_ref, sc_ref, sl_ref, w_ref, n2_ref, rwt_ref, rbb_ref, tri_ref,
     x_o, *tail_refs) = refs
    *tail_outs, wbf, carry = tail_refs
    i = pl.program_id(0)

    @pl.when(i == 0)
    def _():
        wbf[...] = w_ref[0].astype(BF16)

    is_ctx = i < T_CTX // TO
    x = jnp.where(is_ctx, x_refs[0][...], x_refs[1][...]) if split_x else x_refs[0][...]
    att = jnp.where(is_ctx, ac_ref[...], al_ref[...])
    sgo = jnp.where(is_ctx, sc_ref[...], sl_ref[...])
    out = (jnp.dot(att, wbf[0:MIX_A, :], preferred_element_type=F32)
           + jnp.dot(sgo, wbf[MIX_A:MIX_A + MIX_B, :], preferred_element_type=F32))
    x_new = x + m_ref[0, 0, 2:3, :] * out
    x_o[...] = x_new
    for sub_tile in range(TO // TM):
        _route_tail(sub_tile, x_new[sub_tile * TM:(sub_tile + 1) * TM, :],
                    m_ref, n2_ref, rwt_ref, rbb_ref, tri_ref, *tail_outs, carry)


def _out_proj(l, x, mod, att_c, att_l, sg_c, sg_l, w_out, norm2_g, rwt, rbb, tri):
    e = l // 2
    t_in, t_out, t_shape, t_scratch = _tail_specs(l, TO // TM)
    n_ctx = T_CTX // TO
    ctx = lambda w: pl.BlockSpec((TO, w), lambda i: (jnp.minimum(i, n_ctx - 1), 0))
    lat = lambda w: pl.BlockSpec((TO, w), lambda i: (jnp.maximum(i - n_ctx, 0), 0))
    split_x = isinstance(x, tuple)
    if split_x:
        x_specs, x_args, aliases = [ctx(D), lat(D)], [_hbm(x[0]), _hbm(x[1])], {}
    else:
        x_specs, x_args, aliases = [pl.BlockSpec((TO, D), lambda i: (i, 0))], [_hbm(x)], {0: 0}
    in_specs = x_specs + [
        pl.BlockSpec((1, 1, N_MOD, D), lambda i: (l, _cond_of_tile(i * (TO // TM)), 0, 0)),
        ctx(MIX_A), lat(MIX_A), ctx(MIX_B), lat(MIX_B),
        pl.BlockSpec((1, D, D), lambda i: (e, 0, 0)),
    ] + t_in
    return pl.pallas_call(
        functools.partial(_out_kernel, split_x),
        grid=(T // TO,),
        in_specs=in_specs,
        out_specs=[pl.BlockSpec((TO, D), lambda i: (i, 0))] + t_out,
        out_shape=[_hbm_out((T, D), F32)] + t_shape,
        scratch_shapes=[pltpu.VMEM((D, D), BF16)] + t_scratch,
        input_output_aliases=aliases,
        compiler_params=_cparams(("arbitrary",), 40),
        name="out_proj_route",
    )(*x_args, mod, _hbm(att_c), _hbm(att_l), _hbm(sg_c), _hbm(sg_l), w_out,
      norm2_g.reshape(DEPTH, 1, D), rwt, rbb, tri)


def _pool_kernel(x_ref, xp_ref, xn_ref, m_ref, n1_ref, pw_ref, ps_ref,
                 n2_ref, rwt_ref, rbb_ref, tri_ref,
                 x_o, *tail_refs):
    *tail_outs, carry = tail_refs
    i = pl.program_id(0)
    is_lat = i >= N_CTX_TILES
    ti = jnp.where(is_lat, (i - N_CTX_TILES) % LAT_TILES_PER_SEQ, 0)
    last_ti = jnp.where(is_lat, LAT_TILES_PER_SEQ - 1, 0)
    seq_len = jnp.where(is_lat, LAT_S, CTX_S)

    g1n, sc1, sh1 = n1_ref[0], m_ref[0, 0, 1:2, :], m_ref[0, 0, 0:1, :]
    x = x_ref[...]
    h = _modnorm(x, g1n, sc1, sh1)
    hp = jnp.where(ti == 0, 0.0, _modnorm(xp_ref[...], g1n, sc1, sh1))
    hn = jnp.where(ti == last_ti, 0.0, _modnorm(xn_ref[...], g1n, sc1, sh1))
    hcat = jnp.concatenate([hp, h, hn], axis=0)
    n_rows = TM + 2 * POOL_HALO

    t_pos = ti * TM + lax.broadcasted_iota(I32, (TM, 1), 0)
    outs = []
    for g, w in enumerate(POOL_WINDOWS):
        cols = slice(g * POOL_DIM, (g + 1) * POOL_DIM)
        acc = hcat[:, cols]
        acc = acc + pltpu.roll(acc, 1, 0)
        step = 1
        while 2 * step < w:
            acc = pltpu.roll(acc, step, 0) + pltpu.roll(acc, n_rows - step, 0)
            step *= 2
        lo = jnp.maximum(t_pos - w // 2, 0)
        hi = jnp.minimum(t_pos - w // 2 + w, seq_len)
        mean = acc[POOL_HALO:POOL_HALO + TM, :] / (hi - lo).astype(F32)
        dlt = (mean - h[:, cols]).astype(BF16)
        outs.append(jnp.dot(dlt, pw_ref[0, g].astype(BF16), preferred_element_type=F32))
    out = jnp.concatenate(outs, axis=1) * ps_ref[0]
    x_new = x + m_ref[0, 0, 2:3, :] * out
    x_o[...] = x_new
    _route_tail(0, x_new, m_ref, n2_ref, rwt_ref, rbb_ref, tri_ref, *tail_outs, carry)


def _pool_mixer(l, x, mod, norm1_g, pool_w, pool_scale, norm2_g, rwt, rbb, tri):
    o = l // 2
    t_in, t_out, t_shape, t_scratch = _tail_specs(l, 1)
    halo_blocks = TM // POOL_HALO
    in_specs = [
        pl.BlockSpec((TM, D), lambda i: (i, 0)),
        pl.BlockSpec((POOL_HALO, D), lambda i: (jnp.maximum(i * halo_blocks - 1, 0), 0)),
        pl.BlockSpec((POOL_HALO, D), lambda i: (jnp.minimum((i + 1) * halo_blocks, T // POOL_HALO - 1), 0)),
        pl.BlockSpec((1, 1, N_MOD, D), lambda i: (l, _cond_of_tile(i), 0, 0)),
        pl.BlockSpec((1, 1, D), lambda i: (l, 0, 0)),
        pl.BlockSpec((1, len(POOL_WINDOWS), POOL_DIM, POOL_DIM), lambda i: (o, 0, 0, 0)),
        pl.BlockSpec((1, 1, D), lambda i: (o, 0, 0)),
    ] + t_in
    return pl.pallas_call(
        _pool_kernel,
        grid=(N_TILES,),
        in_specs=in_specs,
        out_specs=[pl.BlockSpec((TM, D), lambda i: (i, 0))] + t_out,
        out_shape=[_hbm_out((T, D), F32)] + t_shape,
        scratch_shapes=t_scratch,
        compiler_params=_cparams(("arbitrary",), 32),
        name="pool_route",
    )(_hbm(x), _hbm(x), _hbm(x), mod, norm1_g.reshape(DEPTH, 1, D), pool_w,
      pool_scale.reshape(DEPTH // 2, 1, D),
      norm2_g.reshape(DEPTH, 1, D), rwt, rbb, tri)


def _start_run_copies(tile, nch_ref, loff_ref, gb_ref, make_copy):
    for e in range(N_EXPERTS):
        lo = loff_ref[tile * N_EXPERTS + e]
        gb = gb_ref[tile * N_EXPERTS + e] + e * EXPERT_CAP

        def issue(c, carry, lo=lo, gb=gb):
            make_copy(pl.multiple_of(lo + c * ROW_CHUNK, ROW_CHUNK),
                      pl.multiple_of(gb + c * ROW_CHUNK, ROW_CHUNK)).start()
            return carry

        lax.fori_loop(0, nch_ref[tile * N_EXPERTS + e], issue, 0)


def _wait_run_copies(tile, nch_ref, make_copy):
    total = 0
    for e in range(N_EXPERTS):
        total = total + nch_ref[tile * N_EXPERTS + e]

    def drain(c, carry):
        make_copy(0, 0).wait()
        return carry

    lax.fori_loop(0, total, drain, 0)


def _dispatch_kernel(nch_ref, loff_ref, gb_ref, h_ref, lp_ref, xs_ref, sbuf, sems):
    i = pl.program_id(0)
    slot = i % 2
    r = lax.broadcasted_iota(I32, (SORTED_ROWS, TM), 0)
    hit = (r == lp_ref[0:1, :]) | (r == lp_ref[1:2, :])
    perm = jnp.where(hit, 1.0, 0.0).astype(BF16)
    sbuf[slot] = jnp.dot(perm, h_ref[...], preferred_element_type=F32).astype(BF16)

    def copies_of(s):
        def make_copy(local_row, global_row):
            return pltpu.make_async_copy(sbuf.at[s, pl.ds(local_row, ROW_CHUNK)],
                                         xs_ref.at[pl.ds(global_row, ROW_CHUNK)], sems.at[s])
        return make_copy

    @pl.when(i > 0)
    def _():
        _wait_run_copies(i - 1, nch_ref, copies_of(1 - slot))

    _start_run_copies(i, nch_ref, loff_ref, gb_ref, copies_of(slot))

    @pl.when(i == N_TILES - 1)
    def _():
        _wait_run_copies(i, nch_ref, copies_of(slot))


def _dispatch(layout, h2, lp):
    return pl.pallas_call(
        _dispatch_kernel,
        grid_spec=pltpu.PrefetchScalarGridSpec(
            num_scalar_prefetch=3,
            grid=(N_TILES,),
            in_specs=[pl.BlockSpec((TM, D), lambda i, *_: (i, 0)),
                      pl.BlockSpec((8, TM), lambda i, *_: (0, i))],
            out_specs=pl.BlockSpec(memory_space=pl.ANY),
            scratch_shapes=[pltpu.VMEM((2, SORTED_ROWS, D), BF16), pltpu.SemaphoreType.DMA((2,))],
        ),
        out_shape=_hbm_out((N_EXPERTS * EXPERT_CAP, D), BF16),
        compiler_params=_cparams(("arbitrary",), 16),
        name="moe_dispatch",
    )(*layout, _hbm(h2), lp)


def _moe_kernel(l, rb_ref, e_ref, nv_ref, first_ref, ord_ref, nxt_ref, xs_ref, wg_hbm, wu_hbm, wd_hbm, y_ref,
                wgf, wuf, wdf, wgb, wub, wdb, sems):
    i = pl.program_id(0)
    n_valid = nv_ref[i]

    def weight_copies(e, slot):
        return (pltpu.make_async_copy(wg_hbm.at[l, e], wgf.at[slot], sems.at[slot, 0]),
                pltpu.make_async_copy(wu_hbm.at[l, e], wuf.at[slot], sems.at[slot, 1]),
                pltpu.make_async_copy(wd_hbm.at[l, e], wdf.at[slot], sems.at[slot, 2]))

    @pl.when(first_ref[i] == 1)
    def _():
        slot = ord_ref[i] % 2

        @pl.when(ord_ref[i] == 0)
        def _():
            for cp in weight_copies(e_ref[i], slot):
                cp.start()

        for cp in weight_copies(e_ref[i], slot):
            cp.wait()

        @pl.when(nxt_ref[i] >= 0)
        def _():
            for cp in weight_copies(nxt_ref[i], 1 - slot):
                cp.start()

        wgb[...] = wgf[slot].astype(BF16)
        wub[...] = wuf[slot].astype(BF16)
        wdb[...] = wdf[slot].astype(BF16)

    @pl.when(n_valid > 0)
    def _():
        row = lax.broadcasted_iota(I32, (TE, 1), 0)
        x = jnp.where(row < n_valid, xs_ref[...], jnp.zeros((), BF16))
        gate = jnp.dot(x, wgb[...], preferred_element_type=F32)
        up = jnp.dot(x, wub[...], preferred_element_type=F32)
        act = (gate * _sigmoid(gate) * up).astype(BF16)
        y_ref[...] = jnp.dot(act, wdb[...], preferred_element_type=F32).astype(BF16)


def _experts(l, tile_map, xs, w_gate, w_up, w_down):
    n_pre = len(tile_map)
    blk = lambda i, rb, *_: (rb[i], 0)
    return pl.pallas_call(
        functools.partial(_moe_kernel, l),
        grid_spec=pltpu.PrefetchScalarGridSpec(
            num_scalar_prefetch=n_pre,
            grid=(MOE_TILES,),
            in_specs=[
                pl.BlockSpec((TE, D), blk),
                pl.BlockSpec(memory_space=pl.ANY),
                pl.BlockSpec(memory_space=pl.ANY),
                pl.BlockSpec(memory_space=pl.ANY),
            ],
            out_specs=pl.BlockSpec((TE, D), blk),
            scratch_shapes=[pltpu.VMEM((2, D, D_FF), F32), pltpu.VMEM((2, D, D_FF), F32),
                            pltpu.VMEM((2, D_FF, D), F32),
                            pltpu.VMEM((D, D_FF), BF16), pltpu.VMEM((D, D_FF), BF16),
                            pltpu.VMEM((D_FF, D), BF16),
                            pltpu.SemaphoreType.DMA((2, 3))],
        ),
        out_shape=_hbm_out((N_EXPERTS * EXPERT_CAP, D), BF16),
        compiler_params=_cparams(("arbitrary",), 40),
        name="moe_experts",
    )(*tile_map, _hbm(xs), w_gate, w_up, w_down)


def _combine_kernel(split_out, nch_ref, loff_ref, gb_ref, x_ref, m_ref, lpt_ref, w_ref, y_ref, *refs):
    *o_refs, ybuf, sems = refs
    i = pl.program_id(0)
    slot = i % 2

    def copies_of(s):
        def make_copy(local_row, global_row):
            return pltpu.make_async_copy(y_ref.at[pl.ds(global_row, ROW_CHUNK)],
                                         ybuf.at[s, pl.ds(local_row, ROW_CHUNK)], sems.at[s])
        return make_copy

    @pl.when(i == 0)
    def _():
        ybuf[...] = jnp.zeros_like(ybuf)
        _start_run_copies(i, nch_ref, loff_ref, gb_ref, copies_of(slot))

    @pl.when(i + 1 < N_TILES)
    def _():
        _start_run_copies(i + 1, nch_ref, loff_ref, gb_ref, copies_of(1 - slot))

    _wait_run_copies(i, nch_ref, copies_of(slot))

    col = lax.broadcasted_iota(I32, (TM, SORTED_ROWS), 1)
    pick0 = jnp.where(col == lpt_ref[:, 0:1], 1.0, 0.0).astype(BF16)
    pick1 = jnp.where(col == lpt_ref[:, 1:2], 1.0, 0.0).astype(BF16)
    y0 = jnp.dot(pick0, ybuf[slot], preferred_element_type=F32)
    y1 = jnp.dot(pick1, ybuf[slot], preferred_element_type=F32)
    moe = w_ref[:, 0:1] * y0 + w_ref[:, 1:2] * y1
    res = x_ref[...] + m_ref[0, 0, 5:6, :] * moe
    if split_out:
        @pl.when(i < N_CTX_TILES)
        def _():
            o_refs[0][...] = res

        @pl.when(i >= N_CTX_TILES)
        def _():
            o_refs[1][...] = res
    else:
        o_refs[0][...] = res


def _combine(l, layout, x, mod, lpt, wcol, y, split_out):
    if split_out:
        out_specs = [pl.BlockSpec((TM, D), lambda i, *_: (jnp.minimum(i, N_CTX_TILES - 1), 0)),
                     pl.BlockSpec((TM, D), lambda i, *_: (jnp.maximum(i - N_CTX_TILES, 0), 0))]
        out_shape = [jax.ShapeDtypeStruct((T_CTX, D), F32), jax.ShapeDtypeStruct((T_LAT, D), F32)]
        aliases = {}
    else:
        out_specs = pl.BlockSpec((TM, D), lambda i, *_: (i, 0))
        out_shape = _hbm_out((T, D), F32)
        aliases = {3: 0}
    return pl.pallas_call(
        functools.partial(_combine_kernel, split_out),
        grid_spec=pltpu.PrefetchScalarGridSpec(
            num_scalar_prefetch=3,
            grid=(N_TILES,),
            in_specs=[
                pl.BlockSpec((TM, D), lambda i, *_: (i, 0)),
                pl.BlockSpec((1, 1, N_MOD, D), lambda i, *_: (l, _cond_of_tile(i), 0, 0)),
                pl.BlockSpec((TM, 8), lambda i, *_: (i, 0)),
                pl.BlockSpec((TM, 8), lambda i, *_: (i, 0)),
                pl.BlockSpec(memory_space=pl.ANY),
            ],
            out_specs=out_specs,
            scratch_shapes=[pltpu.VMEM((2, SORTED_ROWS, D), BF16), pltpu.SemaphoreType.DMA((2,))],
        ),
        out_shape=out_shape,
        input_output_aliases=aliases,
        compiler_params=_cparams(("arbitrary",), 24),
        name="moe_combine",
    )(*layout, _hbm(x), mod, lpt, wcol, _hbm(y))


def _tile_map(cnt):
    cnt = cnt[:, 0]
    ids = jnp.arange(N_EXPERTS, dtype=I32)
    n_tiles = (cnt + TE - 1) // TE
    cum = jnp.cumsum(n_tiles)
    total = cum[-1]
    i = jnp.arange(MOE_TILES, dtype=I32)
    ii = jnp.minimum(i, total - 1)
    e = jnp.sum((ii[:, None] >= cum[None, :]).astype(I32), axis=1)
    onehot = (e[:, None] == ids[None, :]).astype(I32)
    pick = lambda v: jnp.sum(onehot * v[None, :], axis=1)
    j = ii - pick(cum - n_tiles)
    active = i < total
    n_valid = jnp.where(active, jnp.clip(pick(cnt) - j * TE, 0, TE), 0)
    first = (active & (j == 0)).astype(I32)
    nonempty = n_tiles > 0
    ordinal = jnp.cumsum(nonempty.astype(I32)) - 1
    later = jnp.where(nonempty[None, :] & (ids[None, :] > ids[:, None]), ids[None, :], N_EXPERTS)
    nxt = jnp.min(later, axis=1)
    nxt = jnp.where(nxt == N_EXPERTS, -1, nxt)
    as_i32 = lambda v: v.astype(I32)
    return tuple(map(as_i32, (e * CAP_TILES + j, e, n_valid, first, pick(ordinal), pick(nxt))))


def _moe(l, x, tail, mod, w_gate, w_up, w_down, split_out):
    h2, lp, rf, nch, loff, gb, cnt = tail
    layout = tuple(a[:, :, 0].reshape(N_TILES * N_EXPERTS) for a in (nch, loff, gb))
    xs = _dispatch(layout, h2, lp)
    y = _experts(l, _tile_map(cnt), xs, w_gate, w_up, w_down)
    return _combine(l, layout, x, mod, lp.T, rf.T, y, split_out)


def _rope_tables():
    t = jnp.arange(LAT_S)
    row = (t // GRID_W).astype(F32)
    col = (t % GRID_W).astype(F32)
    nf = HEAD_DIM // 4
    inv = ROPE_BASE ** (-jnp.arange(nf, dtype=F32) / nf)
    ar = row[:, None] * inv[None, :]
    ac = col[:, None] * inv[None, :]
    cos = jnp.concatenate([jnp.cos(ar), jnp.cos(ar), jnp.cos(ac), jnp.cos(ac)], axis=1)
    sin = jnp.concatenate([-jnp.sin(ar), jnp.sin(ar), -jnp.sin(ac), jnp.sin(ac)], axis=1)
    reps = QK_W // HEAD_DIM
    return jnp.tile(cos, (1, reps)), jnp.tile(sin, (1, reps))


def kernel(x_prompt, x_sample, cache_k, cache_v, c, c_ctx, norm1_g, norm2_g, w_ada, b_ada, w_in, w_out,
           q_norm_g, k_norm_g, lam_q1, lam_k1, lam_q2, lam_k2, subln_g, sg_w, sg_b, pool_w, pool_scale,
           router_w, router_b, w_gate, w_up, w_down):
    x = (x_prompt.reshape(T_CTX, D), x_sample.reshape(T_LAT, D))
    cond = jnp.concatenate([c_ctx[None, :], c], axis=0)
    mod = _modulation(cond, w_ada, b_ada)

    seg_id = np.arange(QK_W) // HEAD_DIM
    seg = jnp.asarray((seg_id[:, None] == seg_id[None, :]) / HEAD_DIM, dtype=BF16)
    tri = jnp.asarray(np.arange(TM)[:, None] < np.arange(TM)[None, :], dtype=BF16)
    rope_tabs = _rope_tables()
    rwt = router_w.T
    rbb = jnp.broadcast_to(router_b[:, None], (N_EXPERTS, TM))
    states = None

    for l in range(DEPTH):
        if l % 2 == 0:
            e = l // 2
            qg = jnp.tile(q_norm_g[e], QK_W // HEAD_DIM)[None, :]
            kg = jnp.tile(k_norm_g[e], QK_W // HEAD_DIM)[None, :]
            sgb = jnp.broadcast_to(sg_b[:, :, :, None], (DEPTH // 2, SG_GROUPS, SG_CHUNK, 128))
            x_c, x_l = x if isinstance(x, tuple) else (x, x)
            common = (mod, norm1_g, w_in, qg, kg, seg, sg_w, sgb)
            q_c, k_c, v_c, sg_c, *states = _front(False, l, x_c, *common, None, states)
            q_l, k_l, v_l, sg_l = _front(True, l, x_l, *common, rope_tabs, None)
            lam = (lam_q1, lam_k1, lam_q2, lam_k2, subln_g)
            att_c = _attention(False, l, q_c, k_c, v_c, None, None, *lam)
            att_l = _attention(True, l, q_l, k_l, v_l, cache_k, cache_v, *lam)
            x, *tail = _out_proj(l, x, mod, att_c, att_l, sg_c, sg_l, w_out, norm2_g, rwt, rbb, tri)
        else:
            x, *tail = _pool_mixer(l, x, mod, norm1_g, pool_w, pool_scale, norm2_g, rwt, rbb, tri)
        x = _moe(l, x, tail, mod, w_gate, w_up, w_down, split_out=(l == DEPTH - 1))
    state_k, state_v = states

    y_prompt = x[0].reshape(N_CTX_B, CTX_S, D)
    y_sample = x[1].reshape(N_LAT_B, LAT_S, D)
    state_k = state_k.reshape(N_CTX_B, DEPTH // 2, CTX_S, N_HEADS, 2, HEAD_DIM)
    state_v = state_v.reshape(N_CTX_B, DEPTH // 2, CTX_S, N_HEADS, VAL_DIM)
    return y_prompt, y_sample, state_k, state_v
```

```python
import functools
import math

import jax
import jax.numpy as jnp
import numpy as np
from jax import lax
from jax.experimental import pallas as pl
from jax.experimental.pallas import tpu as pltpu

F32 = jnp.float32
BF16 = jnp.bfloat16
I32 = jnp.int32

D = 1024
N_CTX_B, CTX_S = 16, 256
N_LAT_B, LAT_S = 2, 1024
PAST = 512
DEPTH = 4
T_CTX = N_CTX_B * CTX_S
T_LAT = N_LAT_B * LAT_S
T = T_CTX + T_LAT
GRID_W = 64
N_HEADS = 4
HEAD_DIM = 64
VAL_DIM = 128
QK_W = 512
MIX_A = 512
MIX_B = 512
IN_W = 2560
SG_GROUPS = 4
SG_CHUNK = 128
POOL_WINDOWS = (2, 4, 8, 16)
POOL_DIM = 256
POOL_HALO = 8
N_EXPERTS = 16
N_GROUPS = 4
PER_GROUP = 4
D_FF = 512
N_MOD = 6
N_COND = 1 + N_LAT_B
EPS = 1e-6
ROPE_BASE = 10000.0

TM = 256
N_TILES = T // TM
N_CTX_TILES = T_CTX // TM
LAT_TILES_PER_SEQ = LAT_S // TM
ROW_CHUNK = 16
PAD_PER_RUN = ROW_CHUNK - 1
SORTED_ROWS = -(-(2 * TM + N_EXPERTS * PAD_PER_RUN) // 128) * 128
TE = 512
TF = 512
TO = 512
CAP_TILES = -(-(T + N_TILES * PAD_PER_RUN) // TE)
EXPERT_CAP = CAP_TILES * TE
MOE_TILES = (2 * T + N_EXPERTS * N_TILES * PAD_PER_RUN) // TE + N_EXPERTS
MOD_TN = 1024
MIB = 1024 * 1024

NT_DIMS = (((1,), (1,)), ((), ()))


def _cparams(sem, vmem_mib):
    return pltpu.CompilerParams(dimension_semantics=sem, vmem_limit_bytes=vmem_mib * MIB)


def _hbm(x):
    try:
        return pltpu.with_memory_space_constraint(x, pltpu.HBM)
    except ValueError:
        return x


def _hbm_out(shape, dtype):
    return pltpu.HBM(shape, dtype)


def _sigmoid(x):
    return 1.0 / (1.0 + jnp.exp(-x))


def _gelu_tanh(x):
    c = math.sqrt(2.0 / math.pi)
    return x * (0.5 * (1.0 + jnp.tanh(c * (x + 0.044715 * (x * x * x)))))


def _modnorm(x, g, scale, shift):
    ms = jnp.mean(x * x, axis=-1, keepdims=True)
    return (x * lax.rsqrt(ms + EPS) * g) * (1.0 + scale) + shift


def _cond_of_tile(i):
    return jnp.where(i < N_CTX_TILES, 0, 1 + (i - N_CTX_TILES) // LAT_TILES_PER_SEQ)


def _mod_kernel(c_ref, w_ref, b_ref, o_ref):
    c = c_ref[...]
    s = (c * _sigmoid(c)).astype(BF16)
    o_ref[0] = jnp.dot(s, w_ref[0].astype(BF16), preferred_element_type=F32) + b_ref[0]


def _modulation(cond, w_ada, b_ada):
    cond8 = jnp.concatenate([cond, jnp.zeros((8 - N_COND, D), F32)], axis=0)
    out = pl.pallas_call(
        _mod_kernel,
        grid=(DEPTH, N_MOD * D // MOD_TN),
        in_specs=[
            pl.BlockSpec((8, D), lambda l, n: (0, 0)),
            pl.BlockSpec((1, D, MOD_TN), lambda l, n: (l, 0, n)),
            pl.BlockSpec((1, 1, MOD_TN), lambda l, n: (l, 0, n)),
        ],
        out_specs=pl.BlockSpec((1, 8, MOD_TN), lambda l, n: (l, 0, n)),
        out_shape=jax.ShapeDtypeStruct((DEPTH, 8, N_MOD * D), F32),
        compiler_params=_cparams(("arbitrary", "arbitrary"), 24),
        name="adaln_mod",
    )(cond8, w_ada, b_ada.reshape(DEPTH, 1, N_MOD * D))
    return out.reshape(DEPTH, 8, N_MOD, D)


def _front_kernel(latent, n_state_in, *refs):
    (x_ref, m_ref, n1_ref, w_ref, qg_ref, kg_ref, seg_ref, sgw_ref, sgb_ref), refs = refs[:9], refs[9:]
    if latent:
        cos_ref, sin_ref, q_o, k_o, v_o, sg_o, wbf = refs
    else:
        q_o, k_o, v_o, sg_o, sk_o, sv_o, wbf = refs[n_state_in:]

    @pl.when(pl.program_id(0) == 0)
    def _():
        for c0 in range(0, IN_W, 512):
            wbf[:, c0:c0 + 512] = w_ref[0, :, c0:c0 + 512].astype(BF16)

    h = _modnorm(x_ref[...], n1_ref[0], m_ref[0, 0, 1:2, :], m_ref[0, 0, 0:1, :])
    proj = jnp.dot(h.astype(BF16), wbf[...], preferred_element_type=F32)

    def qk_norm(z, g):
        ms = jnp.dot((z * z).astype(BF16), seg_ref[...], preferred_element_type=F32)
        return z * lax.rsqrt(ms + EPS) * g

    q = qk_norm(proj[:, 0:QK_W], qg_ref[...])
    k = qk_norm(proj[:, QK_W:2 * QK_W], kg_ref[...])
    v = proj[:, 2 * QK_W:2 * QK_W + MIX_A]

    if latent:
        lane = lax.broadcasted_iota(I32, (1, QK_W), 1)
        first = (lane % 32) < 16

        def rope(z):
            partner = jnp.where(first, pltpu.roll(z, QK_W - 16, 1), pltpu.roll(z, 16, 1))
            return z * cos_ref[...] + partner * sin_ref[...]

        q = rope(q)
        k = rope(k)
    else:
        for b in range(TF // CTX_S):
            sk_o[b, 0] = k[b * CTX_S:(b + 1) * CTX_S, :]
            sv_o[b, 0] = v[b * CTX_S:(b + 1) * CTX_S, :]

    q_o[...] = (q * (HEAD_DIM ** -0.5)).astype(BF16)
    k_o[...] = k.astype(BF16)
    v_o[...] = v.astype(BF16)

    gu0 = 2 * QK_W + MIX_A
    gv0 = gu0 + MIX_B
    for g in range(SG_GROUPS):
        wg = sgw_ref[0, g].astype(BF16)
        for n in range(TF // SG_CHUNK):
            rows = slice(n * SG_CHUNK, (n + 1) * SG_CHUNK)
            cu = slice(gu0 + g * 128, gu0 + (g + 1) * 128)
            cv = slice(gv0 + g * 128, gv0 + (g + 1) * 128)
            gv = _gelu_tanh(proj[rows, cv]).astype(BF16)
            mixed = jnp.dot(wg, gv, preferred_element_type=F32) + sgb_ref[0, g]
            sg_o[rows, g * 128:(g + 1) * 128] = (_gelu_tanh(proj[rows, cu]) * mixed).astype(BF16)


def _front(latent, l, x, mod, norm1_g, w_in, qg, kg, seg, sg_w, sgb, rope_tabs, states):
    e = l // 2
    rows = T_LAT if latent else T_CTX
    n_tiles = rows // TF
    off = T_CTX // TF if (latent and x.shape[0] == T) else 0
    tiles_per_seq = LAT_S // TF
    if latent:
        cond_map = lambda i: (l, 1 + i // tiles_per_seq, 0, 0)
    else:
        cond_map = lambda i: (l, 0, 0, 0)
    in_specs = [
        pl.BlockSpec((TF, D), lambda i: (i + off, 0)),
        pl.BlockSpec((1, 1, N_MOD, D), cond_map),
        pl.BlockSpec((1, 1, D), lambda i: (l, 0, 0)),
        pl.BlockSpec((1, D, IN_W), lambda i: (e, 0, 0)),
        pl.BlockSpec((1, QK_W), lambda i: (0, 0)),
        pl.BlockSpec((1, QK_W), lambda i: (0, 0)),
        pl.BlockSpec((QK_W, QK_W), lambda i: (0, 0)),
        pl.BlockSpec((1, SG_GROUPS, SG_CHUNK, SG_CHUNK), lambda i: (e, 0, 0, 0)),
        pl.BlockSpec((1, SG_GROUPS, SG_CHUNK, 128), lambda i: (e, 0, 0, 0)),
    ]
    args = [_hbm(x), mod, norm1_g.reshape(DEPTH, 1, D), w_in, qg, kg, seg, sg_w, sgb]
    tok = lambda: pl.BlockSpec((TF, QK_W), lambda i: (i, 0))
    out_specs = [tok(), tok(), tok(), tok()]
    out_shape = [_hbm_out((rows, QK_W), BF16)] * 4
    aliases = {}
    if latent:
        cos_t, sin_t = rope_tabs
        in_specs += [pl.BlockSpec((TF, QK_W), lambda i: (i % tiles_per_seq, 0))] * 2
        args += [cos_t, sin_t]
    else:
        if states:
            in_specs += [pl.BlockSpec(memory_space=pl.ANY)] * 2
            args += list(states)
            aliases = {9: 4, 10: 5}
        st = lambda: pl.BlockSpec((TF // CTX_S, 1, CTX_S, QK_W), lambda i: (i, e, 0, 0))
        out_specs += [st(), st()]
        out_shape += [_hbm_out((N_CTX_B, DEPTH // 2, CTX_S, QK_W), F32)] * 2
    return pl.pallas_call(
        functools.partial(_front_kernel, latent, len(aliases)),
        grid=(n_tiles,),
        in_specs=in_specs,
        out_specs=out_specs,
        out_shape=out_shape,
        scratch_shapes=[pltpu.VMEM((D, IN_W), BF16)],
        input_output_aliases=aliases,
        compiler_params=_cparams(("arbitrary",), 48),
        name="front_lat" if latent else "front_ctx",
    )(*args)


def _attn_kernel(has_cache, heads, lam0, *refs):
    if has_cache:
        (q_ref, k_ref, v_ref, ck_ref, cv_ref, lq1, lk1, lq2, lk2, sub_ref, o_ref) = refs
    else:
        (q_ref, k_ref, v_ref, lq1, lk1, lq2, lk2, sub_ref, o_ref) = refs

    lane = lax.broadcasted_iota(I32, (1, VAL_DIM), 1)
    lam = (jnp.exp(jnp.sum(lq1[0] * lk1[0], axis=-1, keepdims=True))
           - jnp.exp(jnp.sum(lq2[0] * lk2[0], axis=-1, keepdims=True)) + lam0)

    for h in range(heads):
        hs = slice(h * VAL_DIM, (h + 1) * VAL_DIM)
        qf = q_ref[:, hs].astype(F32)
        qc = [jnp.where(lane < HEAD_DIM, qf, 0.0).astype(BF16),
              jnp.where(lane >= HEAD_DIM, qf, 0.0).astype(BF16)]
        keys = [k_ref[:, hs]]
        vals = [v_ref[:, hs]]
        if has_cache:
            keys.append(ck_ref[0, 0].astype(BF16))
            vals.append(cv_ref[0, 0].astype(BF16))

        probs = []
        for c in range(2):
            s = [lax.dot_general(qc[c], kk, NT_DIMS, preferred_element_type=F32) for kk in keys]
            m = s[0].max(axis=-1, keepdims=True)
            for sx in s[1:]:
                m = jnp.maximum(m, sx.max(axis=-1, keepdims=True))
            p = [jnp.exp(sx - m) for sx in s]
            den = p[0].sum(axis=-1, keepdims=True)
            for px in p[1:]:
                den = den + px.sum(axis=-1, keepdims=True)
            probs.append((p, den))

        r0 = 1.0 / probs[0][1]
        r1 = lam / probs[1][1]
        o = None
        for idx in range(len(keys)):
            a = (probs[0][0][idx] * r0 - probs[1][0][idx] * r1).astype(BF16)
            t = jnp.dot(a, vals[idx], preferred_element_type=F32)
            o = t if o is None else o + t
        ms = jnp.mean(o * o, axis=-1, keepdims=True)
        o_ref[:, hs] = ((o * lax.rsqrt(ms + EPS) * sub_ref[0]) * (1.0 - lam0)).astype(BF16)


def _attention(has_cache, l, q, k, v, cache_k, cache_v, lam_q1, lam_k1, lam_q2, lam_k2, subln_g):
    e = l // 2
    lam0 = 0.8 - 0.6 * math.exp(-0.3 * l)
    n_b, s_len = (N_LAT_B, LAT_S) if has_cache else (N_CTX_B, CTX_S)
    heads = 1 if has_cache else N_HEADS
    width = heads * VAL_DIM
    nq = s_len // TM
    kv = lambda: pl.BlockSpec((s_len, width), lambda b, h, qi: (b, h))
    par = lambda: pl.BlockSpec((1, 1, HEAD_DIM), lambda b, h, qi: (e, 0, 0))
    in_specs = [pl.BlockSpec((TM, width), lambda b, h, qi: (b * nq + qi, h)), kv(), kv()]
    args = [_hbm(q), _hbm(k), _hbm(v)]
    if has_cache:
        cs = lambda: pl.BlockSpec((1, 1, PAST, VAL_DIM), lambda b, h, qi: (b, e, 0, h))
        in_specs += [cs(), cs()]
        args += [cache_k.reshape(N_LAT_B, DEPTH // 2, PAST, QK_W),
                 cache_v.reshape(N_LAT_B, DEPTH // 2, PAST, MIX_A)]
    in_specs += [par(), par(), par(), par(),
                 pl.BlockSpec((1, 1, VAL_DIM), lambda b, h, qi: (e, 0, 0))]
    r3 = lambda a: a.reshape(DEPTH // 2, 1, -1)
    args += [r3(lam_q1), r3(lam_k1), r3(lam_q2), r3(lam_k2), r3(subln_g)]
    return pl.pallas_call(
        functools.partial(_attn_kernel, has_cache, heads, lam0),
        grid=(n_b, N_HEADS // heads, nq),
        in_specs=in_specs,
        out_specs=pl.BlockSpec((TM, width), lambda b, h, qi: (b * nq + qi, h)),
        out_shape=_hbm_out((n_b * s_len, MIX_A), BF16),
        compiler_params=_cparams(("arbitrary", "arbitrary", "arbitrary"), 32),
        name="attn_lat" if has_cache else "attn_ctx",
    )(*args)


def _route_tail(sub_tile, x_new, m_ref, n2_ref, rwt_ref, rbb_ref, tri_ref,
                hs_o, lp_o, rf_o, nch_o, loff_o, gb_o, cnt_o, carry):
    tok = slice(sub_tile * TM, (sub_tile + 1) * TM)
    if sub_tile == 0:
        @pl.when(pl.program_id(0) == 0)
        def _():
            carry[...] = jnp.zeros_like(carry)

    h2 = _modnorm(x_new, n2_ref[0], m_ref[0, 0, 4:5, :], m_ref[0, 0, 3:4, :])

    hh = h2.astype(BF16)
    hl = (h2 - hh.astype(F32)).astype(BF16)
    rw = rwt_ref[...]
    wh = rw.astype(BF16)
    wl = (rw - wh.astype(F32)).astype(BF16)
    dg = lambda a, b: lax.dot_general(a, b, NT_DIMS, preferred_element_type=F32)
    logits = dg(wh, hh) + dg(wh, hl) + dg(wl, hh)
    score = _sigmoid(logits)
    sel = score + rbb_ref[...]

    neg_inf = jnp.full((1, TM), -jnp.inf, F32)
    grp = []
    for g in range(N_GROUPS):
        s_rows = [sel[PER_GROUP * g + j:PER_GROUP * g + j + 1, :] for j in range(PER_GROUP)]
        c_rows = [score[PER_GROUP * g + j:PER_GROUP * g + j + 1, :] for j in range(PER_GROUP)]
        f_idx = jnp.zeros((1, TM), I32)
        f_val, f_sc = s_rows[0], c_rows[0]
        for j in range(1, PER_GROUP):
            better = s_rows[j] > f_val
            f_idx = jnp.where(better, j, f_idx)
            f_val = jnp.where(better, s_rows[j], f_val)
            f_sc = jnp.where(better, c_rows[j], f_sc)
        s_idx = jnp.zeros((1, TM), I32)
        s_val, s_sc = neg_inf, jnp.zeros((1, TM), F32)
        for j in range(PER_GROUP):
            better = (f_idx != j) & (s_rows[j] > s_val)
            s_idx = jnp.where(better, j, s_idx)
            s_val = jnp.where(better, s_rows[j], s_val)
            s_sc = jnp.where(better, c_rows[j], s_sc)
        grp.append((f_val + s_val, PER_GROUP * g + f_idx, PER_GROUP * g + s_idx, f_sc, s_sc))

    best, e0, e1, c0, c1 = grp[0]
    for g in range(1, N_GROUPS):
        better = grp[g][0] > best
        best = jnp.where(better, grp[g][0], best)
        e0 = jnp.where(better, grp[g][1], e0)
        e1 = jnp.where(better, grp[g][2], e1)
        c0 = jnp.where(better, grp[g][3], c0)
        c1 = jnp.where(better, grp[g][4], c1)
    den = c0 + c1
    w0 = c0 / den
    w1 = c1 / den

    sub = lax.broadcasted_iota(I32, (N_EXPERTS, TM), 0)
    hit0 = sub == jnp.broadcast_to(e0, (N_EXPERTS, TM))
    hit1 = sub == jnp.broadcast_to(e1, (N_EXPERTS, TM))
    onehot = jnp.where(hit0 | hit1, 1.0, 0.0)
    prefix = jnp.dot(onehot.astype(BF16), tri_ref[...], preferred_element_type=F32)
    count = jnp.sum(onehot, axis=1, keepdims=True)
    padded = jnp.floor((count + (ROW_CHUNK - 1)) * (1.0 / ROW_CHUNK)) * ROW_CHUNK

    sub1 = lax.broadcasted_iota(I32, (N_EXPERTS, 1), 0)
    run = jnp.zeros((1, 1), F32)
    local_off = jnp.zeros((N_EXPERTS, 1), F32)
    for e in range(N_EXPERTS):
        local_off = jnp.where(sub1 == e, run, local_off)
        run = run + padded[e:e + 1, :]
    place = prefix + local_off
    lpos0 = jnp.sum(jnp.where(hit0, place, 0.0), axis=0, keepdims=True).astype(I32)
    lpos1 = jnp.sum(jnp.where(hit1, place, 0.0), axis=0, keepdims=True).astype(I32)

    r = lax.broadcasted_iota(I32, (SORTED_ROWS, TM), 0)
    perm = jnp.where((r == lpos0) | (r == lpos1), 1.0, 0.0).astype(BF16)
    hs_o[sub_tile * SORTED_ROWS:(sub_tile + 1) * SORTED_ROWS, :] = jnp.dot(
        perm, hh, preferred_element_type=F32).astype(BF16)

    lanes = lambda col: jnp.broadcast_to(col, (N_EXPERTS, 128)).astype(I32)
    nch_o[sub_tile] = lanes(padded * (1.0 / ROW_CHUNK))
    loff_o[sub_tile] = lanes(local_off)
    gb_o[sub_tile] = carry[...].astype(I32)
    carry[...] = carry[...] + padded
    cnt_o[...] = carry[...].astype(I32)

    sub8 = lax.broadcasted_iota(I32, (8, TM), 0)
    bc = lambda r: jnp.broadcast_to(r, (8, TM))
    lp_o[:, tok] = jnp.where(sub8 == 0, bc(lpos0), jnp.where(sub8 == 1, bc(lpos1), 0))
    rf_o[:, tok] = jnp.where(sub8 == 0, bc(w0), jnp.where(sub8 == 1, bc(w1), 0.0))


def _tail_specs(l, sub_tiles):
    rows = sub_tiles * TM
    in_specs = [
        pl.BlockSpec((1, 1, D), lambda i: (l, 0, 0)),
        pl.BlockSpec((N_EXPERTS, D), lambda i: (0, 0)),
        pl.BlockSpec((N_EXPERTS, TM), lambda i: (0, 0)),
        pl.BlockSpec((TM, TM), lambda i: (0, 0)),
    ]
    per_tile = lambda: pl.BlockSpec((sub_tiles, N_EXPERTS, 128), lambda i: (i, 0, 0))
    out_specs = [
        pl.BlockSpec((sub_tiles * SORTED_ROWS, D), lambda i: (i, 0)),
        pl.BlockSpec((8, rows), lambda i: (0, i)),
        pl.BlockSpec((8, rows), lambda i: (0, i)),
        per_tile(), per_tile(), per_tile(),
        pl.BlockSpec((N_EXPERTS, 128), lambda i: (0, 0)),
    ]
    out_shape = [
        _hbm_out((N_TILES * SORTED_ROWS, D), BF16),
        jax.ShapeDtypeStruct((8, T), I32),
        jax.ShapeDtypeStruct((8, T), F32),
        jax.ShapeDtypeStruct((N_TILES, N_EXPERTS, 128), I32),
        jax.ShapeDtypeStruct((N_TILES, N_EXPERTS, 128), I32),
        jax.ShapeDtypeStruct((N_TILES, N_EXPERTS, 128), I32),
        jax.ShapeDtypeStruct((N_EXPERTS, 128), I32),
    ]
    scratch = [pltpu.VMEM((N_EXPERTS, 128), F32)]
    return in_specs, out_specs, out_shape, scratch


def _out_kernel(split_x, *refs):
    x_refs, refs = refs[:1 + split_x], refs[1 + split_x:]
    (m_ref, ac_ref, al_ref, sc_ref, sl_ref, w_ref, n2_ref, rwt_ref, rbb_ref, tri_ref,
     x_o, *tail_refs) = refs
    *tail_outs, wbf, carry = tail_refs
    i = pl.program_id(0)

    @pl.when(i == 0)
    def _():
        wbf[...] = w_ref[0].astype(BF16)

    is_ctx = i < T_CTX // TO
    x = jnp.where(is_ctx, x_refs[0][...], x_refs[1][...]) if split_x else x_refs[0][...]
    att = jnp.where(is_ctx, ac_ref[...], al_ref[...])
    sgo = jnp.where(is_ctx, sc_ref[...], sl_ref[...])
    out = (jnp.dot(att, wbf[0:MIX_A, :], preferred_element_type=F32)
           + jnp.dot(sgo, wbf[MIX_A:MIX_A + MIX_B, :], preferred_element_type=F32))
    x_new = x + m_ref[0, 0, 2:3, :] * out
    x_o[...] = x_new
    for sub_tile in range(TO // TM):
        _route_tail(sub_tile, x_new[sub_tile * TM:(sub_tile + 1) * TM, :],
                    m_ref, n2_ref, rwt_ref, rbb_ref, tri_ref, *tail_outs, carry)


def _out_proj(l, x, mod, att_c, att_l, sg_c, sg_l, w_out, norm2_g, rwt, rbb, tri):
    e = l // 2
    t_in, t_out, t_shape, t_scratch = _tail_specs(l, TO // TM)
    n_ctx = T_CTX // TO
    ctx = lambda w: pl.BlockSpec((TO, w), lambda i: (jnp.minimum(i, n_ctx - 1), 0))
    lat = lambda w: pl.BlockSpec((TO, w), lambda i: (jnp.maximum(i - n_ctx, 0), 0))
    split_x = isinstance(x, tuple)
    if split_x:
        x_specs, x_args, aliases = [ctx(D), lat(D)], [_hbm(x[0]), _hbm(x[1])], {}
    else:
        x_specs, x_args, aliases = [pl.BlockSpec((TO, D), lambda i: (i, 0))], [_hbm(x)], {0: 0}
    in_specs = x_specs + [
        pl.BlockSpec((1, 1, N_MOD, D), lambda i: (l, _cond_of_tile(i * (TO // TM)), 0, 0)),
        ctx(MIX_A), lat(MIX_A), ctx(MIX_B), lat(MIX_B),
        pl.BlockSpec((1, D, D), lambda i: (e, 0, 0)),
    ] + t_in
    return pl.pallas_call(
        functools.partial(_out_kernel, split_x),
        grid=(T // TO,),
        in_specs=in_specs,
        out_specs=[pl.BlockSpec((TO, D), lambda i: (i, 0))] + t_out,
        out_shape=[_hbm_out((T, D), F32)] + t_shape,
        scratch_shapes=[pltpu.VMEM((D, D), BF16)] + t_scratch,
        input_output_aliases=aliases,
        compiler_params=_cparams(("arbitrary",), 40),
        name="out_proj_route",
    )(*x_args, mod, _hbm(att_c), _hbm(att_l), _hbm(sg_c), _hbm(sg_l), w_out,
      norm2_g.reshape(DEPTH, 1, D), rwt, rbb, tri)


def _pool_kernel(x_ref, xp_ref, xn_ref, m_ref, n1_ref, pw_ref, ps_ref,
                 n2_ref, rwt_ref, rbb_ref, tri_ref,
                 x_o, *tail_refs):
    *tail_outs, carry = tail_refs
    i = pl.program_id(0)
    is_lat = i >= N_CTX_TILES
    ti = jnp.where(is_lat, (i - N_CTX_TILES) % LAT_TILES_PER_SEQ, 0)
    last_ti = jnp.where(is_lat, LAT_TILES_PER_SEQ - 1, 0)
    seq_len = jnp.where(is_lat, LAT_S, CTX_S)

    g1n, sc1, sh1 = n1_ref[0], m_ref[0, 0, 1:2, :], m_ref[0, 0, 0:1, :]
    x = x_ref[...]
    h = _modnorm(x, g1n, sc1, sh1)
    hp = jnp.where(ti == 0, 0.0, _modnorm(xp_ref[...], g1n, sc1, sh1))
    hn = jnp.where(ti == last_ti, 0.0, _modnorm(xn_ref[...], g1n, sc1, sh1))
    hcat = jnp.concatenate([hp, h, hn], axis=0)
    n_rows = TM + 2 * POOL_HALO

    t_pos = ti * TM + lax.broadcasted_iota(I32, (TM, 1), 0)
    outs = []
    for g, w in enumerate(POOL_WINDOWS):
        cols = slice(g * POOL_DIM, (g + 1) * POOL_DIM)
        acc = hcat[:, cols]
        acc = acc + pltpu.roll(acc, 1, 0)
        step = 1
        while 2 * step < w:
            acc = pltpu.roll(acc, step, 0) + pltpu.roll(acc, n_rows - step, 0)
            step *= 2
        lo = jnp.maximum(t_pos - w // 2, 0)
        hi = jnp.minimum(t_pos - w // 2 + w, seq_len)
        mean = acc[POOL_HALO:POOL_HALO + TM, :] / (hi - lo).astype(F32)
        dlt = (mean - h[:, cols]).astype(BF16)
        outs.append(jnp.dot(dlt, pw_ref[0, g].astype(BF16), preferred_element_type=F32))
    out = jnp.concatenate(outs, axis=1) * ps_ref[0]
    x_new = x + m_ref[0, 0, 2:3, :] * out
    x_o[...] = x_new
    _route_tail(0, x_new, m_ref, n2_ref, rwt_ref, rbb_ref, tri_ref, *tail_outs, carry)


def _pool_mixer(l, x, mod, norm1_g, pool_w, pool_scale, norm2_g, rwt, rbb, tri):
    o = l // 2
    t_in, t_out, t_shape, t_scratch = _tail_specs(l, 1)
    halo_blocks = TM // POOL_HALO
    in_specs = [
        pl.BlockSpec((TM, D), lambda i: (i, 0)),
        pl.BlockSpec((POOL_HALO, D), lambda i: (jnp.maximum(i * halo_blocks - 1, 0), 0)),
        pl.BlockSpec((POOL_HALO, D), lambda i: (jnp.minimum((i + 1) * halo_blocks, T // POOL_HALO - 1), 0)),
        pl.BlockSpec((1, 1, N_MOD, D), lambda i: (l, _cond_of_tile(i), 0, 0)),
        pl.BlockSpec((1, 1, D), lambda i: (l, 0, 0)),
        pl.BlockSpec((1, len(POOL_WINDOWS), POOL_DIM, POOL_DIM), lambda i: (o, 0, 0, 0)),
        pl.BlockSpec((1, 1, D), lambda i: (o, 0, 0)),
    ] + t_in
    return pl.pallas_call(
        _pool_kernel,
        grid=(N_TILES,),
        in_specs=in_specs,
        out_specs=[pl.BlockSpec((TM, D), lambda i: (i, 0))] + t_out,
        out_shape=[_hbm_out((T, D), F32)] + t_shape,
        scratch_shapes=t_scratch,
        compiler_params=_cparams(("arbitrary",), 32),
        name="pool_route",
    )(_hbm(x), _hbm(x), _hbm(x), mod, norm1_g.reshape(DEPTH, 1, D), pool_w,
      pool_scale.reshape(DEPTH // 2, 1, D),
      norm2_g.reshape(DEPTH, 1, D), rwt, rbb, tri)


def _start_run_copies(tile, nch_ref, loff_ref, gb_ref, make_copy):
    for e in range(N_EXPERTS):
        lo = loff_ref[tile * N_EXPERTS + e]
        gb = gb_ref[tile * N_EXPERTS + e] + e * EXPERT_CAP

        def issue(c, carry, lo=lo, gb=gb):
            make_copy(pl.multiple_of(lo + c * ROW_CHUNK, ROW_CHUNK),
                      pl.multiple_of(gb + c * ROW_CHUNK, ROW_CHUNK)).start()
            return carry

        lax.fori_loop(0, nch_ref[tile * N_EXPERTS + e], issue, 0)


def _wait_run_copies(tile, nch_ref, make_copy):
    total = 0
    for e in range(N_EXPERTS):
        total = total + nch_ref[tile * N_EXPERTS + e]

    def drain(c, carry):
        make_copy(0, 0).wait()
        return carry

    lax.fori_loop(0, total, drain, 0)


def _dispatch_kernel(nch_ref, loff_ref, gb_ref, hs_ref, xs_ref, sem):
    def tile_copies(tile):
        def make_copy(local_row, global_row):
            src = pl.multiple_of(tile * SORTED_ROWS + local_row, ROW_CHUNK)
            return pltpu.make_async_copy(hs_ref.at[pl.ds(src, ROW_CHUNK)],
                                         xs_ref.at[pl.ds(global_row, ROW_CHUNK)], sem)
        return make_copy

    def start_tile(tile, carry):
        _start_run_copies(tile, nch_ref, loff_ref, gb_ref, tile_copies(tile))
        return carry

    def wait_tile(tile, carry):
        _wait_run_copies(tile, nch_ref, tile_copies(tile))
        return carry

    lax.fori_loop(0, N_TILES, start_tile, 0)
    lax.fori_loop(0, N_TILES, wait_tile, 0)


def _dispatch(layout, hs):
    return pl.pallas_call(
        _dispatch_kernel,
        grid_spec=pltpu.PrefetchScalarGridSpec(
            num_scalar_prefetch=3,
            grid=(1,),
            in_specs=[pl.BlockSpec(memory_space=pl.ANY)],
            out_specs=pl.BlockSpec(memory_space=pl.ANY),
            scratch_shapes=[pltpu.SemaphoreType.DMA],
        ),
        out_shape=_hbm_out((N_EXPERTS * EXPERT_CAP, D), BF16),
        compiler_params=_cparams(("arbitrary",), 16),
        name="moe_dispatch",
    )(*layout, _hbm(hs))


def _moe_kernel(l, rb_ref, e_ref, nv_ref, first_ref, ord_ref, nxt_ref, xs_ref, wg_hbm, wu_hbm, wd_hbm, y_ref,
                wgf, wuf, wdf, wgb, wub, wdb, sems):
    i = pl.program_id(0)
    n_valid = nv_ref[i]

    def weight_copies(e, slot):
        return (pltpu.make_async_copy(wg_hbm.at[l, e], wgf.at[slot], sems.at[slot, 0]),
                pltpu.make_async_copy(wu_hbm.at[l, e], wuf.at[slot], sems.at[slot, 1]),
                pltpu.make_async_copy(wd_hbm.at[l, e], wdf.at[slot], sems.at[slot, 2]))

    @pl.when(first_ref[i] == 1)
    def _():
        slot = ord_ref[i] % 2

        @pl.when(ord_ref[i] == 0)
        def _():
            for cp in weight_copies(e_ref[i], slot):
                cp.start()

        for cp in weight_copies(e_ref[i], slot):
            cp.wait()

        @pl.when(nxt_ref[i] >= 0)
        def _():
            for cp in weight_copies(nxt_ref[i], 1 - slot):
                cp.start()

        wgb[...] = wgf[slot].astype(BF16)
        wub[...] = wuf[slot].astype(BF16)
        wdb[...] = wdf[slot].astype(BF16)

    @pl.when(n_valid > 0)
    def _():
        row = lax.broadcasted_iota(I32, (TE, 1), 0)
        x = jnp.where(row < n_valid, xs_ref[...], jnp.zeros((), BF16))
        gate = jnp.dot(x, wgb[...], preferred_element_type=F32)
        up = jnp.dot(x, wub[...], preferred_element_type=F32)
        act = (gate * _sigmoid(gate) * up).astype(BF16)
        y_ref[...] = jnp.dot(act, wdb[...], preferred_element_type=F32).astype(BF16)


def _experts(l, tile_map, xs, w_gate, w_up, w_down):
    n_pre = len(tile_map)
    blk = lambda i, rb, *_: (rb[i], 0)
    return pl.pallas_call(
        functools.partial(_moe_kernel, l),
        grid_spec=pltpu.PrefetchScalarGridSpec(
            num_scalar_prefetch=n_pre,
            grid=(MOE_TILES,),
            in_specs=[
                pl.BlockSpec((TE, D), blk),
                pl.BlockSpec(memory_space=pl.ANY),
                pl.BlockSpec(memory_space=pl.ANY),
                pl.BlockSpec(memory_space=pl.ANY),
            ],
            out_specs=pl.BlockSpec((TE, D), blk),
            scratch_shapes=[pltpu.VMEM((2, D, D_FF), F32), pltpu.VMEM((2, D, D_FF), F32),
                            pltpu.VMEM((2, D_FF, D), F32),
                            pltpu.VMEM((D, D_FF), BF16), pltpu.VMEM((D, D_FF), BF16),
                            pltpu.VMEM((D_FF, D), BF16),
                            pltpu.SemaphoreType.DMA((2, 3))],
        ),
        out_shape=_hbm_out((N_EXPERTS * EXPERT_CAP, D), BF16),
        compiler_params=_cparams(("arbitrary",), 40),
        name="moe_experts",
    )(*tile_map, _hbm(xs), w_gate, w_up, w_down)


def _combine_kernel(split_out, nch_ref, loff_ref, gb_ref, x_ref, m_ref, lpt_ref, w_ref, y_ref, *refs):
    *o_refs, ybuf, sems = refs
    i = pl.program_id(0)
    slot = i % 2

    def copies_of(s):
        def make_copy(local_row, global_row):
            return pltpu.make_async_copy(y_ref.at[pl.ds(global_row, ROW_CHUNK)],
                                         ybuf.at[s, pl.ds(local_row, ROW_CHUNK)], sems.at[s])
        return make_copy

    @pl.when(i == 0)
    def _():
        ybuf[...] = jnp.zeros_like(ybuf)
        _start_run_copies(i, nch_ref, loff_ref, gb_ref, copies_of(slot))

    @pl.when(i + 1 < N_TILES)
    def _():
        _start_run_copies(i + 1, nch_ref, loff_ref, gb_ref, copies_of(1 - slot))

    _wait_run_copies(i, nch_ref, copies_of(slot))

    col = lax.broadcasted_iota(I32, (TM, SORTED_ROWS), 1)
    pick0 = jnp.where(col == lpt_ref[:, 0:1], 1.0, 0.0).astype(BF16)
    pick1 = jnp.where(col == lpt_ref[:, 1:2], 1.0, 0.0).astype(BF16)
    y0 = jnp.dot(pick0, ybuf[slot], preferred_element_type=F32)
    y1 = jnp.dot(pick1, ybuf[slot], preferred_element_type=F32)
    moe = w_ref[:, 0:1] * y0 + w_ref[:, 1:2] * y1
    res = x_ref[...] + m_ref[0, 0, 5:6, :] * moe
    if split_out:
        @pl.when(i < N_CTX_TILES)
        def _():
            o_refs[0][...] = res

        @pl.when(i >= N_CTX_TILES)
        def _():
            o_refs[1][...] = res
    else:
        o_refs[0][...] = res


def _combine(l, layout, x, mod, lpt, wcol, y, split_out):
    if split_out:
        out_specs = [pl.BlockSpec((TM, D), lambda i, *_: (jnp.minimum(i, N_CTX_TILES - 1), 0)),
                     pl.BlockSpec((TM, D), lambda i, *_: (jnp.maximum(i - N_CTX_TILES, 0), 0))]
        out_shape = [jax.ShapeDtypeStruct((T_CTX, D), F32), jax.ShapeDtypeStruct((T_LAT, D), F32)]
        aliases = {}
    else:
        out_specs = pl.BlockSpec((TM, D), lambda i, *_: (i, 0))
        out_shape = _hbm_out((T, D), F32)
        aliases = {3: 0}
    return pl.pallas_call(
        functools.partial(_combine_kernel, split_out),
        grid_spec=pltpu.PrefetchScalarGridSpec(
            num_scalar_prefetch=3,
            grid=(N_TILES,),
            in_specs=[
                pl.BlockSpec((TM, D), lambda i, *_: (i, 0)),
                pl.BlockSpec((1, 1, N_MOD, D), lambda i, *_: (l, _cond_of_tile(i), 0, 0)),
                pl.BlockSpec((TM, 8), lambda i, *_: (i, 0)),
                pl.BlockSpec((TM, 8), lambda i, *_: (i, 0)),
                pl.BlockSpec(memory_space=pl.ANY),
            ],
            out_specs=out_specs,
            scratch_shapes=[pltpu.VMEM((2, SORTED_ROWS, D), BF16), pltpu.SemaphoreType.DMA((2,))],
        ),
        out_shape=out_shape,
        input_output_aliases=aliases,
        compiler_params=_cparams(("arbitrary",), 24),
        name="moe_combine",
    )(*layout, _hbm(x), mod, lpt, wcol, _hbm(y))


def _tile_map(cnt):
    cnt = cnt[:, 0]
    ids = jnp.arange(N_EXPERTS, dtype=I32)
    n_tiles = (cnt + TE - 1) // TE
    cum = jnp.cumsum(n_tiles)
    total = cum[-1]
    i = jnp.arange(MOE_TILES, dtype=I32)
    ii = jnp.minimum(i, total - 1)
    e = jnp.sum((ii[:, None] >= cum[None, :]).astype(I32), axis=1)
    onehot = (e[:, None] == ids[None, :]).astype(I32)
    pick = lambda v: jnp.sum(onehot * v[None, :], axis=1)
    j = ii - pick(cum - n_tiles)
    active = i < total
    n_valid = jnp.where(active, jnp.clip(pick(cnt) - j * TE, 0, TE), 0)
    first = (active & (j == 0)).astype(I32)
    nonempty = n_tiles > 0
    ordinal = jnp.cumsum(nonempty.astype(I32)) - 1
    later = jnp.where(nonempty[None, :] & (ids[None, :] > ids[:, None]), ids[None, :], N_EXPERTS)
    nxt = jnp.min(later, axis=1)
    nxt = jnp.where(nxt == N_EXPERTS, -1, nxt)
    as_i32 = lambda v: v.astype(I32)
    return tuple(map(as_i32, (e * CAP_TILES + j, e, n_valid, first, pick(ordinal), pick(nxt))))


def _moe(l, x, tail, mod, w_gate, w_up, w_down, split_out):
    hs, lp, rf, nch, loff, gb, cnt = tail
    layout = tuple(a[:, :, 0].reshape(N_TILES * N_EXPERTS) for a in (nch, loff, gb))
    xs = _dispatch(layout, hs)
    y = _experts(l, _tile_map(cnt), xs, w_gate, w_up, w_down)
    return _combine(l, layout, x, mod, lp.T, rf.T, y, split_out)


def _rope_tables():
    t = jnp.arange(LAT_S)
    row = (t // GRID_W).astype(F32)
    col = (t % GRID_W).astype(F32)
    nf = HEAD_DIM // 4
    inv = ROPE_BASE ** (-jnp.arange(nf, dtype=F32) / nf)
    ar = row[:, None] * inv[None, :]
    ac = col[:, None] * inv[None, :]
    cos = jnp.concatenate([jnp.cos(ar), jnp.cos(ar), jnp.cos(ac), jnp.cos(ac)], axis=1)
    sin = jnp.concatenate([-jnp.sin(ar), jnp.sin(ar), -jnp.sin(ac), jnp.sin(ac)], axis=1)
    reps = QK_W // HEAD_DIM
    return jnp.tile(cos, (1, reps)), jnp.tile(sin, (1, reps))


def kernel(x_prompt, x_sample, cache_k, cache_v, c, c_ctx, norm1_g, norm2_g, w_ada, b_ada, w_in, w_out,
           q_norm_g, k_norm_g, lam_q1, lam_k1, lam_q2, lam_k2, subln_g, sg_w, sg_b, pool_w, pool_scale,
           router_w, router_b, w_gate, w_up, w_down):
    x = (x_prompt.reshape(T_CTX, D), x_sample.reshape(T_LAT, D))
    cond = jnp.concatenate([c_ctx[None, :], c], axis=0)
    mod = _modulation(cond, w_ada, b_ada)

    seg_id = np.arange(QK_W) // HEAD_DIM
    seg = jnp.asarray((seg_id[:, None] == seg_id[None, :]) / HEAD_DIM, dtype=BF16)
    tri = jnp.asarray(np.arange(TM)[:, None] < np.arange(TM)[None, :], dtype=BF16)
    rope_tabs = _rope_tables()
    rwt = router_w.T
    rbb = jnp.broadcast_to(router_b[:, None], (N_EXPERTS, TM))
    states = None

    for l in range(DEPTH):
        if l % 2 == 0:
            e = l // 2
            qg = jnp.tile(q_norm_g[e], QK_W // HEAD_DIM)[None, :]
            kg = jnp.tile(k_norm_g[e], QK_W // HEAD_DIM)[None, :]
            sgb = jnp.broadcast_to(sg_b[:, :, :, None], (DEPTH // 2, SG_GROUPS, SG_CHUNK, 128))
            x_c, x_l = x if isinstance(x, tuple) else (x, x)
            common = (mod, norm1_g, w_in, qg, kg, seg, sg_w, sgb)
            q_c, k_c, v_c, sg_c, *states = _front(False, l, x_c, *common, None, states)
            q_l, k_l, v_l, sg_l = _front(True, l, x_l, *common, rope_tabs, None)
            lam = (lam_q1, lam_k1, lam_q2, lam_k2, subln_g)
            att_c = _attention(False, l, q_c, k_c, v_c, None, None, *lam)
            att_l = _attention(True, l, q_l, k_l, v_l, cache_k, cache_v, *lam)
            x, *tail = _out_proj(l, x, mod, att_c, att_l, sg_c, sg_l, w_out, norm2_g, rwt, rbb, tri)
        else:
            x, *tail = _pool_mixer(l, x, mod, norm1_g, pool_w, pool_scale, norm2_g, rwt, rbb, tri)
        x = _moe(l, x, tail, mod, w_gate, w_up, w_down, split_out=(l == DEPTH - 1))
    state_k, state_v = states

    y_prompt = x[0].reshape(N_CTX_B, CTX_S, D)
    y_sample = x[1].reshape(N_LAT_B, LAT_S, D)
    state_k = state_k.reshape(N_CTX_B, DEPTH // 2, CTX_S, N_HEADS, 2, HEAD_DIM)
    state_v = state_v.reshape(N_CTX_B, DEPTH // 2, CTX_S, N_HEADS, VAL_DIM)
    return y_prompt, y_sample, state_k, state_v
```

```python
import functools
import math

import jax
import jax.numpy as jnp
import numpy as np
from jax import lax
from jax.experimental import pallas as pl
from jax.experimental.pallas import tpu as pltpu

F32 = jnp.float32
BF16 = jnp.bfloat16
I32 = jnp.int32

D = 1024
N_CTX_B, CTX_S = 16, 256
N_LAT_B, LAT_S = 2, 1024
PAST = 512
DEPTH = 4
T_CTX = N_CTX_B * CTX_S
T_LAT = N_LAT_B * LAT_S
T = T_CTX + T_LAT
GRID_W = 64
N_HEADS = 4
HEAD_DIM = 64
VAL_DIM = 128
QK_W = 512
MIX_A = 512
MIX_B = 512
IN_W = 2560
SG_GROUPS = 4
SG_CHUNK = 128
POOL_WINDOWS = (2, 4, 8, 16)
POOL_DIM = 256
POOL_HALO = 8
N_EXPERTS = 16
N_GROUPS = 4
PER_GROUP = 4
D_FF = 512
N_MOD = 6
N_COND = 1 + N_LAT_B
EPS = 1e-6
ROPE_BASE = 10000.0

TM = 256
N_TILES = T // TM
N_CTX_TILES = T_CTX // TM
LAT_TILES_PER_SEQ = LAT_S // TM
ROW_CHUNK = 16
PAD_PER_RUN = ROW_CHUNK - 1
SORTED_ROWS = -(-(2 * TM + N_EXPERTS * PAD_PER_RUN) // 128) * 128
TE = 1024
TF = 512
TO = 512
CAP_TILES = -(-(T + N_TILES * PAD_PER_RUN) // TE)
EXPERT_CAP = CAP_TILES * TE
MOE_TILES = (2 * T + N_EXPERTS * N_TILES * PAD_PER_RUN) // TE + N_EXPERTS
MOD_TN = 1024
MIB = 1024 * 1024

NT_DIMS = (((1,), (1,)), ((), ()))


def _cparams(sem, vmem_mib):
    return pltpu.CompilerParams(dimension_semantics=sem, vmem_limit_bytes=vmem_mib * MIB)


def _hbm(x):
    try:
        return pltpu.with_memory_space_constraint(x, pltpu.HBM)
    except ValueError:
        return x


def _hbm_out(shape, dtype):
    return pltpu.HBM(shape, dtype)


def _sigmoid(x):
    return 1.0 / (1.0 + jnp.exp(-x))


def _gelu_tanh(x):
    c = math.sqrt(2.0 / math.pi)
    return x * (0.5 * (1.0 + jnp.tanh(c * (x + 0.044715 * (x * x * x)))))


def _modnorm(x, g, scale, shift):
    ms = jnp.mean(x * x, axis=-1, keepdims=True)
    return (x * lax.rsqrt(ms + EPS) * g) * (1.0 + scale) + shift


def _cond_of_tile(i):
    return jnp.where(i < N_CTX_TILES, 0, 1 + (i - N_CTX_TILES) // LAT_TILES_PER_SEQ)


def _mod_kernel(c_ref, w_ref, b_ref, o_ref):
    c = c_ref[...]
    s = (c * _sigmoid(c)).astype(BF16)
    o_ref[0] = jnp.dot(s, w_ref[0].astype(BF16), preferred_element_type=F32) + b_ref[0]


def _modulation(cond, w_ada, b_ada):
    cond8 = jnp.concatenate([cond, jnp.zeros((8 - N_COND, D), F32)], axis=0)
    out = pl.pallas_call(
        _mod_kernel,
        grid=(DEPTH, N_MOD * D // MOD_TN),
        in_specs=[
            pl.BlockSpec((8, D), lambda l, n: (0, 0)),
            pl.BlockSpec((1, D, MOD_TN), lambda l, n: (l, 0, n)),
            pl.BlockSpec((1, 1, MOD_TN), lambda l, n: (l, 0, n)),
        ],
        out_specs=pl.BlockSpec((1, 8, MOD_TN), lambda l, n: (l, 0, n)),
        out_shape=jax.ShapeDtypeStruct((DEPTH, 8, N_MOD * D), F32),
        compiler_params=_cparams(("arbitrary", "arbitrary"), 24),
        name="adaln_mod",
    )(cond8, w_ada, b_ada.reshape(DEPTH, 1, N_MOD * D))
    return out.reshape(DEPTH, 8, N_MOD, D)


def _front_kernel(latent, n_state_in, *refs):
    (x_ref, m_ref, n1_ref, w_ref, qg_ref, kg_ref, seg_ref, sgw_ref, sgb_ref), refs = refs[:9], refs[9:]
    if latent:
        cos_ref, sin_ref, q_o, k_o, v_o, sg_o, wbf = refs
    else:
        q_o, k_o, v_o, sg_o, sk_o, sv_o, wbf = refs[n_state_in:]

    @pl.when(pl.program_id(0) == 0)
    def _():
        for c0 in range(0, IN_W, 512):
            wbf[:, c0:c0 + 512] = w_ref[0, :, c0:c0 + 512].astype(BF16)

    h = _modnorm(x_ref[...], n1_ref[0], m_ref[0, 0, 1:2, :], m_ref[0, 0, 0:1, :])
    proj = jnp.dot(h.astype(BF16), wbf[...], preferred_element_type=F32)

    def qk_norm(z, g):
        ms = jnp.dot((z * z).astype(BF16), seg_ref[...], preferred_element_type=F32)
        return z * lax.rsqrt(ms + EPS) * g

    q = qk_norm(proj[:, 0:QK_W], qg_ref[...])
    k = qk_norm(proj[:, QK_W:2 * QK_W], kg_ref[...])
    v = proj[:, 2 * QK_W:2 * QK_W + MIX_A]

    if latent:
        lane = lax.broadcasted_iota(I32, (1, QK_W), 1)
        first = (lane % 32) < 16

        def rope(z):
            partner = jnp.where(first, pltpu.roll(z, QK_W - 16, 1), pltpu.roll(z, 16, 1))
            return z * cos_ref[...] + partner * sin_ref[...]

        q = rope(q)
        k = rope(k)
    else:
        for b in range(TF // CTX_S):
            sk_o[b, 0] = k[b * CTX_S:(b + 1) * CTX_S, :]
            sv_o[b, 0] = v[b * CTX_S:(b + 1) * CTX_S, :]

    q_o[...] = (q * (HEAD_DIM ** -0.5)).astype(BF16)
    k_o[...] = k.astype(BF16)
    v_o[...] = v.astype(BF16)

    gu0 = 2 * QK_W + MIX_A
    gv0 = gu0 + MIX_B
    for g in range(SG_GROUPS):
        wg = sgw_ref[0, g].astype(BF16)
        for n in range(TF // SG_CHUNK):
            rows = slice(n * SG_CHUNK, (n + 1) * SG_CHUNK)
            cu = slice(gu0 + g * 128, gu0 + (g + 1) * 128)
            cv = slice(gv0 + g * 128, gv0 + (g + 1) * 128)
            gv = _gelu_tanh(proj[rows, cv]).astype(BF16)
            mixed = jnp.dot(wg, gv, preferred_element_type=F32) + sgb_ref[0, g]
            sg_o[rows, g * 128:(g + 1) * 128] = (_gelu_tanh(proj[rows, cu]) * mixed).astype(BF16)


def _front(latent, l, x, mod, norm1_g, w_in, qg, kg, seg, sg_w, sgb, rope_tabs, states):
    e = l // 2
    rows = T_LAT if latent else T_CTX
    n_tiles = rows // TF
    off = T_CTX // TF if (latent and x.shape[0] == T) else 0
    tiles_per_seq = LAT_S // TF
    if latent:
        cond_map = lambda i: (l, 1 + i // tiles_per_seq, 0, 0)
    else:
        cond_map = lambda i: (l, 0, 0, 0)
    in_specs = [
        pl.BlockSpec((TF, D), lambda i: (i + off, 0)),
        pl.BlockSpec((1, 1, N_MOD, D), cond_map),
        pl.BlockSpec((1, 1, D), lambda i: (l, 0, 0)),
        pl.BlockSpec((1, D, IN_W), lambda i: (e, 0, 0)),
        pl.BlockSpec((1, QK_W), lambda i: (0, 0)),
        pl.BlockSpec((1, QK_W), lambda i: (0, 0)),
        pl.BlockSpec((QK_W, QK_W), lambda i: (0, 0)),
        pl.BlockSpec((1, SG_GROUPS, SG_CHUNK, SG_CHUNK), lambda i: (e, 0, 0, 0)),
        pl.BlockSpec((1, SG_GROUPS, SG_CHUNK, 128), lambda i: (e, 0, 0, 0)),
    ]
    args = [_hbm(x), mod, norm1_g.reshape(DEPTH, 1, D), w_in, qg, kg, seg, sg_w, sgb]
    tok = lambda: pl.BlockSpec((TF, QK_W), lambda i: (i, 0))
    out_specs = [tok(), tok(), tok(), tok()]
    out_shape = [_hbm_out((rows, QK_W), BF16)] * 4
    aliases = {}
    if latent:
        cos_t, sin_t = rope_tabs
        in_specs += [pl.BlockSpec((TF, QK_W), lambda i: (i % tiles_per_seq, 0))] * 2
        args += [cos_t, sin_t]
    else:
        if states:
            in_specs += [pl.BlockSpec(memory_space=pl.ANY)] * 2
            args += list(states)
            aliases = {9: 4, 10: 5}
        st = lambda: pl.BlockSpec((TF // CTX_S, 1, CTX_S, QK_W), lambda i: (i, e, 0, 0))
        out_specs += [st(), st()]
        out_shape += [_hbm_out((N_CTX_B, DEPTH // 2, CTX_S, QK_W), F32)] * 2
    return pl.pallas_call(
        functools.partial(_front_kernel, latent, len(aliases)),
        grid=(n_tiles,),
        in_specs=in_specs,
        out_specs=out_specs,
        out_shape=out_shape,
        scratch_shapes=[pltpu.VMEM((D, IN_W), BF16)],
        input_output_aliases=aliases,
        compiler_params=_cparams(("arbitrary",), 48),
        name="front_lat" if latent else "front_ctx",
    )(*args)


def _attn_kernel(has_cache, heads, lam0, *refs):
    if has_cache:
        (q_ref, k_ref, v_ref, ck_ref, cv_ref, lq1, lk1, lq2, lk2, sub_ref, o_ref) = refs
    else:
        (q_ref, k_ref, v_ref, lq1, lk1, lq2, lk2, sub_ref, o_ref) = refs

    lane = lax.broadcasted_iota(I32, (1, VAL_DIM), 1)
    lam = (jnp.exp(jnp.sum(lq1[0] * lk1[0], axis=-1, keepdims=True))
           - jnp.exp(jnp.sum(lq2[0] * lk2[0], axis=-1, keepdims=True)) + lam0)

    for h in range(heads):
        hs = slice(h * VAL_DIM, (h + 1) * VAL_DIM)
        qf = q_ref[:, hs].astype(F32)
        qc = [jnp.where(lane < HEAD_DIM, qf, 0.0).astype(BF16),
              jnp.where(lane >= HEAD_DIM, qf, 0.0).astype(BF16)]
        keys = [k_ref[:, hs]]
        vals = [v_ref[:, hs]]
        if has_cache:
            keys.append(ck_ref[0, 0].astype(BF16))
            vals.append(cv_ref[0, 0].astype(BF16))

        probs = []
        for c in range(2):
            s = [lax.dot_general(qc[c], kk, NT_DIMS, preferred_element_type=F32) for kk in keys]
            m = s[0].max(axis=-1, keepdims=True)
            for sx in s[1:]:
                m = jnp.maximum(m, sx.max(axis=-1, keepdims=True))
            p = [jnp.exp(sx - m) for sx in s]
            den = p[0].sum(axis=-1, keepdims=True)
            for px in p[1:]:
                den = den + px.sum(axis=-1, keepdims=True)
            probs.append((p, den))

        r0 = 1.0 / probs[0][1]
        r1 = lam / probs[1][1]
        o = None
        for idx in range(len(keys)):
            a = (probs[0][0][idx] * r0 - probs[1][0][idx] * r1).astype(BF16)
            t = jnp.dot(a, vals[idx], preferred_element_type=F32)
            o = t if o is None else o + t
        ms = jnp.mean(o * o, axis=-1, keepdims=True)
        o_ref[:, hs] = ((o * lax.rsqrt(ms + EPS) * sub_ref[0]) * (1.0 - lam0)).astype(BF16)


def _attention(has_cache, l, q, k, v, cache_k, cache_v, lam_q1, lam_k1, lam_q2, lam_k2, subln_g):
    e = l // 2
    lam0 = 0.8 - 0.6 * math.exp(-0.3 * l)
    n_b, s_len = (N_LAT_B, LAT_S) if has_cache else (N_CTX_B, CTX_S)
    heads = 1 if has_cache else N_HEADS
    width = heads * VAL_DIM
    nq = s_len // TM
    kv = lambda: pl.BlockSpec((s_len, width), lambda b, h, qi: (b, h))
    par = lambda: pl.BlockSpec((1, 1, HEAD_DIM), lambda b, h, qi: (e, 0, 0))
    in_specs = [pl.BlockSpec((TM, width), lambda b, h, qi: (b * nq + qi, h)), kv(), kv()]
    args = [_hbm(q), _hbm(k), _hbm(v)]
    if has_cache:
        cs = lambda: pl.BlockSpec((1, 1, PAST, VAL_DIM), lambda b, h, qi: (b, e, 0, h))
        in_specs += [cs(), cs()]
        args += [cache_k.reshape(N_LAT_B, DEPTH // 2, PAST, QK_W),
                 cache_v.reshape(N_LAT_B, DEPTH // 2, PAST, MIX_A)]
    in_specs += [par(), par(), par(), par(),
                 pl.BlockSpec((1, 1, VAL_DIM), lambda b, h, qi: (e, 0, 0))]
    r3 = lambda a: a.reshape(DEPTH // 2, 1, -1)
    args += [r3(lam_q1), r3(lam_k1), r3(lam_q2), r3(lam_k2), r3(subln_g)]
    return pl.pallas_call(
        functools.partial(_attn_kernel, has_cache, heads, lam0),
        grid=(n_b, N_HEADS // heads, nq),
        in_specs=in_specs,
        out_specs=pl.BlockSpec((TM, width), lambda b, h, qi: (b * nq + qi, h)),
        out_shape=_hbm_out((n_b * s_len, MIX_A), BF16),
        compiler_params=_cparams(("arbitrary", "arbitrary", "arbitrary"), 32),
        name="attn_lat" if has_cache else "attn_ctx",
    )(*args)


def _route_tail(sub_tile, x_new, m_ref, n2_ref, rwt_ref, rbb_ref, tri_ref,
                h2_o, lp_o, rf_o, nch_o, loff_o, gb_o, cnt_o, carry):
    tok = slice(sub_tile * TM, (sub_tile + 1) * TM)
    if sub_tile == 0:
        @pl.when(pl.program_id(0) == 0)
        def _():
            carry[...] = jnp.zeros_like(carry)

    h2 = _modnorm(x_new, n2_ref[0], m_ref[0, 0, 4:5, :], m_ref[0, 0, 3:4, :])

    hh = h2.astype(BF16)
    h2_o[tok, :] = hh
    hl = (h2 - hh.astype(F32)).astype(BF16)
    rw = rwt_ref[...]
    wh = rw.astype(BF16)
    wl = (rw - wh.astype(F32)).astype(BF16)
    dg = lambda a, b: lax.dot_general(a, b, NT_DIMS, preferred_element_type=F32)
    logits = dg(wh, hh) + dg(wh, hl) + dg(wl, hh)
    score = _sigmoid(logits)
    sel = score + rbb_ref[...]

    neg_inf = jnp.full((1, TM), -jnp.inf, F32)
    grp = []
    for g in range(N_GROUPS):
        s_rows = [sel[PER_GROUP * g + j:PER_GROUP * g + j + 1, :] for j in range(PER_GROUP)]
        c_rows = [score[PER_GROUP * g + j:PER_GROUP * g + j + 1, :] for j in range(PER_GROUP)]
        f_idx = jnp.zeros((1, TM), I32)
        f_val, f_sc = s_rows[0], c_rows[0]
        for j in range(1, PER_GROUP):
            better = s_rows[j] > f_val
            f_idx = jnp.where(better, j, f_idx)
            f_val = jnp.where(better, s_rows[j], f_val)
            f_sc = jnp.where(better, c_rows[j], f_sc)
        s_idx = jnp.zeros((1, TM), I32)
        s_val, s_sc = neg_inf, jnp.zeros((1, TM), F32)
        for j in range(PER_GROUP):
            better = (f_idx != j) & (s_rows[j] > s_val)
            s_idx = jnp.where(better, j, s_idx)
            s_val = jnp.where(better, s_rows[j], s_val)
            s_sc = jnp.where(better, c_rows[j], s_sc)
        grp.append((f_val + s_val, PER_GROUP * g + f_idx, PER_GROUP * g + s_idx, f_sc, s_sc))

    best, e0, e1, c0, c1 = grp[0]
    for g in range(1, N_GROUPS):
        better = grp[g][0] > best
        best = jnp.where(better, grp[g][0], best)
        e0 = jnp.where(better, grp[g][1], e0)
        e1 = jnp.where(better, grp[g][2], e1)
        c0 = jnp.where(better, grp[g][3], c0)
        c1 = jnp.where(better, grp[g][4], c1)
    den = c0 + c1
    w0 = c0 / den
    w1 = c1 / den

    sub = lax.broadcasted_iota(I32, (N_EXPERTS, TM), 0)
    hit0 = sub == jnp.broadcast_to(e0, (N_EXPERTS, TM))
    hit1 = sub == jnp.broadcast_to(e1, (N_EXPERTS, TM))
    onehot = jnp.where(hit0 | hit1, 1.0, 0.0)
    prefix = jnp.dot(onehot.astype(BF16), tri_ref[...], preferred_element_type=F32)
    count = jnp.sum(onehot, axis=1, keepdims=True)
    padded = jnp.floor((count + (ROW_CHUNK - 1)) * (1.0 / ROW_CHUNK)) * ROW_CHUNK

    sub1 = lax.broadcasted_iota(I32, (N_EXPERTS, 1), 0)
    run = jnp.zeros((1, 1), F32)
    local_off = jnp.zeros((N_EXPERTS, 1), F32)
    for e in range(N_EXPERTS):
        local_off = jnp.where(sub1 == e, run, local_off)
        run = run + padded[e:e + 1, :]
    place = prefix + local_off
    lpos0 = jnp.sum(jnp.where(hit0, place, 0.0), axis=0, keepdims=True).astype(I32)
    lpos1 = jnp.sum(jnp.where(hit1, place, 0.0), axis=0, keepdims=True).astype(I32)

    lanes = lambda col: jnp.broadcast_to(col, (N_EXPERTS, 128)).astype(I32)
    nch_o[sub_tile] = lanes(padded * (1.0 / ROW_CHUNK))
    loff_o[sub_tile] = lanes(local_off)
    gb_o[sub_tile] = carry[...].astype(I32)
    carry[...] = carry[...] + padded
    cnt_o[...] = carry[...].astype(I32)

    sub8 = lax.broadcasted_iota(I32, (8, TM), 0)
    bc = lambda r: jnp.broadcast_to(r, (8, TM))
    lp_o[:, tok] = jnp.where(sub8 == 0, bc(lpos0), jnp.where(sub8 == 1, bc(lpos1), 0))
    rf_o[:, tok] = jnp.where(sub8 == 0, bc(w0), jnp.where(sub8 == 1, bc(w1), 0.0))


def _tail_specs(l, sub_tiles):
    rows = sub_tiles * TM
    in_specs = [
        pl.BlockSpec((1, 1, D), lambda i: (l, 0, 0)),
        pl.BlockSpec((N_EXPERTS, D), lambda i: (0, 0)),
        pl.BlockSpec((N_EXPERTS, TM), lambda i: (0, 0)),
        pl.BlockSpec((TM, TM), lambda i: (0, 0)),
    ]
    per_tile = lambda: pl.BlockSpec((sub_tiles, N_EXPERTS, 128), lambda i: (i, 0, 0))
    out_specs = [
        pl.BlockSpec((rows, D), lambda i: (i, 0)),
        pl.BlockSpec((8, rows), lambda i: (0, i)),
        pl.BlockSpec((8, rows), lambda i: (0, i)),
        per_tile(), per_tile(), per_tile(),
        pl.BlockSpec((N_EXPERTS, 128), lambda i: (0, 0)),
    ]
    out_shape = [
        _hbm_out((T, D), BF16),
        jax.ShapeDtypeStruct((8, T), I32),
        jax.ShapeDtypeStruct((8, T), F32),
        jax.ShapeDtypeStruct((N_TILES, N_EXPERTS, 128), I32),
        jax.ShapeDtypeStruct((N_TILES, N_EXPERTS, 128), I32),
        jax.ShapeDtypeStruct((N_TILES, N_EXPERTS, 128), I32),
        jax.ShapeDtypeStruct((N_EXPERTS, 128), I32),
    ]
    scratch = [pltpu.VMEM((N_EXPERTS, 128), F32)]
    return in_specs, out_specs, out_shape, scratch


def _out_kernel(split_x, *refs):
    x_refs, refs = refs[:1 + split_x], refs[1 + split_x:]
    (m_ref, ac_ref, al_ref, sc_ref, sl_ref, w_ref, n2_ref, rwt_ref, rbb_ref, tri_ref,
     x_o, *tail_refs) = refs
    *tail_outs, wbf, carry = tail_refs
    i = pl.program_id(0)

    @pl.when(i == 0)
    def _():
        wbf[...] = w_ref[0].astype(BF16)

    is_ctx = i < T_CTX // TO
    x = jnp.where(is_ctx, x_refs[0][...], x_refs[1][...]) if split_x else x_refs[0][...]
    att = jnp.where(is_ctx, ac_ref[...], al_ref[...])
    sgo = jnp.where(is_ctx, sc_ref[...], sl_ref[...])
    out = (jnp.dot(att, wbf[0:MIX_A, :], preferred_element_type=F32)
           + jnp.dot(sgo, wbf[MIX_A:MIX_A + MIX_B, :], preferred_element_type=F32))
    x_new = x + m_ref[0, 0, 2:3, :] * out
    x_o[...] = x_new
    for sub_tile in range(TO // TM):
        _route_tail(sub_tile, x_new[sub_tile * TM:(sub_tile + 1) * TM, :],
                    m_ref, n2_ref, rwt_ref, rbb_ref, tri_ref, *tail_outs, carry)


def _out_proj(l, x, mod, att_c, att_l, sg_c, sg_l, w_out, norm2_g, rwt, rbb, tri):
    e = l // 2
    t_in, t_out, t_shape, t_scratch = _tail_specs(l, TO // TM)
    n_ctx = T_CTX // TO
    ctx = lambda w: pl.BlockSpec((TO, w), lambda i: (jnp.minimum(i, n_ctx - 1), 0))
    lat = lambda w: pl.BlockSpec((TO, w), lambda i: (jnp.maximum(i - n_ctx, 0), 0))
    split_x = isinstance(x, tuple)
    if split_x:
        x_specs, x_args, aliases = [ctx(D), lat(D)], [_hbm(x[0]), _hbm(x[1])], {}
    else:
        x_specs, x_args, aliases = [pl.BlockSpec((TO, D), lambda i: (i, 0))], [_hbm(x)], {0: 0}
    in_specs = x_specs + [
        pl.BlockSpec((1, 1, N_MOD, D), lambda i: (l, _cond_of_tile(i * (TO // TM)), 0, 0)),
        ctx(MIX_A), lat(MIX_A), ctx(MIX_B), lat(MIX_B),
        pl.BlockSpec((1, D, D), lambda i: (e, 0, 0)),
    ] + t_in
    return pl.pallas_call(
        functools.partial(_out_kernel, split_x),
        grid=(T // TO,),
        in_specs=in_specs,
        out_specs=[pl.BlockSpec((TO, D), lambda i: (i, 0))] + t_out,
        out_shape=[_hbm_out((T, D), F32)] + t_shape,
        scratch_shapes=[pltpu.VMEM((D, D), BF16)] + t_scratch,
        input_output_aliases=aliases,
        compiler_params=_cparams(("arbitrary",), 40),
        name="out_proj_route",
    )(*x_args, mod, _hbm(att_c), _hbm(att_l), _hbm(sg_c), _hbm(sg_l), w_out,
      norm2_g.reshape(DEPTH, 1, D), rwt, rbb, tri)


def _pool_kernel(x_ref, xp_ref, xn_ref, m_ref, n1_ref, pw_ref, ps_ref,
                 n2_ref, rwt_ref, rbb_ref, tri_ref,
                 x_o, *tail_refs):
    *tail_outs, carry = tail_refs
    i = pl.program_id(0)
    is_lat = i >= N_CTX_TILES
    ti = jnp.where(is_lat, (i - N_CTX_TILES) % LAT_TILES_PER_SEQ, 0)
    last_ti = jnp.where(is_lat, LAT_TILES_PER_SEQ - 1, 0)
    seq_len = jnp.where(is_lat, LAT_S, CTX_S)

    g1n, sc1, sh1 = n1_ref[0], m_ref[0, 0, 1:2, :], m_ref[0, 0, 0:1, :]
    x = x_ref[...]
    h = _modnorm(x, g1n, sc1, sh1)
    hp = jnp.where(ti == 0, 0.0, _modnorm(xp_ref[...], g1n, sc1, sh1))
    hn = jnp.where(ti == last_ti, 0.0, _modnorm(xn_ref[...], g1n, sc1, sh1))
    hcat = jnp.concatenate([hp, h, hn], axis=0)
    n_rows = TM + 2 * POOL_HALO

    t_pos = ti * TM + lax.broadcasted_iota(I32, (TM, 1), 0)
    outs = []
    for g, w in enumerate(POOL_WINDOWS):
        cols = slice(g * POOL_DIM, (g + 1) * POOL_DIM)
        acc = hcat[:, cols]
        acc = acc + pltpu.roll(acc, 1, 0)
        step = 1
        while 2 * step < w:
            acc = pltpu.roll(acc, step, 0) + pltpu.roll(acc, n_rows - step, 0)
            step *= 2
        lo = jnp.maximum(t_pos - w // 2, 0)
        hi = jnp.minimum(t_pos - w // 2 + w, seq_len)
        mean = acc[POOL_HALO:POOL_HALO + TM, :] / (hi - lo).astype(F32)
        dlt = (mean - h[:, cols]).astype(BF16)
        outs.append(jnp.dot(dlt, pw_ref[0, g].astype(BF16), preferred_element_type=F32))
    out = jnp.concatenate(outs, axis=1) * ps_ref[0]
    x_new = x + m_ref[0, 0, 2:3, :] * out
    x_o[...] = x_new
    _route_tail(0, x_new, m_ref, n2_ref, rwt_ref, rbb_ref, tri_ref, *tail_outs, carry)


def _pool_mixer(l, x, mod, norm1_g, pool_w, pool_scale, norm2_g, rwt, rbb, tri):
    o = l // 2
    t_in, t_out, t_shape, t_scratch = _tail_specs(l, 1)
    halo_blocks = TM // POOL_HALO
    in_specs = [
        pl.BlockSpec((TM, D), lambda i: (i, 0)),
        pl.BlockSpec((POOL_HALO, D), lambda i: (jnp.maximum(i * halo_blocks - 1, 0), 0)),
        pl.BlockSpec((POOL_HALO, D), lambda i: (jnp.minimum((i + 1) * halo_blocks, T // POOL_HALO - 1), 0)),
        pl.BlockSpec((1, 1, N_MOD, D), lambda i: (l, _cond_of_tile(i), 0, 0)),
        pl.BlockSpec((1, 1, D), lambda i: (l, 0, 0)),
        pl.BlockSpec((1, len(POOL_WINDOWS), POOL_DIM, POOL_DIM), lambda i: (o, 0, 0, 0)),
        pl.BlockSpec((1, 1, D), lambda i: (o, 0, 0)),
    ] + t_in
    return pl.pallas_call(
        _pool_kernel,
        grid=(N_TILES,),
        in_specs=in_specs,
        out_specs=[pl.BlockSpec((TM, D), lambda i: (i, 0))] + t_out,
        out_shape=[_hbm_out((T, D), F32)] + t_shape,
        scratch_shapes=t_scratch,
        compiler_params=_cparams(("arbitrary",), 32),
        name="pool_route",
    )(_hbm(x), _hbm(x), _hbm(x), mod, norm1_g.reshape(DEPTH, 1, D), pool_w,
      pool_scale.reshape(DEPTH // 2, 1, D),
      norm2_g.reshape(DEPTH, 1, D), rwt, rbb, tri)


def _start_run_copies(tile, nch_ref, loff_ref, gb_ref, make_copy):
    for e in range(N_EXPERTS):
        n = nch_ref[tile * N_EXPERTS + e]
        lo = loff_ref[tile * N_EXPERTS + e]
        gb = gb_ref[tile * N_EXPERTS + e] + e * EXPERT_CAP

        def issue(c, carry, lo=lo, gb=gb):
            make_copy(pl.multiple_of(lo + c * (2 * ROW_CHUNK), ROW_CHUNK),
                      pl.multiple_of(gb + c * (2 * ROW_CHUNK), ROW_CHUNK), 2 * ROW_CHUNK).start()
            return carry

        lax.fori_loop(0, lax.shift_right_logical(n, 1), issue, 0)

        @pl.when((n & 1) == 1)
        def _(n=n, lo=lo, gb=gb):
            make_copy(pl.multiple_of(lo + (n - 1) * ROW_CHUNK, ROW_CHUNK),
                      pl.multiple_of(gb + (n - 1) * ROW_CHUNK, ROW_CHUNK), ROW_CHUNK).start()


def _wait_run_copies(tile, nch_ref, make_copy):
    doubles, singles = 0, 0
    for e in range(N_EXPERTS):
        n = nch_ref[tile * N_EXPERTS + e]
        doubles = doubles + lax.shift_right_logical(n, 1)
        singles = singles + (n & 1)

    def drain(rows):
        def body(c, carry):
            make_copy(0, 0, rows).wait()
            return carry
        return body

    lax.fori_loop(0, doubles, drain(2 * ROW_CHUNK), 0)
    lax.fori_loop(0, singles, drain(ROW_CHUNK), 0)


def _dispatch_kernel(nch_ref, loff_ref, gb_ref, h_ref, lp_ref, xs_ref, sbuf, sems):
    i = pl.program_id(0)
    slot = i % 2
    r = lax.broadcasted_iota(I32, (SORTED_ROWS, TM), 0)
    hit = (r == lp_ref[0:1, :]) | (r == lp_ref[1:2, :])
    perm = jnp.where(hit, 1.0, 0.0).astype(BF16)
    sbuf[slot] = jnp.dot(perm, h_ref[...], preferred_element_type=F32).astype(BF16)

    def copies_of(s):
        def make_copy(local_row, global_row, rows):
            return pltpu.make_async_copy(sbuf.at[s, pl.ds(local_row, rows)],
                                         xs_ref.at[pl.ds(global_row, rows)], sems.at[s])
        return make_copy

    @pl.when(i > 0)
    def _():
        _wait_run_copies(i - 1, nch_ref, copies_of(1 - slot))

    _start_run_copies(i, nch_ref, loff_ref, gb_ref, copies_of(slot))

    @pl.when(i == N_TILES - 1)
    def _():
        _wait_run_copies(i, nch_ref, copies_of(slot))


def _dispatch(layout, h2, lp):
    return pl.pallas_call(
        _dispatch_kernel,
        grid_spec=pltpu.PrefetchScalarGridSpec(
            num_scalar_prefetch=3,
            grid=(N_TILES,),
            in_specs=[pl.BlockSpec((TM, D), lambda i, *_: (i, 0)),
                      pl.BlockSpec((8, TM), lambda i, *_: (0, i))],
            out_specs=pl.BlockSpec(memory_space=pl.ANY),
            scratch_shapes=[pltpu.VMEM((2, SORTED_ROWS, D), BF16), pltpu.SemaphoreType.DMA((2,))],
        ),
        out_shape=_hbm_out((N_EXPERTS * EXPERT_CAP, D), BF16),
        compiler_params=_cparams(("arbitrary",), 16),
        name="moe_dispatch",
    )(*layout, _hbm(h2), lp)


def _moe_kernel(l, rb_ref, e_ref, nv_ref, first_ref, ord_ref, nxt_ref, xs_ref, wg_hbm, wu_hbm, wd_hbm, y_ref,
                wgf, wuf, wdf, wgb, wub, wdb, sems):
    i = pl.program_id(0)
    n_valid = nv_ref[i]

    def weight_copies(e, slot):
        return (pltpu.make_async_copy(wg_hbm.at[l, e], wgf.at[slot], sems.at[slot, 0]),
                pltpu.make_async_copy(wu_hbm.at[l, e], wuf.at[slot], sems.at[slot, 1]),
                pltpu.make_async_copy(wd_hbm.at[l, e], wdf.at[slot], sems.at[slot, 2]))

    @pl.when(first_ref[i] == 1)
    def _():
        slot = ord_ref[i] % 2

        @pl.when(ord_ref[i] == 0)
        def _():
            for cp in weight_copies(e_ref[i], slot):
                cp.start()

        for cp in weight_copies(e_ref[i], slot):
            cp.wait()

        @pl.when(nxt_ref[i] >= 0)
        def _():
            for cp in weight_copies(nxt_ref[i], 1 - slot):
                cp.start()

        wgb[...] = wgf[slot].astype(BF16)
        wub[...] = wuf[slot].astype(BF16)
        wdb[...] = wdf[slot].astype(BF16)

    @pl.when(n_valid > 0)
    def _():
        row = lax.broadcasted_iota(I32, (TE, 1), 0)
        x = jnp.where(row < n_valid, xs_ref[...], jnp.zeros((), BF16))
        gate = jnp.dot(x, wgb[...], preferred_element_type=F32)
        up = jnp.dot(x, wub[...], preferred_element_type=F32)
        act = (gate * _sigmoid(gate) * up).astype(BF16)
        y_ref[...] = jnp.dot(act, wdb[...], preferred_element_type=F32).astype(BF16)


def _experts(l, tile_map, xs, w_gate, w_up, w_down):
    n_pre = len(tile_map)
    blk = lambda i, rb, *_: (rb[i], 0)
    return pl.pallas_call(
        functools.partial(_moe_kernel, l),
        grid_spec=pltpu.PrefetchScalarGridSpec(
            num_scalar_prefetch=n_pre,
            grid=(MOE_TILES,),
            in_specs=[
                pl.BlockSpec((TE, D), blk),
                pl.BlockSpec(memory_space=pl.ANY),
                pl.BlockSpec(memory_space=pl.ANY),
                pl.BlockSpec(memory_space=pl.ANY),
            ],
            out_specs=pl.BlockSpec((TE, D), blk),
            scratch_shapes=[pltpu.VMEM((2, D, D_FF), F32), pltpu.VMEM((2, D, D_FF), F32),
                            pltpu.VMEM((2, D_FF, D), F32),
                            pltpu.VMEM((D, D_FF), BF16), pltpu.VMEM((D, D_FF), BF16),
                            pltpu.VMEM((D_FF, D), BF16),
                            pltpu.SemaphoreType.DMA((2, 3))],
        ),
        out_shape=_hbm_out((N_EXPERTS * EXPERT_CAP, D), BF16),
        compiler_params=_cparams(("arbitrary",), 40),
        name="moe_experts",
    )(*tile_map, _hbm(xs), w_gate, w_up, w_down)


def _combine_kernel(split_out, nch_ref, loff_ref, gb_ref, x_ref, m_ref, lpt_ref, w_ref, y_ref, *refs):
    *o_refs, ybuf, sems = refs
    i = pl.program_id(0)
    slot = i % 2

    def copies_of(s):
        def make_copy(local_row, global_row, rows):
            return pltpu.make_async_copy(y_ref.at[pl.ds(global_row, rows)],
                                         ybuf.at[s, pl.ds(local_row, rows)], sems.at[s])
        return make_copy

    @pl.when(i == 0)
    def _():
        ybuf[...] = jnp.zeros_like(ybuf)
        _start_run_copies(i, nch_ref, loff_ref, gb_ref, copies_of(slot))

    @pl.when(i + 1 < N_TILES)
    def _():
        _start_run_copies(i + 1, nch_ref, loff_ref, gb_ref, copies_of(1 - slot))

    _wait_run_copies(i, nch_ref, copies_of(slot))

    col = lax.broadcasted_iota(I32, (TM, SORTED_ROWS), 1)
    pick0 = jnp.where(col == lpt_ref[:, 0:1], 1.0, 0.0).astype(BF16)
    pick1 = jnp.where(col == lpt_ref[:, 1:2], 1.0, 0.0).astype(BF16)
    y0 = jnp.dot(pick0, ybuf[slot], preferred_element_type=F32)
    y1 = jnp.dot(pick1, ybuf[slot], preferred_element_type=F32)
    moe = w_ref[:, 0:1] * y0 + w_ref[:, 1:2] * y1
    res = x_ref[...] + m_ref[0, 0, 5:6, :] * moe
    if split_out:
        @pl.when(i < N_CTX_TILES)
        def _():
            o_refs[0][...] = res

        @pl.when(i >= N_CTX_TILES)
        def _():
            o_refs[1][...] = res
    else:
        o_refs[0][...] = res


def _combine(l, layout, x, mod, lpt, wcol, y, split_out):
    if split_out:
        out_specs = [pl.BlockSpec((TM, D), lambda i, *_: (jnp.minimum(i, N_CTX_TILES - 1), 0)),
                     pl.BlockSpec((TM, D), lambda i, *_: (jnp.maximum(i - N_CTX_TILES, 0), 0))]
        out_shape = [jax.ShapeDtypeStruct((T_CTX, D), F32), jax.ShapeDtypeStruct((T_LAT, D), F32)]
        aliases = {}
    else:
        out_specs = pl.BlockSpec((TM, D), lambda i, *_: (i, 0))
        out_shape = _hbm_out((T, D), F32)
        aliases = {3: 0}
    return pl.pallas_call(
        functools.partial(_combine_kernel, split_out),
        grid_spec=pltpu.PrefetchScalarGridSpec(
            num_scalar_prefetch=3,
            grid=(N_TILES,),
            in_specs=[
                pl.BlockSpec((TM, D), lambda i, *_: (i, 0)),
                pl.BlockSpec((1, 1, N_MOD, D), lambda i, *_: (l, _cond_of_tile(i), 0, 0)),
                pl.BlockSpec((TM, 8), lambda i, *_: (i, 0)),
                pl.BlockSpec((TM, 8), lambda i, *_: (i, 0)),
                pl.BlockSpec(memory_space=pl.ANY),
            ],
            out_specs=out_specs,
            scratch_shapes=[pltpu.VMEM((2, SORTED_ROWS, D), BF16), pltpu.SemaphoreType.DMA((2,))],
        ),
        out_shape=out_shape,
        input_output_aliases=aliases,
        compiler_params=_cparams(("arbitrary",), 24),
        name="moe_combine",
    )(*layout, _hbm(x), mod, lpt, wcol, _hbm(y))


def _tile_map(cnt):
    cnt = cnt[:, 0]
    ids = jnp.arange(N_EXPERTS, dtype=I32)
    n_tiles = (cnt + TE - 1) // TE
    cum = jnp.cumsum(n_tiles)
    total = cum[-1]
    i = jnp.arange(MOE_TILES, dtype=I32)
    ii = jnp.minimum(i, total - 1)
    e = jnp.sum((ii[:, None] >= cum[None, :]).astype(I32), axis=1)
    onehot = (e[:, None] == ids[None, :]).astype(I32)
    pick = lambda v: jnp.sum(onehot * v[None, :], axis=1)
    j = ii - pick(cum - n_tiles)
    active = i < total
    n_valid = jnp.where(active, jnp.clip(pick(cnt) - j * TE, 0, TE), 0)
    first = (active & (j == 0)).astype(I32)
    nonempty = n_tiles > 0
    ordinal = jnp.cumsum(nonempty.astype(I32)) - 1
    later = jnp.where(nonempty[None, :] & (ids[None, :] > ids[:, None]), ids[None, :], N_EXPERTS)
    nxt = jnp.min(later, axis=1)
    nxt = jnp.where(nxt == N_EXPERTS, -1, nxt)
    as_i32 = lambda v: v.astype(I32)
    return tuple(map(as_i32, (e * CAP_TILES + j, e, n_valid, first, pick(ordinal), pick(nxt))))


def _moe(l, x, tail, mod, w_gate, w_up, w_down, split_out):
    h2, lp, rf, nch, loff, gb, cnt = tail
    layout = tuple(a[:, :, 0].reshape(N_TILES * N_EXPERTS) for a in (nch, loff, gb))
    xs = _dispatch(layout, h2, lp)
    y = _experts(l, _tile_map(cnt), xs, w_gate, w_up, w_down)
    return _combine(l, layout, x, mod, lp.T, rf.T, y, split_out)


def _rope_tables():
    t = jnp.arange(LAT_S)
    row = (t // GRID_W).astype(F32)
    col = (t % GRID_W).astype(F32)
    nf = HEAD_DIM // 4
    inv = ROPE_BASE ** (-jnp.arange(nf, dtype=F32) / nf)
    ar = row[:, None] * inv[None, :]
    ac = col[:, None] * inv[None, :]
    cos = jnp.concatenate([jnp.cos(ar), jnp.cos(ar), jnp.cos(ac), jnp.cos(ac)], axis=1)
    sin = jnp.concatenate([-jnp.sin(ar), jnp.sin(ar), -jnp.sin(ac), jnp.sin(ac)], axis=1)
    reps = QK_W // HEAD_DIM
    return jnp.tile(cos, (1, reps)), jnp.tile(sin, (1, reps))


def kernel(x_prompt, x_sample, cache_k, cache_v, c, c_ctx, norm1_g, norm2_g, w_ada, b_ada, w_in, w_out,
           q_norm_g, k_norm_g, lam_q1, lam_k1, lam_q2, lam_k2, subln_g, sg_w, sg_b, pool_w, pool_scale,
           router_w, router_b, w_gate, w_up, w_down):
    x = (x_prompt.reshape(T_CTX, D), x_sample.reshape(T_LAT, D))
    cond = jnp.concatenate([c_ctx[None, :], c], axis=0)
    mod = _modulation(cond, w_ada, b_ada)

    seg_id = np.arange(QK_W) // HEAD_DIM
    seg = jnp.asarray((seg_id[:, None] == seg_id[None, :]) / HEAD_DIM, dtype=BF16)
    tri = jnp.asarray(np.arange(TM)[:, None] < np.arange(TM)[None, :], dtype=BF16)
    rope_tabs = _rope_tables()
    rwt = router_w.T
    rbb = jnp.broadcast_to(router_b[:, None], (N_EXPERTS, TM))
    states = None

    for l in range(DEPTH):
        if l % 2 == 0:
            e = l // 2
            qg = jnp.tile(q_norm_g[e], QK_W // HEAD_DIM)[None, :]
            kg = jnp.tile(k_norm_g[e], QK_W // HEAD_DIM)[None, :]
            sgb = jnp.broadcast_to(sg_b[:, :, :, None], (DEPTH // 2, SG_GROUPS, SG_CHUNK, 128))
            x_c, x_l = x if isinstance(x, tuple) else (x, x)
            common = (mod, norm1_g, w_in, qg, kg, seg, sg_w, sgb)
            q_c, k_c, v_c, sg_c, *states = _front(False, l, x_c, *common, None, states)
            q_l, k_l, v_l, sg_l = _front(True, l, x_l, *common, rope_tabs, None)
            lam = (lam_q1, lam_k1, lam_q2, lam_k2, subln_g)
            att_c = _attention(False, l, q_c, k_c, v_c, None, None, *lam)
            att_l = _attention(True, l, q_l, k_l, v_l, cache_k, cache_v, *lam)
            x, *tail = _out_proj(l, x, mod, att_c, att_l, sg_c, sg_l, w_out, norm2_g, rwt, rbb, tri)
        else:
            x, *tail = _pool_mixer(l, x, mod, norm1_g, pool_w, pool_scale, norm2_g, rwt, rbb, tri)
        x = _moe(l, x, tail, mod, w_gate, w_up, w_down, split_out=(l == DEPTH - 1))
    state_k, state_v = states

    y_prompt = x[0].reshape(N_CTX_B, CTX_S, D)
    y_sample = x[1].reshape(N_LAT_B, LAT_S, D)
    state_k = state_k.reshape(N_CTX_B, DEPTH // 2, CTX_S, N_HEADS, 2, HEAD_DIM)
    state_v = state_v.reshape(N_CTX_B, DEPTH // 2, CTX_S, N_HEADS, VAL_DIM)
    return y_prompt, y_sample, state_k, state_v
```

```python
import functools
import math

import jax
import jax.numpy as jnp
import numpy as np
from jax import lax
from jax.experimental import pallas as pl
from jax.experimental.pallas import tpu as pltpu

F32 = jnp.float32
BF16 = jnp.bfloat16
I32 = jnp.int32

D = 1024
N_CTX_B, CTX_S = 16, 256
N_LAT_B, LAT_S = 2, 1024
PAST = 512
DEPTH = 4
T_CTX = N_CTX_B * CTX_S
T_LAT = N_LAT_B * LAT_S
T = T_CTX + T_LAT
GRID_W = 64
N_HEADS = 4
HEAD_DIM = 64
VAL_DIM = 128
QK_W = 512
MIX_A = 512
MIX_B = 512
IN_W = 2560
SG_GROUPS = 4
SG_CHUNK = 128
POOL_WINDOWS = (2, 4, 8, 16)
POOL_DIM = 256
POOL_HALO = 8
N_EXPERTS = 16
N_GROUPS = 4
PER_GROUP = 4
D_FF = 512
N_MOD = 6
N_COND = 1 + N_LAT_B
EPS = 1e-6
ROPE_BASE = 10000.0

TM = 256
N_TILES = T // TM
N_CTX_TILES = T_CTX // TM
LAT_TILES_PER_SEQ = LAT_S // TM
ROW_CHUNK = 16
PAD_PER_RUN = ROW_CHUNK - 1
SORTED_ROWS = -(-(2 * TM + N_EXPERTS * PAD_PER_RUN) // 128) * 128
TE = 512
W_SLOTS = 3
TF = 512
TO = 512
CAP_TILES = -(-(T + N_TILES * PAD_PER_RUN) // TE)
EXPERT_CAP = CAP_TILES * TE
MOE_TILES = (2 * T + N_EXPERTS * N_TILES * PAD_PER_RUN) // TE + N_EXPERTS
MOD_TN = 1024
MIB = 1024 * 1024

NT_DIMS = (((1,), (1,)), ((), ()))


def _cparams(sem, vmem_mib):
    return pltpu.CompilerParams(dimension_semantics=sem, vmem_limit_bytes=vmem_mib * MIB)


def _hbm(x):
    try:
        return pltpu.with_memory_space_constraint(x, pltpu.HBM)
    except ValueError:
        return x


def _hbm_out(shape, dtype):
    return pltpu.HBM(shape, dtype)


def _sigmoid(x):
    return 1.0 / (1.0 + jnp.exp(-x))


def _gelu_tanh(x):
    c = math.sqrt(2.0 / math.pi)
    return x * (0.5 * (1.0 + jnp.tanh(c * (x + 0.044715 * (x * x * x)))))


def _modnorm(x, g, scale, shift):
    ms = jnp.mean(x * x, axis=-1, keepdims=True)
    return (x * lax.rsqrt(ms + EPS) * g) * (1.0 + scale) + shift


def _cond_of_tile(i):
    return jnp.where(i < N_CTX_TILES, 0, 1 + (i - N_CTX_TILES) // LAT_TILES_PER_SEQ)


def _mod_kernel(c_ref, w_ref, b_ref, o_ref):
    c = c_ref[...]
    s = (c * _sigmoid(c)).astype(BF16)
    o_ref[0] = jnp.dot(s, w_ref[0].astype(BF16), preferred_element_type=F32) + b_ref[0]


def _modulation(cond, w_ada, b_ada):
    cond8 = jnp.concatenate([cond, jnp.zeros((8 - N_COND, D), F32)], axis=0)
    out = pl.pallas_call(
        _mod_kernel,
        grid=(DEPTH, N_MOD * D // MOD_TN),
        in_specs=[
            pl.BlockSpec((8, D), lambda l, n: (0, 0)),
            pl.BlockSpec((1, D, MOD_TN), lambda l, n: (l, 0, n)),
            pl.BlockSpec((1, 1, MOD_TN), lambda l, n: (l, 0, n)),
        ],
        out_specs=pl.BlockSpec((1, 8, MOD_TN), lambda l, n: (l, 0, n)),
        out_shape=jax.ShapeDtypeStruct((DEPTH, 8, N_MOD * D), F32),
        compiler_params=_cparams(("arbitrary", "arbitrary"), 24),
        name="adaln_mod",
    )(cond8, w_ada, b_ada.reshape(DEPTH, 1, N_MOD * D))
    return out.reshape(DEPTH, 8, N_MOD, D)


def _front_kernel(latent, n_state_in, *refs):
    (x_ref, m_ref, n1_ref, w_ref, qg_ref, kg_ref, seg_ref, sgw_ref, sgb_ref), refs = refs[:9], refs[9:]
    if latent:
        cos_ref, sin_ref, q_o, k_o, v_o, sg_o, wbf = refs
    else:
        q_o, k_o, v_o, sg_o, sk_o, sv_o, wbf = refs[n_state_in:]

    @pl.when(pl.program_id(0) == 0)
    def _():
        for c0 in range(0, IN_W, 512):
            wbf[:, c0:c0 + 512] = w_ref[0, :, c0:c0 + 512].astype(BF16)

    h = _modnorm(x_ref[...], n1_ref[0], m_ref[0, 0, 1:2, :], m_ref[0, 0, 0:1, :])
    proj = jnp.dot(h.astype(BF16), wbf[...], preferred_element_type=F32)

    def qk_norm(z, g):
        ms = jnp.dot((z * z).astype(BF16), seg_ref[...], preferred_element_type=F32)
        return z * lax.rsqrt(ms + EPS) * g

    q = qk_norm(proj[:, 0:QK_W], qg_ref[...])
    k = qk_norm(proj[:, QK_W:2 * QK_W], kg_ref[...])
    v = proj[:, 2 * QK_W:2 * QK_W + MIX_A]

    if latent:
        lane = lax.broadcasted_iota(I32, (1, QK_W), 1)
        first = (lane % 32) < 16

        def rope(z):
            partner = jnp.where(first, pltpu.roll(z, QK_W - 16, 1), pltpu.roll(z, 16, 1))
            return z * cos_ref[...] + partner * sin_ref[...]

        q = rope(q)
        k = rope(k)
    else:
        for b in range(TF // CTX_S):
            sk_o[b, 0] = k[b * CTX_S:(b + 1) * CTX_S, :]
            sv_o[b, 0] = v[b * CTX_S:(b + 1) * CTX_S, :]

    q_o[...] = (q * (HEAD_DIM ** -0.5)).astype(BF16)
    k_o[...] = k.astype(BF16)
    v_o[...] = v.astype(BF16)

    gu0 = 2 * QK_W + MIX_A
    gv0 = gu0 + MIX_B
    for g in range(SG_GROUPS):
        wg = sgw_ref[0, g].astype(BF16)
        for n in range(TF // SG_CHUNK):
            rows = slice(n * SG_CHUNK, (n + 1) * SG_CHUNK)
            cu = slice(gu0 + g * 128, gu0 + (g + 1) * 128)
            cv = slice(gv0 + g * 128, gv0 + (g + 1) * 128)
            gv = _gelu_tanh(proj[rows, cv]).astype(BF16)
            mixed = jnp.dot(wg, gv, preferred_element_type=F32) + sgb_ref[0, g]
            sg_o[rows, g * 128:(g + 1) * 128] = (_gelu_tanh(proj[rows, cu]) * mixed).astype(BF16)


def _front(latent, l, x, mod, norm1_g, w_in, qg, kg, seg, sg_w, sgb, rope_tabs, states):
    e = l // 2
    rows = T_LAT if latent else T_CTX
    n_tiles = rows // TF
    off = T_CTX // TF if (latent and x.shape[0] == T) else 0
    tiles_per_seq = LAT_S // TF
    if latent:
        cond_map = lambda i: (l, 1 + i // tiles_per_seq, 0, 0)
    else:
        cond_map = lambda i: (l, 0, 0, 0)
    in_specs = [
        pl.BlockSpec((TF, D), lambda i: (i + off, 0)),
        pl.BlockSpec((1, 1, N_MOD, D), cond_map),
        pl.BlockSpec((1, 1, D), lambda i: (l, 0, 0)),
        pl.BlockSpec((1, D, IN_W), lambda i: (e, 0, 0)),
        pl.BlockSpec((1, QK_W), lambda i: (0, 0)),
        pl.BlockSpec((1, QK_W), lambda i: (0, 0)),
        pl.BlockSpec((QK_W, QK_W), lambda i: (0, 0)),
        pl.BlockSpec((1, SG_GROUPS, SG_CHUNK, SG_CHUNK), lambda i: (e, 0, 0, 0)),
        pl.BlockSpec((1, SG_GROUPS, SG_CHUNK, 128), lambda i: (e, 0, 0, 0)),
    ]
    args = [_hbm(x), mod, norm1_g.reshape(DEPTH, 1, D), w_in, qg, kg, seg, sg_w, sgb]
    tok = lambda: pl.BlockSpec((TF, QK_W), lambda i: (i, 0))
    out_specs = [tok(), tok(), tok(), tok()]
    out_shape = [_hbm_out((rows, QK_W), BF16)] * 4
    aliases = {}
    if latent:
        cos_t, sin_t = rope_tabs
        in_specs += [pl.BlockSpec((TF, QK_W), lambda i: (i % tiles_per_seq, 0))] * 2
        args += [cos_t, sin_t]
    else:
        if states:
            in_specs += [pl.BlockSpec(memory_space=pl.ANY)] * 2
            args += list(states)
            aliases = {9: 4, 10: 5}
        st = lambda: pl.BlockSpec((TF // CTX_S, 1, CTX_S, QK_W), lambda i: (i, e, 0, 0))
        out_specs += [st(), st()]
        out_shape += [_hbm_out((N_CTX_B, DEPTH // 2, CTX_S, QK_W), F32)] * 2
    return pl.pallas_call(
        functools.partial(_front_kernel, latent, len(aliases)),
        grid=(n_tiles,),
        in_specs=in_specs,
        out_specs=out_specs,
        out_shape=out_shape,
        scratch_shapes=[pltpu.VMEM((D, IN_W), BF16)],
        input_output_aliases=aliases,
        compiler_params=_cparams(("arbitrary",), 48),
        name="front_lat" if latent else "front_ctx",
    )(*args)


def _attn_kernel(has_cache, heads, lam0, *refs):
    if has_cache:
        (q_ref, k_ref, v_ref, ck_ref, cv_ref, lq1, lk1, lq2, lk2, sub_ref, o_ref) = refs
    else:
        (q_ref, k_ref, v_ref, lq1, lk1, lq2, lk2, sub_ref, o_ref) = refs

    lane = lax.broadcasted_iota(I32, (1, VAL_DIM), 1)
    lam = (jnp.exp(jnp.sum(lq1[0] * lk1[0], axis=-1, keepdims=True))
           - jnp.exp(jnp.sum(lq2[0] * lk2[0], axis=-1, keepdims=True)) + lam0)

    for h in range(heads):
        hs = slice(h * VAL_DIM, (h + 1) * VAL_DIM)
        qf = q_ref[:, hs].astype(F32)
        qc = [jnp.where(lane < HEAD_DIM, qf, 0.0).astype(BF16),
              jnp.where(lane >= HEAD_DIM, qf, 0.0).astype(BF16)]
        keys = [k_ref[:, hs]]
        vals = [v_ref[:, hs]]
        if has_cache:
            keys.append(ck_ref[0, 0].astype(BF16))
            vals.append(cv_ref[0, 0].astype(BF16))

        probs = []
        for c in range(2):
            s = [lax.dot_general(qc[c], kk, NT_DIMS, preferred_element_type=F32) for kk in keys]
            m = s[0].max(axis=-1, keepdims=True)
            for sx in s[1:]:
                m = jnp.maximum(m, sx.max(axis=-1, keepdims=True))
            p = [jnp.exp(sx - m) for sx in s]
            den = p[0].sum(axis=-1, keepdims=True)
            for px in p[1:]:
                den = den + px.sum(axis=-1, keepdims=True)
            probs.append((p, den))

        r0 = 1.0 / probs[0][1]
        r1 = lam / probs[1][1]
        o = None
        for idx in range(len(keys)):
            a = (probs[0][0][idx] * r0 - probs[1][0][idx] * r1).astype(BF16)
            t = jnp.dot(a, vals[idx], preferred_element_type=F32)
            o = t if o is None else o + t
        ms = jnp.mean(o * o, axis=-1, keepdims=True)
        o_ref[:, hs] = ((o * lax.rsqrt(ms + EPS) * sub_ref[0]) * (1.0 - lam0)).astype(BF16)


def _attention(has_cache, l, q, k, v, cache_k, cache_v, lam_q1, lam_k1, lam_q2, lam_k2, subln_g):
    e = l // 2
    lam0 = 0.8 - 0.6 * math.exp(-0.3 * l)
    n_b, s_len = (N_LAT_B, LAT_S) if has_cache else (N_CTX_B, CTX_S)
    heads = 1 if has_cache else N_HEADS
    width = heads * VAL_DIM
    nq = s_len // TM
    kv = lambda: pl.BlockSpec((s_len, width), lambda b, h, qi: (b, h))
    par = lambda: pl.BlockSpec((1, 1, HEAD_DIM), lambda b, h, qi: (e, 0, 0))
    in_specs = [pl.BlockSpec((TM, width), lambda b, h, qi: (b * nq + qi, h)), kv(), kv()]
    args = [_hbm(q), _hbm(k), _hbm(v)]
    if has_cache:
        cs = lambda: pl.BlockSpec((1, 1, PAST, VAL_DIM), lambda b, h, qi: (b, e, 0, h))
        in_specs += [cs(), cs()]
        args += [cache_k.reshape(N_LAT_B, DEPTH // 2, PAST, QK_W),
                 cache_v.reshape(N_LAT_B, DEPTH // 2, PAST, MIX_A)]
    in_specs += [par(), par(), par(), par(),
                 pl.BlockSpec((1, 1, VAL_DIM), lambda b, h, qi: (e, 0, 0))]
    r3 = lambda a: a.reshape(DEPTH // 2, 1, -1)
    args += [r3(lam_q1), r3(lam_k1), r3(lam_q2), r3(lam_k2), r3(subln_g)]
    return pl.pallas_call(
        functools.partial(_attn_kernel, has_cache, heads, lam0),
        grid=(n_b, N_HEADS // heads, nq),
        in_specs=in_specs,
        out_specs=pl.BlockSpec((TM, width), lambda b, h, qi: (b * nq + qi, h)),
        out_shape=_hbm_out((n_b * s_len, MIX_A), BF16),
        compiler_params=_cparams(("arbitrary", "arbitrary", "arbitrary"), 32),
        name="attn_lat" if has_cache else "attn_ctx",
    )(*args)


def _route_tail(sub_tile, x_new, m_ref, n2_ref, rwt_ref, rbb_ref, tri_ref,
                h2_o, lp_o, rf_o, nch_o, loff_o, gb_o, cnt_o, carry):
    tok = slice(sub_tile * TM, (sub_tile + 1) * TM)
    if sub_tile == 0:
        @pl.when(pl.program_id(0) == 0)
        def _():
            carry[...] = jnp.zeros_like(carry)

    h2 = _modnorm(x_new, n2_ref[0], m_ref[0, 0, 4:5, :], m_ref[0, 0, 3:4, :])

    hh = h2.astype(BF16)
    h2_o[tok, :] = hh
    hl = (h2 - hh.astype(F32)).astype(BF16)
    rw = rwt_ref[...]
    wh = rw.astype(BF16)
    wl = (rw - wh.astype(F32)).astype(BF16)
    dg = lambda a, b: lax.dot_general(a, b, NT_DIMS, preferred_element_type=F32)
    logits = dg(wh, hh) + dg(wh, hl) + dg(wl, hh)
    score = _sigmoid(logits)
    sel = score + rbb_ref[...]

    neg_inf = jnp.full((1, TM), -jnp.inf, F32)
    grp = []
    for g in range(N_GROUPS):
        s_rows = [sel[PER_GROUP * g + j:PER_GROUP * g + j + 1, :] for j in range(PER_GROUP)]
        c_rows = [score[PER_GROUP * g + j:PER_GROUP * g + j + 1, :] for j in range(PER_GROUP)]
        f_idx = jnp.zeros((1, TM), I32)
        f_val, f_sc = s_rows[0], c_rows[0]
        for j in range(1, PER_GROUP):
            better = s_rows[j] > f_val
            f_idx = jnp.where(better, j, f_idx)
            f_val = jnp.where(better, s_rows[j], f_val)
            f_sc = jnp.where(better, c_rows[j], f_sc)
        s_idx = jnp.zeros((1, TM), I32)
        s_val, s_sc = neg_inf, jnp.zeros((1, TM), F32)
        for j in range(PER_GROUP):
            better = (f_idx != j) & (s_rows[j] > s_val)
            s_idx = jnp.where(better, j, s_idx)
            s_val = jnp.where(better, s_rows[j], s_val)
            s_sc = jnp.where(better, c_rows[j], s_sc)
        grp.append((f_val + s_val, PER_GROUP * g + f_idx, PER_GROUP * g + s_idx, f_sc, s_sc))

    best, e0, e1, c0, c1 = grp[0]
    for g in range(1, N_GROUPS):
        better = grp[g][0] > best
        best = jnp.where(better, grp[g][0], best)
        e0 = jnp.where(better, grp[g][1], e0)
        e1 = jnp.where(better, grp[g][2], e1)
        c0 = jnp.where(better, grp[g][3], c0)
        c1 = jnp.where(better, grp[g][4], c1)
    den = c0 + c1
    w0 = c0 / den
    w1 = c1 / den

    sub = lax.broadcasted_iota(I32, (N_EXPERTS, TM), 0)
    hit0 = sub == jnp.broadcast_to(e0, (N_EXPERTS, TM))
    hit1 = sub == jnp.broadcast_to(e1, (N_EXPERTS, TM))
    onehot = jnp.where(hit0 | hit1, 1.0, 0.0)
    prefix = jnp.dot(onehot.astype(BF16), tri_ref[...], preferred_element_type=F32)
    count = jnp.sum(onehot, axis=1, keepdims=True)
    padded = jnp.floor((count + (ROW_CHUNK - 1)) * (1.0 / ROW_CHUNK)) * ROW_CHUNK

    sub1 = lax.broadcasted_iota(I32, (N_EXPERTS, 1), 0)
    run = jnp.zeros((1, 1), F32)
    local_off = jnp.zeros((N_EXPERTS, 1), F32)
    for e in range(N_EXPERTS):
        local_off = jnp.where(sub1 == e, run, local_off)
        run = run + padded[e:e + 1, :]
    place = prefix + local_off
    lpos0 = jnp.sum(jnp.where(hit0, place, 0.0), axis=0, keepdims=True).astype(I32)
    lpos1 = jnp.sum(jnp.where(hit1, place, 0.0), axis=0, keepdims=True).astype(I32)

    lanes = lambda col: jnp.broadcast_to(col, (N_EXPERTS, 128)).astype(I32)
    nch_o[sub_tile] = lanes(padded * (1.0 / ROW_CHUNK))
    loff_o[sub_tile] = lanes(local_off)
    gb_o[sub_tile] = carry[...].astype(I32)
    carry[...] = carry[...] + padded
    cnt_o[...] = carry[...].astype(I32)

    sub8 = lax.broadcasted_iota(I32, (8, TM), 0)
    bc = lambda r: jnp.broadcast_to(r, (8, TM))
    lp_o[:, tok] = jnp.where(sub8 == 0, bc(lpos0), jnp.where(sub8 == 1, bc(lpos1), 0))
    rf_o[:, tok] = jnp.where(sub8 == 0, bc(w0), jnp.where(sub8 == 1, bc(w1), 0.0))


def _tail_specs(l, sub_tiles):
    rows = sub_tiles * TM
    in_specs = [
        pl.BlockSpec((1, 1, D), lambda i: (l, 0, 0)),
        pl.BlockSpec((N_EXPERTS, D), lambda i: (0, 0)),
        pl.BlockSpec((N_EXPERTS, TM), lambda i: (0, 0)),
        pl.BlockSpec((TM, TM), lambda i: (0, 0)),
    ]
    per_tile = lambda: pl.BlockSpec((sub_tiles, N_EXPERTS, 128), lambda i: (i, 0, 0))
    out_specs = [
        pl.BlockSpec((rows, D), lambda i: (i, 0)),
        pl.BlockSpec((8, rows), lambda i: (0, i)),
        pl.BlockSpec((8, rows), lambda i: (0, i)),
        per_tile(), per_tile(), per_tile(),
        pl.BlockSpec((N_EXPERTS, 128), lambda i: (0, 0)),
    ]
    out_shape = [
        _hbm_out((T, D), BF16),
        jax.ShapeDtypeStruct((8, T), I32),
        jax.ShapeDtypeStruct((8, T), F32),
        jax.ShapeDtypeStruct((N_TILES, N_EXPERTS, 128), I32),
        jax.ShapeDtypeStruct((N_TILES, N_EXPERTS, 128), I32),
        jax.ShapeDtypeStruct((N_TILES, N_EXPERTS, 128), I32),
        jax.ShapeDtypeStruct((N_EXPERTS, 128), I32),
    ]
    scratch = [pltpu.VMEM((N_EXPERTS, 128), F32)]
    return in_specs, out_specs, out_shape, scratch


def _out_kernel(split_x, *refs):
    x_refs, refs = refs[:1 + split_x], refs[1 + split_x:]
    (m_ref, ac_ref, al_ref, sc_ref, sl_ref, w_ref, n2_ref, rwt_ref, rbb_ref, tri_ref,
     x_o, *tail_refs) = refs
    *tail_outs, wbf, carry = tail_refs
    i = pl.program_id(0)

    @pl.when(i == 0)
    def _():
        wbf[...] = w_ref[0].astype(BF16)

    is_ctx = i < T_CTX // TO
    x = jnp.where(is_ctx, x_refs[0][...], x_refs[1][...]) if split_x else x_refs[0][...]
    att = jnp.where(is_ctx, ac_ref[...], al_ref[...])
    sgo = jnp.where(is_ctx, sc_ref[...], sl_ref[...])
    out = (jnp.dot(att, wbf[0:MIX_A, :], preferred_element_type=F32)
           + jnp.dot(sgo, wbf[MIX_A:MIX_A + MIX_B, :], preferred_element_type=F32))
    x_new = x + m_ref[0, 0, 2:3, :] * out
    x_o[...] = x_new
    for sub_tile in range(TO // TM):
        _route_tail(sub_tile, x_new[sub_tile * TM:(sub_tile + 1) * TM, :],
                    m_ref, n2_ref, rwt_ref, rbb_ref, tri_ref, *tail_outs, carry)


def _out_proj(l, x, mod, att_c, att_l, sg_c, sg_l, w_out, norm2_g, rwt, rbb, tri):
    e = l // 2
    t_in, t_out, t_shape, t_scratch = _tail_specs(l, TO // TM)
    n_ctx = T_CTX // TO
    ctx = lambda w: pl.BlockSpec((TO, w), lambda i: (jnp.minimum(i, n_ctx - 1), 0))
    lat = lambda w: pl.BlockSpec((TO, w), lambda i: (jnp.maximum(i - n_ctx, 0), 0))
    split_x = isinstance(x, tuple)
    if split_x:
        x_specs, x_args, aliases = [ctx(D), lat(D)], [_hbm(x[0]), _hbm(x[1])], {}
    else:
        x_specs, x_args, aliases = [pl.BlockSpec((TO, D), lambda i: (i, 0))], [_hbm(x)], {0: 0}
    in_specs = x_specs + [
        pl.BlockSpec((1, 1, N_MOD, D), lambda i: (l, _cond_of_tile(i * (TO // TM)), 0, 0)),
        ctx(MIX_A), lat(MIX_A), ctx(MIX_B), lat(MIX_B),
        pl.BlockSpec((1, D, D), lambda i: (e, 0, 0)),
    ] + t_in
    return pl.pallas_call(
        functools.partial(_out_kernel, split_x),
        grid=(T // TO,),
        in_specs=in_specs,
        out_specs=[pl.BlockSpec((TO, D), lambda i: (i, 0))] + t_out,
        out_shape=[_hbm_out((T, D), F32)] + t_shape,
        scratch_shapes=[pltpu.VMEM((D, D), BF16)] + t_scratch,
        input_output_aliases=aliases,
        compiler_params=_cparams(("arbitrary",), 40),
        name="out_proj_route",
    )(*x_args, mod, _hbm(att_c), _hbm(att_l), _hbm(sg_c), _hbm(sg_l), w_out,
      norm2_g.reshape(DEPTH, 1, D), rwt, rbb, tri)


def _pool_kernel(x_ref, xp_ref, xn_ref, m_ref, n1_ref, pw_ref, ps_ref,
                 n2_ref, rwt_ref, rbb_ref, tri_ref,
                 x_o, *tail_refs):
    *tail_outs, carry = tail_refs
    i = pl.program_id(0)
    is_lat = i >= N_CTX_TILES
    ti = jnp.where(is_lat, (i - N_CTX_TILES) % LAT_TILES_PER_SEQ, 0)
    last_ti = jnp.where(is_lat, LAT_TILES_PER_SEQ - 1, 0)
    seq_len = jnp.where(is_lat, LAT_S, CTX_S)

    g1n, sc1, sh1 = n1_ref[0], m_ref[0, 0, 1:2, :], m_ref[0, 0, 0:1, :]
    x = x_ref[...]
    h = _modnorm(x, g1n, sc1, sh1)
    hp = jnp.where(ti == 0, 0.0, _modnorm(xp_ref[...], g1n, sc1, sh1))
    hn = jnp.where(ti == last_ti, 0.0, _modnorm(xn_ref[...], g1n, sc1, sh1))
    hcat = jnp.concatenate([hp, h, hn], axis=0)
    n_rows = TM + 2 * POOL_HALO

    t_pos = ti * TM + lax.broadcasted_iota(I32, (TM, 1), 0)
    outs = []
    for g, w in enumerate(POOL_WINDOWS):
        cols = slice(g * POOL_DIM, (g + 1) * POOL_DIM)
        acc = hcat[:, cols]
        acc = acc + pltpu.roll(acc, 1, 0)
        step = 1
        while 2 * step < w:
            acc = pltpu.roll(acc, step, 0) + pltpu.roll(acc, n_rows - step, 0)
            step *= 2
        lo = jnp.maximum(t_pos - w // 2, 0)
        hi = jnp.minimum(t_pos - w // 2 + w, seq_len)
        mean = acc[POOL_HALO:POOL_HALO + TM, :] / (hi - lo).astype(F32)
        dlt = (mean - h[:, cols]).astype(BF16)
        outs.append(jnp.dot(dlt, pw_ref[0, g].astype(BF16), preferred_element_type=F32))
    out = jnp.concatenate(outs, axis=1) * ps_ref[0]
    x_new = x + m_ref[0, 0, 2:3, :] * out
    x_o[...] = x_new
    _route_tail(0, x_new, m_ref, n2_ref, rwt_ref, rbb_ref, tri_ref, *tail_outs, carry)


def _pool_mixer(l, x, mod, norm1_g, pool_w, pool_scale, norm2_g, rwt, rbb, tri):
    o = l // 2
    t_in, t_out, t_shape, t_scratch = _tail_specs(l, 1)
    halo_blocks = TM // POOL_HALO
    in_specs = [
        pl.BlockSpec((TM, D), lambda i: (i, 0)),
        pl.BlockSpec((POOL_HALO, D), lambda i: (jnp.maximum(i * halo_blocks - 1, 0), 0)),
        pl.BlockSpec((POOL_HALO, D), lambda i: (jnp.minimum((i + 1) * halo_blocks, T // POOL_HALO - 1), 0)),
        pl.BlockSpec((1, 1, N_MOD, D), lambda i: (l, _cond_of_tile(i), 0, 0)),
        pl.BlockSpec((1, 1, D), lambda i: (l, 0, 0)),
        pl.BlockSpec((1, len(POOL_WINDOWS), POOL_DIM, POOL_DIM), lambda i: (o, 0, 0, 0)),
        pl.BlockSpec((1, 1, D), lambda i: (o, 0, 0)),
    ] + t_in
    return pl.pallas_call(
        _pool_kernel,
        grid=(N_TILES,),
        in_specs=in_specs,
        out_specs=[pl.BlockSpec((TM, D), lambda i: (i, 0))] + t_out,
        out_shape=[_hbm_out((T, D), F32)] + t_shape,
        scratch_shapes=t_scratch,
        compiler_params=_cparams(("arbitrary",), 32),
        name="pool_route",
    )(_hbm(x), _hbm(x), _hbm(x), mod, norm1_g.reshape(DEPTH, 1, D), pool_w,
      pool_scale.reshape(DEPTH // 2, 1, D),
      norm2_g.reshape(DEPTH, 1, D), rwt, rbb, tri)


def _start_run_copies(tile, nch_ref, loff_ref, gb_ref, make_copy):
    for e in range(N_EXPERTS):
        n = nch_ref[tile * N_EXPERTS + e]
        lo = loff_ref[tile * N_EXPERTS + e]
        gb = gb_ref[tile * N_EXPERTS + e] + e * EXPERT_CAP

        def issue(c, carry, lo=lo, gb=gb):
            make_copy(pl.multiple_of(lo + c * (2 * ROW_CHUNK), ROW_CHUNK),
                      pl.multiple_of(gb + c * (2 * ROW_CHUNK), ROW_CHUNK), 2 * ROW_CHUNK).start()
            return carry

        lax.fori_loop(0, lax.shift_right_logical(n, 1), issue, 0)

        @pl.when((n & 1) == 1)
        def _(n=n, lo=lo, gb=gb):
            make_copy(pl.multiple_of(lo + (n - 1) * ROW_CHUNK, ROW_CHUNK),
                      pl.multiple_of(gb + (n - 1) * ROW_CHUNK, ROW_CHUNK), ROW_CHUNK).start()


def _wait_run_copies(tile, nch_ref, make_copy):
    doubles, singles = 0, 0
    for e in range(N_EXPERTS):
        n = nch_ref[tile * N_EXPERTS + e]
        doubles = doubles + lax.shift_right_logical(n, 1)
        singles = singles + (n & 1)

    def drain(rows):
        def body(c, carry):
            make_copy(0, 0, rows).wait()
            return carry
        return body

    lax.fori_loop(0, doubles, drain(2 * ROW_CHUNK), 0)
    lax.fori_loop(0, singles, drain(ROW_CHUNK), 0)


def _dispatch_kernel(nch_ref, loff_ref, gb_ref, h_ref, lp_ref, xs_ref, sbuf, sems):
    i = pl.program_id(0)
    slot = i % 2
    r = lax.broadcasted_iota(I32, (SORTED_ROWS, TM), 0)
    hit = (r == lp_ref[0:1, :]) | (r == lp_ref[1:2, :])
    perm = jnp.where(hit, 1.0, 0.0).astype(BF16)
    sbuf[slot] = jnp.dot(perm, h_ref[...], preferred_element_type=F32).astype(BF16)

    def copies_of(s):
        def make_copy(local_row, global_row, rows):
            return pltpu.make_async_copy(sbuf.at[s, pl.ds(local_row, rows)],
                                         xs_ref.at[pl.ds(global_row, rows)], sems.at[s])
        return make_copy

    @pl.when(i > 0)
    def _():
        _wait_run_copies(i - 1, nch_ref, copies_of(1 - slot))

    _start_run_copies(i, nch_ref, loff_ref, gb_ref, copies_of(slot))

    @pl.when(i == N_TILES - 1)
    def _():
        _wait_run_copies(i, nch_ref, copies_of(slot))


def _dispatch(layout, h2, lp):
    return pl.pallas_call(
        _dispatch_kernel,
        grid_spec=pltpu.PrefetchScalarGridSpec(
            num_scalar_prefetch=3,
            grid=(N_TILES,),
            in_specs=[pl.BlockSpec((TM, D), lambda i, *_: (i, 0)),
                      pl.BlockSpec((8, TM), lambda i, *_: (0, i))],
            out_specs=pl.BlockSpec(memory_space=pl.ANY),
            scratch_shapes=[pltpu.VMEM((2, SORTED_ROWS, D), BF16), pltpu.SemaphoreType.DMA((2,))],
        ),
        out_shape=_hbm_out((N_EXPERTS * EXPERT_CAP, D), BF16),
        compiler_params=_cparams(("arbitrary",), 16),
        name="moe_dispatch",
    )(*layout, _hbm(h2), lp)


def _moe_kernel(l, rb_ref, e_ref, nv_ref, first_ref, ord_ref, nxt_ref, nxt2_ref,
                xs_ref, wg_hbm, wu_hbm, wd_hbm, y_ref, wgf, wuf, wdf, wgb, wub, wdb, sems):
    i = pl.program_id(0)
    n_valid = nv_ref[i]

    def weight_copies(e, slot):
        return (pltpu.make_async_copy(wg_hbm.at[l, e], wgf.at[slot], sems.at[slot, 0]),
                pltpu.make_async_copy(wu_hbm.at[l, e], wuf.at[slot], sems.at[slot, 1]),
                pltpu.make_async_copy(wd_hbm.at[l, e], wdf.at[slot], sems.at[slot, 2]))

    @pl.when(first_ref[i] == 1)
    def _():
        k = ord_ref[i]
        slot = k % W_SLOTS

        @pl.when(k == 0)
        def _():
            for cp in weight_copies(e_ref[i], 0):
                cp.start()

            @pl.when(nxt_ref[i] >= 0)
            def _():
                for cp in weight_copies(nxt_ref[i], 1):
                    cp.start()

        for cp in weight_copies(e_ref[i], slot):
            cp.wait()

        @pl.when(nxt2_ref[i] >= 0)
        def _():
            for cp in weight_copies(nxt2_ref[i], (k + 2) % W_SLOTS):
                cp.start()

        wgb[...] = wgf[slot].astype(BF16)
        wub[...] = wuf[slot].astype(BF16)
        wdb[...] = wdf[slot].astype(BF16)

    @pl.when(n_valid > 0)
    def _():
        row = lax.broadcasted_iota(I32, (TE, 1), 0)
        x = jnp.where(row < n_valid, xs_ref[...], jnp.zeros((), BF16))
        gate = jnp.dot(x, wgb[...], preferred_element_type=F32)
        up = jnp.dot(x, wub[...], preferred_element_type=F32)
        act = (gate * _sigmoid(gate) * up).astype(BF16)
        y_ref[...] = jnp.dot(act, wdb[...], preferred_element_type=F32).astype(BF16)


def _experts(l, tile_map, xs, w_gate, w_up, w_down):
    n_pre = len(tile_map)
    blk = lambda i, rb, *_: (rb[i], 0)
    return pl.pallas_call(
        functools.partial(_moe_kernel, l),
        grid_spec=pltpu.PrefetchScalarGridSpec(
            num_scalar_prefetch=n_pre,
            grid=(MOE_TILES,),
            in_specs=[
                pl.BlockSpec((TE, D), blk),
                pl.BlockSpec(memory_space=pl.ANY),
                pl.BlockSpec(memory_space=pl.ANY),
                pl.BlockSpec(memory_space=pl.ANY),
            ],
            out_specs=pl.BlockSpec((TE, D), blk),
            scratch_shapes=[pltpu.VMEM((W_SLOTS, D, D_FF), F32), pltpu.VMEM((W_SLOTS, D, D_FF), F32),
                            pltpu.VMEM((W_SLOTS, D_FF, D), F32),
                            pltpu.VMEM((D, D_FF), BF16), pltpu.VMEM((D, D_FF), BF16),
                            pltpu.VMEM((D_FF, D), BF16),
                            pltpu.SemaphoreType.DMA((W_SLOTS, 3))],
        ),
        out_shape=_hbm_out((N_EXPERTS * EXPERT_CAP, D), BF16),
        compiler_params=_cparams(("arbitrary",), 48),
        name="moe_experts",
    )(*tile_map, _hbm(xs), w_gate, w_up, w_down)


def _combine_kernel(split_out, nch_ref, loff_ref, gb_ref, x_ref, m_ref, lpt_ref, w_ref, y_ref, *refs):
    *o_refs, ybuf, sems = refs
    i = pl.program_id(0)
    slot = i % 2

    def copies_of(s):
        def make_copy(local_row, global_row, rows):
            return pltpu.make_async_copy(y_ref.at[pl.ds(global_row, rows)],
                                         ybuf.at[s, pl.ds(local_row, rows)], sems.at[s])
        return make_copy

    @pl.when(i == 0)
    def _():
        ybuf[...] = jnp.zeros_like(ybuf)
        _start_run_copies(i, nch_ref, loff_ref, gb_ref, copies_of(slot))

    @pl.when(i + 1 < N_TILES)
    def _():
        _start_run_copies(i + 1, nch_ref, loff_ref, gb_ref, copies_of(1 - slot))

    _wait_run_copies(i, nch_ref, copies_of(slot))

    col = lax.broadcasted_iota(I32, (TM, SORTED_ROWS), 1)
    pick0 = jnp.where(col == lpt_ref[:, 0:1], 1.0, 0.0).astype(BF16)
    pick1 = jnp.where(col == lpt_ref[:, 1:2], 1.0, 0.0).astype(BF16)
    y0 = jnp.dot(pick0, ybuf[slot], preferred_element_type=F32)
    y1 = jnp.dot(pick1, ybuf[slot], preferred_element_type=F32)
    moe = w_ref[:, 0:1] * y0 + w_ref[:, 1:2] * y1
    res = x_ref[...] + m_ref[0, 0, 5:6, :] * moe
    if split_out:
        @pl.when(i < N_CTX_TILES)
        def _():
            o_refs[0][...] = res

        @pl.when(i >= N_CTX_TILES)
        def _():
            o_refs[1][...] = res
    else:
        o_refs[0][...] = res


def _combine(l, layout, x, mod, lpt, wcol, y, split_out):
    if split_out:
        out_specs = [pl.BlockSpec((TM, D), lambda i, *_: (jnp.minimum(i, N_CTX_TILES - 1), 0)),
                     pl.BlockSpec((TM, D), lambda i, *_: (jnp.maximum(i - N_CTX_TILES, 0), 0))]
        out_shape = [jax.ShapeDtypeStruct((T_CTX, D), F32), jax.ShapeDtypeStruct((T_LAT, D), F32)]
        aliases = {}
    else:
        out_specs = pl.BlockSpec((TM, D), lambda i, *_: (i, 0))
        out_shape = _hbm_out((T, D), F32)
        aliases = {3: 0}
    return pl.pallas_call(
        functools.partial(_combine_kernel, split_out),
        grid_spec=pltpu.PrefetchScalarGridSpec(
            num_scalar_prefetch=3,
            grid=(N_TILES,),
            in_specs=[
                pl.BlockSpec((TM, D), lambda i, *_: (i, 0)),
                pl.BlockSpec((1, 1, N_MOD, D), lambda i, *_: (l, _cond_of_tile(i), 0, 0)),
                pl.BlockSpec((TM, 8), lambda i, *_: (i, 0)),
                pl.BlockSpec((TM, 8), lambda i, *_: (i, 0)),
                pl.BlockSpec(memory_space=pl.ANY),
            ],
            out_specs=out_specs,
            scratch_shapes=[pltpu.VMEM((2, SORTED_ROWS, D), BF16), pltpu.SemaphoreType.DMA((2,))],
        ),
        out_shape=out_shape,
        input_output_aliases=aliases,
        compiler_params=_cparams(("arbitrary",), 24),
        name="moe_combine",
    )(*layout, _hbm(x), mod, lpt, wcol, _hbm(y))


def _tile_map(cnt):
    cnt = cnt[:, 0]
    ids = jnp.arange(N_EXPERTS, dtype=I32)
    n_tiles = (cnt + TE - 1) // TE
    cum = jnp.cumsum(n_tiles)
    total = cum[-1]
    i = jnp.arange(MOE_TILES, dtype=I32)
    ii = jnp.minimum(i, total - 1)
    e = jnp.sum((ii[:, None] >= cum[None, :]).astype(I32), axis=1)
    onehot = (e[:, None] == ids[None, :]).astype(I32)
    pick = lambda v: jnp.sum(onehot * v[None, :], axis=1)
    j = ii - pick(cum - n_tiles)
    active = i < total
    n_valid = jnp.where(active, jnp.clip(pick(cnt) - j * TE, 0, TE), 0)
    first = (active & (j == 0)).astype(I32)
    nonempty = n_tiles > 0
    ordinal = jnp.cumsum(nonempty.astype(I32)) - 1
    later = jnp.where(nonempty[None, :] & (ids[None, :] > ids[:, None]), ids[None, :], N_EXPERTS)
    nxt = jnp.min(later, axis=1)
    nxt_of = lambda v: jnp.sum((v[:, None] == ids[None, :]).astype(I32) * nxt[None, :], axis=1)
    nxt2 = jnp.where(nxt == N_EXPERTS, N_EXPERTS, nxt_of(nxt))
    none = lambda v: jnp.where(v == N_EXPERTS, -1, v)
    as_i32 = lambda v: v.astype(I32)
    return tuple(map(as_i32, (e * CAP_TILES + j, e, n_valid, first, pick(ordinal),
                              pick(none(nxt)), pick(none(nxt2)))))


def _moe(l, x, tail, mod, w_gate, w_up, w_down, split_out):
    h2, lp, rf, nch, loff, gb, cnt = tail
    layout = tuple(a[:, :, 0].reshape(N_TILES * N_EXPERTS) for a in (nch, loff, gb))
    xs = _dispatch(layout, h2, lp)
    y = _experts(l, _tile_map(cnt), xs, w_gate, w_up, w_down)
    return _combine(l, layout, x, mod, lp.T, rf.T, y, split_out)


def _rope_tables():
    t = jnp.arange(LAT_S)
    row = (t // GRID_W).astype(F32)
    col = (t % GRID_W).astype(F32)
    nf = HEAD_DIM // 4
    inv = ROPE_BASE ** (-jnp.arange(nf, dtype=F32) / nf)
    ar = row[:, None] * inv[None, :]
    ac = col[:, None] * inv[None, :]
    cos = jnp.concatenate([jnp.cos(ar), jnp.cos(ar), jnp.cos(ac), jnp.cos(ac)], axis=1)
    sin = jnp.concatenate([-jnp.sin(ar), jnp.sin(ar), -jnp.sin(ac), jnp.sin(ac)], axis=1)
    reps = QK_W // HEAD_DIM
    return jnp.tile(cos, (1, reps)), jnp.tile(sin, (1, reps))


def kernel(x_prompt, x_sample, cache_k, cache_v, c, c_ctx, norm1_g, norm2_g, w_ada, b_ada, w_in, w_out,
           q_norm_g, k_norm_g, lam_q1, lam_k1, lam_q2, lam_k2, subln_g, sg_w, sg_b, pool_w, pool_scale,
           router_w, router_b, w_gate, w_up, w_down):
    x = (x_prompt.reshape(T_CTX, D), x_sample.reshape(T_LAT, D))
    cond = jnp.concatenate([c_ctx[None, :], c], axis=0)
    mod = _modulation(cond, w_ada, b_ada)

    seg_id = np.arange(QK_W) // HEAD_DIM
    seg = jnp.asarray((seg_id[:, None] == seg_id[None, :]) / HEAD_DIM, dtype=BF16)
    tri = jnp.asarray(np.arange(TM)[:, None] < np.arange(TM)[None, :], dtype=BF16)
    rope_tabs = _rope_tables()
    rwt = router_w.T
    rbb = jnp.broadcast_to(router_b[:, None], (N_EXPERTS, TM))
    states = None

    for l in range(DEPTH):
        if l % 2 == 0:
            e = l // 2
            qg = jnp.tile(q_norm_g[e], QK_W // HEAD_DIM)[None, :]
            kg = jnp.tile(k_norm_g[e], QK_W // HEAD_DIM)[None, :]
            sgb = jnp.broadcast_to(sg_b[:, :, :, None], (DEPTH // 2, SG_GROUPS, SG_CHUNK, 128))
            x_c, x_l = x if isinstance(x, tuple) else (x, x)
            common = (mod, norm1_g, w_in, qg, kg, seg, sg_w, sgb)
            q_c, k_c, v_c, sg_c, *states = _front(False, l, x_c, *common, None, states)
            q_l, k_l, v_l, sg_l = _front(True, l, x_l, *common, rope_tabs, None)
            lam = (lam_q1, lam_k1, lam_q2, lam_k2, subln_g)
            att_c = _attention(False, l, q_c, k_c, v_c, None, None, *lam)
            att_l = _attention(True, l, q_l, k_l, v_l, cache_k, cache_v, *lam)
            x, *tail = _out_proj(l, x, mod, att_c, att_l, sg_c, sg_l, w_out, norm2_g, rwt, rbb, tri)
        else:
            x, *tail = _pool_mixer(l, x, mod, norm1_g, pool_w, pool_scale, norm2_g, rwt, rbb, tri)
        x = _moe(l, x, tail, mod, w_gate, w_up, w_down, split_out=(l == DEPTH - 1))
    state_k, state_v = states

    y_prompt = x[0].reshape(N_CTX_B, CTX_S, D)
    y_sample = x[1].reshape(N_LAT_B, LAT_S, D)
    state_k = state_k.reshape(N_CTX_B, DEPTH // 2, CTX_S, N_HEADS, 2, HEAD_DIM)
    state_v = state_v.reshape(N_CTX_B, DEPTH // 2, CTX_S, N_HEADS, VAL_DIM)
    return y_prompt, y_sample, state_k, state_v
```

```python
import functools
import math

import jax
import jax.numpy as jnp
import numpy as np
from jax import lax
from jax.experimental import pallas as pl
from jax.experimental.pallas import tpu as pltpu

F32 = jnp.float32
BF16 = jnp.bfloat16
I32 = jnp.int32

D = 1024
N_CTX_B, CTX_S = 16, 256
N_LAT_B, LAT_S = 2, 1024
PAST = 512
DEPTH = 4
T_CTX = N_CTX_B * CTX_S
T_LAT = N_LAT_B * LAT_S
T = T_CTX + T_LAT
GRID_W = 64
N_HEADS = 4
HEAD_DIM = 64
VAL_DIM = 128
QK_W = 512
MIX_A = 512
MIX_B = 512
IN_W = 2560
SG_GROUPS = 4
SG_CHUNK = 128
POOL_WINDOWS = (2, 4, 8, 16)
POOL_DIM = 256
POOL_HALO = 8
N_EXPERTS = 16
N_GROUPS = 4
PER_GROUP = 4
D_FF = 512
N_MOD = 6
N_COND = 1 + N_LAT_B
EPS = 1e-6
ROPE_BASE = 10000.0

TM = 256
N_TILES = T // TM
N_CTX_TILES = T_CTX // TM
LAT_TILES_PER_SEQ = LAT_S // TM
ROW_CHUNK = 16
PAD_PER_RUN = ROW_CHUNK - 1
SORTED_ROWS = -(-(2 * TM + N_EXPERTS * PAD_PER_RUN) // 128) * 128
TE = 512
W_SLOTS = 3
TF = 512
TO = 512
CAP_TILES = -(-(T + N_TILES * PAD_PER_RUN) // TE)
EXPERT_CAP = CAP_TILES * TE
MOE_TILES = (2 * T + N_EXPERTS * N_TILES * PAD_PER_RUN) // TE + N_EXPERTS
MOD_TN = 1024
MIB = 1024 * 1024

NT_DIMS = (((1,), (1,)), ((), ()))


def _cparams(sem, vmem_mib):
    return pltpu.CompilerParams(dimension_semantics=sem, vmem_limit_bytes=vmem_mib * MIB)


def _hbm(x):
    try:
        return pltpu.with_memory_space_constraint(x, pltpu.HBM)
    except ValueError:
        return x


def _hbm_out(shape, dtype):
    return pltpu.HBM(shape, dtype)


def _sigmoid(x):
    return 1.0 / (1.0 + jnp.exp(-x))


def _gelu_tanh(x):
    c = math.sqrt(2.0 / math.pi)
    return x * (0.5 * (1.0 + jnp.tanh(c * (x + 0.044715 * (x * x * x)))))


def _modnorm(x, g, scale, shift):
    ms = jnp.mean(x * x, axis=-1, keepdims=True)
    return (x * lax.rsqrt(ms + EPS) * g) * (1.0 + scale) + shift


def _cond_of_tile(i):
    return jnp.where(i < N_CTX_TILES, 0, 1 + (i - N_CTX_TILES) // LAT_TILES_PER_SEQ)


def _mod_kernel(c_ref, w_ref, b_ref, o_ref):
    c = c_ref[...]
    s = (c * _sigmoid(c)).astype(BF16)
    o_ref[0] = jnp.dot(s, w_ref[0].astype(BF16), preferred_element_type=F32) + b_ref[0]


def _modulation(cond, w_ada, b_ada):
    cond8 = jnp.concatenate([cond, jnp.zeros((8 - N_COND, D), F32)], axis=0)
    out = pl.pallas_call(
        _mod_kernel,
        grid=(DEPTH, N_MOD * D // MOD_TN),
        in_specs=[
            pl.BlockSpec((8, D), lambda l, n: (0, 0)),
            pl.BlockSpec((1, D, MOD_TN), lambda l, n: (l, 0, n)),
            pl.BlockSpec((1, 1, MOD_TN), lambda l, n: (l, 0, n)),
        ],
        out_specs=pl.BlockSpec((1, 8, MOD_TN), lambda l, n: (l, 0, n)),
        out_shape=jax.ShapeDtypeStruct((DEPTH, 8, N_MOD * D), F32),
        compiler_params=_cparams(("arbitrary", "arbitrary"), 24),
        name="adaln_mod",
    )(cond8, w_ada, b_ada.reshape(DEPTH, 1, N_MOD * D))
    return out.reshape(DEPTH, 8, N_MOD, D)


def _front_kernel(latent, n_state_in, *refs):
    (x_ref, m_ref, n1_ref, w_ref, qg_ref, kg_ref, seg_ref, sgw_ref, sgb_ref), refs = refs[:9], refs[9:]
    if latent:
        cos_ref, sin_ref, q_o, k_o, v_o, sg_o, wbf = refs
    else:
        q_o, k_o, v_o, sg_o, sk_o, sv_o, wbf = refs[n_state_in:]

    @pl.when(pl.program_id(0) == 0)
    def _():
        for c0 in range(0, IN_W, 512):
            wbf[:, c0:c0 + 512] = w_ref[0, :, c0:c0 + 512].astype(BF16)

    h = _modnorm(x_ref[...], n1_ref[0], m_ref[0, 0, 1:2, :], m_ref[0, 0, 0:1, :])
    proj = jnp.dot(h.astype(BF16), wbf[...], preferred_element_type=F32)

    def qk_norm(z, g):
        ms = jnp.dot((z * z).astype(BF16), seg_ref[...], preferred_element_type=F32)
        return z * lax.rsqrt(ms + EPS) * g

    q = qk_norm(proj[:, 0:QK_W], qg_ref[...])
    k = qk_norm(proj[:, QK_W:2 * QK_W], kg_ref[...])
    v = proj[:, 2 * QK_W:2 * QK_W + MIX_A]

    if latent:
        lane = lax.broadcasted_iota(I32, (1, QK_W), 1)
        first = (lane % 32) < 16

        def rope(z):
            partner = jnp.where(first, pltpu.roll(z, QK_W - 16, 1), pltpu.roll(z, 16, 1))
            return z * cos_ref[...] + partner * sin_ref[...]

        q = rope(q)
        k = rope(k)
    else:
        for b in range(TF // CTX_S):
            sk_o[b, 0] = k[b * CTX_S:(b + 1) * CTX_S, :]
            sv_o[b, 0] = v[b * CTX_S:(b + 1) * CTX_S, :]

    q_o[...] = (q * (HEAD_DIM ** -0.5)).astype(BF16)
    k_o[...] = k.astype(BF16)
    v_o[...] = v.astype(BF16)

    gu0 = 2 * QK_W + MIX_A
    gv0 = gu0 + MIX_B
    for g in range(SG_GROUPS):
        wg = sgw_ref[0, g].astype(BF16)
        for n in range(TF // SG_CHUNK):
            rows = slice(n * SG_CHUNK, (n + 1) * SG_CHUNK)
            cu = slice(gu0 + g * 128, gu0 + (g + 1) * 128)
            cv = slice(gv0 + g * 128, gv0 + (g + 1) * 128)
            gv = _gelu_tanh(proj[rows, cv]).astype(BF16)
            mixed = jnp.dot(wg, gv, preferred_element_type=F32) + sgb_ref[0, g]
            sg_o[rows, g * 128:(g + 1) * 128] = (_gelu_tanh(proj[rows, cu]) * mixed).astype(BF16)


def _front(latent, l, x, mod, norm1_g, w_in, qg, kg, seg, sg_w, sgb, rope_tabs, states):
    e = l // 2
    rows = T_LAT if latent else T_CTX
    n_tiles = rows // TF
    off = T_CTX // TF if (latent and x.shape[0] == T) else 0
    tiles_per_seq = LAT_S // TF
    if latent:
        cond_map = lambda i: (l, 1 + i // tiles_per_seq, 0, 0)
    else:
        cond_map = lambda i: (l, 0, 0, 0)
    in_specs = [
        pl.BlockSpec((TF, D), lambda i: (i + off, 0)),
        pl.BlockSpec((1, 1, N_MOD, D), cond_map),
        pl.BlockSpec((1, 1, D), lambda i: (l, 0, 0)),
        pl.BlockSpec((1, D, IN_W), lambda i: (e, 0, 0)),
        pl.BlockSpec((1, QK_W), lambda i: (0, 0)),
        pl.BlockSpec((1, QK_W), lambda i: (0, 0)),
        pl.BlockSpec((QK_W, QK_W), lambda i: (0, 0)),
        pl.BlockSpec((1, SG_GROUPS, SG_CHUNK, SG_CHUNK), lambda i: (e, 0, 0, 0)),
        pl.BlockSpec((1, SG_GROUPS, SG_CHUNK, 128), lambda i: (e, 0, 0, 0)),
    ]
    args = [_hbm(x), mod, norm1_g.reshape(DEPTH, 1, D), w_in, qg, kg, seg, sg_w, sgb]
    tok = lambda: pl.BlockSpec((TF, QK_W), lambda i: (i, 0))
    out_specs = [tok(), tok(), tok(), tok()]
    out_shape = [_hbm_out((rows, QK_W), BF16)] * 4
    aliases = {}
    if latent:
        cos_t, sin_t = rope_tabs
        in_specs += [pl.BlockSpec((TF, QK_W), lambda i: (i % tiles_per_seq, 0))] * 2
        args += [cos_t, sin_t]
    else:
        if states:
            in_specs += [pl.BlockSpec(memory_space=pl.ANY)] * 2
            args += list(states)
            aliases = {9: 4, 10: 5}
        st = lambda: pl.BlockSpec((TF // CTX_S, 1, CTX_S, QK_W), lambda i: (i, e, 0, 0))
        out_specs += [st(), st()]
        out_shape += [_hbm_out((N_CTX_B, DEPTH // 2, CTX_S, QK_W), F32)] * 2
    return pl.pallas_call(
        functools.partial(_front_kernel, latent, len(aliases)),
        grid=(n_tiles,),
        in_specs=in_specs,
        out_specs=out_specs,
        out_shape=out_shape,
        scratch_shapes=[pltpu.VMEM((D, IN_W), BF16)],
        input_output_aliases=aliases,
        compiler_params=_cparams(("arbitrary",), 48),
        name="front_lat" if latent else "front_ctx",
    )(*args)


def _attn_kernel(has_cache, heads, lam0, *refs):
    if has_cache:
        (q_ref, k_ref, v_ref, ck_ref, cv_ref, lq1, lk1, lq2, lk2, sub_ref, o_ref) = refs
    else:
        (q_ref, k_ref, v_ref, lq1, lk1, lq2, lk2, sub_ref, o_ref) = refs

    lane = lax.broadcasted_iota(I32, (1, VAL_DIM), 1)
    lam = (jnp.exp(jnp.sum(lq1[0] * lk1[0], axis=-1, keepdims=True))
           - jnp.exp(jnp.sum(lq2[0] * lk2[0], axis=-1, keepdims=True)) + lam0)

    for h in range(heads):
        hs = slice(h * VAL_DIM, (h + 1) * VAL_DIM)
        qf = q_ref[:, hs].astype(F32)
        qc = [jnp.where(lane < HEAD_DIM, qf, 0.0).astype(BF16),
              jnp.where(lane >= HEAD_DIM, qf, 0.0).astype(BF16)]
        keys = [k_ref[:, hs]]
        vals = [v_ref[:, hs]]
        if has_cache:
            keys.append(ck_ref[0, 0].astype(BF16))
            vals.append(cv_ref[0, 0].astype(BF16))

        probs = []
        for c in range(2):
            s = [lax.dot_general(qc[c], kk, NT_DIMS, preferred_element_type=F32) for kk in keys]
            m = s[0].max(axis=-1, keepdims=True)
            for sx in s[1:]:
                m = jnp.maximum(m, sx.max(axis=-1, keepdims=True))
            p = [jnp.exp(sx - m) for sx in s]
            den = p[0].sum(axis=-1, keepdims=True)
            for px in p[1:]:
                den = den + px.sum(axis=-1, keepdims=True)
            probs.append((p, den))

        r0 = 1.0 / probs[0][1]
        r1 = lam / probs[1][1]
        o = None
        for idx in range(len(keys)):
            a = (probs[0][0][idx] * r0 - probs[1][0][idx] * r1).astype(BF16)
            t = jnp.dot(a, vals[idx], preferred_element_type=F32)
            o = t if o is None else o + t
        ms = jnp.mean(o * o, axis=-1, keepdims=True)
        o_ref[:, hs] = ((o * lax.rsqrt(ms + EPS) * sub_ref[0]) * (1.0 - lam0)).astype(BF16)


def _attention(has_cache, l, q, k, v, cache_k, cache_v, lam_q1, lam_k1, lam_q2, lam_k2, subln_g):
    e = l // 2
    lam0 = 0.8 - 0.6 * math.exp(-0.3 * l)
    n_b, s_len = (N_LAT_B, LAT_S) if has_cache else (N_CTX_B, CTX_S)
    heads = 1 if has_cache else N_HEADS
    width = heads * VAL_DIM
    nq = s_len // TM
    kv = lambda: pl.BlockSpec((s_len, width), lambda b, h, qi: (b, h))
    par = lambda: pl.BlockSpec((1, 1, HEAD_DIM), lambda b, h, qi: (e, 0, 0))
    in_specs = [pl.BlockSpec((TM, width), lambda b, h, qi: (b * nq + qi, h)), kv(), kv()]
    args = [_hbm(q), _hbm(k), _hbm(v)]
    if has_cache:
        cs = lambda: pl.BlockSpec((1, 1, PAST, VAL_DIM), lambda b, h, qi: (b, e, 0, h))
        in_specs += [cs(), cs()]
        args += [cache_k.reshape(N_LAT_B, DEPTH // 2, PAST, QK_W),
                 cache_v.reshape(N_LAT_B, DEPTH // 2, PAST, MIX_A)]
    in_specs += [par(), par(), par(), par(),
                 pl.BlockSpec((1, 1, VAL_DIM), lambda b, h, qi: (e, 0, 0))]
    r3 = lambda a: a.reshape(DEPTH // 2, 1, -1)
    args += [r3(lam_q1), r3(lam_k1), r3(lam_q2), r3(lam_k2), r3(subln_g)]
    return pl.pallas_call(
        functools.partial(_attn_kernel, has_cache, heads, lam0),
        grid=(n_b, N_HEADS // heads, nq),
        in_specs=in_specs,
        out_specs=pl.BlockSpec((TM, width), lambda b, h, qi: (b * nq + qi, h)),
        out_shape=_hbm_out((n_b * s_len, MIX_A), BF16),
        compiler_params=_cparams(("arbitrary", "arbitrary", "arbitrary"), 32),
        name="attn_lat" if has_cache else "attn_ctx",
    )(*args)


def _route_tail(sub_tile, x_new, m_ref, n2_ref, rwt_ref, rbb_ref, tri_ref,
                h2_o, lp_o, rf_o, nch_o, loff_o, gb_o, cnt_o, carry):
    tok = slice(sub_tile * TM, (sub_tile + 1) * TM)
    if sub_tile == 0:
        @pl.when(pl.program_id(0) == 0)
        def _():
            carry[...] = jnp.zeros_like(carry)

    h2 = _modnorm(x_new, n2_ref[0], m_ref[0, 0, 4:5, :], m_ref[0, 0, 3:4, :])

    hh = h2.astype(BF16)
    h2_o[tok, :] = hh
    hl = (h2 - hh.astype(F32)).astype(BF16)
    rw = rwt_ref[...]
    wh = rw.astype(BF16)
    wl = (rw - wh.astype(F32)).astype(BF16)
    dg = lambda a, b: lax.dot_general(a, b, NT_DIMS, preferred_element_type=F32)
    logits = dg(wh, hh) + dg(wh, hl) + dg(wl, hh)
    score = _sigmoid(logits)
    sel = score + rbb_ref[...]

    neg_inf = jnp.full((1, TM), -jnp.inf, F32)
    grp = []
    for g in range(N_GROUPS):
        s_rows = [sel[PER_GROUP * g + j:PER_GROUP * g + j + 1, :] for j in range(PER_GROUP)]
        c_rows = [score[PER_GROUP * g + j:PER_GROUP * g + j + 1, :] for j in range(PER_GROUP)]
        f_idx = jnp.zeros((1, TM), I32)
        f_val, f_sc = s_rows[0], c_rows[0]
        for j in range(1, PER_GROUP):
            better = s_rows[j] > f_val
            f_idx = jnp.where(better, j, f_idx)
            f_val = jnp.where(better, s_rows[j], f_val)
            f_sc = jnp.where(better, c_rows[j], f_sc)
        s_idx = jnp.zeros((1, TM), I32)
        s_val, s_sc = neg_inf, jnp.zeros((1, TM), F32)
        for j in range(PER_GROUP):
            better = (f_idx != j) & (s_rows[j] > s_val)
            s_idx = jnp.where(better, j, s_idx)
            s_val = jnp.where(better, s_rows[j], s_val)
            s_sc = jnp.where(better, c_rows[j], s_sc)
        grp.append((f_val + s_val, PER_GROUP * g + f_idx, PER_GROUP * g + s_idx, f_sc, s_sc))

    best, e0, e1, c0, c1 = grp[0]
    for g in range(1, N_GROUPS):
        better = grp[g][0] > best
        best = jnp.where(better, grp[g][0], best)
        e0 = jnp.where(better, grp[g][1], e0)
        e1 = jnp.where(better, grp[g][2], e1)
        c0 = jnp.where(better, grp[g][3], c0)
        c1 = jnp.where(better, grp[g][4], c1)
    den = c0 + c1
    w0 = c0 / den
    w1 = c1 / den

    sub = lax.broadcasted_iota(I32, (N_EXPERTS, TM), 0)
    hit0 = sub == jnp.broadcast_to(e0, (N_EXPERTS, TM))
    hit1 = sub == jnp.broadcast_to(e1, (N_EXPERTS, TM))
    onehot = jnp.where(hit0 | hit1, 1.0, 0.0)
    prefix = jnp.dot(onehot.astype(BF16), tri_ref[...], preferred_element_type=F32)
    count = jnp.sum(onehot, axis=1, keepdims=True)
    padded = jnp.floor((count + (ROW_CHUNK - 1)) * (1.0 / ROW_CHUNK)) * ROW_CHUNK

    sub1 = lax.broadcasted_iota(I32, (N_EXPERTS, 1), 0)
    run = jnp.zeros((1, 1), F32)
    local_off = jnp.zeros((N_EXPERTS, 1), F32)
    for e in range(N_EXPERTS):
        local_off = jnp.where(sub1 == e, run, local_off)
        run = run + padded[e:e + 1, :]
    place = prefix + local_off
    lpos0 = jnp.sum(jnp.where(hit0, place, 0.0), axis=0, keepdims=True).astype(I32)
    lpos1 = jnp.sum(jnp.where(hit1, place, 0.0), axis=0, keepdims=True).astype(I32)

    lanes = lambda col: jnp.broadcast_to(col, (N_EXPERTS, 128)).astype(I32)
    nch_o[sub_tile] = lanes(padded * (1.0 / ROW_CHUNK))
    loff_o[sub_tile] = lanes(local_off)
    gb_o[sub_tile] = carry[...].astype(I32)
    carry[...] = carry[...] + padded
    cnt_o[...] = carry[...].astype(I32)

    sub8 = lax.broadcasted_iota(I32, (8, TM), 0)
    bc = lambda r: jnp.broadcast_to(r, (8, TM))
    lp_o[:, tok] = jnp.where(sub8 == 0, bc(lpos0), jnp.where(sub8 == 1, bc(lpos1), 0))
    rf_o[:, tok] = jnp.where(sub8 == 0, bc(w0), jnp.where(sub8 == 1, bc(w1), 0.0))


def _tail_specs(l, sub_tiles):
    rows = sub_tiles * TM
    in_specs = [
        pl.BlockSpec((1, 1, D), lambda i: (l, 0, 0)),
        pl.BlockSpec((N_EXPERTS, D), lambda i: (0, 0)),
        pl.BlockSpec((N_EXPERTS, TM), lambda i: (0, 0)),
        pl.BlockSpec((TM, TM), lambda i: (0, 0)),
    ]
    per_tile = lambda: pl.BlockSpec((sub_tiles, N_EXPERTS, 128), lambda i: (i, 0, 0))
    out_specs = [
        pl.BlockSpec((rows, D), lambda i: (i, 0)),
        pl.BlockSpec((8, rows), lambda i: (0, i)),
        pl.BlockSpec((8, rows), lambda i: (0, i)),
        per_tile(), per_tile(), per_tile(),
        pl.BlockSpec((N_EXPERTS, 128), lambda i: (0, 0)),
    ]
    out_shape = [
        _hbm_out((T, D), BF16),
        jax.ShapeDtypeStruct((8, T), I32),
        jax.ShapeDtypeStruct((8, T), F32),
        jax.ShapeDtypeStruct((N_TILES, N_EXPERTS, 128), I32),
        jax.ShapeDtypeStruct((N_TILES, N_EXPERTS, 128), I32),
        jax.ShapeDtypeStruct((N_TILES, N_EXPERTS, 128), I32),
        jax.ShapeDtypeStruct((N_EXPERTS, 128), I32),
    ]
    scratch = [pltpu.VMEM((N_EXPERTS, 128), F32)]
    return in_specs, out_specs, out_shape, scratch


def _out_kernel(split_x, *refs):
    x_refs, refs = refs[:1 + split_x], refs[1 + split_x:]
    (m_ref, ac_ref, al_ref, sc_ref, sl_ref, w_ref, n2_ref, rwt_ref, rbb_ref, tri_ref,
     x_o, *tail_refs) = refs
    *tail_outs, wbf, carry = tail_refs
    i = pl.program_id(0)

    @pl.when(i == 0)
    def _():
        wbf[...] = w_ref[0].astype(BF16)

    is_ctx = i < T_CTX // TO
    x = jnp.where(is_ctx, x_refs[0][...], x_refs[1][...]) if split_x else x_refs[0][...]
    att = jnp.where(is_ctx, ac_ref[...], al_ref[...])
    sgo = jnp.where(is_ctx, sc_ref[...], sl_ref[...])
    out = (jnp.dot(att, wbf[0:MIX_A, :], preferred_element_type=F32)
           + jnp.dot(sgo, wbf[MIX_A:MIX_A + MIX_B, :], preferred_element_type=F32))
    x_new = x + m_ref[0, 0, 2:3, :] * out
    x_o[...] = x_new
    for sub_tile in range(TO // TM):
        _route_tail(sub_tile, x_new[sub_tile * TM:(sub_tile + 1) * TM, :],
                    m_ref, n2_ref, rwt_ref, rbb_ref, tri_ref, *tail_outs, carry)


def _out_proj(l, x, mod, att_c, att_l, sg_c, sg_l, w_out, norm2_g, rwt, rbb, tri):
    e = l // 2
    t_in, t_out, t_shape, t_scratch = _tail_specs(l, TO // TM)
    n_ctx = T_CTX // TO
    ctx = lambda w: pl.BlockSpec((TO, w), lambda i: (jnp.minimum(i, n_ctx - 1), 0))
    lat = lambda w: pl.BlockSpec((TO, w), lambda i: (jnp.maximum(i - n_ctx, 0), 0))
    split_x = isinstance(x, tuple)
    if split_x:
        x_specs, x_args, aliases = [ctx(D), lat(D)], [_hbm(x[0]), _hbm(x[1])], {}
    else:
        x_specs, x_args, aliases = [pl.BlockSpec((TO, D), lambda i: (i, 0))], [_hbm(x)], {0: 0}
    in_specs = x_specs + [
        pl.BlockSpec((1, 1, N_MOD, D), lambda i: (l, _cond_of_tile(i * (TO // TM)), 0, 0)),
        ctx(MIX_A), lat(MIX_A), ctx(MIX_B), lat(MIX_B),
        pl.BlockSpec((1, D, D), lambda i: (e, 0, 0)),
    ] + t_in
    return pl.pallas_call(
        functools.partial(_out_kernel, split_x),
        grid=(T // TO,),
        in_specs=in_specs,
        out_specs=[pl.BlockSpec((TO, D), lambda i: (i, 0))] + t_out,
        out_shape=[_hbm_out((T, D), F32)] + t_shape,
        scratch_shapes=[pltpu.VMEM((D, D), BF16)] + t_scratch,
        input_output_aliases=aliases,
        compiler_params=_cparams(("arbitrary",), 40),
        name="out_proj_route",
    )(*x_args, mod, _hbm(att_c), _hbm(att_l), _hbm(sg_c), _hbm(sg_l), w_out,
      norm2_g.reshape(DEPTH, 1, D), rwt, rbb, tri)


def _pool_kernel(x_ref, xp_ref, xn_ref, m_ref, n1_ref, pw_ref, ps_ref,
                 n2_ref, rwt_ref, rbb_ref, tri_ref,
                 x_o, *tail_refs):
    *tail_outs, carry = tail_refs
    i = pl.program_id(0)
    is_lat = i >= N_CTX_TILES
    ti = jnp.where(is_lat, (i - N_CTX_TILES) % LAT_TILES_PER_SEQ, 0)
    last_ti = jnp.where(is_lat, LAT_TILES_PER_SEQ - 1, 0)
    seq_len = jnp.where(is_lat, LAT_S, CTX_S)

    g1n, sc1, sh1 = n1_ref[0], m_ref[0, 0, 1:2, :], m_ref[0, 0, 0:1, :]
    x = x_ref[...]
    h = _modnorm(x, g1n, sc1, sh1)
    hp = jnp.where(ti == 0, 0.0, _modnorm(xp_ref[...], g1n, sc1, sh1))
    hn = jnp.where(ti == last_ti, 0.0, _modnorm(xn_ref[...], g1n, sc1, sh1))
    hcat = jnp.concatenate([hp, h, hn], axis=0)
    n_rows = TM + 2 * POOL_HALO

    t_pos = ti * TM + lax.broadcasted_iota(I32, (TM, 1), 0)
    outs = []
    for g, w in enumerate(POOL_WINDOWS):
        cols = slice(g * POOL_DIM, (g + 1) * POOL_DIM)
        acc = hcat[:, cols]
        acc = acc + pltpu.roll(acc, 1, 0)
        step = 1
        while 2 * step < w:
            acc = pltpu.roll(acc, step, 0) + pltpu.roll(acc, n_rows - step, 0)
            step *= 2
        lo = jnp.maximum(t_pos - w // 2, 0)
        hi = jnp.minimum(t_pos - w // 2 + w, seq_len)
        mean = acc[POOL_HALO:POOL_HALO + TM, :] / (hi - lo).astype(F32)
        dlt = (mean - h[:, cols]).astype(BF16)
        outs.append(jnp.dot(dlt, pw_ref[0, g].astype(BF16), preferred_element_type=F32))
    out = jnp.concatenate(outs, axis=1) * ps_ref[0]
    x_new = x + m_ref[0, 0, 2:3, :] * out
    x_o[...] = x_new
    _route_tail(0, x_new, m_ref, n2_ref, rwt_ref, rbb_ref, tri_ref, *tail_outs, carry)


def _pool_mixer(l, x, mod, norm1_g, pool_w, pool_scale, norm2_g, rwt, rbb, tri):
    o = l // 2
    t_in, t_out, t_shape, t_scratch = _tail_specs(l, 1)
    halo_blocks = TM // POOL_HALO
    in_specs = [
        pl.BlockSpec((TM, D), lambda i: (i, 0)),
        pl.BlockSpec((POOL_HALO, D), lambda i: (jnp.maximum(i * halo_blocks - 1, 0), 0)),
        pl.BlockSpec((POOL_HALO, D), lambda i: (jnp.minimum((i + 1) * halo_blocks, T // POOL_HALO - 1), 0)),
        pl.BlockSpec((1, 1, N_MOD, D), lambda i: (l, _cond_of_tile(i), 0, 0)),
        pl.BlockSpec((1, 1, D), lambda i: (l, 0, 0)),
        pl.BlockSpec((1, len(POOL_WINDOWS), POOL_DIM, POOL_DIM), lambda i: (o, 0, 0, 0)),
        pl.BlockSpec((1, 1, D), lambda i: (o, 0, 0)),
    ] + t_in
    return pl.pallas_call(
        _pool_kernel,
        grid=(N_TILES,),
        in_specs=in_specs,
        out_specs=[pl.BlockSpec((TM, D), lambda i: (i, 0))] + t_out,
        out_shape=[_hbm_out((T, D), F32)] + t_shape,
        scratch_shapes=t_scratch,
        compiler_params=_cparams(("arbitrary",), 32),
        name="pool_route",
    )(_hbm(x), _hbm(x), _hbm(x), mod, norm1_g.reshape(DEPTH, 1, D), pool_w,
      pool_scale.reshape(DEPTH // 2, 1, D),
      norm2_g.reshape(DEPTH, 1, D), rwt, rbb, tri)


def _start_run_copies(tile, nch_ref, loff_ref, gb_ref, make_copy):
    for e in range(N_EXPERTS):
        n = nch_ref[tile * N_EXPERTS + e]
        lo = loff_ref[tile * N_EXPERTS + e]
        gb = gb_ref[tile * N_EXPERTS + e] + e * EXPERT_CAP

        def issue(c, carry, lo=lo, gb=gb):
            make_copy(pl.multiple_of(lo + c * (2 * ROW_CHUNK), ROW_CHUNK),
                      pl.multiple_of(gb + c * (2 * ROW_CHUNK), ROW_CHUNK), 2 * ROW_CHUNK).start()
            return carry

        lax.fori_loop(0, lax.shift_right_logical(n, 1), issue, 0)

        @pl.when((n & 1) == 1)
        def _(n=n, lo=lo, gb=gb):
            make_copy(pl.multiple_of(lo + (n - 1) * ROW_CHUNK, ROW_CHUNK),
                      pl.multiple_of(gb + (n - 1) * ROW_CHUNK, ROW_CHUNK), ROW_CHUNK).start()


def _wait_run_copies(tile, nch_ref, make_copy):
    doubles, singles = 0, 0
    for e in range(N_EXPERTS):
        n = nch_ref[tile * N_EXPERTS + e]
        doubles = doubles + lax.shift_right_logical(n, 1)
        singles = singles + (n & 1)

    def drain(rows):
        def body(c, carry):
            make_copy(0, 0, rows).wait()
            return carry
        return body

    lax.fori_loop(0, doubles, drain(2 * ROW_CHUNK), 0)
    lax.fori_loop(0, singles, drain(ROW_CHUNK), 0)


def _dispatch_kernel(nch_ref, loff_ref, gb_ref, h_ref, lp_ref, xs_ref, sbuf, sems):
    i = pl.program_id(0)
    slot = i % 2
    r = lax.broadcasted_iota(I32, (SORTED_ROWS, TM), 0)
    hit = (r == lp_ref[0:1, :]) | (r == lp_ref[1:2, :])
    perm = jnp.where(hit, 1.0, 0.0).astype(BF16)
    sbuf[slot] = jnp.dot(perm, h_ref[...], preferred_element_type=F32).astype(BF16)

    def copies_of(s):
        def make_copy(local_row, global_row, rows):
            return pltpu.make_async_copy(sbuf.at[s, pl.ds(local_row, rows)],
                                         xs_ref.at[pl.ds(global_row, rows)], sems.at[s])
        return make_copy

    @pl.when(i > 0)
    def _():
        _wait_run_copies(i - 1, nch_ref, copies_of(1 - slot))

    _start_run_copies(i, nch_ref, loff_ref, gb_ref, copies_of(slot))

    @pl.when(i == N_TILES - 1)
    def _():
        _wait_run_copies(i, nch_ref, copies_of(slot))


def _dispatch(layout, h2, lp):
    return pl.pallas_call(
        _dispatch_kernel,
        grid_spec=pltpu.PrefetchScalarGridSpec(
            num_scalar_prefetch=3,
            grid=(N_TILES,),
            in_specs=[pl.BlockSpec((TM, D), lambda i, *_: (i, 0)),
                      pl.BlockSpec((8, TM), lambda i, *_: (0, i))],
            out_specs=pl.BlockSpec(memory_space=pl.ANY),
            scratch_shapes=[pltpu.VMEM((2, SORTED_ROWS, D), BF16), pltpu.SemaphoreType.DMA((2,))],
        ),
        out_shape=_hbm_out((N_EXPERTS * EXPERT_CAP, D), BF16),
        compiler_params=_cparams(("arbitrary",), 16),
        name="moe_dispatch",
    )(*layout, _hbm(h2), lp)


def _moe_kernel(l, rb_ref, e_ref, nv_ref, first_ref, ord_ref, nxt_ref, nxt2_ref,
                xs_ref, wg_hbm, wu_hbm, wd_hbm, y_ref, wgf, wuf, wdf, wgb, wub, wdb, sems):
    i = pl.program_id(0)
    n_valid = nv_ref[i]

    def weight_copies(e, slot):
        return (pltpu.make_async_copy(wg_hbm.at[l, e], wgf.at[slot], sems.at[slot, 0]),
                pltpu.make_async_copy(wu_hbm.at[l, e], wuf.at[slot], sems.at[slot, 1]),
                pltpu.make_async_copy(wd_hbm.at[l, e], wdf.at[slot], sems.at[slot, 2]))

    @pl.when(first_ref[i] == 1)
    def _():
        k = ord_ref[i]
        slot = k % W_SLOTS

        @pl.when(k == 0)
        def _():
            for cp in weight_copies(e_ref[i], 0):
                cp.start(priority=1)

            @pl.when(nxt_ref[i] >= 0)
            def _():
                for cp in weight_copies(nxt_ref[i], 1):
                    cp.start(priority=1)

        for cp in weight_copies(e_ref[i], slot):
            cp.wait()

        @pl.when(nxt2_ref[i] >= 0)
        def _():
            for cp in weight_copies(nxt2_ref[i], (k + 2) % W_SLOTS):
                cp.start(priority=1)

        wgb[...] = wgf[slot].astype(BF16)
        wub[...] = wuf[slot].astype(BF16)
        wdb[...] = wdf[slot].astype(BF16)

    @pl.when(n_valid > 0)
    def _():
        row = lax.broadcasted_iota(I32, (TE, 1), 0)
        x = jnp.where(row < n_valid, xs_ref[...], jnp.zeros((), BF16))
        gate = jnp.dot(x, wgb[...], preferred_element_type=F32)
        up = jnp.dot(x, wub[...], preferred_element_type=F32)
        act = (gate * _sigmoid(gate) * up).astype(BF16)
        y_ref[...] = jnp.dot(act, wdb[...], preferred_element_type=F32).astype(BF16)


def _experts(l, tile_map, xs, w_gate, w_up, w_down):
    n_pre = len(tile_map)
    blk = lambda i, rb, *_: (rb[i], 0)
    return pl.pallas_call(
        functools.partial(_moe_kernel, l),
        grid_spec=pltpu.PrefetchScalarGridSpec(
            num_scalar_prefetch=n_pre,
            grid=(MOE_TILES,),
            in_specs=[
                pl.BlockSpec((TE, D), blk),
                pl.BlockSpec(memory_space=pl.ANY),
                pl.BlockSpec(memory_space=pl.ANY),
                pl.BlockSpec(memory_space=pl.ANY),
            ],
            out_specs=pl.BlockSpec((TE, D), blk),
            scratch_shapes=[pltpu.VMEM((W_SLOTS, D, D_FF), F32), pltpu.VMEM((W_SLOTS, D, D_FF), F32),
                            pltpu.VMEM((W_SLOTS, D_FF, D), F32),
                            pltpu.VMEM((D, D_FF), BF16), pltpu.VMEM((D, D_FF), BF16),
                            pltpu.VMEM((D_FF, D), BF16),
                            pltpu.SemaphoreType.DMA((W_SLOTS, 3))],
        ),
        out_shape=_hbm_out((N_EXPERTS * EXPERT_CAP, D), BF16),
        compiler_params=_cparams(("arbitrary",), 48),
        name="moe_experts",
    )(*tile_map, _hbm(xs), w_gate, w_up, w_down)


def _combine_kernel(split_out, nch_ref, loff_ref, gb_ref, x_ref, m_ref, lpt_ref, w_ref, y_ref, *refs):
    *o_refs, ybuf, sems = refs
    i = pl.program_id(0)
    slot = i % 2

    def copies_of(s):
        def make_copy(local_row, global_row, rows):
            return pltpu.make_async_copy(y_ref.at[pl.ds(global_row, rows)],
                                         ybuf.at[s, pl.ds(local_row, rows)], sems.at[s])
        return make_copy

    @pl.when(i == 0)
    def _():
        ybuf[...] = jnp.zeros_like(ybuf)
        _start_run_copies(i, nch_ref, loff_ref, gb_ref, copies_of(slot))

    @pl.when(i + 1 < N_TILES)
    def _():
        _start_run_copies(i + 1, nch_ref, loff_ref, gb_ref, copies_of(1 - slot))

    _wait_run_copies(i, nch_ref, copies_of(slot))

    col = lax.broadcasted_iota(I32, (TM, SORTED_ROWS), 1)
    pick0 = jnp.where(col == lpt_ref[:, 0:1], 1.0, 0.0).astype(BF16)
    pick1 = jnp.where(col == lpt_ref[:, 1:2], 1.0, 0.0).astype(BF16)
    y0 = jnp.dot(pick0, ybuf[slot], preferred_element_type=F32)
    y1 = jnp.dot(pick1, ybuf[slot], preferred_element_type=F32)
    moe = w_ref[:, 0:1] * y0 + w_ref[:, 1:2] * y1
    res = x_ref[...] + m_ref[0, 0, 5:6, :] * moe
    if split_out:
        @pl.when(i < N_CTX_TILES)
        def _():
            o_refs[0][...] = res

        @pl.when(i >= N_CTX_TILES)
        def _():
            o_refs[1][...] = res
    else:
        o_refs[0][...] = res


def _combine(l, layout, x, mod, lpt, wcol, y, split_out):
    if split_out:
        out_specs = [pl.BlockSpec((TM, D), lambda i, *_: (jnp.minimum(i, N_CTX_TILES - 1), 0)),
                     pl.BlockSpec((TM, D), lambda i, *_: (jnp.maximum(i - N_CTX_TILES, 0), 0))]
        out_shape = [jax.ShapeDtypeStruct((T_CTX, D), F32), jax.ShapeDtypeStruct((T_LAT, D), F32)]
        aliases = {}
    else:
        out_specs = pl.BlockSpec((TM, D), lambda i, *_: (i, 0))
        out_shape = _hbm_out((T, D), F32)
        aliases = {3: 0}
    return pl.pallas_call(
        functools.partial(_combine_kernel, split_out),
        grid_spec=pltpu.PrefetchScalarGridSpec(
            num_scalar_prefetch=3,
            grid=(N_TILES,),
            in_specs=[
                pl.BlockSpec((TM, D), lambda i, *_: (i, 0)),
                pl.BlockSpec((1, 1, N_MOD, D), lambda i, *_: (l, _cond_of_tile(i), 0, 0)),
                pl.BlockSpec((TM, 8), lambda i, *_: (i, 0)),
                pl.BlockSpec((TM, 8), lambda i, *_: (i, 0)),
                pl.BlockSpec(memory_space=pl.ANY),
            ],
            out_specs=out_specs,
            scratch_shapes=[pltpu.VMEM((2, SORTED_ROWS, D), BF16), pltpu.SemaphoreType.DMA((2,))],
        ),
        out_shape=out_shape,
        input_output_aliases=aliases,
        compiler_params=_cparams(("arbitrary",), 24),
        name="moe_combine",
    )(*layout, _hbm(x), mod, lpt, wcol, _hbm(y))


def _tile_map(cnt):
    cnt = cnt[:, 0]
    ids = jnp.arange(N_EXPERTS, dtype=I32)
    n_tiles = (cnt + TE - 1) // TE
    cum = jnp.cumsum(n_tiles)
    total = cum[-1]
    i = jnp.arange(MOE_TILES, dtype=I32)
    ii = jnp.minimum(i, total - 1)
    e = jnp.sum((ii[:, None] >= cum[None, :]).astype(I32), axis=1)
    onehot = (e[:, None] == ids[None, :]).astype(I32)
    pick = lambda v: jnp.sum(onehot * v[None, :], axis=1)
    j = ii - pick(cum - n_tiles)
    active = i < total
    n_valid = jnp.where(active, jnp.clip(pick(cnt) - j * TE, 0, TE), 0)
    first = (active & (j == 0)).astype(I32)
    nonempty = n_tiles > 0
    ordinal = jnp.cumsum(nonempty.astype(I32)) - 1
    later = jnp.where(nonempty[None, :] & (ids[None, :] > ids[:, None]), ids[None, :], N_EXPERTS)
    nxt = jnp.min(later, axis=1)
    nxt_of = lambda v: jnp.sum((v[:, None] == ids[None, :]).astype(I32) * nxt[None, :], axis=1)
    nxt2 = jnp.where(nxt == N_EXPERTS, N_EXPERTS, nxt_of(nxt))
    none = lambda v: jnp.where(v == N_EXPERTS, -1, v)
    as_i32 = lambda v: v.astype(I32)
    return tuple(map(as_i32, (e * CAP_TILES + j, e, n_valid, first, pick(ordinal),
                              pick(none(nxt)), pick(none(nxt2)))))


def _moe(l, x, tail, mod, w_gate, w_up, w_down, split_out):
    h2, lp, rf, nch, loff, gb, cnt = tail
    layout = tuple(a[:, :, 0].reshape(N_TILES * N_EXPERTS) for a in (nch, loff, gb))
    xs = _dispatch(layout, h2, lp)
    y = _experts(l, _tile_map(cnt), xs, w_gate, w_up, w_down)
    return _combine(l, layout, x, mod, lp.T, rf.T, y, split_out)


def _rope_tables():
    t = jnp.arange(LAT_S)
    row = (t // GRID_W).astype(F32)
    col = (t % GRID_W).astype(F32)
    nf = HEAD_DIM // 4
    inv = ROPE_BASE ** (-jnp.arange(nf, dtype=F32) / nf)
    ar = row[:, None] * inv[None, :]
    ac = col[:, None] * inv[None, :]
    cos = jnp.concatenate([jnp.cos(ar), jnp.cos(ar), jnp.cos(ac), jnp.cos(ac)], axis=1)
    sin = jnp.concatenate([-jnp.sin(ar), jnp.sin(ar), -jnp.sin(ac), jnp.sin(ac)], axis=1)
    reps = QK_W // HEAD_DIM
    return jnp.tile(cos, (1, reps)), jnp.tile(sin, (1, reps))


def kernel(x_prompt, x_sample, cache_k, cache_v, c, c_ctx, norm1_g, norm2_g, w_ada, b_ada, w_in, w_out,
           q_norm_g, k_norm_g, lam_q1, lam_k1, lam_q2, lam_k2, subln_g, sg_w, sg_b, pool_w, pool_scale,
           router_w, router_b, w_gate, w_up, w_down):
    x = (x_prompt.reshape(T_CTX, D), x_sample.reshape(T_LAT, D))
    cond = jnp.concatenate([c_ctx[None, :], c], axis=0)
    mod = _modulation(cond, w_ada, b_ada)

    seg_id = np.arange(QK_W) // HEAD_DIM
    seg = jnp.asarray((seg_id[:, None] == seg_id[None, :]) / HEAD_DIM, dtype=BF16)
    tri = jnp.asarray(np.arange(TM)[:, None] < np.arange(TM)[None, :], dtype=BF16)
    rope_tabs = _rope_tables()
    rwt = router_w.T
    rbb = jnp.broadcast_to(router_b[:, None], (N_EXPERTS, TM))
    states = None

    for l in range(DEPTH):
        if l % 2 == 0:
            e = l // 2
            qg = jnp.tile(q_norm_g[e], QK_W // HEAD_DIM)[None, :]
            kg = jnp.tile(k_norm_g[e], QK_W // HEAD_DIM)[None, :]
            sgb = jnp.broadcast_to(sg_b[:, :, :, None], (DEPTH // 2, SG_GROUPS, SG_CHUNK, 128))
            x_c, x_l = x if isinstance(x, tuple) else (x, x)
            common = (mod, norm1_g, w_in, qg, kg, seg, sg_w, sgb)
            q_c, k_c, v_c, sg_c, *states = _front(False, l, x_c, *common, None, states)
            q_l, k_l, v_l, sg_l = _front(True, l, x_l, *common, rope_tabs, None)
            lam = (lam_q1, lam_k1, lam_q2, lam_k2, subln_g)
            att_c = _attention(False, l, q_c, k_c, v_c, None, None, *lam)
            att_l = _attention(True, l, q_l, k_l, v_l, cache_k, cache_v, *lam)
            x, *tail = _out_proj(l, x, mod, att_c, att_l, sg_c, sg_l, w_out, norm2_g, rwt, rbb, tri)
        else:
            x, *tail = _pool_mixer(l, x, mod, norm1_g, pool_w, pool_scale, norm2_g, rwt, rbb, tri)
        x = _moe(l, x, tail, mod, w_gate, w_up, w_down, split_out=(l == DEPTH - 1))
    state_k, state_v = states

    y_prompt = x[0].reshape(N_CTX_B, CTX_S, D)
    y_sample = x[1].reshape(N_LAT_B, LAT_S, D)
    state_k = state_k.reshape(N_CTX_B, DEPTH // 2, CTX_S, N_HEADS, 2, HEAD_DIM)
    state_v = state_v.reshape(N_CTX_B, DEPTH // 2, CTX_S, N_HEADS, VAL_DIM)
    return y_prompt, y_sample, state_k, state_v
```

```python
import functools
import math

import jax
import jax.numpy as jnp
import numpy as np
from jax import lax
from jax.experimental import pallas as pl
from jax.experimental.pallas import tpu as pltpu

F32 = jnp.float32
BF16 = jnp.bfloat16
I32 = jnp.int32

D = 1024
N_CTX_B, CTX_S = 16, 256
N_LAT_B, LAT_S = 2, 1024
PAST = 512
DEPTH = 4
T_CTX = N_CTX_B * CTX_S
T_LAT = N_LAT_B * LAT_S
T = T_CTX + T_LAT
GRID_W = 64
N_HEADS = 4
HEAD_DIM = 64
VAL_DIM = 128
QK_W = 512
MIX_A = 512
MIX_B = 512
IN_W = 2560
SG_GROUPS = 4
SG_CHUNK = 128
POOL_WINDOWS = (2, 4, 8, 16)
POOL_DIM = 256
POOL_HALO = 8
N_EXPERTS = 16
N_GROUPS = 4
PER_GROUP = 4
D_FF = 512
N_MOD = 6
N_COND = 1 + N_LAT_B
EPS = 1e-6
ROPE_BASE = 10000.0

TM = 256
N_TILES = T // TM
N_CTX_TILES = T_CTX // TM
LAT_TILES_PER_SEQ = LAT_S // TM
ROW_CHUNK = 16
PAD_PER_RUN = ROW_CHUNK - 1
SORTED_ROWS = -(-(2 * TM + N_EXPERTS * PAD_PER_RUN) // 128) * 128
TE = 512
W_SLOTS = 3
ATTN_CTX_SEQS = 2
TF = 512
TO = 512
CAP_TILES = -(-(T + N_TILES * PAD_PER_RUN) // TE)
EXPERT_CAP = CAP_TILES * TE
MOE_TILES = (2 * T + N_EXPERTS * N_TILES * PAD_PER_RUN) // TE + N_EXPERTS
MOD_TN = 1024
MIB = 1024 * 1024

NT_DIMS = (((1,), (1,)), ((), ()))


def _cparams(sem, vmem_mib):
    return pltpu.CompilerParams(dimension_semantics=sem, vmem_limit_bytes=vmem_mib * MIB)


def _hbm(x):
    try:
        return pltpu.with_memory_space_constraint(x, pltpu.HBM)
    except ValueError:
        return x


def _hbm_out(shape, dtype):
    return pltpu.HBM(shape, dtype)


def _sigmoid(x):
    return 1.0 / (1.0 + jnp.exp(-x))


def _gelu_tanh(x):
    c = math.sqrt(2.0 / math.pi)
    return x * (0.5 * (1.0 + jnp.tanh(c * (x + 0.044715 * (x * x * x)))))


def _modnorm(x, g, scale, shift):
    ms = jnp.mean(x * x, axis=-1, keepdims=True)
    return (x * lax.rsqrt(ms + EPS) * g) * (1.0 + scale) + shift


def _cond_of_tile(i):
    return jnp.where(i < N_CTX_TILES, 0, 1 + (i - N_CTX_TILES) // LAT_TILES_PER_SEQ)


def _mod_kernel(c_ref, w_ref, b_ref, o_ref):
    c = c_ref[...]
    s = (c * _sigmoid(c)).astype(BF16)
    o_ref[0] = jnp.dot(s, w_ref[0].astype(BF16), preferred_element_type=F32) + b_ref[0]


def _modulation(cond, w_ada, b_ada):
    cond8 = jnp.concatenate([cond, jnp.zeros((8 - N_COND, D), F32)], axis=0)
    out = pl.pallas_call(
        _mod_kernel,
        grid=(DEPTH, N_MOD * D // MOD_TN),
        in_specs=[
            pl.BlockSpec((8, D), lambda l, n: (0, 0)),
            pl.BlockSpec((1, D, MOD_TN), lambda l, n: (l, 0, n)),
            pl.BlockSpec((1, 1, MOD_TN), lambda l, n: (l, 0, n)),
        ],
        out_specs=pl.BlockSpec((1, 8, MOD_TN), lambda l, n: (l, 0, n)),
        out_shape=jax.ShapeDtypeStruct((DEPTH, 8, N_MOD * D), F32),
        compiler_params=_cparams(("arbitrary", "arbitrary"), 24),
        name="adaln_mod",
    )(cond8, w_ada, b_ada.reshape(DEPTH, 1, N_MOD * D))
    return out.reshape(DEPTH, 8, N_MOD, D)


def _front_kernel(latent, n_state_in, *refs):
    (x_ref, m_ref, n1_ref, w_ref, qg_ref, kg_ref, seg_ref, sgw_ref, sgb_ref), refs = refs[:9], refs[9:]
    if latent:
        cos_ref, sin_ref, q_o, k_o, v_o, sg_o, wbf = refs
    else:
        q_o, k_o, v_o, sg_o, sk_o, sv_o, wbf = refs[n_state_in:]

    @pl.when(pl.program_id(0) == 0)
    def _():
        for c0 in range(0, IN_W, 512):
            wbf[:, c0:c0 + 512] = w_ref[0, :, c0:c0 + 512].astype(BF16)

    h = _modnorm(x_ref[...], n1_ref[0], m_ref[0, 0, 1:2, :], m_ref[0, 0, 0:1, :])
    proj = jnp.dot(h.astype(BF16), wbf[...], preferred_element_type=F32)

    def qk_norm(z, g):
        ms = jnp.dot((z * z).astype(BF16), seg_ref[...], preferred_element_type=F32)
        return z * lax.rsqrt(ms + EPS) * g

    q = qk_norm(proj[:, 0:QK_W], qg_ref[...])
    k = qk_norm(proj[:, QK_W:2 * QK_W], kg_ref[...])
    v = proj[:, 2 * QK_W:2 * QK_W + MIX_A]

    if latent:
        lane = lax.broadcasted_iota(I32, (1, QK_W), 1)
        first = (lane % 32) < 16

        def rope(z):
            partner = jnp.where(first, pltpu.roll(z, QK_W - 16, 1), pltpu.roll(z, 16, 1))
            return z * cos_ref[...] + partner * sin_ref[...]

        q = rope(q)
        k = rope(k)
    else:
        for b in range(TF // CTX_S):
            sk_o[b, 0] = k[b * CTX_S:(b + 1) * CTX_S, :]
            sv_o[b, 0] = v[b * CTX_S:(b + 1) * CTX_S, :]

    q_o[...] = (q * (HEAD_DIM ** -0.5)).astype(BF16)
    k_o[...] = k.astype(BF16)
    v_o[...] = v.astype(BF16)

    gu0 = 2 * QK_W + MIX_A
    gv0 = gu0 + MIX_B
    for g in range(SG_GROUPS):
        wg = sgw_ref[0, g].astype(BF16)
        for n in range(TF // SG_CHUNK):
            rows = slice(n * SG_CHUNK, (n + 1) * SG_CHUNK)
            cu = slice(gu0 + g * 128, gu0 + (g + 1) * 128)
            cv = slice(gv0 + g * 128, gv0 + (g + 1) * 128)
            gv = _gelu_tanh(proj[rows, cv]).astype(BF16)
            mixed = jnp.dot(wg, gv, preferred_element_type=F32) + sgb_ref[0, g]
            sg_o[rows, g * 128:(g + 1) * 128] = (_gelu_tanh(proj[rows, cu]) * mixed).astype(BF16)


def _front(latent, l, x, mod, norm1_g, w_in, qg, kg, seg, sg_w, sgb, rope_tabs, states):
    e = l // 2
    rows = T_LAT if latent else T_CTX
    n_tiles = rows // TF
    off = T_CTX // TF if (latent and x.shape[0] == T) else 0
    tiles_per_seq = LAT_S // TF
    if latent:
        cond_map = lambda i: (l, 1 + i // tiles_per_seq, 0, 0)
    else:
        cond_map = lambda i: (l, 0, 0, 0)
    in_specs = [
        pl.BlockSpec((TF, D), lambda i: (i + off, 0)),
        pl.BlockSpec((1, 1, N_MOD, D), cond_map),
        pl.BlockSpec((1, 1, D), lambda i: (l, 0, 0)),
        pl.BlockSpec((1, D, IN_W), lambda i: (e, 0, 0)),
        pl.BlockSpec((1, QK_W), lambda i: (0, 0)),
        pl.BlockSpec((1, QK_W), lambda i: (0, 0)),
        pl.BlockSpec((QK_W, QK_W), lambda i: (0, 0)),
        pl.BlockSpec((1, SG_GROUPS, SG_CHUNK, SG_CHUNK), lambda i: (e, 0, 0, 0)),
        pl.BlockSpec((1, SG_GROUPS, SG_CHUNK, 128), lambda i: (e, 0, 0, 0)),
    ]
    args = [_hbm(x), mod, norm1_g.reshape(DEPTH, 1, D), w_in, qg, kg, seg, sg_w, sgb]
    tok = lambda: pl.BlockSpec((TF, QK_W), lambda i: (i, 0))
    out_specs = [tok(), tok(), tok(), tok()]
    out_shape = [_hbm_out((rows, QK_W), BF16)] * 4
    aliases = {}
    if latent:
        cos_t, sin_t = rope_tabs
        in_specs += [pl.BlockSpec((TF, QK_W), lambda i: (i % tiles_per_seq, 0))] * 2
        args += [cos_t, sin_t]
    else:
        if states:
            in_specs += [pl.BlockSpec(memory_space=pl.ANY)] * 2
            args += list(states)
            aliases = {9: 4, 10: 5}
        st = lambda: pl.BlockSpec((TF // CTX_S, 1, CTX_S, QK_W), lambda i: (i, e, 0, 0))
        out_specs += [st(), st()]
        out_shape += [_hbm_out((N_CTX_B, DEPTH // 2, CTX_S, QK_W), F32)] * 2
    return pl.pallas_call(
        functools.partial(_front_kernel, latent, len(aliases)),
        grid=(n_tiles,),
        in_specs=in_specs,
        out_specs=out_specs,
        out_shape=out_shape,
        scratch_shapes=[pltpu.VMEM((D, IN_W), BF16)],
        input_output_aliases=aliases,
        compiler_params=_cparams(("arbitrary",), 48),
        name="front_lat" if latent else "front_ctx",
    )(*args)


def _attn_kernel(has_cache, seqs, heads, lam0, *refs):
    if has_cache:
        (q_ref, k_ref, v_ref, ck_ref, cv_ref, lq1, lk1, lq2, lk2, sub_ref, o_ref) = refs
    else:
        (q_ref, k_ref, v_ref, lq1, lk1, lq2, lk2, sub_ref, o_ref) = refs

    lane = lax.broadcasted_iota(I32, (1, VAL_DIM), 1)
    lam = (jnp.exp(jnp.sum(lq1[0] * lk1[0], axis=-1, keepdims=True))
           - jnp.exp(jnp.sum(lq2[0] * lk2[0], axis=-1, keepdims=True)) + lam0)
    q_rows = q_ref.shape[0] // seqs
    k_rows = k_ref.shape[0] // seqs

    for sq, h in [(sq, h) for sq in range(seqs) for h in range(heads)]:
        hs = slice(h * VAL_DIM, (h + 1) * VAL_DIM)
        qr = slice(sq * q_rows, (sq + 1) * q_rows)
        kr = slice(sq * k_rows, (sq + 1) * k_rows)
        qf = q_ref[qr, hs].astype(F32)
        qc = [jnp.where(lane < HEAD_DIM, qf, 0.0).astype(BF16),
              jnp.where(lane >= HEAD_DIM, qf, 0.0).astype(BF16)]
        keys = [k_ref[kr, hs]]
        vals = [v_ref[kr, hs]]
        if has_cache:
            keys.append(ck_ref[0, 0].astype(BF16))
            vals.append(cv_ref[0, 0].astype(BF16))

        probs = []
        for c in range(2):
            s = [lax.dot_general(qc[c], kk, NT_DIMS, preferred_element_type=F32) for kk in keys]
            m = s[0].max(axis=-1, keepdims=True)
            for sx in s[1:]:
                m = jnp.maximum(m, sx.max(axis=-1, keepdims=True))
            p = [jnp.exp(sx - m) for sx in s]
            den = p[0].sum(axis=-1, keepdims=True)
            for px in p[1:]:
                den = den + px.sum(axis=-1, keepdims=True)
            probs.append((p, den))

        r0 = 1.0 / probs[0][1]
        r1 = lam / probs[1][1]
        o = None
        for idx in range(len(keys)):
            a = (probs[0][0][idx] * r0 - probs[1][0][idx] * r1).astype(BF16)
            t = jnp.dot(a, vals[idx], preferred_element_type=F32)
            o = t if o is None else o + t
        ms = jnp.mean(o * o, axis=-1, keepdims=True)
        o_ref[qr, hs] = ((o * lax.rsqrt(ms + EPS) * sub_ref[0]) * (1.0 - lam0)).astype(BF16)


def _attention(has_cache, l, q, k, v, cache_k, cache_v, lam_q1, lam_k1, lam_q2, lam_k2, subln_g):
    e = l // 2
    lam0 = 0.8 - 0.6 * math.exp(-0.3 * l)
    n_b, s_len = (N_LAT_B, LAT_S) if has_cache else (N_CTX_B, CTX_S)
    heads = 1 if has_cache else N_HEADS
    seqs = 1 if has_cache else ATTN_CTX_SEQS
    width = heads * VAL_DIM
    nq = s_len // TM
    kv = lambda: pl.BlockSpec((seqs * s_len, width), lambda b, h, qi: (b, h))
    par = lambda: pl.BlockSpec((1, 1, HEAD_DIM), lambda b, h, qi: (e, 0, 0))
    in_specs = [pl.BlockSpec((seqs * TM, width), lambda b, h, qi: (b * nq + qi, h)), kv(), kv()]
    args = [_hbm(q), _hbm(k), _hbm(v)]
    if has_cache:
        cs = lambda: pl.BlockSpec((1, 1, PAST, VAL_DIM), lambda b, h, qi: (b, e, 0, h))
        in_specs += [cs(), cs()]
        args += [cache_k.reshape(N_LAT_B, DEPTH // 2, PAST, QK_W),
                 cache_v.reshape(N_LAT_B, DEPTH // 2, PAST, MIX_A)]
    in_specs += [par(), par(), par(), par(),
                 pl.BlockSpec((1, 1, VAL_DIM), lambda b, h, qi: (e, 0, 0))]
    r3 = lambda a: a.reshape(DEPTH // 2, 1, -1)
    args += [r3(lam_q1), r3(lam_k1), r3(lam_q2), r3(lam_k2), r3(subln_g)]
    return pl.pallas_call(
        functools.partial(_attn_kernel, has_cache, seqs, heads, lam0),
        grid=(n_b // seqs, N_HEADS // heads, nq),
        in_specs=in_specs,
        out_specs=pl.BlockSpec((seqs * TM, width), lambda b, h, qi: (b * nq + qi, h)),
        out_shape=_hbm_out((n_b * s_len, MIX_A), BF16),
        compiler_params=_cparams(("arbitrary", "arbitrary", "arbitrary"), 32),
        name="attn_lat" if has_cache else "attn_ctx",
    )(*args)


def _route_tail(sub_tile, x_new, m_ref, n2_ref, rwt_ref, rbb_ref, tri_ref,
                h2_o, lp_o, rf_o, nch_o, loff_o, gb_o, cnt_o, carry):
    tok = slice(sub_tile * TM, (sub_tile + 1) * TM)
    if sub_tile == 0:
        @pl.when(pl.program_id(0) == 0)
        def _():
            carry[...] = jnp.zeros_like(carry)

    h2 = _modnorm(x_new, n2_ref[0], m_ref[0, 0, 4:5, :], m_ref[0, 0, 3:4, :])

    hh = h2.astype(BF16)
    h2_o[tok, :] = hh
    hl = (h2 - hh.astype(F32)).astype(BF16)
    rw = rwt_ref[...]
    wh = rw.astype(BF16)
    wl = (rw - wh.astype(F32)).astype(BF16)
    dg = lambda a, b: lax.dot_general(a, b, NT_DIMS, preferred_element_type=F32)
    logits = dg(wh, hh) + dg(wh, hl) + dg(wl, hh)
    score = _sigmoid(logits)
    sel = score + rbb_ref[...]

    neg_inf = jnp.full((1, TM), -jnp.inf, F32)
    grp = []
    for g in range(N_GROUPS):
        s_rows = [sel[PER_GROUP * g + j:PER_GROUP * g + j + 1, :] for j in range(PER_GROUP)]
        c_rows = [score[PER_GROUP * g + j:PER_GROUP * g + j + 1, :] for j in range(PER_GROUP)]
        f_idx = jnp.zeros((1, TM), I32)
        f_val, f_sc = s_rows[0], c_rows[0]
        for j in range(1, PER_GROUP):
            better = s_rows[j] > f_val
            f_idx = jnp.where(better, j, f_idx)
            f_val = jnp.where(better, s_rows[j], f_val)
            f_sc = jnp.where(better, c_rows[j], f_sc)
        s_idx = jnp.zeros((1, TM), I32)
        s_val, s_sc = neg_inf, jnp.zeros((1, TM), F32)
        for j in range(PER_GROUP):
            better = (f_idx != j) & (s_rows[j] > s_val)
            s_idx = jnp.where(better, j, s_idx)
            s_val = jnp.where(better, s_rows[j], s_val)
            s_sc = jnp.where(better, c_rows[j], s_sc)
        grp.append((f_val + s_val, PER_GROUP * g + f_idx, PER_GROUP * g + s_idx, f_sc, s_sc))

    best, e0, e1, c0, c1 = grp[0]
    for g in range(1, N_GROUPS):
        better = grp[g][0] > best
        best = jnp.where(better, grp[g][0], best)
        e0 = jnp.where(better, grp[g][1], e0)
        e1 = jnp.where(better, grp[g][2], e1)
        c0 = jnp.where(better, grp[g][3], c0)
        c1 = jnp.where(better, grp[g][4], c1)
    den = c0 + c1
    w0 = c0 / den
    w1 = c1 / den

    sub = lax.broadcasted_iota(I32, (N_EXPERTS, TM), 0)
    hit0 = sub == jnp.broadcast_to(e0, (N_EXPERTS, TM))
    hit1 = sub == jnp.broadcast_to(e1, (N_EXPERTS, TM))
    onehot = jnp.where(hit0 | hit1, 1.0, 0.0)
    prefix = jnp.dot(onehot.astype(BF16), tri_ref[...], preferred_element_type=F32)
    count = jnp.sum(onehot, axis=1, keepdims=True)
    padded = jnp.floor((count + (ROW_CHUNK - 1)) * (1.0 / ROW_CHUNK)) * ROW_CHUNK

    sub1 = lax.broadcasted_iota(I32, (N_EXPERTS, 1), 0)
    run = jnp.zeros((1, 1), F32)
    local_off = jnp.zeros((N_EXPERTS, 1), F32)
    for e in range(N_EXPERTS):
        local_off = jnp.where(sub1 == e, run, local_off)
        run = run + padded[e:e + 1, :]
    place = prefix + local_off
    lpos0 = jnp.sum(jnp.where(hit0, place, 0.0), axis=0, keepdims=True).astype(I32)
    lpos1 = jnp.sum(jnp.where(hit1, place, 0.0), axis=0, keepdims=True).astype(I32)

    lanes = lambda col: jnp.broadcast_to(col, (N_EXPERTS, 128)).astype(I32)
    nch_o[sub_tile] = lanes(padded * (1.0 / ROW_CHUNK))
    loff_o[sub_tile] = lanes(local_off)
    gb_o[sub_tile] = carry[...].astype(I32)
    carry[...] = carry[...] + padded
    cnt_o[...] = carry[...].astype(I32)

    sub8 = lax.broadcasted_iota(I32, (8, TM), 0)
    bc = lambda r: jnp.broadcast_to(r, (8, TM))
    lp_o[:, tok] = jnp.where(sub8 == 0, bc(lpos0), jnp.where(sub8 == 1, bc(lpos1), 0))
    rf_o[:, tok] = jnp.where(sub8 == 0, bc(w0), jnp.where(sub8 == 1, bc(w1), 0.0))


def _tail_specs(l, sub_tiles):
    rows = sub_tiles * TM
    in_specs = [
        pl.BlockSpec((1, 1, D), lambda i: (l, 0, 0)),
        pl.BlockSpec((N_EXPERTS, D), lambda i: (0, 0)),
        pl.BlockSpec((N_EXPERTS, TM), lambda i: (0, 0)),
        pl.BlockSpec((TM, TM), lambda i: (0, 0)),
    ]
    per_tile = lambda: pl.BlockSpec((sub_tiles, N_EXPERTS, 128), lambda i: (i, 0, 0))
    out_specs = [
        pl.BlockSpec((rows, D), lambda i: (i, 0)),
        pl.BlockSpec((8, rows), lambda i: (0, i)),
        pl.BlockSpec((8, rows), lambda i: (0, i)),
        per_tile(), per_tile(), per_tile(),
        pl.BlockSpec((N_EXPERTS, 128), lambda i: (0, 0)),
    ]
    out_shape = [
        _hbm_out((T, D), BF16),
        jax.ShapeDtypeStruct((8, T), I32),
        jax.ShapeDtypeStruct((8, T), F32),
        jax.ShapeDtypeStruct((N_TILES, N_EXPERTS, 128), I32),
        jax.ShapeDtypeStruct((N_TILES, N_EXPERTS, 128), I32),
        jax.ShapeDtypeStruct((N_TILES, N_EXPERTS, 128), I32),
        jax.ShapeDtypeStruct((N_EXPERTS, 128), I32),
    ]
    scratch = [pltpu.VMEM((N_EXPERTS, 128), F32)]
    return in_specs, out_specs, out_shape, scratch


def _out_kernel(split_x, *refs):
    x_refs, refs = refs[:1 + split_x], refs[1 + split_x:]
    (m_ref, ac_ref, al_ref, sc_ref, sl_ref, w_ref, n2_ref, rwt_ref, rbb_ref, tri_ref,
     x_o, *tail_refs) = refs
    *tail_outs, wbf, carry = tail_refs
    i = pl.program_id(0)

    @pl.when(i == 0)
    def _():
        wbf[...] = w_ref[0].astype(BF16)

    is_ctx = i < T_CTX // TO
    x = jnp.where(is_ctx, x_refs[0][...], x_refs[1][...]) if split_x else x_refs[0][...]
    att = jnp.where(is_ctx, ac_ref[...], al_ref[...])
    sgo = jnp.where(is_ctx, sc_ref[...], sl_ref[...])
    out = (jnp.dot(att, wbf[0:MIX_A, :], preferred_element_type=F32)
           + jnp.dot(sgo, wbf[MIX_A:MIX_A + MIX_B, :], preferred_element_type=F32))
    x_new = x + m_ref[0, 0, 2:3, :] * out
    x_o[...] = x_new
    for sub_tile in range(TO // TM):
        _route_tail(sub_tile, x_new[sub_tile * TM:(sub_tile + 1) * TM, :],
                    m_ref, n2_ref, rwt_ref, rbb_ref, tri_ref, *tail_outs, carry)


def _out_proj(l, x, mod, att_c, att_l, sg_c, sg_l, w_out, norm2_g, rwt, rbb, tri):
    e = l // 2
    t_in, t_out, t_shape, t_scratch = _tail_specs(l, TO // TM)
    n_ctx = T_CTX // TO
    ctx = lambda w: pl.BlockSpec((TO, w), lambda i: (jnp.minimum(i, n_ctx - 1), 0))
    lat = lambda w: pl.BlockSpec((TO, w), lambda i: (jnp.maximum(i - n_ctx, 0), 0))
    split_x = isinstance(x, tuple)
    if split_x:
        x_specs, x_args, aliases = [ctx(D), lat(D)], [_hbm(x[0]), _hbm(x[1])], {}
    else:
        x_specs, x_args, aliases = [pl.BlockSpec((TO, D), lambda i: (i, 0))], [_hbm(x)], {0: 0}
    in_specs = x_specs + [
        pl.BlockSpec((1, 1, N_MOD, D), lambda i: (l, _cond_of_tile(i * (TO // TM)), 0, 0)),
        ctx(MIX_A), lat(MIX_A), ctx(MIX_B), lat(MIX_B),
        pl.BlockSpec((1, D, D), lambda i: (e, 0, 0)),
    ] + t_in
    return pl.pallas_call(
        functools.partial(_out_kernel, split_x),
        grid=(T // TO,),
        in_specs=in_specs,
        out_specs=[pl.BlockSpec((TO, D), lambda i: (i, 0))] + t_out,
        out_shape=[_hbm_out((T, D), F32)] + t_shape,
        scratch_shapes=[pltpu.VMEM((D, D), BF16)] + t_scratch,
        input_output_aliases=aliases,
        compiler_params=_cparams(("arbitrary",), 40),
        name="out_proj_route",
    )(*x_args, mod, _hbm(att_c), _hbm(att_l), _hbm(sg_c), _hbm(sg_l), w_out,
      norm2_g.reshape(DEPTH, 1, D), rwt, rbb, tri)


def _pool_kernel(x_ref, xp_ref, xn_ref, m_ref, n1_ref, pw_ref, ps_ref,
                 n2_ref, rwt_ref, rbb_ref, tri_ref,
                 x_o, *tail_refs):
    *tail_outs, carry = tail_refs
    i = pl.program_id(0)
    is_lat = i >= N_CTX_TILES
    ti = jnp.where(is_lat, (i - N_CTX_TILES) % LAT_TILES_PER_SEQ, 0)
    last_ti = jnp.where(is_lat, LAT_TILES_PER_SEQ - 1, 0)
    seq_len = jnp.where(is_lat, LAT_S, CTX_S)

    g1n, sc1, sh1 = n1_ref[0], m_ref[0, 0, 1:2, :], m_ref[0, 0, 0:1, :]
    x = x_ref[...]
    h = _modnorm(x, g1n, sc1, sh1)
    hp = jnp.where(ti == 0, 0.0, _modnorm(xp_ref[...], g1n, sc1, sh1))
    hn = jnp.where(ti == last_ti, 0.0, _modnorm(xn_ref[...], g1n, sc1, sh1))
    hcat = jnp.concatenate([hp, h, hn], axis=0)
    n_rows = TM + 2 * POOL_HALO

    t_pos = ti * TM + lax.broadcasted_iota(I32, (TM, 1), 0)
    outs = []
    for g, w in enumerate(POOL_WINDOWS):
        cols = slice(g * POOL_DIM, (g + 1) * POOL_DIM)
        acc = hcat[:, cols]
        acc = acc + pltpu.roll(acc, 1, 0)
        step = 1
        while 2 * step < w:
            acc = pltpu.roll(acc, step, 0) + pltpu.roll(acc, n_rows - step, 0)
            step *= 2
        lo = jnp.maximum(t_pos - w // 2, 0)
        hi = jnp.minimum(t_pos - w // 2 + w, seq_len)
        mean = acc[POOL_HALO:POOL_HALO + TM, :] / (hi - lo).astype(F32)
        dlt = (mean - h[:, cols]).astype(BF16)
        outs.append(jnp.dot(dlt, pw_ref[0, g].astype(BF16), preferred_element_type=F32))
    out = jnp.concatenate(outs, axis=1) * ps_ref[0]
    x_new = x + m_ref[0, 0, 2:3, :] * out
    x_o[...] = x_new
    _route_tail(0, x_new, m_ref, n2_ref, rwt_ref, rbb_ref, tri_ref, *tail_outs, carry)


def _pool_mixer(l, x, mod, norm1_g, pool_w, pool_scale, norm2_g, rwt, rbb, tri):
    o = l // 2
    t_in, t_out, t_shape, t_scratch = _tail_specs(l, 1)
    halo_blocks = TM // POOL_HALO
    in_specs = [
        pl.BlockSpec((TM, D), lambda i: (i, 0)),
        pl.BlockSpec((POOL_HALO, D), lambda i: (jnp.maximum(i * halo_blocks - 1, 0), 0)),
        pl.BlockSpec((POOL_HALO, D), lambda i: (jnp.minimum((i + 1) * halo_blocks, T // POOL_HALO - 1), 0)),
        pl.BlockSpec((1, 1, N_MOD, D), lambda i: (l, _cond_of_tile(i), 0, 0)),
        pl.BlockSpec((1, 1, D), lambda i: (l, 0, 0)),
        pl.BlockSpec((1, len(POOL_WINDOWS), POOL_DIM, POOL_DIM), lambda i: (o, 0, 0, 0)),
        pl.BlockSpec((1, 1, D), lambda i: (o, 0, 0)),
    ] + t_in
    return pl.pallas_call(
        _pool_kernel,
        grid=(N_TILES,),
        in_specs=in_specs,
        out_specs=[pl.BlockSpec((TM, D), lambda i: (i, 0))] + t_out,
        out_shape=[_hbm_out((T, D), F32)] + t_shape,
        scratch_shapes=t_scratch,
        compiler_params=_cparams(("arbitrary",), 32),
        name="pool_route",
    )(_hbm(x), _hbm(x), _hbm(x), mod, norm1_g.reshape(DEPTH, 1, D), pool_w,
      pool_scale.reshape(DEPTH // 2, 1, D),
      norm2_g.reshape(DEPTH, 1, D), rwt, rbb, tri)


def _start_run_copies(tile, nch_ref, loff_ref, gb_ref, make_copy):
    for e in range(N_EXPERTS):
        n = nch_ref[tile * N_EXPERTS + e]
        lo = loff_ref[tile * N_EXPERTS + e]
        gb = gb_ref[tile * N_EXPERTS + e] + e * EXPERT_CAP

        def issue(c, carry, lo=lo, gb=gb):
            make_copy(pl.multiple_of(lo + c * (2 * ROW_CHUNK), ROW_CHUNK),
                      pl.multiple_of(gb + c * (2 * ROW_CHUNK), ROW_CHUNK), 2 * ROW_CHUNK).start()
            return carry

        lax.fori_loop(0, lax.shift_right_logical(n, 1), issue, 0)

        @pl.when((n & 1) == 1)
        def _(n=n, lo=lo, gb=gb):
            make_copy(pl.multiple_of(lo + (n - 1) * ROW_CHUNK, ROW_CHUNK),
                      pl.multiple_of(gb + (n - 1) * ROW_CHUNK, ROW_CHUNK), ROW_CHUNK).start()


def _wait_run_copies(tile, nch_ref, make_copy):
    doubles, singles = 0, 0
    for e in range(N_EXPERTS):
        n = nch_ref[tile * N_EXPERTS + e]
        doubles = doubles + lax.shift_right_logical(n, 1)
        singles = singles + (n & 1)

    def drain(rows):
        def body(c, carry):
            make_copy(0, 0, rows).wait()
            return carry
        return body

    lax.fori_loop(0, doubles, drain(2 * ROW_CHUNK), 0)
    lax.fori_loop(0, singles, drain(ROW_CHUNK), 0)


def _dispatch_kernel(nch_ref, loff_ref, gb_ref, h_ref, lp_ref, xs_ref, sbuf, sems):
    i = pl.program_id(0)
    slot = i % 2
    r = lax.broadcasted_iota(I32, (SORTED_ROWS, TM), 0)
    hit = (r == lp_ref[0:1, :]) | (r == lp_ref[1:2, :])
    perm = jnp.where(hit, 1.0, 0.0).astype(BF16)
    sbuf[slot] = jnp.dot(perm, h_ref[...], preferred_element_type=F32).astype(BF16)

    def copies_of(s):
        def make_copy(local_row, global_row, rows):
            return pltpu.make_async_copy(sbuf.at[s, pl.ds(local_row, rows)],
                                         xs_ref.at[pl.ds(global_row, rows)], sems.at[s])
        return make_copy

    @pl.when(i > 0)
    def _():
        _wait_run_copies(i - 1, nch_ref, copies_of(1 - slot))

    _start_run_copies(i, nch_ref, loff_ref, gb_ref, copies_of(slot))

    @pl.when(i == N_TILES - 1)
    def _():
        _wait_run_copies(i, nch_ref, copies_of(slot))


def _dispatch(layout, h2, lp):
    return pl.pallas_call(
        _dispatch_kernel,
        grid_spec=pltpu.PrefetchScalarGridSpec(
            num_scalar_prefetch=3,
            grid=(N_TILES,),
            in_specs=[pl.BlockSpec((TM, D), lambda i, *_: (i, 0)),
                      pl.BlockSpec((8, TM), lambda i, *_: (0, i))],
            out_specs=pl.BlockSpec(memory_space=pl.ANY),
            scratch_shapes=[pltpu.VMEM((2, SORTED_ROWS, D), BF16), pltpu.SemaphoreType.DMA((2,))],
        ),
        out_shape=_hbm_out((N_EXPERTS * EXPERT_CAP, D), BF16),
        compiler_params=_cparams(("arbitrary",), 16),
        name="moe_dispatch",
    )(*layout, _hbm(h2), lp)


def _moe_kernel(l, rb_ref, e_ref, nv_ref, first_ref, ord_ref, nxt_ref, nxt2_ref,
                xs_ref, wg_hbm, wu_hbm, wd_hbm, y_ref, wgf, wuf, wdf, wgb, wub, wdb, sems):
    i = pl.program_id(0)
    n_valid = nv_ref[i]

    def weight_copies(e, slot):
        return (pltpu.make_async_copy(wg_hbm.at[l, e], wgf.at[slot], sems.at[slot, 0]),
                pltpu.make_async_copy(wu_hbm.at[l, e], wuf.at[slot], sems.at[slot, 1]),
                pltpu.make_async_copy(wd_hbm.at[l, e], wdf.at[slot], sems.at[slot, 2]))

    @pl.when(first_ref[i] == 1)
    def _():
        k = ord_ref[i]
        slot = k % W_SLOTS

        @pl.when(k == 0)
        def _():
            for cp in weight_copies(e_ref[i], 0):
                cp.start(priority=1)

            @pl.when(nxt_ref[i] >= 0)
            def _():
                for cp in weight_copies(nxt_ref[i], 1):
                    cp.start(priority=1)

        for cp in weight_copies(e_ref[i], slot):
            cp.wait()

        @pl.when(nxt2_ref[i] >= 0)
        def _():
            for cp in weight_copies(nxt2_ref[i], (k + 2) % W_SLOTS):
                cp.start(priority=1)

        wgb[...] = wgf[slot].astype(BF16)
        wub[...] = wuf[slot].astype(BF16)
        wdb[...] = wdf[slot].astype(BF16)

    @pl.when(n_valid > 0)
    def _():
        row = lax.broadcasted_iota(I32, (TE, 1), 0)
        x = jnp.where(row < n_valid, xs_ref[...], jnp.zeros((), BF16))
        gate = jnp.dot(x, wgb[...], preferred_element_type=F32)
        up = jnp.dot(x, wub[...], preferred_element_type=F32)
        act = (gate * _sigmoid(gate) * up).astype(BF16)
        y_ref[...] = jnp.dot(act, wdb[...], preferred_element_type=F32).astype(BF16)


def _experts(l, tile_map, xs, w_gate, w_up, w_down):
    n_pre = len(tile_map)
    blk = lambda i, rb, *_: (rb[i], 0)
    return pl.pallas_call(
        functools.partial(_moe_kernel, l),
        grid_spec=pltpu.PrefetchScalarGridSpec(
            num_scalar_prefetch=n_pre,
            grid=(MOE_TILES,),
            in_specs=[
                pl.BlockSpec((TE, D), blk),
                pl.BlockSpec(memory_space=pl.ANY),
                pl.BlockSpec(memory_space=pl.ANY),
                pl.BlockSpec(memory_space=pl.ANY),
            ],
            out_specs=pl.BlockSpec((TE, D), blk),
            scratch_shapes=[pltpu.VMEM((W_SLOTS, D, D_FF), F32), pltpu.VMEM((W_SLOTS, D, D_FF), F32),
                            pltpu.VMEM((W_SLOTS, D_FF, D), F32),
                            pltpu.VMEM((D, D_FF), BF16), pltpu.VMEM((D, D_FF), BF16),
                            pltpu.VMEM((D_FF, D), BF16),
                            pltpu.SemaphoreType.DMA((W_SLOTS, 3))],
        ),
        out_shape=_hbm_out((N_EXPERTS * EXPERT_CAP, D), BF16),
        compiler_params=_cparams(("arbitrary",), 48),
        name="moe_experts",
    )(*tile_map, _hbm(xs), w_gate, w_up, w_down)


def _combine_kernel(split_out, nch_ref, loff_ref, gb_ref, x_ref, m_ref, lpt_ref, w_ref, y_ref, *refs):
    *o_refs, ybuf, sems = refs
    i = pl.program_id(0)
    slot = i % 2

    def copies_of(s):
        def make_copy(local_row, global_row, rows):
            return pltpu.make_async_copy(y_ref.at[pl.ds(global_row, rows)],
                                         ybuf.at[s, pl.ds(local_row, rows)], sems.at[s])
        return make_copy

    @pl.when(i == 0)
    def _():
        ybuf[...] = jnp.zeros_like(ybuf)
        _start_run_copies(i, nch_ref, loff_ref, gb_ref, copies_of(slot))

    @pl.when(i + 1 < N_TILES)
    def _():
        _start_run_copies(i + 1, nch_ref, loff_ref, gb_ref, copies_of(1 - slot))

    _wait_run_copies(i, nch_ref, copies_of(slot))

    col = lax.broadcasted_iota(I32, (TM, SORTED_ROWS), 1)
    pick = jnp.where(col == lpt_ref[:, 0:1], w_ref[:, 0:1],
                     jnp.where(col == lpt_ref[:, 1:2], w_ref[:, 1:2], 0.0)).astype(BF16)
    moe = jnp.dot(pick, ybuf[slot], preferred_element_type=F32)
    res = x_ref[...] + m_ref[0, 0, 5:6, :] * moe
    if split_out:
        @pl.when(i < N_CTX_TILES)
        def _():
            o_refs[0][...] = res

        @pl.when(i >= N_CTX_TILES)
        def _():
            o_refs[1][...] = res
    else:
        o_refs[0][...] = res


def _combine(l, layout, x, mod, lpt, wcol, y, split_out):
    if split_out:
        out_specs = [pl.BlockSpec((TM, D), lambda i, *_: (jnp.minimum(i, N_CTX_TILES - 1), 0)),
                     pl.BlockSpec((TM, D), lambda i, *_: (jnp.maximum(i - N_CTX_TILES, 0), 0))]
        out_shape = [jax.ShapeDtypeStruct((T_CTX, D), F32), jax.ShapeDtypeStruct((T_LAT, D), F32)]
        aliases = {}
    else:
        out_specs = pl.BlockSpec((TM, D), lambda i, *_: (i, 0))
        out_shape = _hbm_out((T, D), F32)
        aliases = {3: 0}
    return pl.pallas_call(
        functools.partial(_combine_kernel, split_out),
        grid_spec=pltpu.PrefetchScalarGridSpec(
            num_scalar_prefetch=3,
            grid=(N_TILES,),
            in_specs=[
                pl.BlockSpec((TM, D), lambda i, *_: (i, 0)),
                pl.BlockSpec((1, 1, N_MOD, D), lambda i, *_: (l, _cond_of_tile(i), 0, 0)),
                pl.BlockSpec((TM, 8), lambda i, *_: (i, 0)),
                pl.BlockSpec((TM, 8), lambda i, *_: (i, 0)),
                pl.BlockSpec(memory_space=pl.ANY),
            ],
            out_specs=out_specs,
            scratch_shapes=[pltpu.VMEM((2, SORTED_ROWS, D), BF16), pltpu.SemaphoreType.DMA((2,))],
        ),
        out_shape=out_shape,
        input_output_aliases=aliases,
        compiler_params=_cparams(("arbitrary",), 24),
        name="moe_combine",
    )(*layout, _hbm(x), mod, lpt, wcol, _hbm(y))


def _tile_map(cnt):
    cnt = cnt[:, 0]
    ids = jnp.arange(N_EXPERTS, dtype=I32)
    n_tiles = (cnt + TE - 1) // TE
    cum = jnp.cumsum(n_tiles)
    total = cum[-1]
    i = jnp.arange(MOE_TILES, dtype=I32)
    ii = jnp.minimum(i, total - 1)
    e = jnp.sum((ii[:, None] >= cum[None, :]).astype(I32), axis=1)
    onehot = (e[:, None] == ids[None, :]).astype(I32)
    pick = lambda v: jnp.sum(onehot * v[None, :], axis=1)
    j = ii - pick(cum - n_tiles)
    active = i < total
    n_valid = jnp.where(active, jnp.clip(pick(cnt) - j * TE, 0, TE), 0)
    first = (active & (j == 0)).astype(I32)
    nonempty = n_tiles > 0
    ordinal = jnp.cumsum(nonempty.astype(I32)) - 1
    later = jnp.where(nonempty[None, :] & (ids[None, :] > ids[:, None]), ids[None, :], N_EXPERTS)
    nxt = jnp.min(later, axis=1)
    nxt_of = lambda v: jnp.sum((v[:, None] == ids[None, :]).astype(I32) * nxt[None, :], axis=1)
    nxt2 = jnp.where(nxt == N_EXPERTS, N_EXPERTS, nxt_of(nxt))
    none = lambda v: jnp.where(v == N_EXPERTS, -1, v)
    as_i32 = lambda v: v.astype(I32)
    return tuple(map(as_i32, (e * CAP_TILES + j, e, n_valid, first, pick(ordinal),
                              pick(none(nxt)), pick(none(nxt2)))))


def _moe(l, x, tail, mod, w_gate, w_up, w_down, split_out):
    h2, lp, rf, nch, loff, gb, cnt = tail
    layout = tuple(a[:, :, 0].reshape(N_TILES * N_EXPERTS) for a in (nch, loff, gb))
    xs = _dispatch(layout, h2, lp)
    y = _experts(l, _tile_map(cnt), xs, w_gate, w_up, w_down)
    return _combine(l, layout, x, mod, lp.T, rf.T, y, split_out)


def _rope_tables():
    t = jnp.arange(LAT_S)
    row = (t // GRID_W).astype(F32)
    col = (t % GRID_W).astype(F32)
    nf = HEAD_DIM // 4
    inv = ROPE_BASE ** (-jnp.arange(nf, dtype=F32) / nf)
    ar = row[:, None] * inv[None, :]
    ac = col[:, None] * inv[None, :]
    cos = jnp.concatenate([jnp.cos(ar), jnp.cos(ar), jnp.cos(ac), jnp.cos(ac)], axis=1)
    sin = jnp.concatenate([-jnp.sin(ar), jnp.sin(ar), -jnp.sin(ac), jnp.sin(ac)], axis=1)
    reps = QK_W // HEAD_DIM
    return jnp.tile(cos, (1, reps)), jnp.tile(sin, (1, reps))


def kernel(x_prompt, x_sample, cache_k, cache_v, c, c_ctx, norm1_g, norm2_g, w_ada, b_ada, w_in, w_out,
           q_norm_g, k_norm_g, lam_q1, lam_k1, lam_q2, lam_k2, subln_g, sg_w, sg_b, pool_w, pool_scale,
           router_w, router_b, w_gate, w_up, w_down):
    x = (x_prompt.reshape(T_CTX, D), x_sample.reshape(T_LAT, D))
    cond = jnp.concatenate([c_ctx[None, :], c], axis=0)
    mod = _modulation(cond, w_ada, b_ada)

    seg_id = np.arange(QK_W) // HEAD_DIM
    seg = jnp.asarray((seg_id[:, None] == seg_id[None, :]) / HEAD_DIM, dtype=BF16)
    tri = jnp.asarray(np.arange(TM)[:, None] < np.arange(TM)[None, :], dtype=BF16)
    rope_tabs = _rope_tables()
    rwt = router_w.T
    rbb = jnp.broadcast_to(router_b[:, None], (N_EXPERTS, TM))
    states = None

    for l in range(DEPTH):
        if l % 2 == 0:
            e = l // 2
            qg = jnp.tile(q_norm_g[e], QK_W // HEAD_DIM)[None, :]
            kg = jnp.tile(k_norm_g[e], QK_W // HEAD_DIM)[None, :]
            sgb = jnp.broadcast_to(sg_b[:, :, :, None], (DEPTH // 2, SG_GROUPS, SG_CHUNK, 128))
            x_c, x_l = x if isinstance(x, tuple) else (x, x)
            common = (mod, norm1_g, w_in, qg, kg, seg, sg_w, sgb)
            q_c, k_c, v_c, sg_c, *states = _front(False, l, x_c, *common, None, states)
            q_l, k_l, v_l, sg_l = _front(True, l, x_l, *common, rope_tabs, None)
            lam = (lam_q1, lam_k1, lam_q2, lam_k2, subln_g)
            att_c = _attention(False, l, q_c, k_c, v_c, None, None, *lam)
            att_l = _attention(True, l, q_l, k_l, v_l, cache_k, cache_v, *lam)
            x, *tail = _out_proj(l, x, mod, att_c, att_l, sg_c, sg_l, w_out, norm2_g, rwt, rbb, tri)
        else:
            x, *tail = _pool_mixer(l, x, mod, norm1_g, pool_w, pool_scale, norm2_g, rwt, rbb, tri)
        x = _moe(l, x, tail, mod, w_gate, w_up, w_down, split_out=(l == DEPTH - 1))
    state_k, state_v = states

    y_prompt = x[0].reshape(N_CTX_B, CTX_S, D)
    y_sample = x[1].reshape(N_LAT_B, LAT_S, D)
    state_k = state_k.reshape(N_CTX_B, DEPTH // 2, CTX_S, N_HEADS, 2, HEAD_DIM)
    state_v = state_v.reshape(N_CTX_B, DEPTH // 2, CTX_S, N_HEADS, VAL_DIM)
    return y_prompt, y_sample, state_k, state_v
```

```python
import functools
import math

import jax
import jax.numpy as jnp
import numpy as np
from jax import lax
from jax.experimental import pallas as pl
from jax.experimental.pallas import tpu as pltpu

F32 = jnp.float32
BF16 = jnp.bfloat16
I32 = jnp.int32

D = 1024
N_CTX_B, CTX_S = 16, 256
N_LAT_B, LAT_S = 2, 1024
PAST = 512
DEPTH = 4
T_CTX = N_CTX_B * CTX_S
T_LAT = N_LAT_B * LAT_S
T = T_CTX + T_LAT
GRID_W = 64
N_HEADS = 4
HEAD_DIM = 64
VAL_DIM = 128
QK_W = 512
MIX_A = 512
MIX_B = 512
IN_W = 2560
SG_GROUPS = 4
SG_CHUNK = 128
POOL_WINDOWS = (2, 4, 8, 16)
POOL_DIM = 256
POOL_HALO = 8
N_EXPERTS = 16
N_GROUPS = 4
PER_GROUP = 4
D_FF = 512
N_MOD = 6
N_COND = 1 + N_LAT_B
EPS = 1e-6
ROPE_BASE = 10000.0

TM = 256
N_TILES = T // TM
N_CTX_TILES = T_CTX // TM
LAT_TILES_PER_SEQ = LAT_S // TM
ROW_CHUNK = 16
PAD_PER_RUN = ROW_CHUNK - 1
SORTED_ROWS = -(-(2 * TM + N_EXPERTS * PAD_PER_RUN) // 128) * 128
TE = 512
W_SLOTS = 3
ATTN_CTX_SEQS = 2
TF = 512
TO = 512
TP = 512
POOL_RB = 128
CAP_TILES = -(-(T + N_TILES * PAD_PER_RUN) // TE)
EXPERT_CAP = CAP_TILES * TE
MOE_TILES = (2 * T + N_EXPERTS * N_TILES * PAD_PER_RUN) // TE + N_EXPERTS
MOD_TN = 1024
MIB = 1024 * 1024

NT_DIMS = (((1,), (1,)), ((), ()))


def _cparams(sem, vmem_mib):
    return pltpu.CompilerParams(dimension_semantics=sem, vmem_limit_bytes=vmem_mib * MIB)


def _hbm(x):
    try:
        return pltpu.with_memory_space_constraint(x, pltpu.HBM)
    except ValueError:
        return x


def _hbm_out(shape, dtype):
    return pltpu.HBM(shape, dtype)


def _sigmoid(x):
    return 1.0 / (1.0 + jnp.exp(-x))


def _gelu_tanh(x):
    c = math.sqrt(2.0 / math.pi)
    return x * (0.5 * (1.0 + jnp.tanh(c * (x + 0.044715 * (x * x * x)))))


def _modnorm(x, g, scale, shift):
    ms = jnp.mean(x * x, axis=-1, keepdims=True)
    return (x * lax.rsqrt(ms + EPS) * g) * (1.0 + scale) + shift


def _cond_of_tile(i):
    return jnp.where(i < N_CTX_TILES, 0, 1 + (i - N_CTX_TILES) // LAT_TILES_PER_SEQ)


def _mod_kernel(c_ref, w_ref, b_ref, o_ref):
    c = c_ref[...]
    s = (c * _sigmoid(c)).astype(BF16)
    o_ref[0] = jnp.dot(s, w_ref[0].astype(BF16), preferred_element_type=F32) + b_ref[0]


def _modulation(cond, w_ada, b_ada):
    cond8 = jnp.concatenate([cond, jnp.zeros((8 - N_COND, D), F32)], axis=0)
    out = pl.pallas_call(
        _mod_kernel,
        grid=(DEPTH, N_MOD * D // MOD_TN),
        in_specs=[
            pl.BlockSpec((8, D), lambda l, n: (0, 0)),
            pl.BlockSpec((1, D, MOD_TN), lambda l, n: (l, 0, n)),
            pl.BlockSpec((1, 1, MOD_TN), lambda l, n: (l, 0, n)),
        ],
        out_specs=pl.BlockSpec((1, 8, MOD_TN), lambda l, n: (l, 0, n)),
        out_shape=jax.ShapeDtypeStruct((DEPTH, 8, N_MOD * D), F32),
        compiler_params=_cparams(("arbitrary", "arbitrary"), 24),
        name="adaln_mod",
    )(cond8, w_ada, b_ada.reshape(DEPTH, 1, N_MOD * D))
    return out.reshape(DEPTH, 8, N_MOD, D)


def _front_kernel(latent, n_state_in, *refs):
    (x_ref, m_ref, n1_ref, w_ref, qg_ref, kg_ref, seg_ref, sgw_ref, sgb_ref), refs = refs[:9], refs[9:]
    if latent:
        cos_ref, sin_ref, q_o, k_o, v_o, sg_o, wbf = refs
    else:
        q_o, k_o, v_o, sg_o, sk_o, sv_o, wbf = refs[n_state_in:]

    @pl.when(pl.program_id(0) == 0)
    def _():
        for c0 in range(0, IN_W, 512):
            wbf[:, c0:c0 + 512] = w_ref[0, :, c0:c0 + 512].astype(BF16)

    h = _modnorm(x_ref[...], n1_ref[0], m_ref[0, 0, 1:2, :], m_ref[0, 0, 0:1, :])
    proj = jnp.dot(h.astype(BF16), wbf[...], preferred_element_type=F32)

    def qk_norm(z, g):
        ms = jnp.dot((z * z).astype(BF16), seg_ref[...], preferred_element_type=F32)
        return z * lax.rsqrt(ms + EPS) * g

    q = qk_norm(proj[:, 0:QK_W], qg_ref[...])
    k = qk_norm(proj[:, QK_W:2 * QK_W], kg_ref[...])
    v = proj[:, 2 * QK_W:2 * QK_W + MIX_A]

    if latent:
        lane = lax.broadcasted_iota(I32, (1, QK_W), 1)
        first = (lane % 32) < 16

        def rope(z):
            partner = jnp.where(first, pltpu.roll(z, QK_W - 16, 1), pltpu.roll(z, 16, 1))
            return z * cos_ref[...] + partner * sin_ref[...]

        q = rope(q)
        k = rope(k)
    else:
        for b in range(TF // CTX_S):
            sk_o[b, 0] = k[b * CTX_S:(b + 1) * CTX_S, :]
            sv_o[b, 0] = v[b * CTX_S:(b + 1) * CTX_S, :]

    q_o[...] = (q * (HEAD_DIM ** -0.5)).astype(BF16)
    k_o[...] = k.astype(BF16)
    v_o[...] = v.astype(BF16)

    gu0 = 2 * QK_W + MIX_A
    gv0 = gu0 + MIX_B
    for g in range(SG_GROUPS):
        wg = sgw_ref[0, g].astype(BF16)
        for n in range(TF // SG_CHUNK):
            rows = slice(n * SG_CHUNK, (n + 1) * SG_CHUNK)
            cu = slice(gu0 + g * 128, gu0 + (g + 1) * 128)
            cv = slice(gv0 + g * 128, gv0 + (g + 1) * 128)
            gv = _gelu_tanh(proj[rows, cv]).astype(BF16)
            mixed = jnp.dot(wg, gv, preferred_element_type=F32) + sgb_ref[0, g]
            sg_o[rows, g * 128:(g + 1) * 128] = (_gelu_tanh(proj[rows, cu]) * mixed).astype(BF16)


def _front(latent, l, x, mod, norm1_g, w_in, qg, kg, seg, sg_w, sgb, rope_tabs, states):
    e = l // 2
    rows = T_LAT if latent else T_CTX
    n_tiles = rows // TF
    off = T_CTX // TF if (latent and x.shape[0] == T) else 0
    tiles_per_seq = LAT_S // TF
    if latent:
        cond_map = lambda i: (l, 1 + i // tiles_per_seq, 0, 0)
    else:
        cond_map = lambda i: (l, 0, 0, 0)
    in_specs = [
        pl.BlockSpec((TF, D), lambda i: (i + off, 0)),
        pl.BlockSpec((1, 1, N_MOD, D), cond_map),
        pl.BlockSpec((1, 1, D), lambda i: (l, 0, 0)),
        pl.BlockSpec((1, D, IN_W), lambda i: (e, 0, 0)),
        pl.BlockSpec((1, QK_W), lambda i: (0, 0)),
        pl.BlockSpec((1, QK_W), lambda i: (0, 0)),
        pl.BlockSpec((QK_W, QK_W), lambda i: (0, 0)),
        pl.BlockSpec((1, SG_GROUPS, SG_CHUNK, SG_CHUNK), lambda i: (e, 0, 0, 0)),
        pl.BlockSpec((1, SG_GROUPS, SG_CHUNK, 128), lambda i: (e, 0, 0, 0)),
    ]
    args = [_hbm(x), mod, norm1_g.reshape(DEPTH, 1, D), w_in, qg, kg, seg, sg_w, sgb]
    tok = lambda: pl.BlockSpec((TF, QK_W), lambda i: (i, 0))
    out_specs = [tok(), tok(), tok(), tok()]
    out_shape = [_hbm_out((rows, QK_W), BF16)] * 4
    aliases = {}
    if latent:
        cos_t, sin_t = rope_tabs
        in_specs += [pl.BlockSpec((TF, QK_W), lambda i: (i % tiles_per_seq, 0))] * 2
        args += [cos_t, sin_t]
    else:
        if states:
            in_specs += [pl.BlockSpec(memory_space=pl.ANY)] * 2
            args += list(states)
            aliases = {9: 4, 10: 5}
        st = lambda: pl.BlockSpec((TF // CTX_S, 1, CTX_S, QK_W), lambda i: (i, e, 0, 0))
        out_specs += [st(), st()]
        out_shape += [_hbm_out((N_CTX_B, DEPTH // 2, CTX_S, QK_W), F32)] * 2
    return pl.pallas_call(
        functools.partial(_front_kernel, latent, len(aliases)),
        grid=(n_tiles,),
        in_specs=in_specs,
        out_specs=out_specs,
        out_shape=out_shape,
        scratch_shapes=[pltpu.VMEM((D, IN_W), BF16)],
        input_output_aliases=aliases,
        compiler_params=_cparams(("arbitrary",), 48),
        name="front_lat" if latent else "front_ctx",
    )(*args)


def _attn_kernel(has_cache, seqs, heads, lam0, *refs):
    if has_cache:
        (q_ref, k_ref, v_ref, ck_ref, cv_ref, lq1, lk1, lq2, lk2, sub_ref, o_ref) = refs
    else:
        (q_ref, k_ref, v_ref, lq1, lk1, lq2, lk2, sub_ref, o_ref) = refs

    lane = lax.broadcasted_iota(I32, (1, VAL_DIM), 1)
    lam = (jnp.exp(jnp.sum(lq1[0] * lk1[0], axis=-1, keepdims=True))
           - jnp.exp(jnp.sum(lq2[0] * lk2[0], axis=-1, keepdims=True)) + lam0)
    q_rows = q_ref.shape[0] // seqs
    k_rows = k_ref.shape[0] // seqs

    for sq, h in [(sq, h) for sq in range(seqs) for h in range(heads)]:
        hs = slice(h * VAL_DIM, (h + 1) * VAL_DIM)
        qr = slice(sq * q_rows, (sq + 1) * q_rows)
        kr = slice(sq * k_rows, (sq + 1) * k_rows)
        qf = q_ref[qr, hs].astype(F32)
        qc = [jnp.where(lane < HEAD_DIM, qf, 0.0).astype(BF16),
              jnp.where(lane >= HEAD_DIM, qf, 0.0).astype(BF16)]
        keys = [k_ref[kr, hs]]
        vals = [v_ref[kr, hs]]
        if has_cache:
            keys.append(ck_ref[0, 0].astype(BF16))
            vals.append(cv_ref[0, 0].astype(BF16))

        probs = []
        for c in range(2):
            s = [lax.dot_general(qc[c], kk, NT_DIMS, preferred_element_type=F32) for kk in keys]
            m = s[0].max(axis=-1, keepdims=True)
            for sx in s[1:]:
                m = jnp.maximum(m, sx.max(axis=-1, keepdims=True))
            p = [jnp.exp(sx - m) for sx in s]
            den = p[0].sum(axis=-1, keepdims=True)
            for px in p[1:]:
                den = den + px.sum(axis=-1, keepdims=True)
            probs.append((p, den))

        r0 = 1.0 / probs[0][1]
        r1 = lam / probs[1][1]
        o = None
        for idx in range(len(keys)):
            a = (probs[0][0][idx] * r0 - probs[1][0][idx] * r1).astype(BF16)
            t = jnp.dot(a, vals[idx], preferred_element_type=F32)
            o = t if o is None else o + t
        ms = jnp.mean(o * o, axis=-1, keepdims=True)
        o_ref[qr, hs] = ((o * lax.rsqrt(ms + EPS) * sub_ref[0]) * (1.0 - lam0)).astype(BF16)


def _attention(has_cache, l, q, k, v, cache_k, cache_v, lam_q1, lam_k1, lam_q2, lam_k2, subln_g):
    e = l // 2
    lam0 = 0.8 - 0.6 * math.exp(-0.3 * l)
    n_b, s_len = (N_LAT_B, LAT_S) if has_cache else (N_CTX_B, CTX_S)
    heads = 1 if has_cache else N_HEADS
    seqs = 1 if has_cache else ATTN_CTX_SEQS
    width = heads * VAL_DIM
    nq = s_len // TM
    kv = lambda: pl.BlockSpec((seqs * s_len, width), lambda b, h, qi: (b, h))
    par = lambda: pl.BlockSpec((1, 1, HEAD_DIM), lambda b, h, qi: (e, 0, 0))
    in_specs = [pl.BlockSpec((seqs * TM, width), lambda b, h, qi: (b * nq + qi, h)), kv(), kv()]
    args = [_hbm(q), _hbm(k), _hbm(v)]
    if has_cache:
        cs = lambda: pl.BlockSpec((1, 1, PAST, VAL_DIM), lambda b, h, qi: (b, e, 0, h))
        in_specs += [cs(), cs()]
        args += [cache_k.reshape(N_LAT_B, DEPTH // 2, PAST, QK_W),
                 cache_v.reshape(N_LAT_B, DEPTH // 2, PAST, MIX_A)]
    in_specs += [par(), par(), par(), par(),
                 pl.BlockSpec((1, 1, VAL_DIM), lambda b, h, qi: (e, 0, 0))]
    r3 = lambda a: a.reshape(DEPTH // 2, 1, -1)
    args += [r3(lam_q1), r3(lam_k1), r3(lam_q2), r3(lam_k2), r3(subln_g)]
    return pl.pallas_call(
        functools.partial(_attn_kernel, has_cache, seqs, heads, lam0),
        grid=(n_b // seqs, N_HEADS // heads, nq),
        in_specs=in_specs,
        out_specs=pl.BlockSpec((seqs * TM, width), lambda b, h, qi: (b * nq + qi, h)),
        out_shape=_hbm_out((n_b * s_len, MIX_A), BF16),
        compiler_params=_cparams(("arbitrary", "arbitrary", "arbitrary"), 32),
        name="attn_lat" if has_cache else "attn_ctx",
    )(*args)


def _route_tail(sub_tile, x_new, m_ref, n2_ref, rwt_ref, rbb_ref, tri_ref,
                h2_o, lp_o, rf_o, nch_o, loff_o, gb_o, cnt_o, carry):
    tok = slice(sub_tile * TM, (sub_tile + 1) * TM)
    if sub_tile == 0:
        @pl.when(pl.program_id(0) == 0)
        def _():
            carry[...] = jnp.zeros_like(carry)

    h2 = _modnorm(x_new, n2_ref[0], m_ref[0, 0, 4:5, :], m_ref[0, 0, 3:4, :])

    hh = h2.astype(BF16)
    h2_o[tok, :] = hh
    hl = (h2 - hh.astype(F32)).astype(BF16)
    rw = rwt_ref[...]
    wh = rw.astype(BF16)
    wl = (rw - wh.astype(F32)).astype(BF16)
    dg = lambda a, b: lax.dot_general(a, b, NT_DIMS, preferred_element_type=F32)
    logits = dg(wh, hh) + dg(wh, hl) + dg(wl, hh)
    score = _sigmoid(logits)
    sel = score + rbb_ref[...]

    neg_inf = jnp.full((1, TM), -jnp.inf, F32)
    grp = []
    for g in range(N_GROUPS):
        s_rows = [sel[PER_GROUP * g + j:PER_GROUP * g + j + 1, :] for j in range(PER_GROUP)]
        c_rows = [score[PER_GROUP * g + j:PER_GROUP * g + j + 1, :] for j in range(PER_GROUP)]
        f_idx = jnp.zeros((1, TM), I32)
        f_val, f_sc = s_rows[0], c_rows[0]
        for j in range(1, PER_GROUP):
            better = s_rows[j] > f_val
            f_idx = jnp.where(better, j, f_idx)
            f_val = jnp.where(better, s_rows[j], f_val)
            f_sc = jnp.where(better, c_rows[j], f_sc)
        s_idx = jnp.zeros((1, TM), I32)
        s_val, s_sc = neg_inf, jnp.zeros((1, TM), F32)
        for j in range(PER_GROUP):
            better = (f_idx != j) & (s_rows[j] > s_val)
            s_idx = jnp.where(better, j, s_idx)
            s_val = jnp.where(better, s_rows[j], s_val)
            s_sc = jnp.where(better, c_rows[j], s_sc)
        grp.append((f_val + s_val, PER_GROUP * g + f_idx, PER_GROUP * g + s_idx, f_sc, s_sc))

    best, e0, e1, c0, c1 = grp[0]
    for g in range(1, N_GROUPS):
        better = grp[g][0] > best
        best = jnp.where(better, grp[g][0], best)
        e0 = jnp.where(better, grp[g][1], e0)
        e1 = jnp.where(better, grp[g][2], e1)
        c0 = jnp.where(better, grp[g][3], c0)
        c1 = jnp.where(better, grp[g][4], c1)
    den = c0 + c1
    w0 = c0 / den
    w1 = c1 / den

    sub = lax.broadcasted_iota(I32, (N_EXPERTS, TM), 0)
    hit0 = sub == jnp.broadcast_to(e0, (N_EXPERTS, TM))
    hit1 = sub == jnp.broadcast_to(e1, (N_EXPERTS, TM))
    onehot = jnp.where(hit0 | hit1, 1.0, 0.0)
    prefix = jnp.dot(onehot.astype(BF16), tri_ref[...], preferred_element_type=F32)
    count = jnp.sum(onehot, axis=1, keepdims=True)
    padded = jnp.floor((count + (ROW_CHUNK - 1)) * (1.0 / ROW_CHUNK)) * ROW_CHUNK

    sub1 = lax.broadcasted_iota(I32, (N_EXPERTS, 1), 0)
    run = jnp.zeros((1, 1), F32)
    local_off = jnp.zeros((N_EXPERTS, 1), F32)
    for e in range(N_EXPERTS):
        local_off = jnp.where(sub1 == e, run, local_off)
        run = run + padded[e:e + 1, :]
    place = prefix + local_off
    lpos0 = jnp.sum(jnp.where(hit0, place, 0.0), axis=0, keepdims=True).astype(I32)
    lpos1 = jnp.sum(jnp.where(hit1, place, 0.0), axis=0, keepdims=True).astype(I32)

    lanes = lambda col: jnp.broadcast_to(col, (N_EXPERTS, 128)).astype(I32)
    nch_o[sub_tile] = lanes(padded * (1.0 / ROW_CHUNK))
    loff_o[sub_tile] = lanes(local_off)
    gb_o[sub_tile] = carry[...].astype(I32)
    carry[...] = carry[...] + padded
    cnt_o[...] = carry[...].astype(I32)

    sub8 = lax.broadcasted_iota(I32, (8, TM), 0)
    bc = lambda r: jnp.broadcast_to(r, (8, TM))
    lp_o[:, tok] = jnp.where(sub8 == 0, bc(lpos0), jnp.where(sub8 == 1, bc(lpos1), 0))
    rf_o[:, tok] = jnp.where(sub8 == 0, bc(w0), jnp.where(sub8 == 1, bc(w1), 0.0))


def _tail_specs(l, sub_tiles):
    rows = sub_tiles * TM
    in_specs = [
        pl.BlockSpec((1, 1, D), lambda i: (l, 0, 0)),
        pl.BlockSpec((N_EXPERTS, D), lambda i: (0, 0)),
        pl.BlockSpec((N_EXPERTS, TM), lambda i: (0, 0)),
        pl.BlockSpec((TM, TM), lambda i: (0, 0)),
    ]
    per_tile = lambda: pl.BlockSpec((sub_tiles, N_EXPERTS, 128), lambda i: (i, 0, 0))
    out_specs = [
        pl.BlockSpec((rows, D), lambda i: (i, 0)),
        pl.BlockSpec((8, rows), lambda i: (0, i)),
        pl.BlockSpec((8, rows), lambda i: (0, i)),
        per_tile(), per_tile(), per_tile(),
        pl.BlockSpec((N_EXPERTS, 128), lambda i: (0, 0)),
    ]
    out_shape = [
        _hbm_out((T, D), BF16),
        jax.ShapeDtypeStruct((8, T), I32),
        jax.ShapeDtypeStruct((8, T), F32),
        jax.ShapeDtypeStruct((N_TILES, N_EXPERTS, 128), I32),
        jax.ShapeDtypeStruct((N_TILES, N_EXPERTS, 128), I32),
        jax.ShapeDtypeStruct((N_TILES, N_EXPERTS, 128), I32),
        jax.ShapeDtypeStruct((N_EXPERTS, 128), I32),
    ]
    scratch = [pltpu.VMEM((N_EXPERTS, 128), F32)]
    return in_specs, out_specs, out_shape, scratch


def _out_kernel(split_x, *refs):
    x_refs, refs = refs[:1 + split_x], refs[1 + split_x:]
    (m_ref, ac_ref, al_ref, sc_ref, sl_ref, w_ref, n2_ref, rwt_ref, rbb_ref, tri_ref,
     x_o, *tail_refs) = refs
    *tail_outs, wbf, carry = tail_refs
    i = pl.program_id(0)

    @pl.when(i == 0)
    def _():
        wbf[...] = w_ref[0].astype(BF16)

    is_ctx = i < T_CTX // TO
    x = jnp.where(is_ctx, x_refs[0][...], x_refs[1][...]) if split_x else x_refs[0][...]
    att = jnp.where(is_ctx, ac_ref[...], al_ref[...])
    sgo = jnp.where(is_ctx, sc_ref[...], sl_ref[...])
    out = (jnp.dot(att, wbf[0:MIX_A, :], preferred_element_type=F32)
           + jnp.dot(sgo, wbf[MIX_A:MIX_A + MIX_B, :], preferred_element_type=F32))
    x_new = x + m_ref[0, 0, 2:3, :] * out
    x_o[...] = x_new
    for sub_tile in range(TO // TM):
        _route_tail(sub_tile, x_new[sub_tile * TM:(sub_tile + 1) * TM, :],
                    m_ref, n2_ref, rwt_ref, rbb_ref, tri_ref, *tail_outs, carry)


def _out_proj(l, x, mod, att_c, att_l, sg_c, sg_l, w_out, norm2_g, rwt, rbb, tri):
    e = l // 2
    t_in, t_out, t_shape, t_scratch = _tail_specs(l, TO // TM)
    n_ctx = T_CTX // TO
    ctx = lambda w: pl.BlockSpec((TO, w), lambda i: (jnp.minimum(i, n_ctx - 1), 0))
    lat = lambda w: pl.BlockSpec((TO, w), lambda i: (jnp.maximum(i - n_ctx, 0), 0))
    split_x = isinstance(x, tuple)
    if split_x:
        x_specs, x_args, aliases = [ctx(D), lat(D)], [_hbm(x[0]), _hbm(x[1])], {}
    else:
        x_specs, x_args, aliases = [pl.BlockSpec((TO, D), lambda i: (i, 0))], [_hbm(x)], {0: 0}
    in_specs = x_specs + [
        pl.BlockSpec((1, 1, N_MOD, D), lambda i: (l, _cond_of_tile(i * (TO // TM)), 0, 0)),
        ctx(MIX_A), lat(MIX_A), ctx(MIX_B), lat(MIX_B),
        pl.BlockSpec((1, D, D), lambda i: (e, 0, 0)),
    ] + t_in
    return pl.pallas_call(
        functools.partial(_out_kernel, split_x),
        grid=(T // TO,),
        in_specs=in_specs,
        out_specs=[pl.BlockSpec((TO, D), lambda i: (i, 0))] + t_out,
        out_shape=[_hbm_out((T, D), F32)] + t_shape,
        scratch_shapes=[pltpu.VMEM((D, D), BF16)] + t_scratch,
        input_output_aliases=aliases,
        compiler_params=_cparams(("arbitrary",), 40),
        name="out_proj_route",
    )(*x_args, mod, _hbm(att_c), _hbm(att_l), _hbm(sg_c), _hbm(sg_l), w_out,
      norm2_g.reshape(DEPTH, 1, D), rwt, rbb, tri)


def _pool_tables():
    t = np.arange(TP)[:, None]
    s = np.arange(TP + POOL_RB)[None, :]
    p = s - POOL_HALO
    in_cat = s < TP + 2 * POOL_HALO
    bands, sizes = [], []
    for kind in range(3):
        if kind == 0:
            same = (p >= 0) & (p < TP) & (p // CTX_S == t // CTX_S)
        elif kind == 1:
            same = (p >= 0) & (t >= 0)
        else:
            same = (p < TP) & (t >= 0)
        per_w, size_w = [], []
        for w in POOL_WINDOWS:
            band = (p >= t - w // 2) & (p < t - w // 2 + w) & same & in_cat
            size_w.append(band.sum(axis=1, keepdims=True))
            per_w.append(np.stack([band[rb * POOL_RB:(rb + 1) * POOL_RB, rb * POOL_RB:(rb + 2) * POOL_RB]
                                   for rb in range(TP // POOL_RB)]))
        bands.append(np.stack(per_w))
        sizes.append(np.stack(size_w))
    return jnp.asarray(np.stack(bands), dtype=BF16), jnp.asarray(np.stack(sizes), dtype=F32)


def _pool_kernel(x_ref, xp_ref, xn_ref, m_ref, n1_ref, band_ref, size_ref, pw_ref, ps_ref,
                 n2_ref, rwt_ref, rbb_ref, tri_ref,
                 x_o, *tail_refs):
    *tail_outs, carry = tail_refs
    g1n, sc1, sh1 = n1_ref[0], m_ref[0, 0, 1:2, :], m_ref[0, 0, 0:1, :]
    x = x_ref[...]
    h = _modnorm(x, g1n, sc1, sh1)
    hcat = jnp.concatenate([_modnorm(xp_ref[...], g1n, sc1, sh1), h, _modnorm(xn_ref[...], g1n, sc1, sh1),
                            jnp.zeros((POOL_RB - 2 * POOL_HALO, D), F32)], axis=0)
    hi = hcat.astype(BF16)
    lo = (hcat - hi.astype(F32)).astype(BF16)
    dot = lambda a, b: jnp.dot(a, b, preferred_element_type=F32)
    outs = []
    for g in range(len(POOL_WINDOWS)):
        cols = slice(g * POOL_DIM, (g + 1) * POOL_DIM)
        sums = []
        for rb in range(TP // POOL_RB):
            near = slice(rb * POOL_RB, (rb + 2) * POOL_RB)
            band = band_ref[0, g, rb]
            sums.append(dot(band, hi[near, cols]) + dot(band, lo[near, cols]))
        mean = jnp.concatenate(sums, axis=0) / size_ref[0, g]
        dlt = (mean - h[:, cols]).astype(BF16)
        outs.append(dot(dlt, pw_ref[0, g].astype(BF16)))
    out = jnp.concatenate(outs, axis=1) * ps_ref[0]
    x_new = x + m_ref[0, 0, 2:3, :] * out
    x_o[...] = x_new
    for sub_tile in range(TP // TM):
        _route_tail(sub_tile, x_new[sub_tile * TM:(sub_tile + 1) * TM, :],
                    m_ref, n2_ref, rwt_ref, rbb_ref, tri_ref, *tail_outs, carry)


def _pool_mixer(l, x, mod, norm1_g, pool_w, pool_scale, norm2_g, rwt, rbb, tri):
    o = l // 2
    t_in, t_out, t_shape, t_scratch = _tail_specs(l, TP // TM)
    bands, sizes = _pool_tables()
    halo_blocks = TP // POOL_HALO
    n_ctx = T_CTX // TP
    kind = lambda i: jnp.where(i < n_ctx, 0, 1 + (i - n_ctx) % (LAT_S // TP))
    n_w, n_rb = len(POOL_WINDOWS), TP // POOL_RB
    in_specs = [
        pl.BlockSpec((TP, D), lambda i: (i, 0)),
        pl.BlockSpec((POOL_HALO, D), lambda i: (jnp.maximum(i * halo_blocks - 1, 0), 0)),
        pl.BlockSpec((POOL_HALO, D), lambda i: (jnp.minimum((i + 1) * halo_blocks, T // POOL_HALO - 1), 0)),
        pl.BlockSpec((1, 1, N_MOD, D), lambda i: (l, _cond_of_tile(i * (TP // TM)), 0, 0)),
        pl.BlockSpec((1, 1, D), lambda i: (l, 0, 0)),
        pl.BlockSpec((1, n_w, n_rb, POOL_RB, 2 * POOL_RB), lambda i: (kind(i), 0, 0, 0, 0)),
        pl.BlockSpec((1, n_w, TP, 1), lambda i: (kind(i), 0, 0, 0)),
        pl.BlockSpec((1, n_w, POOL_DIM, POOL_DIM), lambda i: (o, 0, 0, 0)),
        pl.BlockSpec((1, 1, D), lambda i: (o, 0, 0)),
    ] + t_in
    return pl.pallas_call(
        _pool_kernel,
        grid=(T // TP,),
        in_specs=in_specs,
        out_specs=[pl.BlockSpec((TP, D), lambda i: (i, 0))] + t_out,
        out_shape=[_hbm_out((T, D), F32)] + t_shape,
        scratch_shapes=t_scratch,
        compiler_params=_cparams(("arbitrary",), 40),
        name="pool_route",
    )(_hbm(x), _hbm(x), _hbm(x), mod, norm1_g.reshape(DEPTH, 1, D), bands, sizes, pool_w,
      pool_scale.reshape(DEPTH // 2, 1, D),
      norm2_g.reshape(DEPTH, 1, D), rwt, rbb, tri)


def _start_run_copies(tile, nch_ref, loff_ref, gb_ref, make_copy):
    for e in range(N_EXPERTS):
        n = nch_ref[tile * N_EXPERTS + e]
        lo = loff_ref[tile * N_EXPERTS + e]
        gb = gb_ref[tile * N_EXPERTS + e] + e * EXPERT_CAP

        def issue(c, carry, lo=lo, gb=gb):
            make_copy(pl.multiple_of(lo + c * (2 * ROW_CHUNK), ROW_CHUNK),
                      pl.multiple_of(gb + c * (2 * ROW_CHUNK), ROW_CHUNK), 2 * ROW_CHUNK).start()
            return carry

        lax.fori_loop(0, lax.shift_right_logical(n, 1), issue, 0)

        @pl.when((n & 1) == 1)
        def _(n=n, lo=lo, gb=gb):
            make_copy(pl.multiple_of(lo + (n - 1) * ROW_CHUNK, ROW_CHUNK),
                      pl.multiple_of(gb + (n - 1) * ROW_CHUNK, ROW_CHUNK), ROW_CHUNK).start()


def _wait_run_copies(tile, nch_ref, make_copy):
    doubles, singles = 0, 0
    for e in range(N_EXPERTS):
        n = nch_ref[tile * N_EXPERTS + e]
        doubles = doubles + lax.shift_right_logical(n, 1)
        singles = singles + (n & 1)

    def drain(rows):
        def body(c, carry):
            make_copy(0, 0, rows).wait()
            return carry
        return body

    lax.fori_loop(0, doubles, drain(2 * ROW_CHUNK), 0)
    lax.fori_loop(0, singles, drain(ROW_CHUNK), 0)


def _dispatch_kernel(nch_ref, loff_ref, gb_ref, h_ref, lp_ref, xs_ref, sbuf, sems):
    i = pl.program_id(0)
    slot = i % 2
    r = lax.broadcasted_iota(I32, (SORTED_ROWS, TM), 0)
    hit = (r == lp_ref[0:1, :]) | (r == lp_ref[1:2, :])
    perm = jnp.where(hit, 1.0, 0.0).astype(BF16)
    sbuf[slot] = jnp.dot(perm, h_ref[...], preferred_element_type=F32).astype(BF16)

    def copies_of(s):
        def make_copy(local_row, global_row, rows):
            return pltpu.make_async_copy(sbuf.at[s, pl.ds(local_row, rows)],
                                         xs_ref.at[pl.ds(global_row, rows)], sems.at[s])
        return make_copy

    @pl.when(i > 0)
    def _():
        _wait_run_copies(i - 1, nch_ref, copies_of(1 - slot))

    _start_run_copies(i, nch_ref, loff_ref, gb_ref, copies_of(slot))

    @pl.when(i == N_TILES - 1)
    def _():
        _wait_run_copies(i, nch_ref, copies_of(slot))


def _dispatch(layout, h2, lp):
    return pl.pallas_call(
        _dispatch_kernel,
        grid_spec=pltpu.PrefetchScalarGridSpec(
            num_scalar_prefetch=3,
            grid=(N_TILES,),
            in_specs=[pl.BlockSpec((TM, D), lambda i, *_: (i, 0)),
                      pl.BlockSpec((8, TM), lambda i, *_: (0, i))],
            out_specs=pl.BlockSpec(memory_space=pl.ANY),
            scratch_shapes=[pltpu.VMEM((2, SORTED_ROWS, D), BF16), pltpu.SemaphoreType.DMA((2,))],
        ),
        out_shape=_hbm_out((N_EXPERTS * EXPERT_CAP, D), BF16),
        compiler_params=_cparams(("arbitrary",), 16),
        name="moe_dispatch",
    )(*layout, _hbm(h2), lp)


def _moe_kernel(l, rb_ref, e_ref, nv_ref, first_ref, ord_ref, nxt_ref, nxt2_ref,
                xs_ref, wg_hbm, wu_hbm, wd_hbm, y_ref, wgf, wuf, wdf, wgb, wub, wdb, sems):
    i = pl.program_id(0)
    n_valid = nv_ref[i]

    def weight_copies(e, slot):
        return (pltpu.make_async_copy(wg_hbm.at[l, e], wgf.at[slot], sems.at[slot, 0]),
                pltpu.make_async_copy(wu_hbm.at[l, e], wuf.at[slot], sems.at[slot, 1]),
                pltpu.make_async_copy(wd_hbm.at[l, e], wdf.at[slot], sems.at[slot, 2]))

    @pl.when(first_ref[i] == 1)
    def _():
        k = ord_ref[i]
        slot = k % W_SLOTS

        @pl.when(k == 0)
        def _():
            for cp in weight_copies(e_ref[i], 0):
                cp.start(priority=1)

            @pl.when(nxt_ref[i] >= 0)
            def _():
                for cp in weight_copies(nxt_ref[i], 1):
                    cp.start(priority=1)

        for cp in weight_copies(e_ref[i], slot):
            cp.wait()

        @pl.when(nxt2_ref[i] >= 0)
        def _():
            for cp in weight_copies(nxt2_ref[i], (k + 2) % W_SLOTS):
                cp.start(priority=1)

        wgb[...] = wgf[slot].astype(BF16)
        wub[...] = wuf[slot].astype(BF16)
        wdb[...] = wdf[slot].astype(BF16)

    @pl.when(n_valid > 0)
    def _():
        row = lax.broadcasted_iota(I32, (TE, 1), 0)
        x = jnp.where(row < n_valid, xs_ref[...], jnp.zeros((), BF16))
        gate = jnp.dot(x, wgb[...], preferred_element_type=F32)
        up = jnp.dot(x, wub[...], preferred_element_type=F32)
        act = (gate * _sigmoid(gate) * up).astype(BF16)
        y_ref[...] = jnp.dot(act, wdb[...], preferred_element_type=F32).astype(BF16)


def _experts(l, tile_map, xs, w_gate, w_up, w_down):
    n_pre = len(tile_map)
    blk = lambda i, rb, *_: (rb[i], 0)
    return pl.pallas_call(
        functools.partial(_moe_kernel, l),
        grid_spec=pltpu.PrefetchScalarGridSpec(
            num_scalar_prefetch=n_pre,
            grid=(MOE_TILES,),
            in_specs=[
                pl.BlockSpec((TE, D), blk),
                pl.BlockSpec(memory_space=pl.ANY),
                pl.BlockSpec(memory_space=pl.ANY),
                pl.BlockSpec(memory_space=pl.ANY),
            ],
            out_specs=pl.BlockSpec((TE, D), blk),
            scratch_shapes=[pltpu.VMEM((W_SLOTS, D, D_FF), F32), pltpu.VMEM((W_SLOTS, D, D_FF), F32),
                            pltpu.VMEM((W_SLOTS, D_FF, D), F32),
                            pltpu.VMEM((D, D_FF), BF16), pltpu.VMEM((D, D_FF), BF16),
                            pltpu.VMEM((D_FF, D), BF16),
                            pltpu.SemaphoreType.DMA((W_SLOTS, 3))],
        ),
        out_shape=_hbm_out((N_EXPERTS * EXPERT_CAP, D), BF16),
        compiler_params=_cparams(("arbitrary",), 48),
        name="moe_experts",
    )(*tile_map, _hbm(xs), w_gate, w_up, w_down)


def _combine_kernel(split_out, nch_ref, loff_ref, gb_ref, x_ref, m_ref, lpt_ref, w_ref, y_ref, *refs):
    *o_refs, ybuf, sems = refs
    i = pl.program_id(0)
    slot = i % 2

    def copies_of(s):
        def make_copy(local_row, global_row, rows):
            return pltpu.make_async_copy(y_ref.at[pl.ds(global_row, rows)],
                                         ybuf.at[s, pl.ds(local_row, rows)], sems.at[s])
        return make_copy

    @pl.when(i == 0)
    def _():
        ybuf[...] = jnp.zeros_like(ybuf)
        _start_run_copies(i, nch_ref, loff_ref, gb_ref, copies_of(slot))

    @pl.when(i + 1 < N_TILES)
    def _():
        _start_run_copies(i + 1, nch_ref, loff_ref, gb_ref, copies_of(1 - slot))

    _wait_run_copies(i, nch_ref, copies_of(slot))

    col = lax.broadcasted_iota(I32, (TM, SORTED_ROWS), 1)
    pick = jnp.where(col == lpt_ref[:, 0:1], w_ref[:, 0:1],
                     jnp.where(col == lpt_ref[:, 1:2], w_ref[:, 1:2], 0.0)).astype(BF16)
    moe = jnp.dot(pick, ybuf[slot], preferred_element_type=F32)
    res = x_ref[...] + m_ref[0, 0, 5:6, :] * moe
    if split_out:
        @pl.when(i < N_CTX_TILES)
        def _():
            o_refs[0][...] = res

        @pl.when(i >= N_CTX_TILES)
        def _():
            o_refs[1][...] = res
    else:
        o_refs[0][...] = res


def _combine(l, layout, x, mod, lpt, wcol, y, split_out):
    if split_out:
        out_specs = [pl.BlockSpec((TM, D), lambda i, *_: (jnp.minimum(i, N_CTX_TILES - 1), 0)),
                     pl.BlockSpec((TM, D), lambda i, *_: (jnp.maximum(i - N_CTX_TILES, 0), 0))]
        out_shape = [jax.ShapeDtypeStruct((T_CTX, D), F32), jax.ShapeDtypeStruct((T_LAT, D), F32)]
        aliases = {}
    else:
        out_specs = pl.BlockSpec((TM, D), lambda i, *_: (i, 0))
        out_shape = _hbm_out((T, D), F32)
        aliases = {3: 0}
    return pl.pallas_call(
        functools.partial(_combine_kernel, split_out),
        grid_spec=pltpu.PrefetchScalarGridSpec(
            num_scalar_prefetch=3,
            grid=(N_TILES,),
            in_specs=[
                pl.BlockSpec((TM, D), lambda i, *_: (i, 0)),
                pl.BlockSpec((1, 1, N_MOD, D), lambda i, *_: (l, _cond_of_tile(i), 0, 0)),
                pl.BlockSpec((TM, 8), lambda i, *_: (i, 0)),
                pl.BlockSpec((TM, 8), lambda i, *_: (i, 0)),
                pl.BlockSpec(memory_space=pl.ANY),
            ],
            out_specs=out_specs,
            scratch_shapes=[pltpu.VMEM((2, SORTED_ROWS, D), BF16), pltpu.SemaphoreType.DMA((2,))],
        ),
        out_shape=out_shape,
        input_output_aliases=aliases,
        compiler_params=_cparams(("arbitrary",), 24),
        name="moe_combine",
    )(*layout, _hbm(x), mod, lpt, wcol, _hbm(y))


def _tile_map(cnt):
    cnt = cnt[:, 0]
    ids = jnp.arange(N_EXPERTS, dtype=I32)
    n_tiles = (cnt + TE - 1) // TE
    cum = jnp.cumsum(n_tiles)
    total = cum[-1]
    i = jnp.arange(MOE_TILES, dtype=I32)
    ii = jnp.minimum(i, total - 1)
    e = jnp.sum((ii[:, None] >= cum[None, :]).astype(I32), axis=1)
    onehot = (e[:, None] == ids[None, :]).astype(I32)
    pick = lambda v: jnp.sum(onehot * v[None, :], axis=1)
    j = ii - pick(cum - n_tiles)
    active = i < total
    n_valid = jnp.where(active, jnp.clip(pick(cnt) - j * TE, 0, TE), 0)
    first = (active & (j == 0)).astype(I32)
    nonempty = n_tiles > 0
    ordinal = jnp.cumsum(nonempty.astype(I32)) - 1
    later = jnp.where(nonempty[None, :] & (ids[None, :] > ids[:, None]), ids[None, :], N_EXPERTS)
    nxt = jnp.min(later, axis=1)
    nxt_of = lambda v: jnp.sum((v[:, None] == ids[None, :]).astype(I32) * nxt[None, :], axis=1)
    nxt2 = jnp.where(nxt == N_EXPERTS, N_EXPERTS, nxt_of(nxt))
    none = lambda v: jnp.where(v == N_EXPERTS, -1, v)
    as_i32 = lambda v: v.astype(I32)
    return tuple(map(as_i32, (e * CAP_TILES + j, e, n_valid, first, pick(ordinal),
                              pick(none(nxt)), pick(none(nxt2)))))


def _moe(l, x, tail, mod, w_gate, w_up, w_down, split_out):
    h2, lp, rf, nch, loff, gb, cnt = tail
    layout = tuple(a[:, :, 0].reshape(N_TILES * N_EXPERTS) for a in (nch, loff, gb))
    xs = _dispatch(layout, h2, lp)
    y = _experts(l, _tile_map(cnt), xs, w_gate, w_up, w_down)
    return _combine(l, layout, x, mod, lp.T, rf.T, y, split_out)


def _rope_tables():
    t = jnp.arange(LAT_S)
    row = (t // GRID_W).astype(F32)
    col = (t % GRID_W).astype(F32)
    nf = HEAD_DIM // 4
    inv = ROPE_BASE ** (-jnp.arange(nf, dtype=F32) / nf)
    ar = row[:, None] * inv[None, :]
    ac = col[:, None] * inv[None, :]
    cos = jnp.concatenate([jnp.cos(ar), jnp.cos(ar), jnp.cos(ac), jnp.cos(ac)], axis=1)
    sin = jnp.concatenate([-jnp.sin(ar), jnp.sin(ar), -jnp.sin(ac), jnp.sin(ac)], axis=1)
    reps = QK_W // HEAD_DIM
    return jnp.tile(cos, (1, reps)), jnp.tile(sin, (1, reps))


def kernel(x_prompt, x_sample, cache_k, cache_v, c, c_ctx, norm1_g, norm2_g, w_ada, b_ada, w_in, w_out,
           q_norm_g, k_norm_g, lam_q1, lam_k1, lam_q2, lam_k2, subln_g, sg_w, sg_b, pool_w, pool_scale,
           router_w, router_b, w_gate, w_up, w_down):
    x = (x_prompt.reshape(T_CTX, D), x_sample.reshape(T_LAT, D))
    cond = jnp.concatenate([c_ctx[None, :], c], axis=0)
    mod = _modulation(cond, w_ada, b_ada)

    seg_id = np.arange(QK_W) // HEAD_DIM
    seg = jnp.asarray((seg_id[:, None] == seg_id[None, :]) / HEAD_DIM, dtype=BF16)
    tri = jnp.asarray(np.arange(TM)[:, None] < np.arange(TM)[None, :], dtype=BF16)
    rope_tabs = _rope_tables()
    rwt = router_w.T
    rbb = jnp.broadcast_to(router_b[:, None], (N_EXPERTS, TM))
    states = None

    for l in range(DEPTH):
        if l % 2 == 0:
            e = l // 2
            qg = jnp.tile(q_norm_g[e], QK_W // HEAD_DIM)[None, :]
            kg = jnp.tile(k_norm_g[e], QK_W // HEAD_DIM)[None, :]
            sgb = jnp.broadcast_to(sg_b[:, :, :, None], (DEPTH // 2, SG_GROUPS, SG_CHUNK, 128))
            x_c, x_l = x if isinstance(x, tuple) else (x, x)
            common = (mod, norm1_g, w_in, qg, kg, seg, sg_w, sgb)
            q_c, k_c, v_c, sg_c, *states = _front(False, l, x_c, *common, None, states)
            q_l, k_l, v_l, sg_l = _front(True, l, x_l, *common, rope_tabs, None)
            lam = (lam_q1, lam_k1, lam_q2, lam_k2, subln_g)
            att_c = _attention(False, l, q_c, k_c, v_c, None, None, *lam)
            att_l = _attention(True, l, q_l, k_l, v_l, cache_k, cache_v, *lam)
            x, *tail = _out_proj(l, x, mod, att_c, att_l, sg_c, sg_l, w_out, norm2_g, rwt, rbb, tri)
        else:
            x, *tail = _pool_mixer(l, x, mod, norm1_g, pool_w, pool_scale, norm2_g, rwt, rbb, tri)
        x = _moe(l, x, tail, mod, w_gate, w_up, w_down, split_out=(l == DEPTH - 1))
    state_k, state_v = states

    y_prompt = x[0].reshape(N_CTX_B, CTX_S, D)
    y_sample = x[1].reshape(N_LAT_B, LAT_S, D)
    state_k = state_k.reshape(N_CTX_B, DEPTH // 2, CTX_S, N_HEADS, 2, HEAD_DIM)
    state_v = state_v.reshape(N_CTX_B, DEPTH // 2, CTX_S, N_HEADS, VAL_DIM)
    return y_prompt, y_sample, state_k, state_v
```

```python
import functools
import math

import jax
import jax.numpy as jnp
import numpy as np
from jax import lax
from jax.experimental import pallas as pl
from jax.experimental.pallas import tpu as pltpu

F32 = jnp.float32
BF16 = jnp.bfloat16
I32 = jnp.int32

D = 1024
N_CTX_B, CTX_S = 16, 256
N_LAT_B, LAT_S = 2, 1024
PAST = 512
DEPTH = 4
T_CTX = N_CTX_B * CTX_S
T_LAT = N_LAT_B * LAT_S
T = T_CTX + T_LAT
GRID_W = 64
N_HEADS = 4
HEAD_DIM = 64
VAL_DIM = 128
QK_W = 512
MIX_A = 512
MIX_B = 512
IN_W = 2560
SG_GROUPS = 4
SG_CHUNK = 128
POOL_WINDOWS = (2, 4, 8, 16)
POOL_DIM = 256
POOL_HALO = 8
N_EXPERTS = 16
N_GROUPS = 4
PER_GROUP = 4
D_FF = 512
N_MOD = 6
N_COND = 1 + N_LAT_B
EPS = 1e-6
ROPE_BASE = 10000.0

TM = 256
N_TILES = T // TM
N_CTX_TILES = T_CTX // TM
LAT_TILES_PER_SEQ = LAT_S // TM
ROW_CHUNK = 16
PAD_PER_RUN = ROW_CHUNK - 1
SORTED_ROWS = -(-(2 * TM + N_EXPERTS * PAD_PER_RUN) // 128) * 128
TE = 512
W_SLOTS = 3
ATTN_CTX_SEQS = 2
MOVE_SUBS = 2
TF = 512
TO = 512
TP = 512
POOL_RB = 128
CAP_TILES = -(-(T + N_TILES * PAD_PER_RUN) // TE)
EXPERT_CAP = CAP_TILES * TE
MOE_TILES = (2 * T + N_EXPERTS * N_TILES * PAD_PER_RUN) // TE + N_EXPERTS
MOD_TN = 1024
MIB = 1024 * 1024

NT_DIMS = (((1,), (1,)), ((), ()))


def _cparams(sem, vmem_mib):
    return pltpu.CompilerParams(dimension_semantics=sem, vmem_limit_bytes=vmem_mib * MIB)


def _hbm(x):
    try:
        return pltpu.with_memory_space_constraint(x, pltpu.HBM)
    except ValueError:
        return x


def _hbm_out(shape, dtype):
    return pltpu.HBM(shape, dtype)


def _sigmoid(x):
    return 1.0 / (1.0 + jnp.exp(-x))


def _gelu_tanh(x):
    c = math.sqrt(2.0 / math.pi)
    return x * (0.5 * (1.0 + jnp.tanh(c * (x + 0.044715 * (x * x * x)))))


def _modnorm(x, g, scale, shift):
    ms = jnp.mean(x * x, axis=-1, keepdims=True)
    return (x * lax.rsqrt(ms + EPS)) * (g * (1.0 + scale)) + shift


def _cond_of_tile(i):
    return jnp.where(i < N_CTX_TILES, 0, 1 + (i - N_CTX_TILES) // LAT_TILES_PER_SEQ)


def _mod_kernel(c_ref, w_ref, b_ref, o_ref):
    c = c_ref[...]
    s = (c * _sigmoid(c)).astype(BF16)
    o_ref[0] = jnp.dot(s, w_ref[0].astype(BF16), preferred_element_type=F32) + b_ref[0]


def _modulation(cond, w_ada, b_ada):
    cond8 = jnp.concatenate([cond, jnp.zeros((8 - N_COND, D), F32)], axis=0)
    out = pl.pallas_call(
        _mod_kernel,
        grid=(DEPTH, N_MOD * D // MOD_TN),
        in_specs=[
            pl.BlockSpec((8, D), lambda l, n: (0, 0)),
            pl.BlockSpec((1, D, MOD_TN), lambda l, n: (l, 0, n)),
            pl.BlockSpec((1, 1, MOD_TN), lambda l, n: (l, 0, n)),
        ],
        out_specs=pl.BlockSpec((1, 8, MOD_TN), lambda l, n: (l, 0, n)),
        out_shape=jax.ShapeDtypeStruct((DEPTH, 8, N_MOD * D), F32),
        compiler_params=_cparams(("arbitrary", "arbitrary"), 24),
        name="adaln_mod",
    )(cond8, w_ada, b_ada.reshape(DEPTH, 1, N_MOD * D))
    return out.reshape(DEPTH, 8, N_MOD, D)


def _front_kernel(latent, n_state_in, *refs):
    (x_ref, m_ref, n1_ref, w_ref, qg_ref, kg_ref, seg_ref, sgw_ref, sgb_ref), refs = refs[:9], refs[9:]
    if latent:
        cos_ref, sin_ref, q_o, k_o, v_o, sg_o, wbf = refs
    else:
        q_o, k_o, v_o, sg_o, sk_o, sv_o, wbf = refs[n_state_in:]

    @pl.when(pl.program_id(0) == 0)
    def _():
        for c0 in range(0, IN_W, 512):
            wbf[:, c0:c0 + 512] = w_ref[0, :, c0:c0 + 512].astype(BF16)

    h = _modnorm(x_ref[...], n1_ref[0], m_ref[0, 0, 1:2, :], m_ref[0, 0, 0:1, :])
    proj = jnp.dot(h.astype(BF16), wbf[...], preferred_element_type=F32)

    def qk_norm(z, g):
        ms = jnp.dot((z * z).astype(BF16), seg_ref[...], preferred_element_type=F32)
        return z * lax.rsqrt(ms + EPS) * g

    q = qk_norm(proj[:, 0:QK_W], qg_ref[...])
    k = qk_norm(proj[:, QK_W:2 * QK_W], kg_ref[...])
    v = proj[:, 2 * QK_W:2 * QK_W + MIX_A]

    if latent:
        lane = lax.broadcasted_iota(I32, (1, QK_W), 1)
        first = (lane % 32) < 16

        def rope(z):
            partner = jnp.where(first, pltpu.roll(z, QK_W - 16, 1), pltpu.roll(z, 16, 1))
            return z * cos_ref[...] + partner * sin_ref[...]

        q = rope(q)
        k = rope(k)
    else:
        for b in range(TF // CTX_S):
            sk_o[b, 0] = k[b * CTX_S:(b + 1) * CTX_S, :]
            sv_o[b, 0] = v[b * CTX_S:(b + 1) * CTX_S, :]

    q_o[...] = (q * (HEAD_DIM ** -0.5)).astype(BF16)
    k_o[...] = k.astype(BF16)
    v_o[...] = v.astype(BF16)

    gu0 = 2 * QK_W + MIX_A
    gv0 = gu0 + MIX_B
    for g in range(SG_GROUPS):
        wg = sgw_ref[0, g].astype(BF16)
        for n in range(TF // SG_CHUNK):
            rows = slice(n * SG_CHUNK, (n + 1) * SG_CHUNK)
            cu = slice(gu0 + g * 128, gu0 + (g + 1) * 128)
            cv = slice(gv0 + g * 128, gv0 + (g + 1) * 128)
            gv = _gelu_tanh(proj[rows, cv]).astype(BF16)
            mixed = jnp.dot(wg, gv, preferred_element_type=F32) + sgb_ref[0, g]
            sg_o[rows, g * 128:(g + 1) * 128] = (_gelu_tanh(proj[rows, cu]) * mixed).astype(BF16)


def _front(latent, l, x, mod, norm1_g, w_in, qg, kg, seg, sg_w, sgb, rope_tabs, states):
    e = l // 2
    rows = T_LAT if latent else T_CTX
    n_tiles = rows // TF
    off = T_CTX // TF if (latent and x.shape[0] == T) else 0
    tiles_per_seq = LAT_S // TF
    if latent:
        cond_map = lambda i: (l, 1 + i // tiles_per_seq, 0, 0)
    else:
        cond_map = lambda i: (l, 0, 0, 0)
    in_specs = [
        pl.BlockSpec((TF, D), lambda i: (i + off, 0)),
        pl.BlockSpec((1, 1, N_MOD, D), cond_map),
        pl.BlockSpec((1, 1, D), lambda i: (l, 0, 0)),
        pl.BlockSpec((1, D, IN_W), lambda i: (e, 0, 0)),
        pl.BlockSpec((1, QK_W), lambda i: (0, 0)),
        pl.BlockSpec((1, QK_W), lambda i: (0, 0)),
        pl.BlockSpec((QK_W, QK_W), lambda i: (0, 0)),
        pl.BlockSpec((1, SG_GROUPS, SG_CHUNK, SG_CHUNK), lambda i: (e, 0, 0, 0)),
        pl.BlockSpec((1, SG_GROUPS, SG_CHUNK, 128), lambda i: (e, 0, 0, 0)),
    ]
    args = [_hbm(x), mod, norm1_g.reshape(DEPTH, 1, D), w_in, qg, kg, seg, sg_w, sgb]
    tok = lambda: pl.BlockSpec((TF, QK_W), lambda i: (i, 0))
    out_specs = [tok(), tok(), tok(), tok()]
    out_shape = [_hbm_out((rows, QK_W), BF16)] * 4
    aliases = {}
    if latent:
        cos_t, sin_t = rope_tabs
        in_specs += [pl.BlockSpec((TF, QK_W), lambda i: (i % tiles_per_seq, 0))] * 2
        args += [cos_t, sin_t]
    else:
        if states:
            in_specs += [pl.BlockSpec(memory_space=pl.ANY)] * 2
            args += list(states)
            aliases = {9: 4, 10: 5}
        st = lambda: pl.BlockSpec((TF // CTX_S, 1, CTX_S, QK_W), lambda i: (i, e, 0, 0))
        out_specs += [st(), st()]
        out_shape += [_hbm_out((N_CTX_B, DEPTH // 2, CTX_S, QK_W), F32)] * 2
    return pl.pallas_call(
        functools.partial(_front_kernel, latent, len(aliases)),
        grid=(n_tiles,),
        in_specs=in_specs,
        out_specs=out_specs,
        out_shape=out_shape,
        scratch_shapes=[pltpu.VMEM((D, IN_W), BF16)],
        input_output_aliases=aliases,
        compiler_params=_cparams(("arbitrary",), 48),
        name="front_lat" if latent else "front_ctx",
    )(*args)


def _attn_kernel(has_cache, seqs, heads, lam0, *refs):
    if has_cache:
        (q_ref, k_ref, v_ref, ck_ref, cv_ref, lq1, lk1, lq2, lk2, sub_ref, o_ref) = refs
    else:
        (q_ref, k_ref, v_ref, lq1, lk1, lq2, lk2, sub_ref, o_ref) = refs

    lane = lax.broadcasted_iota(I32, (1, VAL_DIM), 1)
    lam = (jnp.exp(jnp.sum(lq1[0] * lk1[0], axis=-1, keepdims=True))
           - jnp.exp(jnp.sum(lq2[0] * lk2[0], axis=-1, keepdims=True)) + lam0)
    q_rows = q_ref.shape[0] // seqs
    k_rows = k_ref.shape[0] // seqs

    for sq, h in [(sq, h) for sq in range(seqs) for h in range(heads)]:
        hs = slice(h * VAL_DIM, (h + 1) * VAL_DIM)
        qr = slice(sq * q_rows, (sq + 1) * q_rows)
        kr = slice(sq * k_rows, (sq + 1) * k_rows)
        qf = q_ref[qr, hs].astype(F32)
        qc = [jnp.where(lane < HEAD_DIM, qf, 0.0).astype(BF16),
              jnp.where(lane >= HEAD_DIM, qf, 0.0).astype(BF16)]
        keys = [k_ref[kr, hs]]
        vals = [v_ref[kr, hs]]
        if has_cache:
            keys.append(ck_ref[0, 0].astype(BF16))
            vals.append(cv_ref[0, 0].astype(BF16))

        probs = []
        for c in range(2):
            s = [lax.dot_general(qc[c], kk, NT_DIMS, preferred_element_type=F32) for kk in keys]
            m = s[0].max(axis=-1, keepdims=True)
            for sx in s[1:]:
                m = jnp.maximum(m, sx.max(axis=-1, keepdims=True))
            p = [jnp.exp(sx - m) for sx in s]
            den = p[0].sum(axis=-1, keepdims=True)
            for px in p[1:]:
                den = den + px.sum(axis=-1, keepdims=True)
            probs.append((p, den))

        r0 = 1.0 / probs[0][1]
        r1 = lam / probs[1][1]
        o = None
        for idx in range(len(keys)):
            a = (probs[0][0][idx] * r0 - probs[1][0][idx] * r1).astype(BF16)
            t = jnp.dot(a, vals[idx], preferred_element_type=F32)
            o = t if o is None else o + t
        ms = jnp.mean(o * o, axis=-1, keepdims=True)
        o_ref[qr, hs] = ((o * lax.rsqrt(ms + EPS) * sub_ref[0]) * (1.0 - lam0)).astype(BF16)


def _attention(has_cache, l, q, k, v, cache_k, cache_v, lam_q1, lam_k1, lam_q2, lam_k2, subln_g):
    e = l // 2
    lam0 = 0.8 - 0.6 * math.exp(-0.3 * l)
    n_b, s_len = (N_LAT_B, LAT_S) if has_cache else (N_CTX_B, CTX_S)
    heads = 1 if has_cache else N_HEADS
    seqs = 1 if has_cache else ATTN_CTX_SEQS
    width = heads * VAL_DIM
    nq = s_len // TM
    kv = lambda: pl.BlockSpec((seqs * s_len, width), lambda b, h, qi: (b, h))
    par = lambda: pl.BlockSpec((1, 1, HEAD_DIM), lambda b, h, qi: (e, 0, 0))
    in_specs = [pl.BlockSpec((seqs * TM, width), lambda b, h, qi: (b * nq + qi, h)), kv(), kv()]
    args = [_hbm(q), _hbm(k), _hbm(v)]
    if has_cache:
        cs = lambda: pl.BlockSpec((1, 1, PAST, VAL_DIM), lambda b, h, qi: (b, e, 0, h))
        in_specs += [cs(), cs()]
        args += [cache_k.reshape(N_LAT_B, DEPTH // 2, PAST, QK_W),
                 cache_v.reshape(N_LAT_B, DEPTH // 2, PAST, MIX_A)]
    in_specs += [par(), par(), par(), par(),
                 pl.BlockSpec((1, 1, VAL_DIM), lambda b, h, qi: (e, 0, 0))]
    r3 = lambda a: a.reshape(DEPTH // 2, 1, -1)
    args += [r3(lam_q1), r3(lam_k1), r3(lam_q2), r3(lam_k2), r3(subln_g)]
    return pl.pallas_call(
        functools.partial(_attn_kernel, has_cache, seqs, heads, lam0),
        grid=(n_b // seqs, N_HEADS // heads, nq),
        in_specs=in_specs,
        out_specs=pl.BlockSpec((seqs * TM, width), lambda b, h, qi: (b * nq + qi, h)),
        out_shape=_hbm_out((n_b * s_len, MIX_A), BF16),
        compiler_params=_cparams(("arbitrary", "arbitrary", "arbitrary"), 32),
        name="attn_lat" if has_cache else "attn_ctx",
    )(*args)


def _route_tail(sub_tile, x_new, m_ref, n2_ref, rwt_ref, rbb_ref, tri_ref,
                h2_o, lp_o, rf_o, nch_o, loff_o, gb_o, cnt_o, carry):
    tok = slice(sub_tile * TM, (sub_tile + 1) * TM)
    if sub_tile == 0:
        @pl.when(pl.program_id(0) == 0)
        def _():
            carry[...] = jnp.zeros_like(carry)

    h2 = _modnorm(x_new, n2_ref[0], m_ref[0, 0, 4:5, :], m_ref[0, 0, 3:4, :])

    hh = h2.astype(BF16)
    h2_o[tok, :] = hh
    hl = (h2 - hh.astype(F32)).astype(BF16)
    rw = rwt_ref[...]
    wh = rw.astype(BF16)
    wl = (rw - wh.astype(F32)).astype(BF16)
    dg = lambda a, b: lax.dot_general(a, b, NT_DIMS, preferred_element_type=F32)
    logits = dg(wh, hh) + dg(wh, hl) + dg(wl, hh)
    score = _sigmoid(logits)
    sel = score + rbb_ref[...]

    neg_inf = jnp.full((1, TM), -jnp.inf, F32)
    grp = []
    for g in range(N_GROUPS):
        s_rows = [sel[PER_GROUP * g + j:PER_GROUP * g + j + 1, :] for j in range(PER_GROUP)]
        c_rows = [score[PER_GROUP * g + j:PER_GROUP * g + j + 1, :] for j in range(PER_GROUP)]
        f_idx = jnp.zeros((1, TM), I32)
        f_val, f_sc = s_rows[0], c_rows[0]
        for j in range(1, PER_GROUP):
            better = s_rows[j] > f_val
            f_idx = jnp.where(better, j, f_idx)
            f_val = jnp.where(better, s_rows[j], f_val)
            f_sc = jnp.where(better, c_rows[j], f_sc)
        s_idx = jnp.zeros((1, TM), I32)
        s_val, s_sc = neg_inf, jnp.zeros((1, TM), F32)
        for j in range(PER_GROUP):
            better = (f_idx != j) & (s_rows[j] > s_val)
            s_idx = jnp.where(better, j, s_idx)
            s_val = jnp.where(better, s_rows[j], s_val)
            s_sc = jnp.where(better, c_rows[j], s_sc)
        grp.append((f_val + s_val, PER_GROUP * g + f_idx, PER_GROUP * g + s_idx, f_sc, s_sc))

    best, e0, e1, c0, c1 = grp[0]
    for g in range(1, N_GROUPS):
        better = grp[g][0] > best
        best = jnp.where(better, grp[g][0], best)
        e0 = jnp.where(better, grp[g][1], e0)
        e1 = jnp.where(better, grp[g][2], e1)
        c0 = jnp.where(better, grp[g][3], c0)
        c1 = jnp.where(better, grp[g][4], c1)
    den = c0 + c1
    w0 = c0 / den
    w1 = c1 / den

    sub = lax.broadcasted_iota(I32, (N_EXPERTS, TM), 0)
    hit0 = sub == jnp.broadcast_to(e0, (N_EXPERTS, TM))
    hit1 = sub == jnp.broadcast_to(e1, (N_EXPERTS, TM))
    onehot = jnp.where(hit0 | hit1, 1.0, 0.0)
    prefix = jnp.dot(onehot.astype(BF16), tri_ref[...], preferred_element_type=F32)
    count = jnp.sum(onehot, axis=1, keepdims=True)
    padded = jnp.floor((count + (ROW_CHUNK - 1)) * (1.0 / ROW_CHUNK)) * ROW_CHUNK

    sub1 = lax.broadcasted_iota(I32, (N_EXPERTS, 1), 0)
    run = jnp.zeros((1, 1), F32)
    local_off = jnp.zeros((N_EXPERTS, 1), F32)
    for e in range(N_EXPERTS):
        local_off = jnp.where(sub1 == e, run, local_off)
        run = run + padded[e:e + 1, :]
    place = prefix + local_off
    lpos0 = jnp.sum(jnp.where(hit0, place, 0.0), axis=0, keepdims=True).astype(I32)
    lpos1 = jnp.sum(jnp.where(hit1, place, 0.0), axis=0, keepdims=True).astype(I32)

    lanes = lambda col: jnp.broadcast_to(col, (N_EXPERTS, 128)).astype(I32)
    nch_o[sub_tile] = lanes(padded * (1.0 / ROW_CHUNK))
    loff_o[sub_tile] = lanes(local_off)
    gb_o[sub_tile] = carry[...].astype(I32)
    carry[...] = carry[...] + padded
    cnt_o[...] = carry[...].astype(I32)

    sub8 = lax.broadcasted_iota(I32, (8, TM), 0)
    bc = lambda r: jnp.broadcast_to(r, (8, TM))
    lp_o[:, tok] = jnp.where(sub8 == 0, bc(lpos0), jnp.where(sub8 == 1, bc(lpos1), 0))
    rf_o[:, tok] = jnp.where(sub8 == 0, bc(w0), jnp.where(sub8 == 1, bc(w1), 0.0))


def _tail_specs(l, sub_tiles):
    rows = sub_tiles * TM
    in_specs = [
        pl.BlockSpec((1, 1, D), lambda i: (l, 0, 0)),
        pl.BlockSpec((N_EXPERTS, D), lambda i: (0, 0)),
        pl.BlockSpec((N_EXPERTS, TM), lambda i: (0, 0)),
        pl.BlockSpec((TM, TM), lambda i: (0, 0)),
    ]
    per_tile = lambda: pl.BlockSpec((sub_tiles, N_EXPERTS, 128), lambda i: (i, 0, 0))
    out_specs = [
        pl.BlockSpec((rows, D), lambda i: (i, 0)),
        pl.BlockSpec((8, rows), lambda i: (0, i)),
        pl.BlockSpec((8, rows), lambda i: (0, i)),
        per_tile(), per_tile(), per_tile(),
        pl.BlockSpec((N_EXPERTS, 128), lambda i: (0, 0)),
    ]
    out_shape = [
        _hbm_out((T, D), BF16),
        jax.ShapeDtypeStruct((8, T), I32),
        jax.ShapeDtypeStruct((8, T), F32),
        jax.ShapeDtypeStruct((N_TILES, N_EXPERTS, 128), I32),
        jax.ShapeDtypeStruct((N_TILES, N_EXPERTS, 128), I32),
        jax.ShapeDtypeStruct((N_TILES, N_EXPERTS, 128), I32),
        jax.ShapeDtypeStruct((N_EXPERTS, 128), I32),
    ]
    scratch = [pltpu.VMEM((N_EXPERTS, 128), F32)]
    return in_specs, out_specs, out_shape, scratch


def _out_kernel(split_x, *refs):
    x_refs, refs = refs[:1 + split_x], refs[1 + split_x:]
    (m_ref, ac_ref, al_ref, sc_ref, sl_ref, w_ref, n2_ref, rwt_ref, rbb_ref, tri_ref,
     x_o, *tail_refs) = refs
    *tail_outs, wbf, carry = tail_refs
    i = pl.program_id(0)

    @pl.when(i == 0)
    def _():
        wbf[...] = w_ref[0].astype(BF16)

    is_ctx = i < T_CTX // TO
    x = jnp.where(is_ctx, x_refs[0][...], x_refs[1][...]) if split_x else x_refs[0][...]
    att = jnp.where(is_ctx, ac_ref[...], al_ref[...])
    sgo = jnp.where(is_ctx, sc_ref[...], sl_ref[...])
    out = (jnp.dot(att, wbf[0:MIX_A, :], preferred_element_type=F32)
           + jnp.dot(sgo, wbf[MIX_A:MIX_A + MIX_B, :], preferred_element_type=F32))
    x_new = x + m_ref[0, 0, 2:3, :] * out
    x_o[...] = x_new
    for sub_tile in range(TO // TM):
        _route_tail(sub_tile, x_new[sub_tile * TM:(sub_tile + 1) * TM, :],
                    m_ref, n2_ref, rwt_ref, rbb_ref, tri_ref, *tail_outs, carry)


def _out_proj(l, x, mod, att_c, att_l, sg_c, sg_l, w_out, norm2_g, rwt, rbb, tri):
    e = l // 2
    t_in, t_out, t_shape, t_scratch = _tail_specs(l, TO // TM)
    n_ctx = T_CTX // TO
    ctx = lambda w: pl.BlockSpec((TO, w), lambda i: (jnp.minimum(i, n_ctx - 1), 0))
    lat = lambda w: pl.BlockSpec((TO, w), lambda i: (jnp.maximum(i - n_ctx, 0), 0))
    split_x = isinstance(x, tuple)
    if split_x:
        x_specs, x_args, aliases = [ctx(D), lat(D)], [_hbm(x[0]), _hbm(x[1])], {}
    else:
        x_specs, x_args, aliases = [pl.BlockSpec((TO, D), lambda i: (i, 0))], [_hbm(x)], {0: 0}
    in_specs = x_specs + [
        pl.BlockSpec((1, 1, N_MOD, D), lambda i: (l, _cond_of_tile(i * (TO // TM)), 0, 0)),
        ctx(MIX_A), lat(MIX_A), ctx(MIX_B), lat(MIX_B),
        pl.BlockSpec((1, D, D), lambda i: (e, 0, 0)),
    ] + t_in
    return pl.pallas_call(
        functools.partial(_out_kernel, split_x),
        grid=(T // TO,),
        in_specs=in_specs,
        out_specs=[pl.BlockSpec((TO, D), lambda i: (i, 0))] + t_out,
        out_shape=[_hbm_out((T, D), F32)] + t_shape,
        scratch_shapes=[pltpu.VMEM((D, D), BF16)] + t_scratch,
        input_output_aliases=aliases,
        compiler_params=_cparams(("arbitrary",), 40),
        name="out_proj_route",
    )(*x_args, mod, _hbm(att_c), _hbm(att_l), _hbm(sg_c), _hbm(sg_l), w_out,
      norm2_g.reshape(DEPTH, 1, D), rwt, rbb, tri)


def _pool_tables():
    t = np.arange(TP)[:, None]
    s = np.arange(TP + POOL_RB)[None, :]
    p = s - POOL_HALO
    in_cat = s < TP + 2 * POOL_HALO
    bands, sizes = [], []
    for kind in range(3):
        if kind == 0:
            same = (p >= 0) & (p < TP) & (p // CTX_S == t // CTX_S)
        elif kind == 1:
            same = (p >= 0) & (t >= 0)
        else:
            same = (p < TP) & (t >= 0)
        per_w, size_w = [], []
        for w in POOL_WINDOWS:
            band = (p >= t - w // 2) & (p < t - w // 2 + w) & same & in_cat
            size_w.append(band.sum(axis=1, keepdims=True))
            per_w.append(np.stack([band[rb * POOL_RB:(rb + 1) * POOL_RB, rb * POOL_RB:(rb + 2) * POOL_RB]
                                   for rb in range(TP // POOL_RB)]))
        bands.append(np.stack(per_w))
        sizes.append(np.stack(size_w))
    return jnp.asarray(np.stack(bands), dtype=BF16), jnp.asarray(np.stack(sizes), dtype=F32)


def _pool_kernel(x_ref, xp_ref, xn_ref, m_ref, n1_ref, band_ref, size_ref, pw_ref, ps_ref,
                 n2_ref, rwt_ref, rbb_ref, tri_ref,
                 x_o, *tail_refs):
    *tail_outs, carry = tail_refs
    g1n, sc1, sh1 = n1_ref[0], m_ref[0, 0, 1:2, :], m_ref[0, 0, 0:1, :]
    x = x_ref[...]
    h = _modnorm(x, g1n, sc1, sh1)
    hcat = jnp.concatenate([_modnorm(xp_ref[...], g1n, sc1, sh1), h, _modnorm(xn_ref[...], g1n, sc1, sh1),
                            jnp.zeros((POOL_RB - 2 * POOL_HALO, D), F32)], axis=0)
    hi = hcat.astype(BF16)
    lo = (hcat - hi.astype(F32)).astype(BF16)
    dot = lambda a, b: jnp.dot(a, b, preferred_element_type=F32)
    outs = []
    for g in range(len(POOL_WINDOWS)):
        cols = slice(g * POOL_DIM, (g + 1) * POOL_DIM)
        sums = []
        for rb in range(TP // POOL_RB):
            near = slice(rb * POOL_RB, (rb + 2) * POOL_RB)
            band = band_ref[0, g, rb]
            sums.append(dot(band, hi[near, cols]) + dot(band, lo[near, cols]))
        mean = jnp.concatenate(sums, axis=0) / size_ref[0, g]
        dlt = (mean - h[:, cols]).astype(BF16)
        outs.append(dot(dlt, pw_ref[0, g].astype(BF16)))
    out = jnp.concatenate(outs, axis=1) * ps_ref[0]
    x_new = x + m_ref[0, 0, 2:3, :] * out
    x_o[...] = x_new
    for sub_tile in range(TP // TM):
        _route_tail(sub_tile, x_new[sub_tile * TM:(sub_tile + 1) * TM, :],
                    m_ref, n2_ref, rwt_ref, rbb_ref, tri_ref, *tail_outs, carry)


def _pool_mixer(l, x, mod, norm1_g, pool_w, pool_scale, norm2_g, rwt, rbb, tri):
    o = l // 2
    t_in, t_out, t_shape, t_scratch = _tail_specs(l, TP // TM)
    bands, sizes = _pool_tables()
    halo_blocks = TP // POOL_HALO
    n_ctx = T_CTX // TP
    kind = lambda i: jnp.where(i < n_ctx, 0, 1 + (i - n_ctx) % (LAT_S // TP))
    n_w, n_rb = len(POOL_WINDOWS), TP // POOL_RB
    in_specs = [
        pl.BlockSpec((TP, D), lambda i: (i, 0)),
        pl.BlockSpec((POOL_HALO, D), lambda i: (jnp.maximum(i * halo_blocks - 1, 0), 0)),
        pl.BlockSpec((POOL_HALO, D), lambda i: (jnp.minimum((i + 1) * halo_blocks, T // POOL_HALO - 1), 0)),
        pl.BlockSpec((1, 1, N_MOD, D), lambda i: (l, _cond_of_tile(i * (TP // TM)), 0, 0)),
        pl.BlockSpec((1, 1, D), lambda i: (l, 0, 0)),
        pl.BlockSpec((1, n_w, n_rb, POOL_RB, 2 * POOL_RB), lambda i: (kind(i), 0, 0, 0, 0)),
        pl.BlockSpec((1, n_w, TP, 1), lambda i: (kind(i), 0, 0, 0)),
        pl.BlockSpec((1, n_w, POOL_DIM, POOL_DIM), lambda i: (o, 0, 0, 0)),
        pl.BlockSpec((1, 1, D), lambda i: (o, 0, 0)),
    ] + t_in
    return pl.pallas_call(
        _pool_kernel,
        grid=(T // TP,),
        in_specs=in_specs,
        out_specs=[pl.BlockSpec((TP, D), lambda i: (i, 0))] + t_out,
        out_shape=[_hbm_out((T, D), F32)] + t_shape,
        scratch_shapes=t_scratch,
        compiler_params=_cparams(("arbitrary",), 40),
        name="pool_route",
    )(_hbm(x), _hbm(x), _hbm(x), mod, norm1_g.reshape(DEPTH, 1, D), bands, sizes, pool_w,
      pool_scale.reshape(DEPTH // 2, 1, D),
      norm2_g.reshape(DEPTH, 1, D), rwt, rbb, tri)


def _start_run_copies(tile, nch_ref, loff_ref, gb_ref, make_copy):
    for e in range(N_EXPERTS):
        n = nch_ref[tile * N_EXPERTS + e]
        lo = loff_ref[tile * N_EXPERTS + e]
        gb = gb_ref[tile * N_EXPERTS + e] + e * EXPERT_CAP

        def issue(c, carry, lo=lo, gb=gb):
            make_copy(pl.multiple_of(lo + c * (2 * ROW_CHUNK), ROW_CHUNK),
                      pl.multiple_of(gb + c * (2 * ROW_CHUNK), ROW_CHUNK), 2 * ROW_CHUNK).start()
            return carry

        lax.fori_loop(0, lax.shift_right_logical(n, 1), issue, 0)

        @pl.when((n & 1) == 1)
        def _(n=n, lo=lo, gb=gb):
            make_copy(pl.multiple_of(lo + (n - 1) * ROW_CHUNK, ROW_CHUNK),
                      pl.multiple_of(gb + (n - 1) * ROW_CHUNK, ROW_CHUNK), ROW_CHUNK).start()


def _wait_run_copies(tile, nch_ref, make_copy):
    doubles, singles = 0, 0
    for e in range(N_EXPERTS):
        n = nch_ref[tile * N_EXPERTS + e]
        doubles = doubles + lax.shift_right_logical(n, 1)
        singles = singles + (n & 1)

    def drain(rows):
        def body(c, carry):
            make_copy(0, 0, rows).wait()
            return carry
        return body

    lax.fori_loop(0, doubles, drain(2 * ROW_CHUNK), 0)
    lax.fori_loop(0, singles, drain(ROW_CHUNK), 0)


def _dispatch_kernel(nch_ref, loff_ref, gb_ref, h_ref, lp_ref, xs_ref, sbuf, sems):
    i = pl.program_id(0)
    slot = i % 2
    r = lax.broadcasted_iota(I32, (SORTED_ROWS, TM), 0)
    for sub in range(MOVE_SUBS):
        tok = slice(sub * TM, (sub + 1) * TM)
        hit = (r == lp_ref[0:1, tok]) | (r == lp_ref[1:2, tok])
        perm = jnp.where(hit, 1.0, 0.0).astype(BF16)
        sbuf[slot, sub] = jnp.dot(perm, h_ref[tok, :], preferred_element_type=F32).astype(BF16)

    def copies_of(s, sub):
        def make_copy(local_row, global_row, rows):
            return pltpu.make_async_copy(sbuf.at[s, sub, pl.ds(local_row, rows)],
                                         xs_ref.at[pl.ds(global_row, rows)], sems.at[s])
        return make_copy

    @pl.when(i > 0)
    def _():
        for sub in range(MOVE_SUBS):
            _wait_run_copies((i - 1) * MOVE_SUBS + sub, nch_ref, copies_of(1 - slot, sub))

    for sub in range(MOVE_SUBS):
        _start_run_copies(i * MOVE_SUBS + sub, nch_ref, loff_ref, gb_ref, copies_of(slot, sub))

    @pl.when(i == N_TILES // MOVE_SUBS - 1)
    def _():
        for sub in range(MOVE_SUBS):
            _wait_run_copies(i * MOVE_SUBS + sub, nch_ref, copies_of(slot, sub))


def _dispatch(layout, h2, lp):
    rows = MOVE_SUBS * TM
    return pl.pallas_call(
        _dispatch_kernel,
        grid_spec=pltpu.PrefetchScalarGridSpec(
            num_scalar_prefetch=3,
            grid=(N_TILES // MOVE_SUBS,),
            in_specs=[pl.BlockSpec((rows, D), lambda i, *_: (i, 0)),
                      pl.BlockSpec((8, rows), lambda i, *_: (0, i))],
            out_specs=pl.BlockSpec(memory_space=pl.ANY),
            scratch_shapes=[pltpu.VMEM((2, MOVE_SUBS, SORTED_ROWS, D), BF16),
                            pltpu.SemaphoreType.DMA((2,))],
        ),
        out_shape=_hbm_out((N_EXPERTS * EXPERT_CAP, D), BF16),
        compiler_params=_cparams(("arbitrary",), 16),
        name="moe_dispatch",
    )(*layout, _hbm(h2), lp)


def _moe_kernel(l, rb_ref, e_ref, nv_ref, first_ref, ord_ref, nxt_ref, nxt2_ref,
                xs_ref, wg_hbm, wu_hbm, wd_hbm, y_ref, wgf, wuf, wdf, wgb, wub, wdb, sems):
    i = pl.program_id(0)
    n_valid = nv_ref[i]

    def weight_copies(e, slot):
        return (pltpu.make_async_copy(wg_hbm.at[l, e], wgf.at[slot], sems.at[slot, 0]),
                pltpu.make_async_copy(wu_hbm.at[l, e], wuf.at[slot], sems.at[slot, 1]),
                pltpu.make_async_copy(wd_hbm.at[l, e], wdf.at[slot], sems.at[slot, 2]))

    @pl.when(first_ref[i] == 1)
    def _():
        k = ord_ref[i]
        slot = k % W_SLOTS

        @pl.when(k == 0)
        def _():
            for cp in weight_copies(e_ref[i], 0):
                cp.start(priority=1)

            @pl.when(nxt_ref[i] >= 0)
            def _():
                for cp in weight_copies(nxt_ref[i], 1):
                    cp.start(priority=1)

        for cp in weight_copies(e_ref[i], slot):
            cp.wait()

        @pl.when(nxt2_ref[i] >= 0)
        def _():
            for cp in weight_copies(nxt2_ref[i], (k + 2) % W_SLOTS):
                cp.start(priority=1)

        wgb[...] = wgf[slot].astype(BF16)
        wub[...] = wuf[slot].astype(BF16)
        wdb[...] = wdf[slot].astype(BF16)

    @pl.when(n_valid > 0)
    def _():
        row = lax.broadcasted_iota(I32, (TE, 1), 0)
        x = jnp.where(row < n_valid, xs_ref[...], jnp.zeros((), BF16))
        gate = jnp.dot(x, wgb[...], preferred_element_type=F32)
        up = jnp.dot(x, wub[...], preferred_element_type=F32)
        act = (gate * _sigmoid(gate) * up).astype(BF16)
        y_ref[...] = jnp.dot(act, wdb[...], preferred_element_type=F32).astype(BF16)


def _experts(l, tile_map, xs, w_gate, w_up, w_down):
    n_pre = len(tile_map)
    blk = lambda i, rb, *_: (rb[i], 0)
    return pl.pallas_call(
        functools.partial(_moe_kernel, l),
        grid_spec=pltpu.PrefetchScalarGridSpec(
            num_scalar_prefetch=n_pre,
            grid=(MOE_TILES,),
            in_specs=[
                pl.BlockSpec((TE, D), blk),
                pl.BlockSpec(memory_space=pl.ANY),
                pl.BlockSpec(memory_space=pl.ANY),
                pl.BlockSpec(memory_space=pl.ANY),
            ],
            out_specs=pl.BlockSpec((TE, D), blk),
            scratch_shapes=[pltpu.VMEM((W_SLOTS, D, D_FF), F32), pltpu.VMEM((W_SLOTS, D, D_FF), F32),
                            pltpu.VMEM((W_SLOTS, D_FF, D), F32),
                            pltpu.VMEM((D, D_FF), BF16), pltpu.VMEM((D, D_FF), BF16),
                            pltpu.VMEM((D_FF, D), BF16),
                            pltpu.SemaphoreType.DMA((W_SLOTS, 3))],
        ),
        out_shape=_hbm_out((N_EXPERTS * EXPERT_CAP, D), BF16),
        compiler_params=_cparams(("arbitrary",), 48),
        name="moe_experts",
    )(*tile_map, _hbm(xs), w_gate, w_up, w_down)


def _combine_kernel(split_out, nch_ref, loff_ref, gb_ref, x_ref, m_ref, lpt_ref, w_ref, y_ref, *refs):
    *o_refs, ybuf, sems = refs
    i = pl.program_id(0)
    slot = i % 2

    def copies_of(s, sub):
        def make_copy(local_row, global_row, rows):
            return pltpu.make_async_copy(y_ref.at[pl.ds(global_row, rows)],
                                         ybuf.at[s, sub, pl.ds(local_row, rows)], sems.at[s])
        return make_copy

    @pl.when(i == 0)
    def _():
        ybuf[...] = jnp.zeros_like(ybuf)
        for sub in range(MOVE_SUBS):
            _start_run_copies(sub, nch_ref, loff_ref, gb_ref, copies_of(slot, sub))

    @pl.when(i + 1 < N_TILES // MOVE_SUBS)
    def _():
        for sub in range(MOVE_SUBS):
            _start_run_copies((i + 1) * MOVE_SUBS + sub, nch_ref, loff_ref, gb_ref, copies_of(1 - slot, sub))

    for sub in range(MOVE_SUBS):
        _wait_run_copies(i * MOVE_SUBS + sub, nch_ref, copies_of(slot, sub))

    col = lax.broadcasted_iota(I32, (TM, SORTED_ROWS), 1)
    parts = []
    for sub in range(MOVE_SUBS):
        tok = slice(sub * TM, (sub + 1) * TM)
        pick = jnp.where(col == lpt_ref[tok, 0:1], w_ref[tok, 0:1],
                         jnp.where(col == lpt_ref[tok, 1:2], w_ref[tok, 1:2], 0.0)).astype(BF16)
        parts.append(jnp.dot(pick, ybuf[slot, sub], preferred_element_type=F32))
    res = x_ref[...] + m_ref[0, 0, 5:6, :] * jnp.concatenate(parts, axis=0)
    if split_out:
        @pl.when(i < T_CTX // (MOVE_SUBS * TM))
        def _():
            o_refs[0][...] = res

        @pl.when(i >= T_CTX // (MOVE_SUBS * TM))
        def _():
            o_refs[1][...] = res
    else:
        o_refs[0][...] = res


def _combine(l, layout, x, mod, lpt, wcol, y, split_out):
    rows = MOVE_SUBS * TM
    n_ctx = T_CTX // rows
    if split_out:
        out_specs = [pl.BlockSpec((rows, D), lambda i, *_: (jnp.minimum(i, n_ctx - 1), 0)),
                     pl.BlockSpec((rows, D), lambda i, *_: (jnp.maximum(i - n_ctx, 0), 0))]
        out_shape = [jax.ShapeDtypeStruct((T_CTX, D), F32), jax.ShapeDtypeStruct((T_LAT, D), F32)]
        aliases = {}
    else:
        out_specs = pl.BlockSpec((rows, D), lambda i, *_: (i, 0))
        out_shape = _hbm_out((T, D), F32)
        aliases = {3: 0}
    return pl.pallas_call(
        functools.partial(_combine_kernel, split_out),
        grid_spec=pltpu.PrefetchScalarGridSpec(
            num_scalar_prefetch=3,
            grid=(N_TILES // MOVE_SUBS,),
            in_specs=[
                pl.BlockSpec((rows, D), lambda i, *_: (i, 0)),
                pl.BlockSpec((1, 1, N_MOD, D), lambda i, *_: (l, _cond_of_tile(i * MOVE_SUBS), 0, 0)),
                pl.BlockSpec((rows, 8), lambda i, *_: (i, 0)),
                pl.BlockSpec((rows, 8), lambda i, *_: (i, 0)),
                pl.BlockSpec(memory_space=pl.ANY),
            ],
            out_specs=out_specs,
            scratch_shapes=[pltpu.VMEM((2, MOVE_SUBS, SORTED_ROWS, D), BF16),
                            pltpu.SemaphoreType.DMA((2,))],
        ),
        out_shape=out_shape,
        input_output_aliases=aliases,
        compiler_params=_cparams(("arbitrary",), 24),
        name="moe_combine",
    )(*layout, _hbm(x), mod, lpt, wcol, _hbm(y))


def _tile_map(cnt):
    cnt = cnt[:, 0]
    ids = jnp.arange(N_EXPERTS, dtype=I32)
    n_tiles = (cnt + TE - 1) // TE
    cum = jnp.cumsum(n_tiles)
    total = cum[-1]
    i = jnp.arange(MOE_TILES, dtype=I32)
    ii = jnp.minimum(i, total - 1)
    e = jnp.sum((ii[:, None] >= cum[None, :]).astype(I32), axis=1)
    onehot = (e[:, None] == ids[None, :]).astype(I32)
    pick = lambda v: jnp.sum(onehot * v[None, :], axis=1)
    j = ii - pick(cum - n_tiles)
    active = i < total
    n_valid = jnp.where(active, jnp.clip(pick(cnt) - j * TE, 0, TE), 0)
    first = (active & (j == 0)).astype(I32)
    nonempty = n_tiles > 0
    ordinal = jnp.cumsum(nonempty.astype(I32)) - 1
    later = jnp.where(nonempty[None, :] & (ids[None, :] > ids[:, None]), ids[None, :], N_EXPERTS)
    nxt = jnp.min(later, axis=1)
    nxt_of = lambda v: jnp.sum((v[:, None] == ids[None, :]).astype(I32) * nxt[None, :], axis=1)
    nxt2 = jnp.where(nxt == N_EXPERTS, N_EXPERTS, nxt_of(nxt))
    none = lambda v: jnp.where(v == N_EXPERTS, -1, v)
    as_i32 = lambda v: v.astype(I32)
    return tuple(map(as_i32, (e * CAP_TILES + j, e, n_valid, first, pick(ordinal),
                              pick(none(nxt)), pick(none(nxt2)))))


def _moe(l, x, tail, mod, w_gate, w_up, w_down, split_out):
    h2, lp, rf, nch, loff, gb, cnt = tail
    layout = tuple(a[:, :, 0].reshape(N_TILES * N_EXPERTS) for a in (nch, loff, gb))
    xs = _dispatch(layout, h2, lp)
    y = _experts(l, _tile_map(cnt), xs, w_gate, w_up, w_down)
    return _combine(l, layout, x, mod, lp.T, rf.T, y, split_out)


def _rope_tables():
    t = jnp.arange(LAT_S)
    row = (t // GRID_W).astype(F32)
    col = (t % GRID_W).astype(F32)
    nf = HEAD_DIM // 4
    inv = ROPE_BASE ** (-jnp.arange(nf, dtype=F32) / nf)
    ar = row[:, None] * inv[None, :]
    ac = col[:, None] * inv[None, :]
    cos = jnp.concatenate([jnp.cos(ar), jnp.cos(ar), jnp.cos(ac), jnp.cos(ac)], axis=1)
    sin = jnp.concatenate([-jnp.sin(ar), jnp.sin(ar), -jnp.sin(ac), jnp.sin(ac)], axis=1)
    reps = QK_W // HEAD_DIM
    return jnp.tile(cos, (1, reps)), jnp.tile(sin, (1, reps))


def kernel(x_prompt, x_sample, cache_k, cache_v, c, c_ctx, norm1_g, norm2_g, w_ada, b_ada, w_in, w_out,
           q_norm_g, k_norm_g, lam_q1, lam_k1, lam_q2, lam_k2, subln_g, sg_w, sg_b, pool_w, pool_scale,
           router_w, router_b, w_gate, w_up, w_down):
    x = (x_prompt.reshape(T_CTX, D), x_sample.reshape(T_LAT, D))
    cond = jnp.concatenate([c_ctx[None, :], c], axis=0)
    mod = _modulation(cond, w_ada, b_ada)

    seg_id = np.arange(QK_W) // HEAD_DIM
    seg = jnp.asarray((seg_id[:, None] == seg_id[None, :]) / HEAD_DIM, dtype=BF16)
    tri = jnp.asarray(np.arange(TM)[:, None] < np.arange(TM)[None, :], dtype=BF16)
    rope_tabs = _rope_tables()
    rwt = router_w.T
    rbb = jnp.broadcast_to(router_b[:, None], (N_EXPERTS, TM))
    states = None

    for l in range(DEPTH):
        if l % 2 == 0:
            e = l // 2
            qg = jnp.tile(q_norm_g[e], QK_W // HEAD_DIM)[None, :]
            kg = jnp.tile(k_norm_g[e], QK_W // HEAD_DIM)[None, :]
            sgb = jnp.broadcast_to(sg_b[:, :, :, None], (DEPTH // 2, SG_GROUPS, SG_CHUNK, 128))
            x_c, x_l = x if isinstance(x, tuple) else (x, x)
            common = (mod, norm1_g, w_in, qg, kg, seg, sg_w, sgb)
            q_c, k_c, v_c, sg_c, *states = _front(False, l, x_c, *common, None, states)
            q_l, k_l, v_l, sg_l = _front(True, l, x_l, *common, rope_tabs, None)
            lam = (lam_q1, lam_k1, lam_q2, lam_k2, subln_g)
            att_c = _attention(False, l, q_c, k_c, v_c, None, None, *lam)
            att_l = _attention(True, l, q_l, k_l, v_l, cache_k, cache_v, *lam)
            x, *tail = _out_proj(l, x, mod, att_c, att_l, sg_c, sg_l, w_out, norm2_g, rwt, rbb, tri)
        else:
            x, *tail = _pool_mixer(l, x, mod, norm1_g, pool_w, pool_scale, norm2_g, rwt, rbb, tri)
        x = _moe(l, x, tail, mod, w_gate, w_up, w_down, split_out=(l == DEPTH - 1))
    state_k, state_v = states

    y_prompt = x[0].reshape(N_CTX_B, CTX_S, D)
    y_sample = x[1].reshape(N_LAT_B, LAT_S, D)
    state_k = state_k.reshape(N_CTX_B, DEPTH // 2, CTX_S, N_HEADS, 2, HEAD_DIM)
    state_v = state_v.reshape(N_CTX_B, DEPTH // 2, CTX_S, N_HEADS, VAL_DIM)
    return y_prompt, y_sample, state_k, state_v
```

```python
import functools
import math

import jax
import jax.numpy as jnp
import numpy as np
from jax import lax
from jax.experimental import pallas as pl
from jax.experimental.pallas import tpu as pltpu

F32 = jnp.float32
BF16 = jnp.bfloat16
I32 = jnp.int32

D = 1024
N_CTX_B, CTX_S = 16, 256
N_LAT_B, LAT_S = 2, 1024
PAST = 512
DEPTH = 4
T_CTX = N_CTX_B * CTX_S
T_LAT = N_LAT_B * LAT_S
T = T_CTX + T_LAT
GRID_W = 64
N_HEADS = 4
HEAD_DIM = 64
VAL_DIM = 128
QK_W = 512
MIX_A = 512
MIX_B = 512
IN_W = 2560
SG_GROUPS = 4
SG_CHUNK = 128
POOL_WINDOWS = (2, 4, 8, 16)
POOL_DIM = 256
POOL_HALO = 8
N_EXPERTS = 16
N_GROUPS = 4
PER_GROUP = 4
D_FF = 512
N_MOD = 6
N_COND = 1 + N_LAT_B
EPS = 1e-6
ROPE_BASE = 10000.0

TM = 256
N_TILES = T // TM
N_CTX_TILES = T_CTX // TM
LAT_TILES_PER_SEQ = LAT_S // TM
ROW_CHUNK = 16
PAD_PER_RUN = ROW_CHUNK - 1
SORTED_ROWS = -(-(2 * TM + N_EXPERTS * PAD_PER_RUN) // 128) * 128
TE = 512
W_SLOTS = 3
ATTN_CTX_SEQS = 2
MOVE_SUBS = 4
TF = 512
TO = 512
TP = 512
POOL_RB = 128
CAP_TILES = -(-(T + N_TILES * PAD_PER_RUN) // TE)
EXPERT_CAP = CAP_TILES * TE
MOE_TILES = (2 * T + N_EXPERTS * N_TILES * PAD_PER_RUN) // TE + N_EXPERTS
MOD_TN = 1024
MIB = 1024 * 1024

NT_DIMS = (((1,), (1,)), ((), ()))


def _cparams(sem, vmem_mib):
    return pltpu.CompilerParams(dimension_semantics=sem, vmem_limit_bytes=vmem_mib * MIB)


def _hbm(x):
    try:
        return pltpu.with_memory_space_constraint(x, pltpu.HBM)
    except ValueError:
        return x


def _hbm_out(shape, dtype):
    return pltpu.HBM(shape, dtype)


def _sigmoid(x):
    return 1.0 / (1.0 + jnp.exp(-x))


def _gelu_tanh(x):
    c = math.sqrt(2.0 / math.pi)
    return x * (0.5 * (1.0 + jnp.tanh(c * (x + 0.044715 * (x * x * x)))))


def _modnorm(x, g, scale, shift):
    ms = jnp.mean(x * x, axis=-1, keepdims=True)
    return (x * lax.rsqrt(ms + EPS)) * (g * (1.0 + scale)) + shift


def _cond_of_tile(i):
    return jnp.where(i < N_CTX_TILES, 0, 1 + (i - N_CTX_TILES) // LAT_TILES_PER_SEQ)


def _mod_kernel(c_ref, w_ref, b_ref, o_ref):
    c = c_ref[...]
    s = (c * _sigmoid(c)).astype(BF16)
    o_ref[0] = jnp.dot(s, w_ref[0].astype(BF16), preferred_element_type=F32) + b_ref[0]


def _modulation(cond, w_ada, b_ada):
    cond8 = jnp.concatenate([cond, jnp.zeros((8 - N_COND, D), F32)], axis=0)
    out = pl.pallas_call(
        _mod_kernel,
        grid=(DEPTH, N_MOD * D // MOD_TN),
        in_specs=[
            pl.BlockSpec((8, D), lambda l, n: (0, 0)),
            pl.BlockSpec((1, D, MOD_TN), lambda l, n: (l, 0, n)),
            pl.BlockSpec((1, 1, MOD_TN), lambda l, n: (l, 0, n)),
        ],
        out_specs=pl.BlockSpec((1, 8, MOD_TN), lambda l, n: (l, 0, n)),
        out_shape=jax.ShapeDtypeStruct((DEPTH, 8, N_MOD * D), F32),
        compiler_params=_cparams(("arbitrary", "arbitrary"), 24),
        name="adaln_mod",
    )(cond8, w_ada, b_ada.reshape(DEPTH, 1, N_MOD * D))
    return out.reshape(DEPTH, 8, N_MOD, D)


def _front_kernel(latent, n_state_in, *refs):
    (x_ref, m_ref, n1_ref, w_ref, qg_ref, kg_ref, seg_ref, sgw_ref, sgb_ref), refs = refs[:9], refs[9:]
    if latent:
        cos_ref, sin_ref, q_o, k_o, v_o, sg_o, wbf = refs
    else:
        q_o, k_o, v_o, sg_o, sk_o, sv_o, wbf = refs[n_state_in:]

    @pl.when(pl.program_id(0) == 0)
    def _():
        for c0 in range(0, IN_W, 512):
            wbf[:, c0:c0 + 512] = w_ref[0, :, c0:c0 + 512].astype(BF16)

    h = _modnorm(x_ref[...], n1_ref[0], m_ref[0, 0, 1:2, :], m_ref[0, 0, 0:1, :])
    proj = jnp.dot(h.astype(BF16), wbf[...], preferred_element_type=F32)

    def qk_norm(z, g):
        ms = jnp.dot((z * z).astype(BF16), seg_ref[...], preferred_element_type=F32)
        return z * lax.rsqrt(ms + EPS) * g

    q = qk_norm(proj[:, 0:QK_W], qg_ref[...])
    k = qk_norm(proj[:, QK_W:2 * QK_W], kg_ref[...])
    v = proj[:, 2 * QK_W:2 * QK_W + MIX_A]

    if latent:
        lane = lax.broadcasted_iota(I32, (1, QK_W), 1)
        first = (lane % 32) < 16

        def rope(z):
            partner = jnp.where(first, pltpu.roll(z, QK_W - 16, 1), pltpu.roll(z, 16, 1))
            return z * cos_ref[...] + partner * sin_ref[...]

        q = rope(q)
        k = rope(k)
    else:
        for b in range(TF // CTX_S):
            sk_o[b, 0] = k[b * CTX_S:(b + 1) * CTX_S, :]
            sv_o[b, 0] = v[b * CTX_S:(b + 1) * CTX_S, :]

    q_o[...] = (q * (HEAD_DIM ** -0.5)).astype(BF16)
    k_o[...] = k.astype(BF16)
    v_o[...] = v.astype(BF16)

    gu0 = 2 * QK_W + MIX_A
    gv0 = gu0 + MIX_B
    for g in range(SG_GROUPS):
        wg = sgw_ref[0, g].astype(BF16)
        for n in range(TF // SG_CHUNK):
            rows = slice(n * SG_CHUNK, (n + 1) * SG_CHUNK)
            cu = slice(gu0 + g * 128, gu0 + (g + 1) * 128)
            cv = slice(gv0 + g * 128, gv0 + (g + 1) * 128)
            gv = _gelu_tanh(proj[rows, cv]).astype(BF16)
            mixed = jnp.dot(wg, gv, preferred_element_type=F32) + sgb_ref[0, g]
            sg_o[rows, g * 128:(g + 1) * 128] = (_gelu_tanh(proj[rows, cu]) * mixed).astype(BF16)


def _front(latent, l, x, mod, norm1_g, w_in, qg, kg, seg, sg_w, sgb, rope_tabs, states):
    e = l // 2
    rows = T_LAT if latent else T_CTX
    n_tiles = rows // TF
    off = T_CTX // TF if (latent and x.shape[0] == T) else 0
    tiles_per_seq = LAT_S // TF
    if latent:
        cond_map = lambda i: (l, 1 + i // tiles_per_seq, 0, 0)
    else:
        cond_map = lambda i: (l, 0, 0, 0)
    in_specs = [
        pl.BlockSpec((TF, D), lambda i: (i + off, 0)),
        pl.BlockSpec((1, 1, N_MOD, D), cond_map),
        pl.BlockSpec((1, 1, D), lambda i: (l, 0, 0)),
        pl.BlockSpec((1, D, IN_W), lambda i: (e, 0, 0)),
        pl.BlockSpec((1, QK_W), lambda i: (0, 0)),
        pl.BlockSpec((1, QK_W), lambda i: (0, 0)),
        pl.BlockSpec((QK_W, QK_W), lambda i: (0, 0)),
        pl.BlockSpec((1, SG_GROUPS, SG_CHUNK, SG_CHUNK), lambda i: (e, 0, 0, 0)),
        pl.BlockSpec((1, SG_GROUPS, SG_CHUNK, 128), lambda i: (e, 0, 0, 0)),
    ]
    args = [_hbm(x), mod, norm1_g.reshape(DEPTH, 1, D), w_in, qg, kg, seg, sg_w, sgb]
    tok = lambda: pl.BlockSpec((TF, QK_W), lambda i: (i, 0))
    out_specs = [tok(), tok(), tok(), tok()]
    out_shape = [_hbm_out((rows, QK_W), BF16)] * 4
    aliases = {}
    if latent:
        cos_t, sin_t = rope_tabs
        in_specs += [pl.BlockSpec((TF, QK_W), lambda i: (i % tiles_per_seq, 0))] * 2
        args += [cos_t, sin_t]
    else:
        if states:
            in_specs += [pl.BlockSpec(memory_space=pl.ANY)] * 2
            args += list(states)
            aliases = {9: 4, 10: 5}
        st = lambda: pl.BlockSpec((TF // CTX_S, 1, CTX_S, QK_W), lambda i: (i, e, 0, 0))
        out_specs += [st(), st()]
        out_shape += [_hbm_out((N_CTX_B, DEPTH // 2, CTX_S, QK_W), F32)] * 2
    return pl.pallas_call(
        functools.partial(_front_kernel, latent, len(aliases)),
        grid=(n_tiles,),
        in_specs=in_specs,
        out_specs=out_specs,
        out_shape=out_shape,
        scratch_shapes=[pltpu.VMEM((D, IN_W), BF16)],
        input_output_aliases=aliases,
        compiler_params=_cparams(("arbitrary",), 48),
        name="front_lat" if latent else "front_ctx",
    )(*args)


def _attn_kernel(has_cache, seqs, heads, lam0, *refs):
    if has_cache:
        (q_ref, k_ref, v_ref, ck_ref, cv_ref, lq1, lk1, lq2, lk2, sub_ref, o_ref) = refs
    else:
        (q_ref, k_ref, v_ref, lq1, lk1, lq2, lk2, sub_ref, o_ref) = refs

    lane = lax.broadcasted_iota(I32, (1, VAL_DIM), 1)
    lam = (jnp.exp(jnp.sum(lq1[0] * lk1[0], axis=-1, keepdims=True))
           - jnp.exp(jnp.sum(lq2[0] * lk2[0], axis=-1, keepdims=True)) + lam0)
    q_rows = q_ref.shape[0] // seqs
    k_rows = k_ref.shape[0] // seqs

    for sq, h in [(sq, h) for sq in range(seqs) for h in range(heads)]:
        hs = slice(h * VAL_DIM, (h + 1) * VAL_DIM)
        qr = slice(sq * q_rows, (sq + 1) * q_rows)
        kr = slice(sq * k_rows, (sq + 1) * k_rows)
        qf = q_ref[qr, hs].astype(F32)
        qc = [jnp.where(lane < HEAD_DIM, qf, 0.0).astype(BF16),
              jnp.where(lane >= HEAD_DIM, qf, 0.0).astype(BF16)]
        keys = [k_ref[kr, hs]]
        vals = [v_ref[kr, hs]]
        if has_cache:
            keys.append(ck_ref[0, 0].astype(BF16))
            vals.append(cv_ref[0, 0].astype(BF16))

        probs = []
        for c in range(2):
            s = [lax.dot_general(qc[c], kk, NT_DIMS, preferred_element_type=F32) for kk in keys]
            m = s[0].max(axis=-1, keepdims=True)
            for sx in s[1:]:
                m = jnp.maximum(m, sx.max(axis=-1, keepdims=True))
            p = [jnp.exp(sx - m) for sx in s]
            den = p[0].sum(axis=-1, keepdims=True)
            for px in p[1:]:
                den = den + px.sum(axis=-1, keepdims=True)
            probs.append((p, den))

        r0 = 1.0 / probs[0][1]
        r1 = lam / probs[1][1]
        o = None
        for idx in range(len(keys)):
            a = (probs[0][0][idx] * r0 - probs[1][0][idx] * r1).astype(BF16)
            t = jnp.dot(a, vals[idx], preferred_element_type=F32)
            o = t if o is None else o + t
        ms = jnp.mean(o * o, axis=-1, keepdims=True)
        o_ref[qr, hs] = ((o * lax.rsqrt(ms + EPS) * sub_ref[0]) * (1.0 - lam0)).astype(BF16)


def _attention(has_cache, l, q, k, v, cache_k, cache_v, lam_q1, lam_k1, lam_q2, lam_k2, subln_g):
    e = l // 2
    lam0 = 0.8 - 0.6 * math.exp(-0.3 * l)
    n_b, s_len = (N_LAT_B, LAT_S) if has_cache else (N_CTX_B, CTX_S)
    heads = 1 if has_cache else N_HEADS
    seqs = 1 if has_cache else ATTN_CTX_SEQS
    width = heads * VAL_DIM
    nq = s_len // TM
    kv = lambda: pl.BlockSpec((seqs * s_len, width), lambda b, h, qi: (b, h))
    par = lambda: pl.BlockSpec((1, 1, HEAD_DIM), lambda b, h, qi: (e, 0, 0))
    in_specs = [pl.BlockSpec((seqs * TM, width), lambda b, h, qi: (b * nq + qi, h)), kv(), kv()]
    args = [_hbm(q), _hbm(k), _hbm(v)]
    if has_cache:
        cs = lambda: pl.BlockSpec((1, 1, PAST, VAL_DIM), lambda b, h, qi: (b, e, 0, h))
        in_specs += [cs(), cs()]
        args += [cache_k.reshape(N_LAT_B, DEPTH // 2, PAST, QK_W),
                 cache_v.reshape(N_LAT_B, DEPTH // 2, PAST, MIX_A)]
    in_specs += [par(), par(), par(), par(),
                 pl.BlockSpec((1, 1, VAL_DIM), lambda b, h, qi: (e, 0, 0))]
    r3 = lambda a: a.reshape(DEPTH // 2, 1, -1)
    args += [r3(lam_q1), r3(lam_k1), r3(lam_q2), r3(lam_k2), r3(subln_g)]
    return pl.pallas_call(
        functools.partial(_attn_kernel, has_cache, seqs, heads, lam0),
        grid=(n_b // seqs, N_HEADS // heads, nq),
        in_specs=in_specs,
        out_specs=pl.BlockSpec((seqs * TM, width), lambda b, h, qi: (b * nq + qi, h)),
        out_shape=_hbm_out((n_b * s_len, MIX_A), BF16),
        compiler_params=_cparams(("arbitrary", "arbitrary", "arbitrary"), 32),
        name="attn_lat" if has_cache else "attn_ctx",
    )(*args)


def _route_tail(sub_tile, x_new, m_ref, n2_ref, rwt_ref, rbb_ref, tri_ref,
                h2_o, lp_o, rf_o, nch_o, loff_o, gb_o, cnt_o, carry):
    tok = slice(sub_tile * TM, (sub_tile + 1) * TM)
    if sub_tile == 0:
        @pl.when(pl.program_id(0) == 0)
        def _():
            carry[...] = jnp.zeros_like(carry)

    h2 = _modnorm(x_new, n2_ref[0], m_ref[0, 0, 4:5, :], m_ref[0, 0, 3:4, :])

    hh = h2.astype(BF16)
    h2_o[tok, :] = hh
    hl = (h2 - hh.astype(F32)).astype(BF16)
    rw = rwt_ref[...]
    wh = rw.astype(BF16)
    wl = (rw - wh.astype(F32)).astype(BF16)
    dg = lambda a, b: lax.dot_general(a, b, NT_DIMS, preferred_element_type=F32)
    logits = dg(wh, hh) + dg(wh, hl) + dg(wl, hh)
    score = _sigmoid(logits)
    sel = score + rbb_ref[...]

    neg_inf = jnp.full((1, TM), -jnp.inf, F32)
    grp = []
    for g in range(N_GROUPS):
        s_rows = [sel[PER_GROUP * g + j:PER_GROUP * g + j + 1, :] for j in range(PER_GROUP)]
        c_rows = [score[PER_GROUP * g + j:PER_GROUP * g + j + 1, :] for j in range(PER_GROUP)]
        f_idx = jnp.zeros((1, TM), I32)
        f_val, f_sc = s_rows[0], c_rows[0]
        for j in range(1, PER_GROUP):
            better = s_rows[j] > f_val
            f_idx = jnp.where(better, j, f_idx)
            f_val = jnp.where(better, s_rows[j], f_val)
            f_sc = jnp.where(better, c_rows[j], f_sc)
        s_idx = jnp.zeros((1, TM), I32)
        s_val, s_sc = neg_inf, jnp.zeros((1, TM), F32)
        for j in range(PER_GROUP):
            better = (f_idx != j) & (s_rows[j] > s_val)
            s_idx = jnp.where(better, j, s_idx)
            s_val = jnp.where(better, s_rows[j], s_val)
            s_sc = jnp.where(better, c_rows[j], s_sc)
        grp.append((f_val + s_val, PER_GROUP * g + f_idx, PER_GROUP * g + s_idx, f_sc, s_sc))

    best, e0, e1, c0, c1 = grp[0]
    for g in range(1, N_GROUPS):
        better = grp[g][0] > best
        best = jnp.where(better, grp[g][0], best)
        e0 = jnp.where(better, grp[g][1], e0)
        e1 = jnp.where(better, grp[g][2], e1)
        c0 = jnp.where(better, grp[g][3], c0)
        c1 = jnp.where(better, grp[g][4], c1)
    den = c0 + c1
    w0 = c0 / den
    w1 = c1 / den

    sub = lax.broadcasted_iota(I32, (N_EXPERTS, TM), 0)
    hit0 = sub == jnp.broadcast_to(e0, (N_EXPERTS, TM))
    hit1 = sub == jnp.broadcast_to(e1, (N_EXPERTS, TM))
    onehot = jnp.where(hit0 | hit1, 1.0, 0.0)
    prefix = jnp.dot(onehot.astype(BF16), tri_ref[...], preferred_element_type=F32)
    count = jnp.sum(onehot, axis=1, keepdims=True)
    padded = jnp.floor((count + (ROW_CHUNK - 1)) * (1.0 / ROW_CHUNK)) * ROW_CHUNK

    sub1 = lax.broadcasted_iota(I32, (N_EXPERTS, 1), 0)
    run = jnp.zeros((1, 1), F32)
    local_off = jnp.zeros((N_EXPERTS, 1), F32)
    for e in range(N_EXPERTS):
        local_off = jnp.where(sub1 == e, run, local_off)
        run = run + padded[e:e + 1, :]
    place = prefix + local_off
    lpos0 = jnp.sum(jnp.where(hit0, place, 0.0), axis=0, keepdims=True).astype(I32)
    lpos1 = jnp.sum(jnp.where(hit1, place, 0.0), axis=0, keepdims=True).astype(I32)

    lanes = lambda col: jnp.broadcast_to(col, (N_EXPERTS, 128)).astype(I32)
    nch_o[sub_tile] = lanes(padded * (1.0 / ROW_CHUNK))
    loff_o[sub_tile] = lanes(local_off)
    gb_o[sub_tile] = carry[...].astype(I32)
    carry[...] = carry[...] + padded
    cnt_o[...] = carry[...].astype(I32)

    sub8 = lax.broadcasted_iota(I32, (8, TM), 0)
    bc = lambda r: jnp.broadcast_to(r, (8, TM))
    lp_o[:, tok] = jnp.where(sub8 == 0, bc(lpos0), jnp.where(sub8 == 1, bc(lpos1), 0))
    rf_o[:, tok] = jnp.where(sub8 == 0, bc(w0), jnp.where(sub8 == 1, bc(w1), 0.0))


def _tail_specs(l, sub_tiles):
    rows = sub_tiles * TM
    in_specs = [
        pl.BlockSpec((1, 1, D), lambda i: (l, 0, 0)),
        pl.BlockSpec((N_EXPERTS, D), lambda i: (0, 0)),
        pl.BlockSpec((N_EXPERTS, TM), lambda i: (0, 0)),
        pl.BlockSpec((TM, TM), lambda i: (0, 0)),
    ]
    per_tile = lambda: pl.BlockSpec((sub_tiles, N_EXPERTS, 128), lambda i: (i, 0, 0))
    out_specs = [
        pl.BlockSpec((rows, D), lambda i: (i, 0)),
        pl.BlockSpec((8, rows), lambda i: (0, i)),
        pl.BlockSpec((8, rows), lambda i: (0, i)),
        per_tile(), per_tile(), per_tile(),
        pl.BlockSpec((N_EXPERTS, 128), lambda i: (0, 0)),
    ]
    out_shape = [
        _hbm_out((T, D), BF16),
        jax.ShapeDtypeStruct((8, T), I32),
        jax.ShapeDtypeStruct((8, T), F32),
        jax.ShapeDtypeStruct((N_TILES, N_EXPERTS, 128), I32),
        jax.ShapeDtypeStruct((N_TILES, N_EXPERTS, 128), I32),
        jax.ShapeDtypeStruct((N_TILES, N_EXPERTS, 128), I32),
        jax.ShapeDtypeStruct((N_EXPERTS, 128), I32),
    ]
    scratch = [pltpu.VMEM((N_EXPERTS, 128), F32)]
    return in_specs, out_specs, out_shape, scratch


def _out_kernel(split_x, *refs):
    x_refs, refs = refs[:1 + split_x], refs[1 + split_x:]
    (m_ref, ac_ref, al_ref, sc_ref, sl_ref, w_ref, n2_ref, rwt_ref, rbb_ref, tri_ref,
     x_o, *tail_refs) = refs
    *tail_outs, wbf, carry = tail_refs
    i = pl.program_id(0)

    @pl.when(i == 0)
    def _():
        wbf[...] = w_ref[0].astype(BF16)

    is_ctx = i < T_CTX // TO
    x = jnp.where(is_ctx, x_refs[0][...], x_refs[1][...]) if split_x else x_refs[0][...]
    att = jnp.where(is_ctx, ac_ref[...], al_ref[...])
    sgo = jnp.where(is_ctx, sc_ref[...], sl_ref[...])
    out = (jnp.dot(att, wbf[0:MIX_A, :], preferred_element_type=F32)
           + jnp.dot(sgo, wbf[MIX_A:MIX_A + MIX_B, :], preferred_element_type=F32))
    x_new = x + m_ref[0, 0, 2:3, :] * out
    x_o[...] = x_new
    for sub_tile in range(TO // TM):
        _route_tail(sub_tile, x_new[sub_tile * TM:(sub_tile + 1) * TM, :],
                    m_ref, n2_ref, rwt_ref, rbb_ref, tri_ref, *tail_outs, carry)


def _out_proj(l, x, mod, att_c, att_l, sg_c, sg_l, w_out, norm2_g, rwt, rbb, tri):
    e = l // 2
    t_in, t_out, t_shape, t_scratch = _tail_specs(l, TO // TM)
    n_ctx = T_CTX // TO
    ctx = lambda w: pl.BlockSpec((TO, w), lambda i: (jnp.minimum(i, n_ctx - 1), 0))
    lat = lambda w: pl.BlockSpec((TO, w), lambda i: (jnp.maximum(i - n_ctx, 0), 0))
    split_x = isinstance(x, tuple)
    if split_x:
        x_specs, x_args, aliases = [ctx(D), lat(D)], [_hbm(x[0]), _hbm(x[1])], {}
    else:
        x_specs, x_args, aliases = [pl.BlockSpec((TO, D), lambda i: (i, 0))], [_hbm(x)], {0: 0}
    in_specs = x_specs + [
        pl.BlockSpec((1, 1, N_MOD, D), lambda i: (l, _cond_of_tile(i * (TO // TM)), 0, 0)),
        ctx(MIX_A), lat(MIX_A), ctx(MIX_B), lat(MIX_B),
        pl.BlockSpec((1, D, D), lambda i: (e, 0, 0)),
    ] + t_in
    return pl.pallas_call(
        functools.partial(_out_kernel, split_x),
        grid=(T // TO,),
        in_specs=in_specs,
        out_specs=[pl.BlockSpec((TO, D), lambda i: (i, 0))] + t_out,
        out_shape=[_hbm_out((T, D), F32)] + t_shape,
        scratch_shapes=[pltpu.VMEM((D, D), BF16)] + t_scratch,
        input_output_aliases=aliases,
        compiler_params=_cparams(("arbitrary",), 40),
        name="out_proj_route",
    )(*x_args, mod, _hbm(att_c), _hbm(att_l), _hbm(sg_c), _hbm(sg_l), w_out,
      norm2_g.reshape(DEPTH, 1, D), rwt, rbb, tri)


def _pool_tables():
    t = np.arange(TP)[:, None]
    s = np.arange(TP + POOL_RB)[None, :]
    p = s - POOL_HALO
    in_cat = s < TP + 2 * POOL_HALO
    bands, sizes = [], []
    for kind in range(3):
        if kind == 0:
            same = (p >= 0) & (p < TP) & (p // CTX_S == t // CTX_S)
        elif kind == 1:
            same = (p >= 0) & (t >= 0)
        else:
            same = (p < TP) & (t >= 0)
        per_w, size_w = [], []
        for w in POOL_WINDOWS:
            band = (p >= t - w // 2) & (p < t - w // 2 + w) & same & in_cat
            size_w.append(band.sum(axis=1, keepdims=True))
            per_w.append(np.stack([band[rb * POOL_RB:(rb + 1) * POOL_RB, rb * POOL_RB:(rb + 2) * POOL_RB]
                                   for rb in range(TP // POOL_RB)]))
        bands.append(np.stack(per_w))
        sizes.append(np.stack(size_w))
    return jnp.asarray(np.stack(bands), dtype=BF16), jnp.asarray(np.stack(sizes), dtype=F32)


def _pool_kernel(x_ref, xp_ref, xn_ref, m_ref, n1_ref, band_ref, size_ref, pw_ref, ps_ref,
                 n2_ref, rwt_ref, rbb_ref, tri_ref,
                 x_o, *tail_refs):
    *tail_outs, carry = tail_refs
    g1n, sc1, sh1 = n1_ref[0], m_ref[0, 0, 1:2, :], m_ref[0, 0, 0:1, :]
    x = x_ref[...]
    h = _modnorm(x, g1n, sc1, sh1)
    hcat = jnp.concatenate([_modnorm(xp_ref[...], g1n, sc1, sh1), h, _modnorm(xn_ref[...], g1n, sc1, sh1),
                            jnp.zeros((POOL_RB - 2 * POOL_HALO, D), F32)], axis=0)
    hi = hcat.astype(BF16)
    lo = (hcat - hi.astype(F32)).astype(BF16)
    dot = lambda a, b: jnp.dot(a, b, preferred_element_type=F32)
    outs = []
    for g in range(len(POOL_WINDOWS)):
        cols = slice(g * POOL_DIM, (g + 1) * POOL_DIM)
        sums = []
        for rb in range(TP // POOL_RB):
            near = slice(rb * POOL_RB, (rb + 2) * POOL_RB)
            band = band_ref[0, g, rb]
            sums.append(dot(band, hi[near, cols]) + dot(band, lo[near, cols]))
        mean = jnp.concatenate(sums, axis=0) / size_ref[0, g]
        dlt = (mean - h[:, cols]).astype(BF16)
        outs.append(dot(dlt, pw_ref[0, g].astype(BF16)))
    out = jnp.concatenate(outs, axis=1) * ps_ref[0]
    x_new = x + m_ref[0, 0, 2:3, :] * out
    x_o[...] = x_new
    for sub_tile in range(TP // TM):
        _route_tail(sub_tile, x_new[sub_tile * TM:(sub_tile + 1) * TM, :],
                    m_ref, n2_ref, rwt_ref, rbb_ref, tri_ref, *tail_outs, carry)


def _pool_mixer(l, x, mod, norm1_g, pool_w, pool_scale, norm2_g, rwt, rbb, tri):
    o = l // 2
    t_in, t_out, t_shape, t_scratch = _tail_specs(l, TP // TM)
    bands, sizes = _pool_tables()
    halo_blocks = TP // POOL_HALO
    n_ctx = T_CTX // TP
    kind = lambda i: jnp.where(i < n_ctx, 0, 1 + (i - n_ctx) % (LAT_S // TP))
    n_w, n_rb = len(POOL_WINDOWS), TP // POOL_RB
    in_specs = [
        pl.BlockSpec((TP, D), lambda i: (i, 0)),
        pl.BlockSpec((POOL_HALO, D), lambda i: (jnp.maximum(i * halo_blocks - 1, 0), 0)),
        pl.BlockSpec((POOL_HALO, D), lambda i: (jnp.minimum((i + 1) * halo_blocks, T // POOL_HALO - 1), 0)),
        pl.BlockSpec((1, 1, N_MOD, D), lambda i: (l, _cond_of_tile(i * (TP // TM)), 0, 0)),
        pl.BlockSpec((1, 1, D), lambda i: (l, 0, 0)),
        pl.BlockSpec((1, n_w, n_rb, POOL_RB, 2 * POOL_RB), lambda i: (kind(i), 0, 0, 0, 0)),
        pl.BlockSpec((1, n_w, TP, 1), lambda i: (kind(i), 0, 0, 0)),
        pl.BlockSpec((1, n_w, POOL_DIM, POOL_DIM), lambda i: (o, 0, 0, 0)),
        pl.BlockSpec((1, 1, D), lambda i: (o, 0, 0)),
    ] + t_in
    return pl.pallas_call(
        _pool_kernel,
        grid=(T // TP,),
        in_specs=in_specs,
        out_specs=[pl.BlockSpec((TP, D), lambda i: (i, 0))] + t_out,
        out_shape=[_hbm_out((T, D), F32)] + t_shape,
        scratch_shapes=t_scratch,
        compiler_params=_cparams(("arbitrary",), 40),
        name="pool_route",
    )(_hbm(x), _hbm(x), _hbm(x), mod, norm1_g.reshape(DEPTH, 1, D), bands, sizes, pool_w,
      pool_scale.reshape(DEPTH // 2, 1, D),
      norm2_g.reshape(DEPTH, 1, D), rwt, rbb, tri)


def _start_run_copies(tile, nch_ref, loff_ref, gb_ref, make_copy):
    for e in range(N_EXPERTS):
        n = nch_ref[tile * N_EXPERTS + e]
        lo = loff_ref[tile * N_EXPERTS + e]
        gb = gb_ref[tile * N_EXPERTS + e] + e * EXPERT_CAP

        def issue(c, carry, lo=lo, gb=gb):
            make_copy(pl.multiple_of(lo + c * (2 * ROW_CHUNK), ROW_CHUNK),
                      pl.multiple_of(gb + c * (2 * ROW_CHUNK), ROW_CHUNK), 2 * ROW_CHUNK).start()
            return carry

        lax.fori_loop(0, lax.shift_right_logical(n, 1), issue, 0)

        @pl.when((n & 1) == 1)
        def _(n=n, lo=lo, gb=gb):
            make_copy(pl.multiple_of(lo + (n - 1) * ROW_CHUNK, ROW_CHUNK),
                      pl.multiple_of(gb + (n - 1) * ROW_CHUNK, ROW_CHUNK), ROW_CHUNK).start()


def _wait_run_copies(tile, nch_ref, make_copy):
    doubles, singles = 0, 0
    for e in range(N_EXPERTS):
        n = nch_ref[tile * N_EXPERTS + e]
        doubles = doubles + lax.shift_right_logical(n, 1)
        singles = singles + (n & 1)

    def drain(rows):
        def body(c, carry):
            make_copy(0, 0, rows).wait()
            return carry
        return body

    lax.fori_loop(0, doubles, drain(2 * ROW_CHUNK), 0)
    lax.fori_loop(0, singles, drain(ROW_CHUNK), 0)


def _dispatch_kernel(nch_ref, loff_ref, gb_ref, h_ref, lp_ref, xs_ref, sbuf, sems):
    i = pl.program_id(0)
    slot = i % 2
    r = lax.broadcasted_iota(I32, (SORTED_ROWS, TM), 0)
    for sub in range(MOVE_SUBS):
        tok = slice(sub * TM, (sub + 1) * TM)
        hit = (r == lp_ref[0:1, tok]) | (r == lp_ref[1:2, tok])
        perm = jnp.where(hit, 1.0, 0.0).astype(BF16)
        sbuf[slot, sub] = jnp.dot(perm, h_ref[tok, :], preferred_element_type=F32).astype(BF16)

    def copies_of(s, sub):
        def make_copy(local_row, global_row, rows):
            return pltpu.make_async_copy(sbuf.at[s, sub, pl.ds(local_row, rows)],
                                         xs_ref.at[pl.ds(global_row, rows)], sems.at[s])
        return make_copy

    @pl.when(i > 0)
    def _():
        for sub in range(MOVE_SUBS):
            _wait_run_copies((i - 1) * MOVE_SUBS + sub, nch_ref, copies_of(1 - slot, sub))

    for sub in range(MOVE_SUBS):
        _start_run_copies(i * MOVE_SUBS + sub, nch_ref, loff_ref, gb_ref, copies_of(slot, sub))

    @pl.when(i == N_TILES // MOVE_SUBS - 1)
    def _():
        for sub in range(MOVE_SUBS):
            _wait_run_copies(i * MOVE_SUBS + sub, nch_ref, copies_of(slot, sub))


def _dispatch(layout, h2, lp):
    rows = MOVE_SUBS * TM
    return pl.pallas_call(
        _dispatch_kernel,
        grid_spec=pltpu.PrefetchScalarGridSpec(
            num_scalar_prefetch=3,
            grid=(N_TILES // MOVE_SUBS,),
            in_specs=[pl.BlockSpec((rows, D), lambda i, *_: (i, 0)),
                      pl.BlockSpec((8, rows), lambda i, *_: (0, i))],
            out_specs=pl.BlockSpec(memory_space=pl.ANY),
            scratch_shapes=[pltpu.VMEM((2, MOVE_SUBS, SORTED_ROWS, D), BF16),
                            pltpu.SemaphoreType.DMA((2,))],
        ),
        out_shape=_hbm_out((N_EXPERTS * EXPERT_CAP, D), BF16),
        compiler_params=_cparams(("arbitrary",), 32),
        name="moe_dispatch",
    )(*layout, _hbm(h2), lp)


def _moe_kernel(l, rb_ref, e_ref, nv_ref, first_ref, ord_ref, nxt_ref, nxt2_ref,
                xs_ref, wg_hbm, wu_hbm, wd_hbm, y_ref, wgf, wuf, wdf, wgb, wub, wdb, sems):
    i = pl.program_id(0)
    n_valid = nv_ref[i]

    def weight_copies(e, slot):
        return (pltpu.make_async_copy(wg_hbm.at[l, e], wgf.at[slot], sems.at[slot, 0]),
                pltpu.make_async_copy(wu_hbm.at[l, e], wuf.at[slot], sems.at[slot, 1]),
                pltpu.make_async_copy(wd_hbm.at[l, e], wdf.at[slot], sems.at[slot, 2]))

    @pl.when(first_ref[i] == 1)
    def _():
        k = ord_ref[i]
        slot = k % W_SLOTS

        @pl.when(k == 0)
        def _():
            for cp in weight_copies(e_ref[i], 0):
                cp.start(priority=1)

            @pl.when(nxt_ref[i] >= 0)
            def _():
                for cp in weight_copies(nxt_ref[i], 1):
                    cp.start(priority=1)

        for cp in weight_copies(e_ref[i], slot):
            cp.wait()

        @pl.when(nxt2_ref[i] >= 0)
        def _():
            for cp in weight_copies(nxt2_ref[i], (k + 2) % W_SLOTS):
                cp.start(priority=1)

        wgb[...] = wgf[slot].astype(BF16)
        wub[...] = wuf[slot].astype(BF16)
        wdb[...] = wdf[slot].astype(BF16)

    @pl.when(n_valid > 0)
    def _():
        row = lax.broadcasted_iota(I32, (TE, 1), 0)
        x = jnp.where(row < n_valid, xs_ref[...], jnp.zeros((), BF16))
        gate = jnp.dot(x, wgb[...], preferred_element_type=F32)
        up = jnp.dot(x, wub[...], preferred_element_type=F32)
        act = (gate * _sigmoid(gate) * up).astype(BF16)
        y_ref[...] = jnp.dot(act, wdb[...], preferred_element_type=F32).astype(BF16)


def _experts(l, tile_map, xs, w_gate, w_up, w_down):
    n_pre = len(tile_map)
    blk = lambda i, rb, *_: (rb[i], 0)
    return pl.pallas_call(
        functools.partial(_moe_kernel, l),
        grid_spec=pltpu.PrefetchScalarGridSpec(
            num_scalar_prefetch=n_pre,
            grid=(MOE_TILES,),
            in_specs=[
                pl.BlockSpec((TE, D), blk),
                pl.BlockSpec(memory_space=pl.ANY),
                pl.BlockSpec(memory_space=pl.ANY),
                pl.BlockSpec(memory_space=pl.ANY),
            ],
            out_specs=pl.BlockSpec((TE, D), blk),
            scratch_shapes=[pltpu.VMEM((W_SLOTS, D, D_FF), F32), pltpu.VMEM((W_SLOTS, D, D_FF), F32),
                            pltpu.VMEM((W_SLOTS, D_FF, D), F32),
                            pltpu.VMEM((D, D_FF), BF16), pltpu.VMEM((D, D_FF), BF16),
                            pltpu.VMEM((D_FF, D), BF16),
                            pltpu.SemaphoreType.DMA((W_SLOTS, 3))],
        ),
        out_shape=_hbm_out((N_EXPERTS * EXPERT_CAP, D), BF16),
        compiler_params=_cparams(("arbitrary",), 48),
        name="moe_experts",
    )(*tile_map, _hbm(xs), w_gate, w_up, w_down)


def _combine_kernel(split_out, nch_ref, loff_ref, gb_ref, x_ref, m_ref, lpt_ref, w_ref, y_ref, *refs):
    *o_refs, ybuf, sems = refs
    i = pl.program_id(0)
    slot = i % 2

    def copies_of(s, sub):
        def make_copy(local_row, global_row, rows):
            return pltpu.make_async_copy(y_ref.at[pl.ds(global_row, rows)],
                                         ybuf.at[s, sub, pl.ds(local_row, rows)], sems.at[s])
        return make_copy

    @pl.when(i == 0)
    def _():
        ybuf[...] = jnp.zeros_like(ybuf)
        for sub in range(MOVE_SUBS):
            _start_run_copies(sub, nch_ref, loff_ref, gb_ref, copies_of(slot, sub))

    @pl.when(i + 1 < N_TILES // MOVE_SUBS)
    def _():
        for sub in range(MOVE_SUBS):
            _start_run_copies((i + 1) * MOVE_SUBS + sub, nch_ref, loff_ref, gb_ref, copies_of(1 - slot, sub))

    for sub in range(MOVE_SUBS):
        _wait_run_copies(i * MOVE_SUBS + sub, nch_ref, copies_of(slot, sub))

    col = lax.broadcasted_iota(I32, (TM, SORTED_ROWS), 1)
    parts = []
    for sub in range(MOVE_SUBS):
        tok = slice(sub * TM, (sub + 1) * TM)
        pick = jnp.where(col == lpt_ref[tok, 0:1], w_ref[tok, 0:1],
                         jnp.where(col == lpt_ref[tok, 1:2], w_ref[tok, 1:2], 0.0)).astype(BF16)
        parts.append(jnp.dot(pick, ybuf[slot, sub], preferred_element_type=F32))
    res = x_ref[...] + m_ref[0, 0, 5:6, :] * jnp.concatenate(parts, axis=0)
    if split_out:
        @pl.when(i < T_CTX // (MOVE_SUBS * TM))
        def _():
            o_refs[0][...] = res

        @pl.when(i >= T_CTX // (MOVE_SUBS * TM))
        def _():
            o_refs[1][...] = res
    else:
        o_refs[0][...] = res


def _combine(l, layout, x, mod, lpt, wcol, y, split_out):
    rows = MOVE_SUBS * TM
    n_ctx = T_CTX // rows
    if split_out:
        out_specs = [pl.BlockSpec((rows, D), lambda i, *_: (jnp.minimum(i, n_ctx - 1), 0)),
                     pl.BlockSpec((rows, D), lambda i, *_: (jnp.maximum(i - n_ctx, 0), 0))]
        out_shape = [jax.ShapeDtypeStruct((T_CTX, D), F32), jax.ShapeDtypeStruct((T_LAT, D), F32)]
        aliases = {}
    else:
        out_specs = pl.BlockSpec((rows, D), lambda i, *_: (i, 0))
        out_shape = _hbm_out((T, D), F32)
        aliases = {3: 0}
    return pl.pallas_call(
        functools.partial(_combine_kernel, split_out),
        grid_spec=pltpu.PrefetchScalarGridSpec(
            num_scalar_prefetch=3,
            grid=(N_TILES // MOVE_SUBS,),
            in_specs=[
                pl.BlockSpec((rows, D), lambda i, *_: (i, 0)),
                pl.BlockSpec((1, 1, N_MOD, D), lambda i, *_: (l, _cond_of_tile(i * MOVE_SUBS), 0, 0)),
                pl.BlockSpec((rows, 8), lambda i, *_: (i, 0)),
                pl.BlockSpec((rows, 8), lambda i, *_: (i, 0)),
                pl.BlockSpec(memory_space=pl.ANY),
            ],
            out_specs=out_specs,
            scratch_shapes=[pltpu.VMEM((2, MOVE_SUBS, SORTED_ROWS, D), BF16),
                            pltpu.SemaphoreType.DMA((2,))],
        ),
        out_shape=out_shape,
        input_output_aliases=aliases,
        compiler_params=_cparams(("arbitrary",), 48),
        name="moe_combine",
    )(*layout, _hbm(x), mod, lpt, wcol, _hbm(y))


def _tile_map(cnt):
    cnt = cnt[:, 0]
    ids = jnp.arange(N_EXPERTS, dtype=I32)
    n_tiles = (cnt + TE - 1) // TE
    cum = jnp.cumsum(n_tiles)
    total = cum[-1]
    i = jnp.arange(MOE_TILES, dtype=I32)
    ii = jnp.minimum(i, total - 1)
    e = jnp.sum((ii[:, None] >= cum[None, :]).astype(I32), axis=1)
    onehot = (e[:, None] == ids[None, :]).astype(I32)
    pick = lambda v: jnp.sum(onehot * v[None, :], axis=1)
    j = ii - pick(cum - n_tiles)
    active = i < total
    n_valid = jnp.where(active, jnp.clip(pick(cnt) - j * TE, 0, TE), 0)
    first = (active & (j == 0)).astype(I32)
    nonempty = n_tiles > 0
    ordinal = jnp.cumsum(nonempty.astype(I32)) - 1
    later = jnp.where(nonempty[None, :] & (ids[None, :] > ids[:, None]), ids[None, :], N_EXPERTS)
    nxt = jnp.min(later, axis=1)
    nxt_of = lambda v: jnp.sum((v[:, None] == ids[None, :]).astype(I32) * nxt[None, :], axis=1)
    nxt2 = jnp.where(nxt == N_EXPERTS, N_EXPERTS, nxt_of(nxt))
    none = lambda v: jnp.where(v == N_EXPERTS, -1, v)
    as_i32 = lambda v: v.astype(I32)
    return tuple(map(as_i32, (e * CAP_TILES + j, e, n_valid, first, pick(ordinal),
                              pick(none(nxt)), pick(none(nxt2)))))


def _moe(l, x, tail, mod, w_gate, w_up, w_down, split_out):
    h2, lp, rf, nch, loff, gb, cnt = tail
    layout = tuple(a[:, :, 0].reshape(N_TILES * N_EXPERTS) for a in (nch, loff, gb))
    xs = _dispatch(layout, h2, lp)
    y = _experts(l, _tile_map(cnt), xs, w_gate, w_up, w_down)
    return _combine(l, layout, x, mod, lp.T, rf.T, y, split_out)


def _rope_tables():
    t = jnp.arange(LAT_S)
    row = (t // GRID_W).astype(F32)
    col = (t % GRID_W).astype(F32)
    nf = HEAD_DIM // 4
    inv = ROPE_BASE ** (-jnp.arange(nf, dtype=F32) / nf)
    ar = row[:, None] * inv[None, :]
    ac = col[:, None] * inv[None, :]
    cos = jnp.concatenate([jnp.cos(ar), jnp.cos(ar), jnp.cos(ac), jnp.cos(ac)], axis=1)
    sin = jnp.concatenate([-jnp.sin(ar), jnp.sin(ar), -jnp.sin(ac), jnp.sin(ac)], axis=1)
    reps = QK_W // HEAD_DIM
    return jnp.tile(cos, (1, reps)), jnp.tile(sin, (1, reps))


def kernel(x_prompt, x_sample, cache_k, cache_v, c, c_ctx, norm1_g, norm2_g, w_ada, b_ada, w_in, w_out,
           q_norm_g, k_norm_g, lam_q1, lam_k1, lam_q2, lam_k2, subln_g, sg_w, sg_b, pool_w, pool_scale,
           router_w, router_b, w_gate, w_up, w_down):
    x = (x_prompt.reshape(T_CTX, D), x_sample.reshape(T_LAT, D))
    cond = jnp.concatenate([c_ctx[None, :], c], axis=0)
    mod = _modulation(cond, w_ada, b_ada)

    seg_id = np.arange(QK_W) // HEAD_DIM
    seg = jnp.asarray((seg_id[:, None] == seg_id[None, :]) / HEAD_DIM, dtype=BF16)
    tri = jnp.asarray(np.arange(TM)[:, None] < np.arange(TM)[None, :], dtype=BF16)
    rope_tabs = _rope_tables()
    rwt = router_w.T
    rbb = jnp.broadcast_to(router_b[:, None], (N_EXPERTS, TM))
    states = None

    for l in range(DEPTH):
        if l % 2 == 0:
            e = l // 2
            qg = jnp.tile(q_norm_g[e], QK_W // HEAD_DIM)[None, :]
            kg = jnp.tile(k_norm_g[e], QK_W // HEAD_DIM)[None, :]
            sgb = jnp.broadcast_to(sg_b[:, :, :, None], (DEPTH // 2, SG_GROUPS, SG_CHUNK, 128))
            x_c, x_l = x if isinstance(x, tuple) else (x, x)
            common = (mod, norm1_g, w_in, qg, kg, seg, sg_w, sgb)
            q_c, k_c, v_c, sg_c, *states = _front(False, l, x_c, *common, None, states)
            q_l, k_l, v_l, sg_l = _front(True, l, x_l, *common, rope_tabs, None)
            lam = (lam_q1, lam_k1, lam_q2, lam_k2, subln_g)
            att_c = _attention(False, l, q_c, k_c, v_c, None, None, *lam)
            att_l = _attention(True, l, q_l, k_l, v_l, cache_k, cache_v, *lam)
            x, *tail = _out_proj(l, x, mod, att_c, att_l, sg_c, sg_l, w_out, norm2_g, rwt, rbb, tri)
        else:
            x, *tail = _pool_mixer(l, x, mod, norm1_g, pool_w, pool_scale, norm2_g, rwt, rbb, tri)
        x = _moe(l, x, tail, mod, w_gate, w_up, w_down, split_out=(l == DEPTH - 1))
    state_k, state_v = states

    y_prompt = x[0].reshape(N_CTX_B, CTX_S, D)
    y_sample = x[1].reshape(N_LAT_B, LAT_S, D)
    state_k = state_k.reshape(N_CTX_B, DEPTH // 2, CTX_S, N_HEADS, 2, HEAD_DIM)
    state_v = state_v.reshape(N_CTX_B, DEPTH // 2, CTX_S, N_HEADS, VAL_DIM)
    return y_prompt, y_sample, state_k, state_v
```

```python
import functools
import math

import jax
import jax.numpy as jnp
import numpy as np
from jax import lax
from jax.experimental import pallas as pl
from jax.experimental.pallas import tpu as pltpu

F32 = jnp.float32
BF16 = jnp.bfloat16
I32 = jnp.int32

D = 1024
N_CTX_B, CTX_S = 16, 256
N_LAT_B, LAT_S = 2, 1024
PAST = 512
DEPTH = 4
T_CTX = N_CTX_B * CTX_S
T_LAT = N_LAT_B * LAT_S
T = T_CTX + T_LAT
GRID_W = 64
N_HEADS = 4
HEAD_DIM = 64
VAL_DIM = 128
QK_W = 512
MIX_A = 512
MIX_B = 512
IN_W = 2560
SG_GROUPS = 4
SG_CHUNK = 128
POOL_WINDOWS = (2, 4, 8, 16)
POOL_DIM = 256
POOL_HALO = 8
N_EXPERTS = 16
N_GROUPS = 4
PER_GROUP = 4
D_FF = 512
N_MOD = 6
N_COND = 1 + N_LAT_B
EPS = 1e-6
ROPE_BASE = 10000.0

TM = 256
N_TILES = T // TM
N_CTX_TILES = T_CTX // TM
LAT_TILES_PER_SEQ = LAT_S // TM
ROW_CHUNK = 16
PAD_PER_RUN = ROW_CHUNK - 1
SORTED_ROWS = -(-(2 * TM + N_EXPERTS * PAD_PER_RUN) // 128) * 128
TE = 512
W_SLOTS = 3
ATTN_CTX_SEQS = 2
ATTN_LAT_HEADS = 2
MOVE_SUBS = 2
TF = 512
TO = 512
TP = LAT_S
POOL_RB = 128
CAP_TILES = -(-(T + N_TILES * PAD_PER_RUN) // TE)
EXPERT_CAP = CAP_TILES * TE
MOE_TILES = (2 * T + N_EXPERTS * N_TILES * PAD_PER_RUN) // TE + N_EXPERTS
MOD_TN = 2048
MIB = 1024 * 1024

NT_DIMS = (((1,), (1,)), ((), ()))


def _cparams(sem, vmem_mib):
    return pltpu.CompilerParams(dimension_semantics=sem, vmem_limit_bytes=vmem_mib * MIB)


def _hbm(x):
    try:
        return pltpu.with_memory_space_constraint(x, pltpu.HBM)
    except ValueError:
        return x


def _hbm_out(shape, dtype):
    return pltpu.HBM(shape, dtype)


def _sigmoid(x):
    return 1.0 / (1.0 + jnp.exp(-x))


def _gelu_tanh(x):
    c = math.sqrt(2.0 / math.pi)
    return x * (0.5 * (1.0 + jnp.tanh(c * (x + 0.044715 * (x * x * x)))))


def _modnorm(x, g, scale, shift):
    ms = jnp.mean(x * x, axis=-1, keepdims=True)
    return (x * lax.rsqrt(ms + EPS)) * (g * (1.0 + scale)) + shift


def _cond_of_tile(i):
    return jnp.where(i < N_CTX_TILES, 0, 1 + (i - N_CTX_TILES) // LAT_TILES_PER_SEQ)


def _mod_kernel(c_ref, w_ref, b_ref, o_ref):
    c = c_ref[...]
    s = (c * _sigmoid(c)).astype(BF16)
    o_ref[0] = jnp.dot(s, w_ref[0].astype(BF16), preferred_element_type=F32) + b_ref[0]


def _modulation(cond, w_ada, b_ada):
    cond8 = jnp.concatenate([cond, jnp.zeros((8 - N_COND, D), F32)], axis=0)
    out = pl.pallas_call(
        _mod_kernel,
        grid=(DEPTH, N_MOD * D // MOD_TN),
        in_specs=[
            pl.BlockSpec((8, D), lambda l, n: (0, 0)),
            pl.BlockSpec((1, D, MOD_TN), lambda l, n: (l, 0, n)),
            pl.BlockSpec((1, 1, MOD_TN), lambda l, n: (l, 0, n)),
        ],
        out_specs=pl.BlockSpec((1, 8, MOD_TN), lambda l, n: (l, 0, n)),
        out_shape=jax.ShapeDtypeStruct((DEPTH, 8, N_MOD * D), F32),
        compiler_params=_cparams(("arbitrary", "arbitrary"), 40),
        name="adaln_mod",
    )(cond8, w_ada, b_ada.reshape(DEPTH, 1, N_MOD * D))
    return out.reshape(DEPTH, 8, N_MOD, D)


def _front_kernel(latent, n_state_in, *refs):
    (x_ref, m_ref, n1_ref, w_ref, qg_ref, kg_ref, seg_ref, sgw_ref, sgb_ref), refs = refs[:9], refs[9:]
    if latent:
        cos_ref, sin_ref, q_o, k_o, v_o, sg_o, wbf = refs
    else:
        q_o, k_o, v_o, sg_o, sk_o, sv_o, wbf = refs[n_state_in:]

    @pl.when(pl.program_id(0) == 0)
    def _():
        for c0 in range(0, IN_W, 512):
            wbf[:, c0:c0 + 512] = w_ref[0, :, c0:c0 + 512].astype(BF16)

    h = _modnorm(x_ref[...], n1_ref[0], m_ref[0, 0, 1:2, :], m_ref[0, 0, 0:1, :])
    proj = jnp.dot(h.astype(BF16), wbf[...], preferred_element_type=F32)

    def qk_norm(z, g):
        ms = jnp.dot((z * z).astype(BF16), seg_ref[...], preferred_element_type=F32)
        return z * lax.rsqrt(ms + EPS) * g

    q = qk_norm(proj[:, 0:QK_W], qg_ref[...])
    k = qk_norm(proj[:, QK_W:2 * QK_W], kg_ref[...])
    v = proj[:, 2 * QK_W:2 * QK_W + MIX_A]

    if latent:
        lane = lax.broadcasted_iota(I32, (1, QK_W), 1)
        first = (lane % 32) < 16

        def rope(z):
            partner = jnp.where(first, pltpu.roll(z, QK_W - 16, 1), pltpu.roll(z, 16, 1))
            return z * cos_ref[...] + partner * sin_ref[...]

        q = rope(q)
        k = rope(k)
    else:
        for b in range(TF // CTX_S):
            sk_o[b, 0] = k[b * CTX_S:(b + 1) * CTX_S, :]
            sv_o[b, 0] = v[b * CTX_S:(b + 1) * CTX_S, :]

    q_o[...] = (q * (HEAD_DIM ** -0.5)).astype(BF16)
    k_o[...] = k.astype(BF16)
    v_o[...] = v.astype(BF16)

    gu0 = 2 * QK_W + MIX_A
    gv0 = gu0 + MIX_B
    for g in range(SG_GROUPS):
        wg = sgw_ref[0, g].astype(BF16)
        for n in range(TF // SG_CHUNK):
            rows = slice(n * SG_CHUNK, (n + 1) * SG_CHUNK)
            cu = slice(gu0 + g * 128, gu0 + (g + 1) * 128)
            cv = slice(gv0 + g * 128, gv0 + (g + 1) * 128)
            gv = _gelu_tanh(proj[rows, cv]).astype(BF16)
            mixed = jnp.dot(wg, gv, preferred_element_type=F32) + sgb_ref[0, g]
            sg_o[rows, g * 128:(g + 1) * 128] = (_gelu_tanh(proj[rows, cu]) * mixed).astype(BF16)


def _front(latent, l, x, mod, norm1_g, w_in, qg, kg, seg, sg_w, sgb, rope_tabs, states):
    e = l // 2
    rows = T_LAT if latent else T_CTX
    n_tiles = rows // TF
    off = T_CTX // TF if (latent and x.shape[0] == T) else 0
    tiles_per_seq = LAT_S // TF
    if latent:
        cond_map = lambda i: (l, 1 + i // tiles_per_seq, 0, 0)
    else:
        cond_map = lambda i: (l, 0, 0, 0)
    in_specs = [
        pl.BlockSpec((TF, D), lambda i: (i + off, 0)),
        pl.BlockSpec((1, 1, N_MOD, D), cond_map),
        pl.BlockSpec((1, 1, D), lambda i: (l, 0, 0)),
        pl.BlockSpec((1, D, IN_W), lambda i: (e, 0, 0)),
        pl.BlockSpec((1, QK_W), lambda i: (0, 0)),
        pl.BlockSpec((1, QK_W), lambda i: (0, 0)),
        pl.BlockSpec((QK_W, QK_W), lambda i: (0, 0)),
        pl.BlockSpec((1, SG_GROUPS, SG_CHUNK, SG_CHUNK), lambda i: (e, 0, 0, 0)),
        pl.BlockSpec((1, SG_GROUPS, SG_CHUNK, 128), lambda i: (e, 0, 0, 0)),
    ]
    args = [_hbm(x), mod, norm1_g.reshape(DEPTH, 1, D), w_in, qg, kg, seg, sg_w, sgb]
    tok = lambda: pl.BlockSpec((TF, QK_W), lambda i: (i, 0))
    out_specs = [tok(), tok(), tok(), tok()]
    out_shape = [_hbm_out((rows, QK_W), BF16)] * 4
    aliases = {}
    if latent:
        cos_t, sin_t = rope_tabs
        in_specs += [pl.BlockSpec((TF, QK_W), lambda i: (i % tiles_per_seq, 0))] * 2
        args += [cos_t, sin_t]
    else:
        if states:
            in_specs += [pl.BlockSpec(memory_space=pl.ANY)] * 2
            args += list(states)
            aliases = {9: 4, 10: 5}
        st = lambda: pl.BlockSpec((TF // CTX_S, 1, CTX_S, QK_W), lambda i: (i, e, 0, 0))
        out_specs += [st(), st()]
        out_shape += [_hbm_out((N_CTX_B, DEPTH // 2, CTX_S, QK_W), F32)] * 2
    return pl.pallas_call(
        functools.partial(_front_kernel, latent, len(aliases)),
        grid=(n_tiles,),
        in_specs=in_specs,
        out_specs=out_specs,
        out_shape=out_shape,
        scratch_shapes=[pltpu.VMEM((D, IN_W), BF16)],
        input_output_aliases=aliases,
        compiler_params=_cparams(("arbitrary",), 48),
        name="front_lat" if latent else "front_ctx",
    )(*args)


def _attn_kernel(has_cache, seqs, heads, lam0, *refs):
    if has_cache:
        (q_ref, k_ref, v_ref, ck_ref, cv_ref, lq1, lk1, lq2, lk2, sub_ref, o_ref) = refs
    else:
        (q_ref, k_ref, v_ref, lq1, lk1, lq2, lk2, sub_ref, o_ref) = refs

    lane = lax.broadcasted_iota(I32, (1, VAL_DIM), 1)
    lam = (jnp.exp(jnp.sum(lq1[0] * lk1[0], axis=-1, keepdims=True))
           - jnp.exp(jnp.sum(lq2[0] * lk2[0], axis=-1, keepdims=True)) + lam0)
    q_rows = q_ref.shape[0] // seqs
    k_rows = k_ref.shape[0] // seqs

    for sq, h in [(sq, h) for sq in range(seqs) for h in range(heads)]:
        hs = slice(h * VAL_DIM, (h + 1) * VAL_DIM)
        qr = slice(sq * q_rows, (sq + 1) * q_rows)
        kr = slice(sq * k_rows, (sq + 1) * k_rows)
        qf = q_ref[qr, hs].astype(F32)
        qc = [jnp.where(lane < HEAD_DIM, qf, 0.0).astype(BF16),
              jnp.where(lane >= HEAD_DIM, qf, 0.0).astype(BF16)]
        keys = [k_ref[kr, hs]]
        vals = [v_ref[kr, hs]]
        if has_cache:
            keys.append(ck_ref[0, 0, :, hs].astype(BF16))
            vals.append(cv_ref[0, 0, :, hs].astype(BF16))

        probs = []
        for c in range(2):
            s = [lax.dot_general(qc[c], kk, NT_DIMS, preferred_element_type=F32) for kk in keys]
            m = s[0].max(axis=-1, keepdims=True)
            for sx in s[1:]:
                m = jnp.maximum(m, sx.max(axis=-1, keepdims=True))
            p = [jnp.exp(sx - m) for sx in s]
            den = p[0].sum(axis=-1, keepdims=True)
            for px in p[1:]:
                den = den + px.sum(axis=-1, keepdims=True)
            probs.append((p, den))

        r0 = 1.0 / probs[0][1]
        r1 = lam / probs[1][1]
        o = None
        for idx in range(len(keys)):
            a = (probs[0][0][idx] * r0 - probs[1][0][idx] * r1).astype(BF16)
            t = jnp.dot(a, vals[idx], preferred_element_type=F32)
            o = t if o is None else o + t
        ms = jnp.mean(o * o, axis=-1, keepdims=True)
        o_ref[qr, hs] = ((o * lax.rsqrt(ms + EPS) * sub_ref[0]) * (1.0 - lam0)).astype(BF16)


def _attention(has_cache, l, q, k, v, cache_k, cache_v, lam_q1, lam_k1, lam_q2, lam_k2, subln_g):
    e = l // 2
    lam0 = 0.8 - 0.6 * math.exp(-0.3 * l)
    n_b, s_len = (N_LAT_B, LAT_S) if has_cache else (N_CTX_B, CTX_S)
    heads = ATTN_LAT_HEADS if has_cache else N_HEADS
    seqs = 1 if has_cache else ATTN_CTX_SEQS
    width = heads * VAL_DIM
    nq = s_len // TM
    kv = lambda: pl.BlockSpec((seqs * s_len, width), lambda b, h, qi: (b, h))
    par = lambda: pl.BlockSpec((1, 1, HEAD_DIM), lambda b, h, qi: (e, 0, 0))
    in_specs = [pl.BlockSpec((seqs * TM, width), lambda b, h, qi: (b * nq + qi, h)), kv(), kv()]
    args = [_hbm(q), _hbm(k), _hbm(v)]
    if has_cache:
        cs = lambda: pl.BlockSpec((1, 1, PAST, width), lambda b, h, qi: (b, e, 0, h))
        in_specs += [cs(), cs()]
        args += [cache_k.reshape(N_LAT_B, DEPTH // 2, PAST, QK_W),
                 cache_v.reshape(N_LAT_B, DEPTH // 2, PAST, MIX_A)]
    in_specs += [par(), par(), par(), par(),
                 pl.BlockSpec((1, 1, VAL_DIM), lambda b, h, qi: (e, 0, 0))]
    r3 = lambda a: a.reshape(DEPTH // 2, 1, -1)
    args += [r3(lam_q1), r3(lam_k1), r3(lam_q2), r3(lam_k2), r3(subln_g)]
    return pl.pallas_call(
        functools.partial(_attn_kernel, has_cache, seqs, heads, lam0),
        grid=(n_b // seqs, N_HEADS // heads, nq),
        in_specs=in_specs,
        out_specs=pl.BlockSpec((seqs * TM, width), lambda b, h, qi: (b * nq + qi, h)),
        out_shape=_hbm_out((n_b * s_len, MIX_A), BF16),
        compiler_params=_cparams(("arbitrary", "arbitrary", "arbitrary"), 32),
        name="attn_lat" if has_cache else "attn_ctx",
    )(*args)


def _route_tail(sub_tile, x_new, m_ref, n2_ref, rwt_ref, rbb_ref, tri_ref,
                h2_o, lp_o, rf_o, nch_o, loff_o, gb_o, cnt_o, carry):
    tok = slice(sub_tile * TM, (sub_tile + 1) * TM)
    if sub_tile == 0:
        @pl.when(pl.program_id(0) == 0)
        def _():
            carry[...] = jnp.zeros_like(carry)

    h2 = _modnorm(x_new, n2_ref[0], m_ref[0, 0, 4:5, :], m_ref[0, 0, 3:4, :])

    hh = h2.astype(BF16)
    h2_o[tok, :] = hh
    hl = (h2 - hh.astype(F32)).astype(BF16)
    rw = rwt_ref[...]
    wh = rw.astype(BF16)
    wl = (rw - wh.astype(F32)).astype(BF16)
    dg = lambda a, b: lax.dot_general(a, b, NT_DIMS, preferred_element_type=F32)
    logits = dg(wh, hh) + dg(wh, hl) + dg(wl, hh)
    score = _sigmoid(logits)
    sel = score + rbb_ref[...]

    neg_inf = jnp.full((1, TM), -jnp.inf, F32)
    grp = []
    for g in range(N_GROUPS):
        s_rows = [sel[PER_GROUP * g + j:PER_GROUP * g + j + 1, :] for j in range(PER_GROUP)]
        c_rows = [score[PER_GROUP * g + j:PER_GROUP * g + j + 1, :] for j in range(PER_GROUP)]
        f_idx = jnp.zeros((1, TM), I32)
        f_val, f_sc = s_rows[0], c_rows[0]
        for j in range(1, PER_GROUP):
            better = s_rows[j] > f_val
            f_idx = jnp.where(better, j, f_idx)
            f_val = jnp.where(better, s_rows[j], f_val)
            f_sc = jnp.where(better, c_rows[j], f_sc)
        s_idx = jnp.zeros((1, TM), I32)
        s_val, s_sc = neg_inf, jnp.zeros((1, TM), F32)
        for j in range(PER_GROUP):
            better = (f_idx != j) & (s_rows[j] > s_val)
            s_idx = jnp.where(better, j, s_idx)
            s_val = jnp.where(better, s_rows[j], s_val)
            s_sc = jnp.where(better, c_rows[j], s_sc)
        grp.append((f_val + s_val, PER_GROUP * g + f_idx, PER_GROUP * g + s_idx, f_sc, s_sc))

    best, e0, e1, c0, c1 = grp[0]
    for g in range(1, N_GROUPS):
        better = grp[g][0] > best
        best = jnp.where(better, grp[g][0], best)
        e0 = jnp.where(better, grp[g][1], e0)
        e1 = jnp.where(better, grp[g][2], e1)
        c0 = jnp.where(better, grp[g][3], c0)
        c1 = jnp.where(better, grp[g][4], c1)
    den = c0 + c1
    w0 = c0 / den
    w1 = c1 / den

    sub = lax.broadcasted_iota(I32, (N_EXPERTS, TM), 0)
    hit0 = sub == jnp.broadcast_to(e0, (N_EXPERTS, TM))
    hit1 = sub == jnp.broadcast_to(e1, (N_EXPERTS, TM))
    onehot = jnp.where(hit0 | hit1, 1.0, 0.0)
    prefix = jnp.dot(onehot.astype(BF16), tri_ref[...], preferred_element_type=F32)
    count = jnp.sum(onehot, axis=1, keepdims=True)
    padded = jnp.floor((count + (ROW_CHUNK - 1)) * (1.0 / ROW_CHUNK)) * ROW_CHUNK

    sub1 = lax.broadcasted_iota(I32, (N_EXPERTS, 1), 0)
    run = jnp.zeros((1, 1), F32)
    local_off = jnp.zeros((N_EXPERTS, 1), F32)
    for e in range(N_EXPERTS):
        local_off = jnp.where(sub1 == e, run, local_off)
        run = run + padded[e:e + 1, :]
    place = prefix + local_off
    lpos0 = jnp.sum(jnp.where(hit0, place, 0.0), axis=0, keepdims=True).astype(I32)
    lpos1 = jnp.sum(jnp.where(hit1, place, 0.0), axis=0, keepdims=True).astype(I32)

    lanes = lambda col: jnp.broadcast_to(col, (N_EXPERTS, 128)).astype(I32)
    nch_o[sub_tile] = lanes(padded * (1.0 / ROW_CHUNK))
    loff_o[sub_tile] = lanes(local_off)
    gb_o[sub_tile] = carry[...].astype(I32)
    carry[...] = carry[...] + padded
    cnt_o[...] = carry[...].astype(I32)

    sub8 = lax.broadcasted_iota(I32, (8, TM), 0)
    bc = lambda r: jnp.broadcast_to(r, (8, TM))
    lp_o[:, tok] = jnp.where(sub8 == 0, bc(lpos0), jnp.where(sub8 == 1, bc(lpos1), 0))
    rf_o[:, tok] = jnp.where(sub8 == 0, bc(w0), jnp.where(sub8 == 1, bc(w1), 0.0))


def _tail_specs(l, sub_tiles):
    rows = sub_tiles * TM
    in_specs = [
        pl.BlockSpec((1, 1, D), lambda i: (l, 0, 0)),
        pl.BlockSpec((N_EXPERTS, D), lambda i: (0, 0)),
        pl.BlockSpec((N_EXPERTS, TM), lambda i: (0, 0)),
        pl.BlockSpec((TM, TM), lambda i: (0, 0)),
    ]
    per_tile = lambda: pl.BlockSpec((sub_tiles, N_EXPERTS, 128), lambda i: (i, 0, 0))
    out_specs = [
        pl.BlockSpec((rows, D), lambda i: (i, 0)),
        pl.BlockSpec((8, rows), lambda i: (0, i)),
        pl.BlockSpec((8, rows), lambda i: (0, i)),
        per_tile(), per_tile(), per_tile(),
        pl.BlockSpec((N_EXPERTS, 128), lambda i: (0, 0)),
    ]
    out_shape = [
        _hbm_out((T, D), BF16),
        jax.ShapeDtypeStruct((8, T), I32),
        jax.ShapeDtypeStruct((8, T), F32),
        jax.ShapeDtypeStruct((N_TILES, N_EXPERTS, 128), I32),
        jax.ShapeDtypeStruct((N_TILES, N_EXPERTS, 128), I32),
        jax.ShapeDtypeStruct((N_TILES, N_EXPERTS, 128), I32),
        jax.ShapeDtypeStruct((N_EXPERTS, 128), I32),
    ]
    scratch = [pltpu.VMEM((N_EXPERTS, 128), F32)]
    return in_specs, out_specs, out_shape, scratch


def _out_kernel(split_x, *refs):
    x_refs, refs = refs[:1 + split_x], refs[1 + split_x:]
    (m_ref, ac_ref, al_ref, sc_ref, sl_ref, w_ref, n2_ref, rwt_ref, rbb_ref, tri_ref,
     x_o, *tail_refs) = refs
    *tail_outs, wbf, carry = tail_refs
    i = pl.program_id(0)

    @pl.when(i == 0)
    def _():
        wbf[...] = w_ref[0].astype(BF16)

    is_ctx = i < T_CTX // TO
    x = jnp.where(is_ctx, x_refs[0][...], x_refs[1][...]) if split_x else x_refs[0][...]
    att = jnp.where(is_ctx, ac_ref[...], al_ref[...])
    sgo = jnp.where(is_ctx, sc_ref[...], sl_ref[...])
    out = (jnp.dot(att, wbf[0:MIX_A, :], preferred_element_type=F32)
           + jnp.dot(sgo, wbf[MIX_A:MIX_A + MIX_B, :], preferred_element_type=F32))
    x_new = x + m_ref[0, 0, 2:3, :] * out
    x_o[...] = x_new
    for sub_tile in range(TO // TM):
        _route_tail(sub_tile, x_new[sub_tile * TM:(sub_tile + 1) * TM, :],
                    m_ref, n2_ref, rwt_ref, rbb_ref, tri_ref, *tail_outs, carry)


def _out_proj(l, x, mod, att_c, att_l, sg_c, sg_l, w_out, norm2_g, rwt, rbb, tri):
    e = l // 2
    t_in, t_out, t_shape, t_scratch = _tail_specs(l, TO // TM)
    n_ctx = T_CTX // TO
    ctx = lambda w: pl.BlockSpec((TO, w), lambda i: (jnp.minimum(i, n_ctx - 1), 0))
    lat = lambda w: pl.BlockSpec((TO, w), lambda i: (jnp.maximum(i - n_ctx, 0), 0))
    split_x = isinstance(x, tuple)
    if split_x:
        x_specs, x_args, aliases = [ctx(D), lat(D)], [_hbm(x[0]), _hbm(x[1])], {}
    else:
        x_specs, x_args, aliases = [pl.BlockSpec((TO, D), lambda i: (i, 0))], [_hbm(x)], {0: 0}
    in_specs = x_specs + [
        pl.BlockSpec((1, 1, N_MOD, D), lambda i: (l, _cond_of_tile(i * (TO // TM)), 0, 0)),
        ctx(MIX_A), lat(MIX_A), ctx(MIX_B), lat(MIX_B),
        pl.BlockSpec((1, D, D), lambda i: (e, 0, 0)),
    ] + t_in
    return pl.pallas_call(
        functools.partial(_out_kernel, split_x),
        grid=(T // TO,),
        in_specs=in_specs,
        out_specs=[pl.BlockSpec((TO, D), lambda i: (i, 0))] + t_out,
        out_shape=[_hbm_out((T, D), F32)] + t_shape,
        scratch_shapes=[pltpu.VMEM((D, D), BF16)] + t_scratch,
        input_output_aliases=aliases,
        compiler_params=_cparams(("arbitrary",), 40),
        name="out_proj_route",
    )(*x_args, mod, _hbm(att_c), _hbm(att_l), _hbm(sg_c), _hbm(sg_l), w_out,
      norm2_g.reshape(DEPTH, 1, D), rwt, rbb, tri)


def _pool_tables():
    t = np.arange(TP)[:, None]
    s = np.arange(TP + POOL_RB)[None, :]
    p = s - POOL_HALO
    bands, sizes = [], []
    for seq_len in (CTX_S, LAT_S):
        same = (p >= 0) & (p < TP) & (p // seq_len == t // seq_len)
        per_w, size_w = [], []
        for w in POOL_WINDOWS:
            band = (p >= t - w // 2) & (p < t - w // 2 + w) & same
            size_w.append(band.sum(axis=1, keepdims=True))
            per_w.append(np.stack([band[rb * POOL_RB:(rb + 1) * POOL_RB, rb * POOL_RB:(rb + 2) * POOL_RB]
                                   for rb in range(TP // POOL_RB)]))
        bands.append(np.stack(per_w))
        sizes.append(np.stack(size_w))
    return jnp.asarray(np.stack(bands), dtype=BF16), jnp.asarray(np.stack(sizes), dtype=F32)


def _pool_kernel(x_ref, m_ref, n1_ref, band_ref, size_ref, pw_ref, ps_ref,
                 n2_ref, rwt_ref, rbb_ref, tri_ref,
                 x_o, *tail_refs):
    *tail_outs, carry = tail_refs
    x = x_ref[...]
    h = _modnorm(x, n1_ref[0], m_ref[0, 0, 1:2, :], m_ref[0, 0, 0:1, :])
    hcat = jnp.concatenate([jnp.zeros((POOL_HALO, D), F32), h,
                            jnp.zeros((POOL_RB - POOL_HALO, D), F32)], axis=0)
    hi = hcat.astype(BF16)
    lo = (hcat - hi.astype(F32)).astype(BF16)
    dot = lambda a, b: jnp.dot(a, b, preferred_element_type=F32)
    outs = []
    for g in range(len(POOL_WINDOWS)):
        cols = slice(g * POOL_DIM, (g + 1) * POOL_DIM)
        sums = []
        for rb in range(TP // POOL_RB):
            near = slice(rb * POOL_RB, (rb + 2) * POOL_RB)
            band = band_ref[0, g, rb]
            sums.append(dot(band, hi[near, cols]) + dot(band, lo[near, cols]))
        mean = jnp.concatenate(sums, axis=0) / size_ref[0, g]
        dlt = (mean - h[:, cols]).astype(BF16)
        outs.append(dot(dlt, pw_ref[0, g].astype(BF16)))
    out = jnp.concatenate(outs, axis=1) * ps_ref[0]
    x_new = x + m_ref[0, 0, 2:3, :] * out
    x_o[...] = x_new
    for sub_tile in range(TP // TM):
        _route_tail(sub_tile, x_new[sub_tile * TM:(sub_tile + 1) * TM, :],
                    m_ref, n2_ref, rwt_ref, rbb_ref, tri_ref, *tail_outs, carry)


def _pool_mixer(l, x, mod, norm1_g, pool_w, pool_scale, norm2_g, rwt, rbb, tri):
    o = l // 2
    t_in, t_out, t_shape, t_scratch = _tail_specs(l, TP // TM)
    bands, sizes = _pool_tables()
    n_ctx = T_CTX // TP
    kind = lambda i: jnp.where(i < n_ctx, 0, 1)
    n_w, n_rb = len(POOL_WINDOWS), TP // POOL_RB
    in_specs = [
        pl.BlockSpec((TP, D), lambda i: (i, 0)),
        pl.BlockSpec((1, 1, N_MOD, D), lambda i: (l, _cond_of_tile(i * (TP // TM)), 0, 0)),
        pl.BlockSpec((1, 1, D), lambda i: (l, 0, 0)),
        pl.BlockSpec((1, n_w, n_rb, POOL_RB, 2 * POOL_RB), lambda i: (kind(i), 0, 0, 0, 0)),
        pl.BlockSpec((1, n_w, TP, 1), lambda i: (kind(i), 0, 0, 0)),
        pl.BlockSpec((1, n_w, POOL_DIM, POOL_DIM), lambda i: (o, 0, 0, 0)),
        pl.BlockSpec((1, 1, D), lambda i: (o, 0, 0)),
    ] + t_in
    return pl.pallas_call(
        _pool_kernel,
        grid=(T // TP,),
        in_specs=in_specs,
        out_specs=[pl.BlockSpec((TP, D), lambda i: (i, 0))] + t_out,
        out_shape=[_hbm_out((T, D), F32)] + t_shape,
        scratch_shapes=t_scratch,
        compiler_params=_cparams(("arbitrary",), 52),
        name="pool_route",
    )(_hbm(x), mod, norm1_g.reshape(DEPTH, 1, D), bands, sizes, pool_w,
      pool_scale.reshape(DEPTH // 2, 1, D),
      norm2_g.reshape(DEPTH, 1, D), rwt, rbb, tri)


def _start_run_copies(tile, nch_ref, loff_ref, gb_ref, make_copy):
    for e in range(N_EXPERTS):
        n = nch_ref[tile * N_EXPERTS + e]
        lo = loff_ref[tile * N_EXPERTS + e]
        gb = gb_ref[tile * N_EXPERTS + e] + e * EXPERT_CAP

        def issue(c, carry, lo=lo, gb=gb):
            make_copy(pl.multiple_of(lo + c * (2 * ROW_CHUNK), ROW_CHUNK),
                      pl.multiple_of(gb + c * (2 * ROW_CHUNK), ROW_CHUNK), 2 * ROW_CHUNK).start()
            return carry

        lax.fori_loop(0, lax.shift_right_logical(n, 1), issue, 0)

        @pl.when((n & 1) == 1)
        def _(n=n, lo=lo, gb=gb):
            make_copy(pl.multiple_of(lo + (n - 1) * ROW_CHUNK, ROW_CHUNK),
                      pl.multiple_of(gb + (n - 1) * ROW_CHUNK, ROW_CHUNK), ROW_CHUNK).start()


def _wait_run_copies(tile, nch_ref, make_copy):
    doubles, singles = 0, 0
    for e in range(N_EXPERTS):
        n = nch_ref[tile * N_EXPERTS + e]
        doubles = doubles + lax.shift_right_logical(n, 1)
        singles = singles + (n & 1)

    def drain(rows):
        def body(c, carry):
            make_copy(0, 0, rows).wait()
            return carry
        return body

    lax.fori_loop(0, doubles, drain(2 * ROW_CHUNK), 0)
    lax.fori_loop(0, singles, drain(ROW_CHUNK), 0)


def _dispatch_kernel(nch_ref, loff_ref, gb_ref, h_ref, lp_ref, xs_ref, sbuf, sems):
    i = pl.program_id(0)
    slot = i % 2
    r = lax.broadcasted_iota(I32, (SORTED_ROWS, TM), 0)
    for sub in range(MOVE_SUBS):
        tok = slice(sub * TM, (sub + 1) * TM)
        hit = (r == lp_ref[0:1, tok]) | (r == lp_ref[1:2, tok])
        perm = jnp.where(hit, 1.0, 0.0).astype(BF16)
        sbuf[slot, sub] = jnp.dot(perm, h_ref[tok, :], preferred_element_type=F32).astype(BF16)

    def copies_of(s, sub):
        def make_copy(local_row, global_row, rows):
            return pltpu.make_async_copy(sbuf.at[s, sub, pl.ds(local_row, rows)],
                                         xs_ref.at[pl.ds(global_row, rows)], sems.at[s])
        return make_copy

    @pl.when(i > 0)
    def _():
        for sub in range(MOVE_SUBS):
            _wait_run_copies((i - 1) * MOVE_SUBS + sub, nch_ref, copies_of(1 - slot, sub))

    for sub in range(MOVE_SUBS):
        _start_run_copies(i * MOVE_SUBS + sub, nch_ref, loff_ref, gb_ref, copies_of(slot, sub))

    @pl.when(i == N_TILES // MOVE_SUBS - 1)
    def _():
        for sub in range(MOVE_SUBS):
            _wait_run_copies(i * MOVE_SUBS + sub, nch_ref, copies_of(slot, sub))


def _dispatch(layout, h2, lp):
    rows = MOVE_SUBS * TM
    return pl.pallas_call(
        _dispatch_kernel,
        grid_spec=pltpu.PrefetchScalarGridSpec(
            num_scalar_prefetch=3,
            grid=(N_TILES // MOVE_SUBS,),
            in_specs=[pl.BlockSpec((rows, D), lambda i, *_: (i, 0)),
                      pl.BlockSpec((8, rows), lambda i, *_: (0, i))],
            out_specs=pl.BlockSpec(memory_space=pl.ANY),
            scratch_shapes=[pltpu.VMEM((2, MOVE_SUBS, SORTED_ROWS, D), BF16),
                            pltpu.SemaphoreType.DMA((2,))],
        ),
        out_shape=_hbm_out((N_EXPERTS * EXPERT_CAP, D), BF16),
        compiler_params=_cparams(("arbitrary",), 32),
        name="moe_dispatch",
    )(*layout, _hbm(h2), lp)


def _moe_kernel(l, rb_ref, e_ref, nv_ref, first_ref, ord_ref, nxt_ref, nxt2_ref,
                xs_ref, wg_hbm, wu_hbm, wd_hbm, y_ref, wgf, wuf, wdf, wgb, wub, wdb, sems):
    i = pl.program_id(0)
    n_valid = nv_ref[i]

    def weight_copies(e, slot):
        return (pltpu.make_async_copy(wg_hbm.at[l, e], wgf.at[slot], sems.at[slot, 0]),
                pltpu.make_async_copy(wu_hbm.at[l, e], wuf.at[slot], sems.at[slot, 1]),
                pltpu.make_async_copy(wd_hbm.at[l, e], wdf.at[slot], sems.at[slot, 2]))

    @pl.when(first_ref[i] == 1)
    def _():
        k = ord_ref[i]
        slot = k % W_SLOTS

        @pl.when(k == 0)
        def _():
            for cp in weight_copies(e_ref[i], 0):
                cp.start(priority=1)

            @pl.when(nxt_ref[i] >= 0)
            def _():
                for cp in weight_copies(nxt_ref[i], 1):
                    cp.start(priority=1)

        for cp in weight_copies(e_ref[i], slot):
            cp.wait()

        @pl.when(nxt2_ref[i] >= 0)
        def _():
            for cp in weight_copies(nxt2_ref[i], (k + 2) % W_SLOTS):
                cp.start(priority=1)

        wgb[...] = wgf[slot].astype(BF16)
        wub[...] = wuf[slot].astype(BF16)
        wdb[...] = wdf[slot].astype(BF16)

    @pl.when(n_valid > 0)
    def _():
        row = lax.broadcasted_iota(I32, (TE, 1), 0)
        x = jnp.where(row < n_valid, xs_ref[...], jnp.zeros((), BF16))
        gate = jnp.dot(x, wgb[...], preferred_element_type=F32)
        up = jnp.dot(x, wub[...], preferred_element_type=F32)
        act = (gate * _sigmoid(gate) * up).astype(BF16)
        y_ref[...] = jnp.dot(act, wdb[...], preferred_element_type=F32).astype(BF16)


def _experts(l, tile_map, xs, w_gate, w_up, w_down):
    n_pre = len(tile_map)
    blk = lambda i, rb, *_: (rb[i], 0)
    return pl.pallas_call(
        functools.partial(_moe_kernel, l),
        grid_spec=pltpu.PrefetchScalarGridSpec(
            num_scalar_prefetch=n_pre,
            grid=(MOE_TILES,),
            in_specs=[
                pl.BlockSpec((TE, D), blk),
                pl.BlockSpec(memory_space=pl.ANY),
                pl.BlockSpec(memory_space=pl.ANY),
                pl.BlockSpec(memory_space=pl.ANY),
            ],
            out_specs=pl.BlockSpec((TE, D), blk),
            scratch_shapes=[pltpu.VMEM((W_SLOTS, D, D_FF), F32), pltpu.VMEM((W_SLOTS, D, D_FF), F32),
                            pltpu.VMEM((W_SLOTS, D_FF, D), F32),
                            pltpu.VMEM((D, D_FF), BF16), pltpu.VMEM((D, D_FF), BF16),
                            pltpu.VMEM((D_FF, D), BF16),
                            pltpu.SemaphoreType.DMA((W_SLOTS, 3))],
        ),
        out_shape=_hbm_out((N_EXPERTS * EXPERT_CAP, D), BF16),
        compiler_params=_cparams(("arbitrary",), 48),
        name="moe_experts",
    )(*tile_map, _hbm(xs), w_gate, w_up, w_down)


def _combine_kernel(split_out, nch_ref, loff_ref, gb_ref, x_ref, m_ref, lpt_ref, w_ref, y_ref, *refs):
    *o_refs, ybuf, sems = refs
    i = pl.program_id(0)
    slot = i % 2

    def copies_of(s, sub):
        def make_copy(local_row, global_row, rows):
            return pltpu.make_async_copy(y_ref.at[pl.ds(global_row, rows)],
                                         ybuf.at[s, sub, pl.ds(local_row, rows)], sems.at[s])
        return make_copy

    @pl.when(i == 0)
    def _():
        ybuf[...] = jnp.zeros_like(ybuf)
        for sub in range(MOVE_SUBS):
            _start_run_copies(sub, nch_ref, loff_ref, gb_ref, copies_of(slot, sub))

    @pl.when(i + 1 < N_TILES // MOVE_SUBS)
    def _():
        for sub in range(MOVE_SUBS):
            _start_run_copies((i + 1) * MOVE_SUBS + sub, nch_ref, loff_ref, gb_ref, copies_of(1 - slot, sub))

    for sub in range(MOVE_SUBS):
        _wait_run_copies(i * MOVE_SUBS + sub, nch_ref, copies_of(slot, sub))

    col = lax.broadcasted_iota(I32, (TM, SORTED_ROWS), 1)
    parts = []
    for sub in range(MOVE_SUBS):
        tok = slice(sub * TM, (sub + 1) * TM)
        pick = jnp.where(col == lpt_ref[tok, 0:1], w_ref[tok, 0:1],
                         jnp.where(col == lpt_ref[tok, 1:2], w_ref[tok, 1:2], 0.0)).astype(BF16)
        parts.append(jnp.dot(pick, ybuf[slot, sub], preferred_element_type=F32))
    res = x_ref[...] + m_ref[0, 0, 5:6, :] * jnp.concatenate(parts, axis=0)
    if split_out:
        @pl.when(i < T_CTX // (MOVE_SUBS * TM))
        def _():
            o_refs[0][...] = res

        @pl.when(i >= T_CTX // (MOVE_SUBS * TM))
        def _():
            o_refs[1][...] = res
    else:
        o_refs[0][...] = res


def _combine(l, layout, x, mod, lpt, wcol, y, split_out):
    rows = MOVE_SUBS * TM
    n_ctx = T_CTX // rows
    if split_out:
        out_specs = [pl.BlockSpec((rows, D), lambda i, *_: (jnp.minimum(i, n_ctx - 1), 0)),
                     pl.BlockSpec((rows, D), lambda i, *_: (jnp.maximum(i - n_ctx, 0), 0))]
        out_shape = [jax.ShapeDtypeStruct((T_CTX, D), F32), jax.ShapeDtypeStruct((T_LAT, D), F32)]
        aliases = {}
    else:
        out_specs = pl.BlockSpec((rows, D), lambda i, *_: (i, 0))
        out_shape = _hbm_out((T, D), F32)
        aliases = {3: 0}
    return pl.pallas_call(
        functools.partial(_combine_kernel, split_out),
        grid_spec=pltpu.PrefetchScalarGridSpec(
            num_scalar_prefetch=3,
            grid=(N_TILES // MOVE_SUBS,),
            in_specs=[
                pl.BlockSpec((rows, D), lambda i, *_: (i, 0)),
                pl.BlockSpec((1, 1, N_MOD, D), lambda i, *_: (l, _cond_of_tile(i * MOVE_SUBS), 0, 0)),
                pl.BlockSpec((rows, 8), lambda i, *_: (i, 0)),
                pl.BlockSpec((rows, 8), lambda i, *_: (i, 0)),
                pl.BlockSpec(memory_space=pl.ANY),
            ],
            out_specs=out_specs,
            scratch_shapes=[pltpu.VMEM((2, MOVE_SUBS, SORTED_ROWS, D), BF16),
                            pltpu.SemaphoreType.DMA((2,))],
        ),
        out_shape=out_shape,
        input_output_aliases=aliases,
        compiler_params=_cparams(("arbitrary",), 48),
        name="moe_combine",
    )(*layout, _hbm(x), mod, lpt, wcol, _hbm(y))


def _tile_map(cnt):
    cnt = cnt[:, 0]
    ids = jnp.arange(N_EXPERTS, dtype=I32)
    n_tiles = (cnt + TE - 1) // TE
    cum = jnp.cumsum(n_tiles)
    total = cum[-1]
    i = jnp.arange(MOE_TILES, dtype=I32)
    ii = jnp.minimum(i, total - 1)
    e = jnp.sum((ii[:, None] >= cum[None, :]).astype(I32), axis=1)
    onehot = (e[:, None] == ids[None, :]).astype(I32)
    pick = lambda v: jnp.sum(onehot * v[None, :], axis=1)
    j = ii - pick(cum - n_tiles)
    active = i < total
    n_valid = jnp.where(active, jnp.clip(pick(cnt) - j * TE, 0, TE), 0)
    first = (active & (j == 0)).astype(I32)
    nonempty = n_tiles > 0
    ordinal = jnp.cumsum(nonempty.astype(I32)) - 1
    later = jnp.where(nonempty[None, :] & (ids[None, :] > ids[:, None]), ids[None, :], N_EXPERTS)
    nxt = jnp.min(later, axis=1)
    nxt_of = lambda v: jnp.sum((v[:, None] == ids[None, :]).astype(I32) * nxt[None, :], axis=1)
    nxt2 = jnp.where(nxt == N_EXPERTS, N_EXPERTS, nxt_of(nxt))
    none = lambda v: jnp.where(v == N_EXPERTS, -1, v)
    as_i32 = lambda v: v.astype(I32)
    return tuple(map(as_i32, (e * CAP_TILES + j, e, n_valid, first, pick(ordinal),
                              pick(none(nxt)), pick(none(nxt2)))))


def _moe(l, x, tail, mod, w_gate, w_up, w_down, split_out):
    h2, lp, rf, nch, loff, gb, cnt = tail
    layout = tuple(a[:, :, 0].reshape(N_TILES * N_EXPERTS) for a in (nch, loff, gb))
    xs = _dispatch(layout, h2, lp)
    y = _experts(l, _tile_map(cnt), xs, w_gate, w_up, w_down)
    return _combine(l, layout, x, mod, lp.T, rf.T, y, split_out)


def _rope_tables():
    t = jnp.arange(LAT_S)
    row = (t // GRID_W).astype(F32)
    col = (t % GRID_W).astype(F32)
    nf = HEAD_DIM // 4
    inv = ROPE_BASE ** (-jnp.arange(nf, dtype=F32) / nf)
    ar = row[:, None] * inv[None, :]
    ac = col[:, None] * inv[None, :]
    cos = jnp.concatenate([jnp.cos(ar), jnp.cos(ar), jnp.cos(ac), jnp.cos(ac)], axis=1)
    sin = jnp.concatenate([-jnp.sin(ar), jnp.sin(ar), -jnp.sin(ac), jnp.sin(ac)], axis=1)
    reps = QK_W // HEAD_DIM
    return jnp.tile(cos, (1, reps)), jnp.tile(sin, (1, reps))


def kernel(x_prompt, x_sample, cache_k, cache_v, c, c_ctx, norm1_g, norm2_g, w_ada, b_ada, w_in, w_out,
           q_norm_g, k_norm_g, lam_q1, lam_k1, lam_q2, lam_k2, subln_g, sg_w, sg_b, pool_w, pool_scale,
           router_w, router_b, w_gate, w_up, w_down):
    x = (x_prompt.reshape(T_CTX, D), x_sample.reshape(T_LAT, D))
    cond = jnp.concatenate([c_ctx[None, :], c], axis=0)
    mod = _modulation(cond, w_ada, b_ada)

    seg_id = np.arange(QK_W) // HEAD_DIM
    seg = jnp.asarray((seg_id[:, None] == seg_id[None, :]) / HEAD_DIM, dtype=BF16)
    tri = jnp.asarray(np.arange(TM)[:, None] < np.arange(TM)[None, :], dtype=BF16)
    rope_tabs = _rope_tables()
    rwt = router_w.T
    rbb = jnp.broadcast_to(router_b[:, None], (N_EXPERTS, TM))
    states = None

    for l in range(DEPTH):
        if l % 2 == 0:
            e = l // 2
            qg = jnp.tile(q_norm_g[e], QK_W // HEAD_DIM)[None, :]
            kg = jnp.tile(k_norm_g[e], QK_W // HEAD_DIM)[None, :]
            sgb = jnp.broadcast_to(sg_b[:, :, :, None], (DEPTH // 2, SG_GROUPS, SG_CHUNK, 128))
            x_c, x_l = x if isinstance(x, tuple) else (x, x)
            common = (mod, norm1_g, w_in, qg, kg, seg, sg_w, sgb)
            q_c, k_c, v_c, sg_c, *states = _front(False, l, x_c, *common, None, states)
            q_l, k_l, v_l, sg_l = _front(True, l, x_l, *common, rope_tabs, None)
            lam = (lam_q1, lam_k1, lam_q2, lam_k2, subln_g)
            att_c = _attention(False, l, q_c, k_c, v_c, None, None, *lam)
            att_l = _attention(True, l, q_l, k_l, v_l, cache_k, cache_v, *lam)
            x, *tail = _out_proj(l, x, mod, att_c, att_l, sg_c, sg_l, w_out, norm2_g, rwt, rbb, tri)
        else:
            x, *tail = _pool_mixer(l, x, mod, norm1_g, pool_w, pool_scale, norm2_g, rwt, rbb, tri)
        x = _moe(l, x, tail, mod, w_gate, w_up, w_down, split_out=(l == DEPTH - 1))
    state_k, state_v = states

    y_prompt = x[0].reshape(N_CTX_B, CTX_S, D)
    y_sample = x[1].reshape(N_LAT_B, LAT_S, D)
    state_k = state_k.reshape(N_CTX_B, DEPTH // 2, CTX_S, N_HEADS, 2, HEAD_DIM)
    state_v = state_v.reshape(N_CTX_B, DEPTH // 2, CTX_S, N_HEADS, VAL_DIM)
    return y_prompt, y_sample, state_k, state_v
```

```python
import functools
import math

import jax
import jax.numpy as jnp
import numpy as np
from jax import lax
from jax.experimental import pallas as pl
from jax.experimental.pallas import tpu as pltpu

F32 = jnp.float32
BF16 = jnp.bfloat16
I32 = jnp.int32

D = 1024
N_CTX_B, CTX_S = 16, 256
N_LAT_B, LAT_S = 2, 1024
PAST = 512
DEPTH = 4
T_CTX = N_CTX_B * CTX_S
T_LAT = N_LAT_B * LAT_S
T = T_CTX + T_LAT
GRID_W = 64
N_HEADS = 4
HEAD_DIM = 64
VAL_DIM = 128
QK_W = 512
MIX_A = 512
MIX_B = 512
IN_W = 2560
SG_GROUPS = 4
SG_CHUNK = 128
POOL_WINDOWS = (2, 4, 8, 16)
POOL_DIM = 256
POOL_HALO = 8
N_EXPERTS = 16
N_GROUPS = 4
PER_GROUP = 4
D_FF = 512
N_MOD = 6
N_COND = 1 + N_LAT_B
EPS = 1e-6
ROPE_BASE = 10000.0

TM = 256
N_TILES = T // TM
N_CTX_TILES = T_CTX // TM
LAT_TILES_PER_SEQ = LAT_S // TM
ROW_CHUNK = 16
PAD_PER_RUN = ROW_CHUNK - 1
SORTED_ROWS = -(-(2 * TM + N_EXPERTS * PAD_PER_RUN) // 128) * 128
TE = 512
W_SLOTS = 3
ATTN_CTX_SEQS = 2
ATTN_LAT_HEADS = 2
MOVE_SUBS = 2
TF = 512
TO = 512
TP = LAT_S
POOL_RB = 128
CAP_TILES = -(-(T + N_TILES * PAD_PER_RUN) // TE)
EXPERT_CAP = CAP_TILES * TE
MOE_TILES = (2 * T + N_EXPERTS * N_TILES * PAD_PER_RUN) // TE + N_EXPERTS
MOD_TN = 2048
MIB = 1024 * 1024

NT_DIMS = (((1,), (1,)), ((), ()))


def _cparams(sem, vmem_mib):
    return pltpu.CompilerParams(dimension_semantics=sem, vmem_limit_bytes=vmem_mib * MIB)


def _hbm(x):
    try:
        return pltpu.with_memory_space_constraint(x, pltpu.HBM)
    except ValueError:
        return x


def _hbm_out(shape, dtype):
    return pltpu.HBM(shape, dtype)


def _sigmoid(x):
    return 1.0 / (1.0 + jnp.exp(-x))


def _gelu_tanh(x):
    c = math.sqrt(2.0 / math.pi)
    return x * (0.5 * (1.0 + jnp.tanh(c * (x + 0.044715 * (x * x * x)))))


def _modnorm(x, g, scale, shift):
    ms = jnp.mean(x * x, axis=-1, keepdims=True)
    return (x * lax.rsqrt(ms + EPS)) * (g * (1.0 + scale)) + shift


def _cond_of_tile(i):
    return jnp.where(i < N_CTX_TILES, 0, 1 + (i - N_CTX_TILES) // LAT_TILES_PER_SEQ)


def _mod_kernel(c_ref, w_ref, b_ref, o_ref):
    c = c_ref[...]
    s = (c * _sigmoid(c)).astype(BF16)
    o_ref[0] = jnp.dot(s, w_ref[0].astype(BF16), preferred_element_type=F32) + b_ref[0]


def _modulation(cond, w_ada, b_ada):
    cond8 = jnp.concatenate([cond, jnp.zeros((8 - N_COND, D), F32)], axis=0)
    out = pl.pallas_call(
        _mod_kernel,
        grid=(DEPTH, N_MOD * D // MOD_TN),
        in_specs=[
            pl.BlockSpec((8, D), lambda l, n: (0, 0)),
            pl.BlockSpec((1, D, MOD_TN), lambda l, n: (l, 0, n)),
            pl.BlockSpec((1, 1, MOD_TN), lambda l, n: (l, 0, n)),
        ],
        out_specs=pl.BlockSpec((1, 8, MOD_TN), lambda l, n: (l, 0, n)),
        out_shape=jax.ShapeDtypeStruct((DEPTH, 8, N_MOD * D), F32),
        compiler_params=_cparams(("arbitrary", "arbitrary"), 40),
        name="adaln_mod",
    )(cond8, w_ada, b_ada.reshape(DEPTH, 1, N_MOD * D))
    return out.reshape(DEPTH, 8, N_MOD, D)


def _front_kernel(latent, n_state_in, *refs):
    (x_ref, m_ref, n1_ref, w_ref, qg_ref, kg_ref, seg_ref, sgw_ref, sgb_ref), refs = refs[:9], refs[9:]
    if latent:
        cos_ref, sin_ref, q_o, k_o, v_o, sg_o, wbf = refs
    else:
        q_o, k_o, v_o, sg_o, sk_o, sv_o, wbf = refs[n_state_in:]

    @pl.when(pl.program_id(0) == 0)
    def _():
        for c0 in range(0, IN_W, 512):
            wbf[:, c0:c0 + 512] = w_ref[0, :, c0:c0 + 512].astype(BF16)

    h = _modnorm(x_ref[...], n1_ref[0], m_ref[0, 0, 1:2, :], m_ref[0, 0, 0:1, :])
    proj = jnp.dot(h.astype(BF16), wbf[...], preferred_element_type=F32)

    def qk_norm(z, g):
        ms = jnp.dot((z * z).astype(BF16), seg_ref[...], preferred_element_type=F32)
        return z * lax.rsqrt(ms + EPS) * g

    q = qk_norm(proj[:, 0:QK_W], qg_ref[...])
    k = qk_norm(proj[:, QK_W:2 * QK_W], kg_ref[...])
    v = proj[:, 2 * QK_W:2 * QK_W + MIX_A]

    if latent:
        lane = lax.broadcasted_iota(I32, (1, QK_W), 1)
        first = (lane % 32) < 16

        def rope(z):
            partner = jnp.where(first, pltpu.roll(z, QK_W - 16, 1), pltpu.roll(z, 16, 1))
            return z * cos_ref[...] + partner * sin_ref[...]

        q = rope(q)
        k = rope(k)
    else:
        for b in range(TF // CTX_S):
            sk_o[b, 0] = k[b * CTX_S:(b + 1) * CTX_S, :]
            for hd in range(N_HEADS):
                sv_o[b, 0, :, hd, :] = v[b * CTX_S:(b + 1) * CTX_S, hd * VAL_DIM:(hd + 1) * VAL_DIM]

    q_o[...] = (q * (HEAD_DIM ** -0.5)).astype(BF16)
    k_o[...] = k.astype(BF16)
    v_o[...] = v.astype(BF16)

    gu0 = 2 * QK_W + MIX_A
    gv0 = gu0 + MIX_B
    for g in range(SG_GROUPS):
        wg = sgw_ref[0, g].astype(BF16)
        for n in range(TF // SG_CHUNK):
            rows = slice(n * SG_CHUNK, (n + 1) * SG_CHUNK)
            cu = slice(gu0 + g * 128, gu0 + (g + 1) * 128)
            cv = slice(gv0 + g * 128, gv0 + (g + 1) * 128)
            gv = _gelu_tanh(proj[rows, cv]).astype(BF16)
            mixed = jnp.dot(wg, gv, preferred_element_type=F32) + sgb_ref[0, g]
            sg_o[rows, g * 128:(g + 1) * 128] = (_gelu_tanh(proj[rows, cu]) * mixed).astype(BF16)


def _front(latent, l, x, mod, norm1_g, w_in, qg, kg, seg, sg_w, sgb, rope_tabs, states):
    e = l // 2
    rows = T_LAT if latent else T_CTX
    n_tiles = rows // TF
    off = T_CTX // TF if (latent and x.shape[0] == T) else 0
    tiles_per_seq = LAT_S // TF
    if latent:
        cond_map = lambda i: (l, 1 + i // tiles_per_seq, 0, 0)
    else:
        cond_map = lambda i: (l, 0, 0, 0)
    in_specs = [
        pl.BlockSpec((TF, D), lambda i: (i + off, 0)),
        pl.BlockSpec((1, 1, N_MOD, D), cond_map),
        pl.BlockSpec((1, 1, D), lambda i: (l, 0, 0)),
        pl.BlockSpec((1, D, IN_W), lambda i: (e, 0, 0)),
        pl.BlockSpec((1, QK_W), lambda i: (0, 0)),
        pl.BlockSpec((1, QK_W), lambda i: (0, 0)),
        pl.BlockSpec((QK_W, QK_W), lambda i: (0, 0)),
        pl.BlockSpec((1, SG_GROUPS, SG_CHUNK, SG_CHUNK), lambda i: (e, 0, 0, 0)),
        pl.BlockSpec((1, SG_GROUPS, SG_CHUNK, 128), lambda i: (e, 0, 0, 0)),
    ]
    args = [_hbm(x), mod, norm1_g.reshape(DEPTH, 1, D), w_in, qg, kg, seg, sg_w, sgb]
    tok = lambda: pl.BlockSpec((TF, QK_W), lambda i: (i, 0))
    out_specs = [tok(), tok(), tok(), tok()]
    out_shape = [_hbm_out((rows, QK_W), BF16)] * 4
    aliases = {}
    if latent:
        cos_t, sin_t = rope_tabs
        in_specs += [pl.BlockSpec((TF, QK_W), lambda i: (i % tiles_per_seq, 0))] * 2
        args += [cos_t, sin_t]
    else:
        if states:
            in_specs += [pl.BlockSpec(memory_space=pl.ANY)] * 2
            args += list(states)
            aliases = {9: 4, 10: 5}
        out_specs += [pl.BlockSpec((TF // CTX_S, 1, CTX_S, QK_W), lambda i: (i, e, 0, 0)),
                      pl.BlockSpec((TF // CTX_S, 1, CTX_S, N_HEADS, VAL_DIM), lambda i: (i, e, 0, 0, 0))]
        out_shape += [_hbm_out((N_CTX_B, DEPTH // 2, CTX_S, QK_W), F32),
                      _hbm_out((N_CTX_B, DEPTH // 2, CTX_S, N_HEADS, VAL_DIM), F32)]
    return pl.pallas_call(
        functools.partial(_front_kernel, latent, len(aliases)),
        grid=(n_tiles,),
        in_specs=in_specs,
        out_specs=out_specs,
        out_shape=out_shape,
        scratch_shapes=[pltpu.VMEM((D, IN_W), BF16)],
        input_output_aliases=aliases,
        compiler_params=_cparams(("arbitrary",), 48),
        name="front_lat" if latent else "front_ctx",
    )(*args)


def _attn_kernel(has_cache, seqs, heads, lam0, *refs):
    if has_cache:
        (q_ref, k_ref, v_ref, ck_ref, cv_ref, lq1, lk1, lq2, lk2, sub_ref, o_ref) = refs
    else:
        (q_ref, k_ref, v_ref, lq1, lk1, lq2, lk2, sub_ref, o_ref) = refs

    lane = lax.broadcasted_iota(I32, (1, VAL_DIM), 1)
    lam = (jnp.exp(jnp.sum(lq1[0] * lk1[0], axis=-1, keepdims=True))
           - jnp.exp(jnp.sum(lq2[0] * lk2[0], axis=-1, keepdims=True)) + lam0)
    q_rows = q_ref.shape[0] // seqs
    k_rows = k_ref.shape[0] // seqs

    for sq, h in [(sq, h) for sq in range(seqs) for h in range(heads)]:
        hs = slice(h * VAL_DIM, (h + 1) * VAL_DIM)
        qr = slice(sq * q_rows, (sq + 1) * q_rows)
        kr = slice(sq * k_rows, (sq + 1) * k_rows)
        qf = q_ref[qr, hs].astype(F32)
        qc = [jnp.where(lane < HEAD_DIM, qf, 0.0).astype(BF16),
              jnp.where(lane >= HEAD_DIM, qf, 0.0).astype(BF16)]
        keys = [k_ref[kr, hs]]
        vals = [v_ref[kr, hs]]
        if has_cache:
            keys.append(ck_ref[0, 0, :, hs].astype(BF16))
            vals.append(cv_ref[0, 0, :, hs].astype(BF16))

        probs = []
        for c in range(2):
            s = [lax.dot_general(qc[c], kk, NT_DIMS, preferred_element_type=F32) for kk in keys]
            m = s[0].max(axis=-1, keepdims=True)
            for sx in s[1:]:
                m = jnp.maximum(m, sx.max(axis=-1, keepdims=True))
            p = [jnp.exp(sx - m) for sx in s]
            den = p[0].sum(axis=-1, keepdims=True)
            for px in p[1:]:
                den = den + px.sum(axis=-1, keepdims=True)
            probs.append((p, den))

        r0 = 1.0 / probs[0][1]
        r1 = lam / probs[1][1]
        o = None
        for idx in range(len(keys)):
            a = (probs[0][0][idx] * r0 - probs[1][0][idx] * r1).astype(BF16)
            t = jnp.dot(a, vals[idx], preferred_element_type=F32)
            o = t if o is None else o + t
        ms = jnp.mean(o * o, axis=-1, keepdims=True)
        o_ref[qr, hs] = ((o * lax.rsqrt(ms + EPS) * sub_ref[0]) * (1.0 - lam0)).astype(BF16)


def _attention(has_cache, l, q, k, v, cache_k, cache_v, lam_q1, lam_k1, lam_q2, lam_k2, subln_g):
    e = l // 2
    lam0 = 0.8 - 0.6 * math.exp(-0.3 * l)
    n_b, s_len = (N_LAT_B, LAT_S) if has_cache else (N_CTX_B, CTX_S)
    heads = ATTN_LAT_HEADS if has_cache else N_HEADS
    seqs = 1 if has_cache else ATTN_CTX_SEQS
    width = heads * VAL_DIM
    nq = s_len // TM
    kv = lambda: pl.BlockSpec((seqs * s_len, width), lambda b, h, qi: (b, h))
    par = lambda: pl.BlockSpec((1, 1, HEAD_DIM), lambda b, h, qi: (e, 0, 0))
    in_specs = [pl.BlockSpec((seqs * TM, width), lambda b, h, qi: (b * nq + qi, h)), kv(), kv()]
    args = [_hbm(q), _hbm(k), _hbm(v)]
    if has_cache:
        cs = lambda: pl.BlockSpec((1, 1, PAST, width), lambda b, h, qi: (b, e, 0, h))
        in_specs += [cs(), cs()]
        args += [cache_k.reshape(N_LAT_B, DEPTH // 2, PAST, QK_W),
                 cache_v.reshape(N_LAT_B, DEPTH // 2, PAST, MIX_A)]
    in_specs += [par(), par(), par(), par(),
                 pl.BlockSpec((1, 1, VAL_DIM), lambda b, h, qi: (e, 0, 0))]
    r3 = lambda a: a.reshape(DEPTH // 2, 1, -1)
    args += [r3(lam_q1), r3(lam_k1), r3(lam_q2), r3(lam_k2), r3(subln_g)]
    return pl.pallas_call(
        functools.partial(_attn_kernel, has_cache, seqs, heads, lam0),
        grid=(n_b // seqs, N_HEADS // heads, nq),
        in_specs=in_specs,
        out_specs=pl.BlockSpec((seqs * TM, width), lambda b, h, qi: (b * nq + qi, h)),
        out_shape=_hbm_out((n_b * s_len, MIX_A), BF16),
        compiler_params=_cparams(("arbitrary", "arbitrary", "arbitrary"), 32),
        name="attn_lat" if has_cache else "attn_ctx",
    )(*args)


def _route_tail(sub_tile, x_new, m_ref, n2_ref, rwt_ref, rbb_ref, tri_ref,
                h2_o, lp_o, rf_o, nch_o, loff_o, gb_o, cnt_o, carry):
    tok = slice(sub_tile * TM, (sub_tile + 1) * TM)
    if sub_tile == 0:
        @pl.when(pl.program_id(0) == 0)
        def _():
            carry[...] = jnp.zeros_like(carry)

    h2 = _modnorm(x_new, n2_ref[0], m_ref[0, 0, 4:5, :], m_ref[0, 0, 3:4, :])

    hh = h2.astype(BF16)
    h2_o[tok, :] = hh
    hl = (h2 - hh.astype(F32)).astype(BF16)
    rw = rwt_ref[...]
    wh = rw.astype(BF16)
    wl = (rw - wh.astype(F32)).astype(BF16)
    dg = lambda a, b: lax.dot_general(a, b, NT_DIMS, preferred_element_type=F32)
    logits = dg(wh, hh) + dg(wh, hl) + dg(wl, hh)
    score = _sigmoid(logits)
    sel = score + rbb_ref[...]

    neg_inf = jnp.full((1, TM), -jnp.inf, F32)
    grp = []
    for g in range(N_GROUPS):
        s_rows = [sel[PER_GROUP * g + j:PER_GROUP * g + j + 1, :] for j in range(PER_GROUP)]
        c_rows = [score[PER_GROUP * g + j:PER_GROUP * g + j + 1, :] for j in range(PER_GROUP)]
        f_idx = jnp.zeros((1, TM), I32)
        f_val, f_sc = s_rows[0], c_rows[0]
        for j in range(1, PER_GROUP):
            better = s_rows[j] > f_val
            f_idx = jnp.where(better, j, f_idx)
            f_val = jnp.where(better, s_rows[j], f_val)
            f_sc = jnp.where(better, c_rows[j], f_sc)
        s_idx = jnp.zeros((1, TM), I32)
        s_val, s_sc = neg_inf, jnp.zeros((1, TM), F32)
        for j in range(PER_GROUP):
            better = (f_idx != j) & (s_rows[j] > s_val)
            s_idx = jnp.where(better, j, s_idx)
            s_val = jnp.where(better, s_rows[j], s_val)
            s_sc = jnp.where(better, c_rows[j], s_sc)
        grp.append((f_val + s_val, PER_GROUP * g + f_idx, PER_GROUP * g + s_idx, f_sc, s_sc))

    best, e0, e1, c0, c1 = grp[0]
    for g in range(1, N_GROUPS):
        better = grp[g][0] > best
        best = jnp.where(better, grp[g][0], best)
        e0 = jnp.where(better, grp[g][1], e0)
        e1 = jnp.where(better, grp[g][2], e1)
        c0 = jnp.where(better, grp[g][3], c0)
        c1 = jnp.where(better, grp[g][4], c1)
    den = c0 + c1
    w0 = c0 / den
    w1 = c1 / den

    sub = lax.broadcasted_iota(I32, (N_EXPERTS, TM), 0)
    hit0 = sub == jnp.broadcast_to(e0, (N_EXPERTS, TM))
    hit1 = sub == jnp.broadcast_to(e1, (N_EXPERTS, TM))
    onehot = jnp.where(hit0 | hit1, 1.0, 0.0)
    prefix = jnp.dot(onehot.astype(BF16), tri_ref[...], preferred_element_type=F32)
    count = jnp.sum(onehot, axis=1, keepdims=True)
    padded = jnp.floor((count + (ROW_CHUNK - 1)) * (1.0 / ROW_CHUNK)) * ROW_CHUNK

    sub1 = lax.broadcasted_iota(I32, (N_EXPERTS, 1), 0)
    run = jnp.zeros((1, 1), F32)
    local_off = jnp.zeros((N_EXPERTS, 1), F32)
    for e in range(N_EXPERTS):
        local_off = jnp.where(sub1 == e, run, local_off)
        run = run + padded[e:e + 1, :]
    place = prefix + local_off
    lpos0 = jnp.sum(jnp.where(hit0, place, 0.0), axis=0, keepdims=True).astype(I32)
    lpos1 = jnp.sum(jnp.where(hit1, place, 0.0), axis=0, keepdims=True).astype(I32)

    lanes = lambda col: jnp.broadcast_to(col, (N_EXPERTS, 128)).astype(I32)
    nch_o[sub_tile] = lanes(padded * (1.0 / ROW_CHUNK))
    loff_o[sub_tile] = lanes(local_off)
    gb_o[sub_tile] = carry[...].astype(I32)
    carry[...] = carry[...] + padded
    cnt_o[...] = carry[...].astype(I32)

    sub8 = lax.broadcasted_iota(I32, (8, TM), 0)
    bc = lambda r: jnp.broadcast_to(r, (8, TM))
    lp_o[:, tok] = jnp.where(sub8 == 0, bc(lpos0), jnp.where(sub8 == 1, bc(lpos1), 0))
    rf_o[:, tok] = jnp.where(sub8 == 0, bc(w0), jnp.where(sub8 == 1, bc(w1), 0.0))


def _tail_specs(l, sub_tiles):
    rows = sub_tiles * TM
    in_specs = [
        pl.BlockSpec((1, 1, D), lambda i: (l, 0, 0)),
        pl.BlockSpec((N_EXPERTS, D), lambda i: (0, 0)),
        pl.BlockSpec((N_EXPERTS, TM), lambda i: (0, 0)),
        pl.BlockSpec((TM, TM), lambda i: (0, 0)),
    ]
    per_tile = lambda: pl.BlockSpec((sub_tiles, N_EXPERTS, 128), lambda i: (i, 0, 0))
    out_specs = [
        pl.BlockSpec((rows, D), lambda i: (i, 0)),
        pl.BlockSpec((8, rows), lambda i: (0, i)),
        pl.BlockSpec((8, rows), lambda i: (0, i)),
        per_tile(), per_tile(), per_tile(),
        pl.BlockSpec((N_EXPERTS, 128), lambda i: (0, 0)),
    ]
    out_shape = [
        _hbm_out((T, D), BF16),
        jax.ShapeDtypeStruct((8, T), I32),
        jax.ShapeDtypeStruct((8, T), F32),
        jax.ShapeDtypeStruct((N_TILES, N_EXPERTS, 128), I32),
        jax.ShapeDtypeStruct((N_TILES, N_EXPERTS, 128), I32),
        jax.ShapeDtypeStruct((N_TILES, N_EXPERTS, 128), I32),
        jax.ShapeDtypeStruct((N_EXPERTS, 128), I32),
    ]
    scratch = [pltpu.VMEM((N_EXPERTS, 128), F32)]
    return in_specs, out_specs, out_shape, scratch


def _out_kernel(split_x, *refs):
    x_refs, refs = refs[:1 + split_x], refs[1 + split_x:]
    (m_ref, ac_ref, al_ref, sc_ref, sl_ref, w_ref, n2_ref, rwt_ref, rbb_ref, tri_ref,
     x_o, *tail_refs) = refs
    *tail_outs, wbf, carry = tail_refs
    i = pl.program_id(0)

    @pl.when(i == 0)
    def _():
        wbf[...] = w_ref[0].astype(BF16)

    is_ctx = i < T_CTX // TO
    x = jnp.where(is_ctx, x_refs[0][...], x_refs[1][...]) if split_x else x_refs[0][...]
    att = jnp.where(is_ctx, ac_ref[...], al_ref[...])
    sgo = jnp.where(is_ctx, sc_ref[...], sl_ref[...])
    out = (jnp.dot(att, wbf[0:MIX_A, :], preferred_element_type=F32)
           + jnp.dot(sgo, wbf[MIX_A:MIX_A + MIX_B, :], preferred_element_type=F32))
    x_new = x + m_ref[0, 0, 2:3, :] * out
    x_o[...] = x_new
    for sub_tile in range(TO // TM):
        _route_tail(sub_tile, x_new[sub_tile * TM:(sub_tile + 1) * TM, :],
                    m_ref, n2_ref, rwt_ref, rbb_ref, tri_ref, *tail_outs, carry)


def _out_proj(l, x, mod, att_c, att_l, sg_c, sg_l, w_out, norm2_g, rwt, rbb, tri):
    e = l // 2
    t_in, t_out, t_shape, t_scratch = _tail_specs(l, TO // TM)
    n_ctx = T_CTX // TO
    ctx = lambda w: pl.BlockSpec((TO, w), lambda i: (jnp.minimum(i, n_ctx - 1), 0))
    lat = lambda w: pl.BlockSpec((TO, w), lambda i: (jnp.maximum(i - n_ctx, 0), 0))
    split_x = isinstance(x, tuple)
    if split_x:
        x_specs, x_args, aliases = [ctx(D), lat(D)], [_hbm(x[0]), _hbm(x[1])], {}
    else:
        x_specs, x_args, aliases = [pl.BlockSpec((TO, D), lambda i: (i, 0))], [_hbm(x)], {0: 0}
    in_specs = x_specs + [
        pl.BlockSpec((1, 1, N_MOD, D), lambda i: (l, _cond_of_tile(i * (TO // TM)), 0, 0)),
        ctx(MIX_A), lat(MIX_A), ctx(MIX_B), lat(MIX_B),
        pl.BlockSpec((1, D, D), lambda i: (e, 0, 0)),
    ] + t_in
    return pl.pallas_call(
        functools.partial(_out_kernel, split_x),
        grid=(T // TO,),
        in_specs=in_specs,
        out_specs=[pl.BlockSpec((TO, D), lambda i: (i, 0))] + t_out,
        out_shape=[_hbm_out((T, D), F32)] + t_shape,
        scratch_shapes=[pltpu.VMEM((D, D), BF16)] + t_scratch,
        input_output_aliases=aliases,
        compiler_params=_cparams(("arbitrary",), 40),
        name="out_proj_route",
    )(*x_args, mod, _hbm(att_c), _hbm(att_l), _hbm(sg_c), _hbm(sg_l), w_out,
      norm2_g.reshape(DEPTH, 1, D), rwt, rbb, tri)


def _pool_tables():
    t = np.arange(TP)[:, None]
    s = np.arange(TP + POOL_RB)[None, :]
    p = s - POOL_HALO
    bands, sizes = [], []
    for seq_len in (CTX_S, LAT_S):
        same = (p >= 0) & (p < TP) & (p // seq_len == t // seq_len)
        per_w, size_w = [], []
        for w in POOL_WINDOWS:
            band = (p >= t - w // 2) & (p < t - w // 2 + w) & same
            size_w.append(band.sum(axis=1, keepdims=True))
            per_w.append(np.stack([band[rb * POOL_RB:(rb + 1) * POOL_RB, rb * POOL_RB:(rb + 2) * POOL_RB]
                                   for rb in range(TP // POOL_RB)]))
        bands.append(np.stack(per_w))
        sizes.append(np.stack(size_w))
    return jnp.asarray(np.stack(bands), dtype=BF16), jnp.asarray(np.stack(sizes), dtype=F32)


def _pool_kernel(x_ref, m_ref, n1_ref, band_ref, size_ref, pw_ref, ps_ref,
                 n2_ref, rwt_ref, rbb_ref, tri_ref,
                 x_o, *tail_refs):
    *tail_outs, carry = tail_refs
    x = x_ref[...]
    h = _modnorm(x, n1_ref[0], m_ref[0, 0, 1:2, :], m_ref[0, 0, 0:1, :])
    hcat = jnp.concatenate([jnp.zeros((POOL_HALO, D), F32), h,
                            jnp.zeros((POOL_RB - POOL_HALO, D), F32)], axis=0)
    hi = hcat.astype(BF16)
    lo = (hcat - hi.astype(F32)).astype(BF16)
    dot = lambda a, b: jnp.dot(a, b, preferred_element_type=F32)
    outs = []
    for g in range(len(POOL_WINDOWS)):
        cols = slice(g * POOL_DIM, (g + 1) * POOL_DIM)
        sums = []
        for rb in range(TP // POOL_RB):
            near = slice(rb * POOL_RB, (rb + 2) * POOL_RB)
            band = band_ref[0, g, rb]
            sums.append(dot(band, hi[near, cols]) + dot(band, lo[near, cols]))
        mean = jnp.concatenate(sums, axis=0) / size_ref[0, g]
        dlt = (mean - h[:, cols]).astype(BF16)
        outs.append(dot(dlt, pw_ref[0, g].astype(BF16)))
    out = jnp.concatenate(outs, axis=1) * ps_ref[0]
    x_new = x + m_ref[0, 0, 2:3, :] * out
    x_o[...] = x_new
    for sub_tile in range(TP // TM):
        _route_tail(sub_tile, x_new[sub_tile * TM:(sub_tile + 1) * TM, :],
                    m_ref, n2_ref, rwt_ref, rbb_ref, tri_ref, *tail_outs, carry)


def _pool_mixer(l, x, mod, norm1_g, pool_w, pool_scale, norm2_g, rwt, rbb, tri):
    o = l // 2
    t_in, t_out, t_shape, t_scratch = _tail_specs(l, TP // TM)
    bands, sizes = _pool_tables()
    n_ctx = T_CTX // TP
    kind = lambda i: jnp.where(i < n_ctx, 0, 1)
    n_w, n_rb = len(POOL_WINDOWS), TP // POOL_RB
    in_specs = [
        pl.BlockSpec((TP, D), lambda i: (i, 0)),
        pl.BlockSpec((1, 1, N_MOD, D), lambda i: (l, _cond_of_tile(i * (TP // TM)), 0, 0)),
        pl.BlockSpec((1, 1, D), lambda i: (l, 0, 0)),
        pl.BlockSpec((1, n_w, n_rb, POOL_RB, 2 * POOL_RB), lambda i: (kind(i), 0, 0, 0, 0)),
        pl.BlockSpec((1, n_w, TP, 1), lambda i: (kind(i), 0, 0, 0)),
        pl.BlockSpec((1, n_w, POOL_DIM, POOL_DIM), lambda i: (o, 0, 0, 0)),
        pl.BlockSpec((1, 1, D), lambda i: (o, 0, 0)),
    ] + t_in
    return pl.pallas_call(
        _pool_kernel,
        grid=(T // TP,),
        in_specs=in_specs,
        out_specs=[pl.BlockSpec((TP, D), lambda i: (i, 0))] + t_out,
        out_shape=[_hbm_out((T, D), F32)] + t_shape,
        scratch_shapes=t_scratch,
        compiler_params=_cparams(("arbitrary",), 52),
        name="pool_route",
    )(_hbm(x), mod, norm1_g.reshape(DEPTH, 1, D), bands, sizes, pool_w,
      pool_scale.reshape(DEPTH // 2, 1, D),
      norm2_g.reshape(DEPTH, 1, D), rwt, rbb, tri)


def _start_run_copies(tile, nch_ref, loff_ref, gb_ref, make_copy):
    for e in range(N_EXPERTS):
        n = nch_ref[tile * N_EXPERTS + e]
        lo = loff_ref[tile * N_EXPERTS + e]
        gb = gb_ref[tile * N_EXPERTS + e] + e * EXPERT_CAP

        def issue(c, carry, lo=lo, gb=gb):
            make_copy(pl.multiple_of(lo + c * (2 * ROW_CHUNK), ROW_CHUNK),
                      pl.multiple_of(gb + c * (2 * ROW_CHUNK), ROW_CHUNK), 2 * ROW_CHUNK).start()
            return carry

        lax.fori_loop(0, lax.shift_right_logical(n, 1), issue, 0)

        @pl.when((n & 1) == 1)
        def _(n=n, lo=lo, gb=gb):
            make_copy(pl.multiple_of(lo + (n - 1) * ROW_CHUNK, ROW_CHUNK),
                      pl.multiple_of(gb + (n - 1) * ROW_CHUNK, ROW_CHUNK), ROW_CHUNK).start()


def _wait_run_copies(tile, nch_ref, make_copy):
    doubles, singles = 0, 0
    for e in range(N_EXPERTS):
        n = nch_ref[tile * N_EXPERTS + e]
        doubles = doubles + lax.shift_right_logical(n, 1)
        singles = singles + (n & 1)

    def drain(rows):
        def body(c, carry):
            make_copy(0, 0, rows).wait()
            return carry
        return body

    lax.fori_loop(0, doubles, drain(2 * ROW_CHUNK), 0)
    lax.fori_loop(0, singles, drain(ROW_CHUNK), 0)


def _dispatch_kernel(nch_ref, loff_ref, gb_ref, h_ref, lp_ref, xs_ref, sbuf, sems):
    i = pl.program_id(0)
    slot = i % 2
    r = lax.broadcasted_iota(I32, (SORTED_ROWS, TM), 0)
    for sub in range(MOVE_SUBS):
        tok = slice(sub * TM, (sub + 1) * TM)
        hit = (r == lp_ref[0:1, tok]) | (r == lp_ref[1:2, tok])
        perm = jnp.where(hit, 1.0, 0.0).astype(BF16)
        sbuf[slot, sub] = jnp.dot(perm, h_ref[tok, :], preferred_element_type=F32).astype(BF16)

    def copies_of(s, sub):
        def make_copy(local_row, global_row, rows):
            return pltpu.make_async_copy(sbuf.at[s, sub, pl.ds(local_row, rows)],
                                         xs_ref.at[pl.ds(global_row, rows)], sems.at[s])
        return make_copy

    @pl.when(i > 0)
    def _():
        for sub in range(MOVE_SUBS):
            _wait_run_copies((i - 1) * MOVE_SUBS + sub, nch_ref, copies_of(1 - slot, sub))

    for sub in range(MOVE_SUBS):
        _start_run_copies(i * MOVE_SUBS + sub, nch_ref, loff_ref, gb_ref, copies_of(slot, sub))

    @pl.when(i == N_TILES // MOVE_SUBS - 1)
    def _():
        for sub in range(MOVE_SUBS):
            _wait_run_copies(i * MOVE_SUBS + sub, nch_ref, copies_of(slot, sub))


def _dispatch(layout, h2, lp):
    rows = MOVE_SUBS * TM
    return pl.pallas_call(
        _dispatch_kernel,
        grid_spec=pltpu.PrefetchScalarGridSpec(
            num_scalar_prefetch=3,
            grid=(N_TILES // MOVE_SUBS,),
            in_specs=[pl.BlockSpec((rows, D), lambda i, *_: (i, 0)),
                      pl.BlockSpec((8, rows), lambda i, *_: (0, i))],
            out_specs=pl.BlockSpec(memory_space=pl.ANY),
            scratch_shapes=[pltpu.VMEM((2, MOVE_SUBS, SORTED_ROWS, D), BF16),
                            pltpu.SemaphoreType.DMA((2,))],
        ),
        out_shape=_hbm_out((N_EXPERTS * EXPERT_CAP, D), BF16),
        compiler_params=_cparams(("arbitrary",), 32),
        name="moe_dispatch",
    )(*layout, _hbm(h2), lp)


def _moe_kernel(l, rb_ref, e_ref, nv_ref, first_ref, ord_ref, nxt_ref, nxt2_ref,
                xs_ref, wg_hbm, wu_hbm, wd_hbm, y_ref, wgf, wuf, wdf, wgb, wub, wdb, sems):
    i = pl.program_id(0)
    n_valid = nv_ref[i]

    def weight_copies(e, slot):
        return (pltpu.make_async_copy(wg_hbm.at[l, e], wgf.at[slot], sems.at[slot, 0]),
                pltpu.make_async_copy(wu_hbm.at[l, e], wuf.at[slot], sems.at[slot, 1]),
                pltpu.make_async_copy(wd_hbm.at[l, e], wdf.at[slot], sems.at[slot, 2]))

    @pl.when(first_ref[i] == 1)
    def _():
        k = ord_ref[i]
        slot = k % W_SLOTS

        @pl.when(k == 0)
        def _():
            for cp in weight_copies(e_ref[i], 0):
                cp.start(priority=1)

            @pl.when(nxt_ref[i] >= 0)
            def _():
                for cp in weight_copies(nxt_ref[i], 1):
                    cp.start(priority=1)

        for cp in weight_copies(e_ref[i], slot):
            cp.wait()

        @pl.when(nxt2_ref[i] >= 0)
        def _():
            for cp in weight_copies(nxt2_ref[i], (k + 2) % W_SLOTS):
                cp.start(priority=1)

        wgb[...] = wgf[slot].astype(BF16)
        wub[...] = wuf[slot].astype(BF16)
        wdb[...] = wdf[slot].astype(BF16)

    @pl.when(n_valid > 0)
    def _():
        row = lax.broadcasted_iota(I32, (TE, 1), 0)
        x = jnp.where(row < n_valid, xs_ref[...], jnp.zeros((), BF16))
        gate = jnp.dot(x, wgb[...], preferred_element_type=F32)
        up = jnp.dot(x, wub[...], preferred_element_type=F32)
        act = (gate * _sigmoid(gate) * up).astype(BF16)
        y_ref[...] = jnp.dot(act, wdb[...], preferred_element_type=F32).astype(BF16)


def _experts(l, tile_map, xs, w_gate, w_up, w_down):
    n_pre = len(tile_map)
    blk = lambda i, rb, *_: (rb[i], 0)
    return pl.pallas_call(
        functools.partial(_moe_kernel, l),
        grid_spec=pltpu.PrefetchScalarGridSpec(
            num_scalar_prefetch=n_pre,
            grid=(MOE_TILES,),
            in_specs=[
                pl.BlockSpec((TE, D), blk),
                pl.BlockSpec(memory_space=pl.ANY),
                pl.BlockSpec(memory_space=pl.ANY),
                pl.BlockSpec(memory_space=pl.ANY),
            ],
            out_specs=pl.BlockSpec((TE, D), blk),
            scratch_shapes=[pltpu.VMEM((W_SLOTS, D, D_FF), F32), pltpu.VMEM((W_SLOTS, D, D_FF), F32),
                            pltpu.VMEM((W_SLOTS, D_FF, D), F32),
                            pltpu.VMEM((D, D_FF), BF16), pltpu.VMEM((D, D_FF), BF16),
                            pltpu.VMEM((D_FF, D), BF16),
                            pltpu.SemaphoreType.DMA((W_SLOTS, 3))],
        ),
        out_shape=_hbm_out((N_EXPERTS * EXPERT_CAP, D), BF16),
        compiler_params=_cparams(("arbitrary",), 48),
        name="moe_experts",
    )(*tile_map, _hbm(xs), w_gate, w_up, w_down)


def _combine_kernel(split_out, nch_ref, loff_ref, gb_ref, x_ref, m_ref, lpt_ref, w_ref, y_ref, *refs):
    *o_refs, ybuf, sems = refs
    i = pl.program_id(0)
    slot = i % 2

    def copies_of(s, sub):
        def make_copy(local_row, global_row, rows):
            return pltpu.make_async_copy(y_ref.at[pl.ds(global_row, rows)],
                                         ybuf.at[s, sub, pl.ds(local_row, rows)], sems.at[s])
        return make_copy

    @pl.when(i == 0)
    def _():
        ybuf[...] = jnp.zeros_like(ybuf)
        for sub in range(MOVE_SUBS):
            _start_run_copies(sub, nch_ref, loff_ref, gb_ref, copies_of(slot, sub))

    @pl.when(i + 1 < N_TILES // MOVE_SUBS)
    def _():
        for sub in range(MOVE_SUBS):
            _start_run_copies((i + 1) * MOVE_SUBS + sub, nch_ref, loff_ref, gb_ref, copies_of(1 - slot, sub))

    for sub in range(MOVE_SUBS):
        _wait_run_copies(i * MOVE_SUBS + sub, nch_ref, copies_of(slot, sub))

    col = lax.broadcasted_iota(I32, (TM, SORTED_ROWS), 1)
    parts = []
    for sub in range(MOVE_SUBS):
        tok = slice(sub * TM, (sub + 1) * TM)
        pick = jnp.where(col == lpt_ref[tok, 0:1], w_ref[tok, 0:1],
                         jnp.where(col == lpt_ref[tok, 1:2], w_ref[tok, 1:2], 0.0)).astype(BF16)
        parts.append(jnp.dot(pick, ybuf[slot, sub], preferred_element_type=F32))
    res = x_ref[...] + m_ref[0, 0, 5:6, :] * jnp.concatenate(parts, axis=0)
    if split_out:
        @pl.when(i < T_CTX // (MOVE_SUBS * TM))
        def _():
            o_refs[0][...] = res

        @pl.when(i >= T_CTX // (MOVE_SUBS * TM))
        def _():
            o_refs[1][...] = res
    else:
        o_refs[0][...] = res


def _combine(l, layout, x, mod, lpt, wcol, y, split_out):
    rows = MOVE_SUBS * TM
    n_ctx = T_CTX // rows
    if split_out:
        out_specs = [pl.BlockSpec((rows, D), lambda i, *_: (jnp.minimum(i, n_ctx - 1), 0)),
                     pl.BlockSpec((rows, D), lambda i, *_: (jnp.maximum(i - n_ctx, 0), 0))]
        out_shape = [jax.ShapeDtypeStruct((T_CTX, D), F32), jax.ShapeDtypeStruct((T_LAT, D), F32)]
        aliases = {}
    else:
        out_specs = pl.BlockSpec((rows, D), lambda i, *_: (i, 0))
        out_shape = _hbm_out((T, D), F32)
        aliases = {3: 0}
    return pl.pallas_call(
        functools.partial(_combine_kernel, split_out),
        grid_spec=pltpu.PrefetchScalarGridSpec(
            num_scalar_prefetch=3,
            grid=(N_TILES // MOVE_SUBS,),
            in_specs=[
                pl.BlockSpec((rows, D), lambda i, *_: (i, 0)),
                pl.BlockSpec((1, 1, N_MOD, D), lambda i, *_: (l, _cond_of_tile(i * MOVE_SUBS), 0, 0)),
                pl.BlockSpec((rows, 8), lambda i, *_: (i, 0)),
                pl.BlockSpec((rows, 8), lambda i, *_: (i, 0)),
                pl.BlockSpec(memory_space=pl.ANY),
            ],
            out_specs=out_specs,
            scratch_shapes=[pltpu.VMEM((2, MOVE_SUBS, SORTED_ROWS, D), BF16),
                            pltpu.SemaphoreType.DMA((2,))],
        ),
        out_shape=out_shape,
        input_output_aliases=aliases,
        compiler_params=_cparams(("arbitrary",), 48),
        name="moe_combine",
    )(*layout, _hbm(x), mod, lpt, wcol, _hbm(y))


def _tile_map(cnt):
    cnt = cnt[:, 0]
    ids = jnp.arange(N_EXPERTS, dtype=I32)
    n_tiles = (cnt + TE - 1) // TE
    cum = jnp.cumsum(n_tiles)
    total = cum[-1]
    i = jnp.arange(MOE_TILES, dtype=I32)
    ii = jnp.minimum(i, total - 1)
    e = jnp.sum((ii[:, None] >= cum[None, :]).astype(I32), axis=1)
    onehot = (e[:, None] == ids[None, :]).astype(I32)
    pick = lambda v: jnp.sum(onehot * v[None, :], axis=1)
    j = ii - pick(cum - n_tiles)
    active = i < total
    n_valid = jnp.where(active, jnp.clip(pick(cnt) - j * TE, 0, TE), 0)
    first = (active & (j == 0)).astype(I32)
    nonempty = n_tiles > 0
    ordinal = jnp.cumsum(nonempty.astype(I32)) - 1
    later = jnp.where(nonempty[None, :] & (ids[None, :] > ids[:, None]), ids[None, :], N_EXPERTS)
    nxt = jnp.min(later, axis=1)
    nxt_of = lambda v: jnp.sum((v[:, None] == ids[None, :]).astype(I32) * nxt[None, :], axis=1)
    nxt2 = jnp.where(nxt == N_EXPERTS, N_EXPERTS, nxt_of(nxt))
    none = lambda v: jnp.where(v == N_EXPERTS, -1, v)
    as_i32 = lambda v: v.astype(I32)
    return tuple(map(as_i32, (e * CAP_TILES + j, e, n_valid, first, pick(ordinal),
                              pick(none(nxt)), pick(none(nxt2)))))


def _moe(l, x, tail, mod, w_gate, w_up, w_down, split_out):
    h2, lp, rf, nch, loff, gb, cnt = tail
    layout = tuple(a[:, :, 0].reshape(N_TILES * N_EXPERTS) for a in (nch, loff, gb))
    xs = _dispatch(layout, h2, lp)
    y = _experts(l, _tile_map(cnt), xs, w_gate, w_up, w_down)
    return _combine(l, layout, x, mod, lp.T, rf.T, y, split_out)


def _rope_tables():
    t = jnp.arange(LAT_S)
    row = (t // GRID_W).astype(F32)
    col = (t % GRID_W).astype(F32)
    nf = HEAD_DIM // 4
    inv = ROPE_BASE ** (-jnp.arange(nf, dtype=F32) / nf)
    ar = row[:, None] * inv[None, :]
    ac = col[:, None] * inv[None, :]
    cos = jnp.concatenate([jnp.cos(ar), jnp.cos(ar), jnp.cos(ac), jnp.cos(ac)], axis=1)
    sin = jnp.concatenate([-jnp.sin(ar), jnp.sin(ar), -jnp.sin(ac), jnp.sin(ac)], axis=1)
    reps = QK_W // HEAD_DIM
    return jnp.tile(cos, (1, reps)), jnp.tile(sin, (1, reps))


def kernel(x_prompt, x_sample, cache_k, cache_v, c, c_ctx, norm1_g, norm2_g, w_ada, b_ada, w_in, w_out,
           q_norm_g, k_norm_g, lam_q1, lam_k1, lam_q2, lam_k2, subln_g, sg_w, sg_b, pool_w, pool_scale,
           router_w, router_b, w_gate, w_up, w_down):
    x = (x_prompt.reshape(T_CTX, D), x_sample.reshape(T_LAT, D))
    cond = jnp.concatenate([c_ctx[None, :], c], axis=0)
    mod = _modulation(cond, w_ada, b_ada)

    seg_id = np.arange(QK_W) // HEAD_DIM
    seg = jnp.asarray((seg_id[:, None] == seg_id[None, :]) / HEAD_DIM, dtype=BF16)
    tri = jnp.asarray(np.arange(TM)[:, None] < np.arange(TM)[None, :], dtype=BF16)
    rope_tabs = _rope_tables()
    rwt = router_w.T
    rbb = jnp.broadcast_to(router_b[:, None], (N_EXPERTS, TM))
    states = None

    for l in range(DEPTH):
        if l % 2 == 0:
            e = l // 2
            qg = jnp.tile(q_norm_g[e], QK_W // HEAD_DIM)[None, :]
            kg = jnp.tile(k_norm_g[e], QK_W // HEAD_DIM)[None, :]
            sgb = jnp.broadcast_to(sg_b[:, :, :, None], (DEPTH // 2, SG_GROUPS, SG_CHUNK, 128))
            x_c, x_l = x if isinstance(x, tuple) else (x, x)
            common = (mod, norm1_g, w_in, qg, kg, seg, sg_w, sgb)
            q_c, k_c, v_c, sg_c, *states = _front(False, l, x_c, *common, None, states)
            q_l, k_l, v_l, sg_l = _front(True, l, x_l, *common, rope_tabs, None)
            lam = (lam_q1, lam_k1, lam_q2, lam_k2, subln_g)
            att_c = _attention(False, l, q_c, k_c, v_c, None, None, *lam)
            att_l = _attention(True, l, q_l, k_l, v_l, cache_k, cache_v, *lam)
            x, *tail = _out_proj(l, x, mod, att_c, att_l, sg_c, sg_l, w_out, norm2_g, rwt, rbb, tri)
        else:
            x, *tail = _pool_mixer(l, x, mod, norm1_g, pool_w, pool_scale, norm2_g, rwt, rbb, tri)
        x = _moe(l, x, tail, mod, w_gate, w_up, w_down, split_out=(l == DEPTH - 1))
    state_k, state_v = states

    y_prompt = x[0].reshape(N_CTX_B, CTX_S, D)
    y_sample = x[1].reshape(N_LAT_B, LAT_S, D)
    state_k = state_k.reshape(N_CTX_B, DEPTH // 2, CTX_S, N_HEADS, 2, HEAD_DIM)
    return y_prompt, y_sample, state_k, state_v
```

```python
import functools
import math

import jax
import jax.numpy as jnp
import numpy as np
from jax import lax
from jax.experimental import pallas as pl
from jax.experimental.pallas import tpu as pltpu

F32 = jnp.float32
BF16 = jnp.bfloat16
I32 = jnp.int32

D = 1024
N_CTX_B, CTX_S = 16, 256
N_LAT_B, LAT_S = 2, 1024
PAST = 512
DEPTH = 4
T_CTX = N_CTX_B * CTX_S
T_LAT = N_LAT_B * LAT_S
T = T_CTX + T_LAT
GRID_W = 64
N_HEADS = 4
HEAD_DIM = 64
VAL_DIM = 128
QK_W = 512
MIX_A = 512
MIX_B = 512
IN_W = 2560
SG_GROUPS = 4
SG_CHUNK = 128
POOL_WINDOWS = (2, 4, 8, 16)
POOL_DIM = 256
POOL_HALO = 8
N_EXPERTS = 16
N_GROUPS = 4
PER_GROUP = 4
D_FF = 512
N_MOD = 6
N_COND = 1 + N_LAT_B
EPS = 1e-6
ROPE_BASE = 10000.0

TM = 256
N_TILES = T // TM
N_CTX_TILES = T_CTX // TM
LAT_TILES_PER_SEQ = LAT_S // TM
ROW_CHUNK = 16
PAD_PER_RUN = ROW_CHUNK - 1
SORTED_ROWS = -(-(2 * TM + N_EXPERTS * PAD_PER_RUN) // 128) * 128
TE = 512
W_SLOTS = 3
ATTN_CTX_SEQS = 2
ATTN_LAT_HEADS = 4
MOVE_SUBS = 2
TF = 512
TO = 512
TP = LAT_S
POOL_RB = 128
CAP_TILES = -(-(T + N_TILES * PAD_PER_RUN) // TE)
EXPERT_CAP = CAP_TILES * TE
MOE_TILES = (2 * T + N_EXPERTS * N_TILES * PAD_PER_RUN) // TE + N_EXPERTS
MOD_TN = 2048
MIB = 1024 * 1024

NT_DIMS = (((1,), (1,)), ((), ()))


def _cparams(sem, vmem_mib):
    return pltpu.CompilerParams(dimension_semantics=sem, vmem_limit_bytes=vmem_mib * MIB)


def _hbm(x):
    try:
        return pltpu.with_memory_space_constraint(x, pltpu.HBM)
    except ValueError:
        return x


def _hbm_out(shape, dtype):
    return pltpu.HBM(shape, dtype)


def _sigmoid(x):
    return 1.0 / (1.0 + jnp.exp(-x))


def _gelu_tanh(x):
    c = math.sqrt(2.0 / math.pi)
    return x * (0.5 * (1.0 + jnp.tanh(c * (x + 0.044715 * (x * x * x)))))


def _modnorm(x, g, scale, shift):
    ms = jnp.mean(x * x, axis=-1, keepdims=True)
    return (x * lax.rsqrt(ms + EPS)) * (g * (1.0 + scale)) + shift


def _cond_of_tile(i):
    return jnp.where(i < N_CTX_TILES, 0, 1 + (i - N_CTX_TILES) // LAT_TILES_PER_SEQ)


def _mod_kernel(c_ref, w_ref, b_ref, o_ref):
    c = c_ref[...]
    s = (c * _sigmoid(c)).astype(BF16)
    o_ref[0] = jnp.dot(s, w_ref[0].astype(BF16), preferred_element_type=F32) + b_ref[0]


def _modulation(cond, w_ada, b_ada):
    cond8 = jnp.concatenate([cond, jnp.zeros((8 - N_COND, D), F32)], axis=0)
    out = pl.pallas_call(
        _mod_kernel,
        grid=(DEPTH, N_MOD * D // MOD_TN),
        in_specs=[
            pl.BlockSpec((8, D), lambda l, n: (0, 0)),
            pl.BlockSpec((1, D, MOD_TN), lambda l, n: (l, 0, n)),
            pl.BlockSpec((1, 1, MOD_TN), lambda l, n: (l, 0, n)),
        ],
        out_specs=pl.BlockSpec((1, 8, MOD_TN), lambda l, n: (l, 0, n)),
        out_shape=jax.ShapeDtypeStruct((DEPTH, 8, N_MOD * D), F32),
        compiler_params=_cparams(("arbitrary", "arbitrary"), 40),
        name="adaln_mod",
    )(cond8, w_ada, b_ada.reshape(DEPTH, 1, N_MOD * D))
    return out.reshape(DEPTH, 8, N_MOD, D)


def _front_kernel(latent, n_state_in, *refs):
    (x_ref, m_ref, n1_ref, w_ref, qg_ref, kg_ref, seg_ref, sgw_ref, sgb_ref), refs = refs[:9], refs[9:]
    if latent:
        cos_ref, sin_ref, q_o, k_o, v_o, sg_o, wbf = refs
    else:
        q_o, k_o, v_o, sg_o, sk_o, sv_o, wbf = refs[n_state_in:]

    @pl.when(pl.program_id(0) == 0)
    def _():
        for c0 in range(0, IN_W, 512):
            wbf[:, c0:c0 + 512] = w_ref[0, :, c0:c0 + 512].astype(BF16)

    h = _modnorm(x_ref[...], n1_ref[0], m_ref[0, 0, 1:2, :], m_ref[0, 0, 0:1, :])
    proj = jnp.dot(h.astype(BF16), wbf[...], preferred_element_type=F32)

    def qk_norm(z, g):
        ms = jnp.dot((z * z).astype(BF16), seg_ref[...], preferred_element_type=F32)
        return z * lax.rsqrt(ms + EPS) * g

    q = qk_norm(proj[:, 0:QK_W], qg_ref[...])
    k = qk_norm(proj[:, QK_W:2 * QK_W], kg_ref[...])
    v = proj[:, 2 * QK_W:2 * QK_W + MIX_A]

    if latent:
        lane = lax.broadcasted_iota(I32, (1, QK_W), 1)
        first = (lane % 32) < 16

        def rope(z):
            partner = jnp.where(first, pltpu.roll(z, QK_W - 16, 1), pltpu.roll(z, 16, 1))
            return z * cos_ref[...] + partner * sin_ref[...]

        q = rope(q)
        k = rope(k)
    else:
        for b in range(TF // CTX_S):
            sk_o[b, 0] = k[b * CTX_S:(b + 1) * CTX_S, :]
            for hd in range(N_HEADS):
                sv_o[b, 0, :, hd, :] = v[b * CTX_S:(b + 1) * CTX_S, hd * VAL_DIM:(hd + 1) * VAL_DIM]

    q_o[...] = (q * (HEAD_DIM ** -0.5)).astype(BF16)
    k_o[...] = k.astype(BF16)
    v_o[...] = v.astype(BF16)

    gu0 = 2 * QK_W + MIX_A
    gv0 = gu0 + MIX_B
    for g in range(SG_GROUPS):
        wg = sgw_ref[0, g].astype(BF16)
        for n in range(TF // SG_CHUNK):
            rows = slice(n * SG_CHUNK, (n + 1) * SG_CHUNK)
            cu = slice(gu0 + g * 128, gu0 + (g + 1) * 128)
            cv = slice(gv0 + g * 128, gv0 + (g + 1) * 128)
            gv = _gelu_tanh(proj[rows, cv]).astype(BF16)
            mixed = jnp.dot(wg, gv, preferred_element_type=F32) + sgb_ref[0, g]
            sg_o[rows, g * 128:(g + 1) * 128] = (_gelu_tanh(proj[rows, cu]) * mixed).astype(BF16)


def _front(latent, l, x, mod, norm1_g, w_in, qg, kg, seg, sg_w, sgb, rope_tabs, states):
    e = l // 2
    rows = T_LAT if latent else T_CTX
    n_tiles = rows // TF
    off = T_CTX // TF if (latent and x.shape[0] == T) else 0
    tiles_per_seq = LAT_S // TF
    if latent:
        cond_map = lambda i: (l, 1 + i // tiles_per_seq, 0, 0)
    else:
        cond_map = lambda i: (l, 0, 0, 0)
    in_specs = [
        pl.BlockSpec((TF, D), lambda i: (i + off, 0)),
        pl.BlockSpec((1, 1, N_MOD, D), cond_map),
        pl.BlockSpec((1, 1, D), lambda i: (l, 0, 0)),
        pl.BlockSpec((1, D, IN_W), lambda i: (e, 0, 0)),
        pl.BlockSpec((1, QK_W), lambda i: (0, 0)),
        pl.BlockSpec((1, QK_W), lambda i: (0, 0)),
        pl.BlockSpec((QK_W, QK_W), lambda i: (0, 0)),
        pl.BlockSpec((1, SG_GROUPS, SG_CHUNK, SG_CHUNK), lambda i: (e, 0, 0, 0)),
        pl.BlockSpec((1, SG_GROUPS, SG_CHUNK, 128), lambda i: (e, 0, 0, 0)),
    ]
    args = [_hbm(x), mod, norm1_g.reshape(DEPTH, 1, D), w_in, qg, kg, seg, sg_w, sgb]
    tok = lambda: pl.BlockSpec((TF, QK_W), lambda i: (i, 0))
    out_specs = [tok(), tok(), tok(), tok()]
    out_shape = [_hbm_out((rows, QK_W), BF16)] * 4
    aliases = {}
    if latent:
        cos_t, sin_t = rope_tabs
        in_specs += [pl.BlockSpec((TF, QK_W), lambda i: (i % tiles_per_seq, 0))] * 2
        args += [cos_t, sin_t]
    else:
        if states:
            in_specs += [pl.BlockSpec(memory_space=pl.ANY)] * 2
            args += list(states)
            aliases = {9: 4, 10: 5}
        out_specs += [pl.BlockSpec((TF // CTX_S, 1, CTX_S, QK_W), lambda i: (i, e, 0, 0)),
                      pl.BlockSpec((TF // CTX_S, 1, CTX_S, N_HEADS, VAL_DIM), lambda i: (i, e, 0, 0, 0))]
        out_shape += [_hbm_out((N_CTX_B, DEPTH // 2, CTX_S, QK_W), F32),
                      _hbm_out((N_CTX_B, DEPTH // 2, CTX_S, N_HEADS, VAL_DIM), F32)]
    return pl.pallas_call(
        functools.partial(_front_kernel, latent, len(aliases)),
        grid=(n_tiles,),
        in_specs=in_specs,
        out_specs=out_specs,
        out_shape=out_shape,
        scratch_shapes=[pltpu.VMEM((D, IN_W), BF16)],
        input_output_aliases=aliases,
        compiler_params=_cparams(("arbitrary",), 48),
        name="front_lat" if latent else "front_ctx",
    )(*args)


def _attn_kernel(has_cache, seqs, heads, lam0, *refs):
    if has_cache:
        (q_ref, k_ref, v_ref, ck_ref, cv_ref, lq1, lk1, lq2, lk2, sub_ref, o_ref) = refs
    else:
        (q_ref, k_ref, v_ref, lq1, lk1, lq2, lk2, sub_ref, o_ref) = refs

    lane = lax.broadcasted_iota(I32, (1, VAL_DIM), 1)
    lam = (jnp.exp(jnp.sum(lq1[0] * lk1[0], axis=-1, keepdims=True))
           - jnp.exp(jnp.sum(lq2[0] * lk2[0], axis=-1, keepdims=True)) + lam0)
    q_rows = q_ref.shape[0] // seqs
    k_rows = k_ref.shape[0] // seqs

    for sq, h in [(sq, h) for sq in range(seqs) for h in range(heads)]:
        hs = slice(h * VAL_DIM, (h + 1) * VAL_DIM)
        qr = slice(sq * q_rows, (sq + 1) * q_rows)
        kr = slice(sq * k_rows, (sq + 1) * k_rows)
        qf = q_ref[qr, hs].astype(F32)
        qc = [jnp.where(lane < HEAD_DIM, qf, 0.0).astype(BF16),
              jnp.where(lane >= HEAD_DIM, qf, 0.0).astype(BF16)]
        keys = [k_ref[kr, hs]]
        vals = [v_ref[kr, hs]]
        if has_cache:
            keys.append(ck_ref[0, 0, :, hs].astype(BF16))
            vals.append(cv_ref[0, 0, :, h, :].astype(BF16))

        probs = []
        for c in range(2):
            s = [lax.dot_general(qc[c], kk, NT_DIMS, preferred_element_type=F32) for kk in keys]
            m = s[0].max(axis=-1, keepdims=True)
            for sx in s[1:]:
                m = jnp.maximum(m, sx.max(axis=-1, keepdims=True))
            p = [jnp.exp(sx - m) for sx in s]
            den = p[0].sum(axis=-1, keepdims=True)
            for px in p[1:]:
                den = den + px.sum(axis=-1, keepdims=True)
            probs.append((p, den))

        r0 = 1.0 / probs[0][1]
        r1 = lam / probs[1][1]
        o = None
        for idx in range(len(keys)):
            a = (probs[0][0][idx] * r0 - probs[1][0][idx] * r1).astype(BF16)
            t = jnp.dot(a, vals[idx], preferred_element_type=F32)
            o = t if o is None else o + t
        ms = jnp.mean(o * o, axis=-1, keepdims=True)
        o_ref[qr, hs] = ((o * lax.rsqrt(ms + EPS) * sub_ref[0]) * (1.0 - lam0)).astype(BF16)


def _attention(has_cache, l, q, k, v, cache_k, cache_v, lam_q1, lam_k1, lam_q2, lam_k2, subln_g):
    e = l // 2
    lam0 = 0.8 - 0.6 * math.exp(-0.3 * l)
    n_b, s_len = (N_LAT_B, LAT_S) if has_cache else (N_CTX_B, CTX_S)
    heads = ATTN_LAT_HEADS if has_cache else N_HEADS
    seqs = 1 if has_cache else ATTN_CTX_SEQS
    width = heads * VAL_DIM
    nq = s_len // TM
    kv = lambda: pl.BlockSpec((seqs * s_len, width), lambda b, h, qi: (b, h))
    par = lambda: pl.BlockSpec((1, 1, HEAD_DIM), lambda b, h, qi: (e, 0, 0))
    in_specs = [pl.BlockSpec((seqs * TM, width), lambda b, h, qi: (b * nq + qi, h)), kv(), kv()]
    args = [_hbm(q), _hbm(k), _hbm(v)]
    if has_cache:
        in_specs += [pl.BlockSpec((1, 1, PAST, width), lambda b, h, qi: (b, e, 0, h)),
                     pl.BlockSpec((1, 1, PAST, heads, VAL_DIM), lambda b, h, qi: (b, e, 0, h, 0))]
        args += [cache_k.reshape(N_LAT_B, DEPTH // 2, PAST, QK_W), cache_v]
    in_specs += [par(), par(), par(), par(),
                 pl.BlockSpec((1, 1, VAL_DIM), lambda b, h, qi: (e, 0, 0))]
    r3 = lambda a: a.reshape(DEPTH // 2, 1, -1)
    args += [r3(lam_q1), r3(lam_k1), r3(lam_q2), r3(lam_k2), r3(subln_g)]
    return pl.pallas_call(
        functools.partial(_attn_kernel, has_cache, seqs, heads, lam0),
        grid=(n_b // seqs, N_HEADS // heads, nq),
        in_specs=in_specs,
        out_specs=pl.BlockSpec((seqs * TM, width), lambda b, h, qi: (b * nq + qi, h)),
        out_shape=_hbm_out((n_b * s_len, MIX_A), BF16),
        compiler_params=_cparams(("arbitrary", "arbitrary", "arbitrary"), 32),
        name="attn_lat" if has_cache else "attn_ctx",
    )(*args)


def _route_tail(sub_tile, x_new, m_ref, n2_ref, rwt_ref, rbb_ref, tri_ref,
                h2_o, lp_o, rf_o, nch_o, loff_o, gb_o, cnt_o, carry):
    tok = slice(sub_tile * TM, (sub_tile + 1) * TM)
    if sub_tile == 0:
        @pl.when(pl.program_id(0) == 0)
        def _():
            carry[...] = jnp.zeros_like(carry)

    h2 = _modnorm(x_new, n2_ref[0], m_ref[0, 0, 4:5, :], m_ref[0, 0, 3:4, :])

    hh = h2.astype(BF16)
    h2_o[tok, :] = hh
    hl = (h2 - hh.astype(F32)).astype(BF16)
    rw = rwt_ref[...]
    wh = rw.astype(BF16)
    wl = (rw - wh.astype(F32)).astype(BF16)
    dg = lambda a, b: lax.dot_general(a, b, NT_DIMS, preferred_element_type=F32)
    logits = dg(wh, hh) + dg(wh, hl) + dg(wl, hh)
    score = _sigmoid(logits)
    sel = score + rbb_ref[...]

    neg_inf = jnp.full((1, TM), -jnp.inf, F32)
    grp = []
    for g in range(N_GROUPS):
        s_rows = [sel[PER_GROUP * g + j:PER_GROUP * g + j + 1, :] for j in range(PER_GROUP)]
        c_rows = [score[PER_GROUP * g + j:PER_GROUP * g + j + 1, :] for j in range(PER_GROUP)]
        f_idx = jnp.zeros((1, TM), I32)
        f_val, f_sc = s_rows[0], c_rows[0]
        for j in range(1, PER_GROUP):
            better = s_rows[j] > f_val
            f_idx = jnp.where(better, j, f_idx)
            f_val = jnp.where(better, s_rows[j], f_val)
            f_sc = jnp.where(better, c_rows[j], f_sc)
        s_idx = jnp.zeros((1, TM), I32)
        s_val, s_sc = neg_inf, jnp.zeros((1, TM), F32)
        for j in range(PER_GROUP):
            better = (f_idx != j) & (s_rows[j] > s_val)
            s_idx = jnp.where(better, j, s_idx)
            s_val = jnp.where(better, s_rows[j], s_val)
            s_sc = jnp.where(better, c_rows[j], s_sc)
        grp.append((f_val + s_val, PER_GROUP * g + f_idx, PER_GROUP * g + s_idx, f_sc, s_sc))

    best, e0, e1, c0, c1 = grp[0]
    for g in range(1, N_GROUPS):
        better = grp[g][0] > best
        best = jnp.where(better, grp[g][0], best)
        e0 = jnp.where(better, grp[g][1], e0)
        e1 = jnp.where(better, grp[g][2], e1)
        c0 = jnp.where(better, grp[g][3], c0)
        c1 = jnp.where(better, grp[g][4], c1)
    den = c0 + c1
    w0 = c0 / den
    w1 = c1 / den

    sub = lax.broadcasted_iota(I32, (N_EXPERTS, TM), 0)
    hit0 = sub == jnp.broadcast_to(e0, (N_EXPERTS, TM))
    hit1 = sub == jnp.broadcast_to(e1, (N_EXPERTS, TM))
    onehot = jnp.where(hit0 | hit1, 1.0, 0.0)
    prefix = jnp.dot(onehot.astype(BF16), tri_ref[...], preferred_element_type=F32)
    count = jnp.sum(onehot, axis=1, keepdims=True)
    padded = jnp.floor((count + (ROW_CHUNK - 1)) * (1.0 / ROW_CHUNK)) * ROW_CHUNK

    sub1 = lax.broadcasted_iota(I32, (N_EXPERTS, 1), 0)
    run = jnp.zeros((1, 1), F32)
    local_off = jnp.zeros((N_EXPERTS, 1), F32)
    for e in range(N_EXPERTS):
        local_off = jnp.where(sub1 == e, run, local_off)
        run = run + padded[e:e + 1, :]
    place = prefix + local_off
    lpos0 = jnp.sum(jnp.where(hit0, place, 0.0), axis=0, keepdims=True).astype(I32)
    lpos1 = jnp.sum(jnp.where(hit1, place, 0.0), axis=0, keepdims=True).astype(I32)

    lanes = lambda col: jnp.broadcast_to(col, (N_EXPERTS, 128)).astype(I32)
    nch_o[sub_tile] = lanes(padded * (1.0 / ROW_CHUNK))
    loff_o[sub_tile] = lanes(local_off)
    gb_o[sub_tile] = carry[...].astype(I32)
    carry[...] = carry[...] + padded
    cnt_o[...] = carry[...].astype(I32)

    sub8 = lax.broadcasted_iota(I32, (8, TM), 0)
    bc = lambda r: jnp.broadcast_to(r, (8, TM))
    lp_o[:, tok] = jnp.where(sub8 == 0, bc(lpos0), jnp.where(sub8 == 1, bc(lpos1), 0))
    rf_o[:, tok] = jnp.where(sub8 == 0, bc(w0), jnp.where(sub8 == 1, bc(w1), 0.0))


def _tail_specs(l, sub_tiles):
    rows = sub_tiles * TM
    in_specs = [
        pl.BlockSpec((1, 1, D), lambda i: (l, 0, 0)),
        pl.BlockSpec((N_EXPERTS, D), lambda i: (0, 0)),
        pl.BlockSpec((N_EXPERTS, TM), lambda i: (0, 0)),
        pl.BlockSpec((TM, TM), lambda i: (0, 0)),
    ]
    per_tile = lambda: pl.BlockSpec((sub_tiles, N_EXPERTS, 128), lambda i: (i, 0, 0))
    out_specs = [
        pl.BlockSpec((rows, D), lambda i: (i, 0)),
        pl.BlockSpec((8, rows), lambda i: (0, i)),
        pl.BlockSpec((8, rows), lambda i: (0, i)),
        per_tile(), per_tile(), per_tile(),
        pl.BlockSpec((N_EXPERTS, 128), lambda i: (0, 0)),
    ]
    out_shape = [
        _hbm_out((T, D), BF16),
        jax.ShapeDtypeStruct((8, T), I32),
        jax.ShapeDtypeStruct((8, T), F32),
        jax.ShapeDtypeStruct((N_TILES, N_EXPERTS, 128), I32),
        jax.ShapeDtypeStruct((N_TILES, N_EXPERTS, 128), I32),
        jax.ShapeDtypeStruct((N_TILES, N_EXPERTS, 128), I32),
        jax.ShapeDtypeStruct((N_EXPERTS, 128), I32),
    ]
    scratch = [pltpu.VMEM((N_EXPERTS, 128), F32)]
    return in_specs, out_specs, out_shape, scratch


def _out_kernel(split_x, *refs):
    x_refs, refs = refs[:1 + split_x], refs[1 + split_x:]
    (m_ref, ac_ref, al_ref, sc_ref, sl_ref, w_ref, n2_ref, rwt_ref, rbb_ref, tri_ref,
     x_o, *tail_refs) = refs
    *tail_outs, wbf, carry = tail_refs
    i = pl.program_id(0)

    @pl.when(i == 0)
    def _():
        wbf[...] = w_ref[0].astype(BF16)

    is_ctx = i < T_CTX // TO
    x = jnp.where(is_ctx, x_refs[0][...], x_refs[1][...]) if split_x else x_refs[0][...]
    att = jnp.where(is_ctx, ac_ref[...], al_ref[...])
    sgo = jnp.where(is_ctx, sc_ref[...], sl_ref[...])
    out = (jnp.dot(att, wbf[0:MIX_A, :], preferred_element_type=F32)
           + jnp.dot(sgo, wbf[MIX_A:MIX_A + MIX_B, :], preferred_element_type=F32))
    x_new = x + m_ref[0, 0, 2:3, :] * out
    x_o[...] = x_new
    for sub_tile in range(TO // TM):
        _route_tail(sub_tile, x_new[sub_tile * TM:(sub_tile + 1) * TM, :],
                    m_ref, n2_ref, rwt_ref, rbb_ref, tri_ref, *tail_outs, carry)


def _out_proj(l, x, mod, att_c, att_l, sg_c, sg_l, w_out, norm2_g, rwt, rbb, tri):
    e = l // 2
    t_in, t_out, t_shape, t_scratch = _tail_specs(l, TO // TM)
    n_ctx = T_CTX // TO
    ctx = lambda w: pl.BlockSpec((TO, w), lambda i: (jnp.minimum(i, n_ctx - 1), 0))
    lat = lambda w: pl.BlockSpec((TO, w), lambda i: (jnp.maximum(i - n_ctx, 0), 0))
    split_x = isinstance(x, tuple)
    if split_x:
        x_specs, x_args, aliases = [ctx(D), lat(D)], [_hbm(x[0]), _hbm(x[1])], {}
    else:
        x_specs, x_args, aliases = [pl.BlockSpec((TO, D), lambda i: (i, 0))], [_hbm(x)], {0: 0}
    in_specs = x_specs + [
        pl.BlockSpec((1, 1, N_MOD, D), lambda i: (l, _cond_of_tile(i * (TO // TM)), 0, 0)),
        ctx(MIX_A), lat(MIX_A), ctx(MIX_B), lat(MIX_B),
        pl.BlockSpec((1, D, D), lambda i: (e, 0, 0)),
    ] + t_in
    return pl.pallas_call(
        functools.partial(_out_kernel, split_x),
        grid=(T // TO,),
        in_specs=in_specs,
        out_specs=[pl.BlockSpec((TO, D), lambda i: (i, 0))] + t_out,
        out_shape=[_hbm_out((T, D), F32)] + t_shape,
        scratch_shapes=[pltpu.VMEM((D, D), BF16)] + t_scratch,
        input_output_aliases=aliases,
        compiler_params=_cparams(("arbitrary",), 40),
        name="out_proj_route",
    )(*x_args, mod, _hbm(att_c), _hbm(att_l), _hbm(sg_c), _hbm(sg_l), w_out,
      norm2_g.reshape(DEPTH, 1, D), rwt, rbb, tri)


def _pool_tables():
    t = np.arange(TP)[:, None]
    s = np.arange(TP + POOL_RB)[None, :]
    p = s - POOL_HALO
    bands, sizes = [], []
    for seq_len in (CTX_S, LAT_S):
        same = (p >= 0) & (p < TP) & (p // seq_len == t // seq_len)
        per_w, size_w = [], []
        for w in POOL_WINDOWS:
            band = (p >= t - w // 2) & (p < t - w // 2 + w) & same
            size_w.append(band.sum(axis=1, keepdims=True))
            per_w.append(np.stack([band[rb * POOL_RB:(rb + 1) * POOL_RB, rb * POOL_RB:(rb + 2) * POOL_RB]
                                   for rb in range(TP // POOL_RB)]))
        bands.append(np.stack(per_w))
        sizes.append(np.stack(size_w))
    return jnp.asarray(np.stack(bands), dtype=BF16), jnp.asarray(np.stack(sizes), dtype=F32)


def _pool_kernel(x_ref, m_ref, n1_ref, band_ref, size_ref, pw_ref, ps_ref,
                 n2_ref, rwt_ref, rbb_ref, tri_ref,
                 x_o, *tail_refs):
    *tail_outs, carry = tail_refs
    x = x_ref[...]
    h = _modnorm(x, n1_ref[0], m_ref[0, 0, 1:2, :], m_ref[0, 0, 0:1, :])
    hcat = jnp.concatenate([jnp.zeros((POOL_HALO, D), F32), h,
                            jnp.zeros((POOL_RB - POOL_HALO, D), F32)], axis=0)
    hi = hcat.astype(BF16)
    lo = (hcat - hi.astype(F32)).astype(BF16)
    dot = lambda a, b: jnp.dot(a, b, preferred_element_type=F32)
    outs = []
    for g in range(len(POOL_WINDOWS)):
        cols = slice(g * POOL_DIM, (g + 1) * POOL_DIM)
        sums = []
        for rb in range(TP // POOL_RB):
            near = slice(rb * POOL_RB, (rb + 2) * POOL_RB)
            band = band_ref[0, g, rb]
            sums.append(dot(band, hi[near, cols]) + dot(band, lo[near, cols]))
        mean = jnp.concatenate(sums, axis=0) / size_ref[0, g]
        dlt = (mean - h[:, cols]).astype(BF16)
        outs.append(dot(dlt, pw_ref[0, g].astype(BF16)))
    out = jnp.concatenate(outs, axis=1) * ps_ref[0]
    x_new = x + m_ref[0, 0, 2:3, :] * out
    x_o[...] = x_new
    for sub_tile in range(TP // TM):
        _route_tail(sub_tile, x_new[sub_tile * TM:(sub_tile + 1) * TM, :],
                    m_ref, n2_ref, rwt_ref, rbb_ref, tri_ref, *tail_outs, carry)


def _pool_mixer(l, x, mod, norm1_g, pool_w, pool_scale, norm2_g, rwt, rbb, tri):
    o = l // 2
    t_in, t_out, t_shape, t_scratch = _tail_specs(l, TP // TM)
    bands, sizes = _pool_tables()
    n_ctx = T_CTX // TP
    kind = lambda i: jnp.where(i < n_ctx, 0, 1)
    n_w, n_rb = len(POOL_WINDOWS), TP // POOL_RB
    in_specs = [
        pl.BlockSpec((TP, D), lambda i: (i, 0)),
        pl.BlockSpec((1, 1, N_MOD, D), lambda i: (l, _cond_of_tile(i * (TP // TM)), 0, 0)),
        pl.BlockSpec((1, 1, D), lambda i: (l, 0, 0)),
        pl.BlockSpec((1, n_w, n_rb, POOL_RB, 2 * POOL_RB), lambda i: (kind(i), 0, 0, 0, 0)),
        pl.BlockSpec((1, n_w, TP, 1), lambda i: (kind(i), 0, 0, 0)),
        pl.BlockSpec((1, n_w, POOL_DIM, POOL_DIM), lambda i: (o, 0, 0, 0)),
        pl.BlockSpec((1, 1, D), lambda i: (o, 0, 0)),
    ] + t_in
    return pl.pallas_call(
        _pool_kernel,
        grid=(T // TP,),
        in_specs=in_specs,
        out_specs=[pl.BlockSpec((TP, D), lambda i: (i, 0))] + t_out,
        out_shape=[_hbm_out((T, D), F32)] + t_shape,
        scratch_shapes=t_scratch,
        compiler_params=_cparams(("arbitrary",), 52),
        name="pool_route",
    )(_hbm(x), mod, norm1_g.reshape(DEPTH, 1, D), bands, sizes, pool_w,
      pool_scale.reshape(DEPTH // 2, 1, D),
      norm2_g.reshape(DEPTH, 1, D), rwt, rbb, tri)


def _start_run_copies(tile, nch_ref, loff_ref, gb_ref, make_copy):
    for e in range(N_EXPERTS):
        n = nch_ref[tile * N_EXPERTS + e]
        lo = loff_ref[tile * N_EXPERTS + e]
        gb = gb_ref[tile * N_EXPERTS + e] + e * EXPERT_CAP

        def issue(c, carry, lo=lo, gb=gb):
            make_copy(pl.multiple_of(lo + c * (2 * ROW_CHUNK), ROW_CHUNK),
                      pl.multiple_of(gb + c * (2 * ROW_CHUNK), ROW_CHUNK), 2 * ROW_CHUNK).start()
            return carry

        lax.fori_loop(0, lax.shift_right_logical(n, 1), issue, 0)

        @pl.when((n & 1) == 1)
        def _(n=n, lo=lo, gb=gb):
            make_copy(pl.multiple_of(lo + (n - 1) * ROW_CHUNK, ROW_CHUNK),
                      pl.multiple_of(gb + (n - 1) * ROW_CHUNK, ROW_CHUNK), ROW_CHUNK).start()


def _wait_run_copies(tile, nch_ref, make_copy):
    doubles, singles = 0, 0
    for e in range(N_EXPERTS):
        n = nch_ref[tile * N_EXPERTS + e]
        doubles = doubles + lax.shift_right_logical(n, 1)
        singles = singles + (n & 1)

    def drain(rows):
        def body(c, carry):
            make_copy(0, 0, rows).wait()
            return carry
        return body

    lax.fori_loop(0, doubles, drain(2 * ROW_CHUNK), 0)
    lax.fori_loop(0, singles, drain(ROW_CHUNK), 0)


def _dispatch_kernel(nch_ref, loff_ref, gb_ref, h_ref, lp_ref, xs_ref, sbuf, sems):
    i = pl.program_id(0)
    slot = i % 2
    r = lax.broadcasted_iota(I32, (SORTED_ROWS, TM), 0)
    for sub in range(MOVE_SUBS):
        tok = slice(sub * TM, (sub + 1) * TM)
        hit = (r == lp_ref[0:1, tok]) | (r == lp_ref[1:2, tok])
        perm = jnp.where(hit, 1.0, 0.0).astype(BF16)
        sbuf[slot, sub] = jnp.dot(perm, h_ref[tok, :], preferred_element_type=F32).astype(BF16)

    def copies_of(s, sub):
        def make_copy(local_row, global_row, rows):
            return pltpu.make_async_copy(sbuf.at[s, sub, pl.ds(local_row, rows)],
                                         xs_ref.at[pl.ds(global_row, rows)], sems.at[s])
        return make_copy

    @pl.when(i > 0)
    def _():
        for sub in range(MOVE_SUBS):
            _wait_run_copies((i - 1) * MOVE_SUBS + sub, nch_ref, copies_of(1 - slot, sub))

    for sub in range(MOVE_SUBS):
        _start_run_copies(i * MOVE_SUBS + sub, nch_ref, loff_ref, gb_ref, copies_of(slot, sub))

    @pl.when(i == N_TILES // MOVE_SUBS - 1)
    def _():
        for sub in range(MOVE_SUBS):
            _wait_run_copies(i * MOVE_SUBS + sub, nch_ref, copies_of(slot, sub))


def _dispatch(layout, h2, lp):
    rows = MOVE_SUBS * TM
    return pl.pallas_call(
        _dispatch_kernel,
        grid_spec=pltpu.PrefetchScalarGridSpec(
            num_scalar_prefetch=3,
            grid=(N_TILES // MOVE_SUBS,),
            in_specs=[pl.BlockSpec((rows, D), lambda i, *_: (i, 0)),
                      pl.BlockSpec((8, rows), lambda i, *_: (0, i))],
            out_specs=pl.BlockSpec(memory_space=pl.ANY),
            scratch_shapes=[pltpu.VMEM((2, MOVE_SUBS, SORTED_ROWS, D), BF16),
                            pltpu.SemaphoreType.DMA((2,))],
        ),
        out_shape=_hbm_out((N_EXPERTS * EXPERT_CAP, D), BF16),
        compiler_params=_cparams(("arbitrary",), 32),
        name="moe_dispatch",
    )(*layout, _hbm(h2), lp)


def _moe_kernel(l, rb_ref, e_ref, nv_ref, first_ref, ord_ref, nxt_ref, nxt2_ref,
                xs_ref, wg_hbm, wu_hbm, wd_hbm, y_ref, wgf, wuf, wdf, wgb, wub, wdb, sems):
    i = pl.program_id(0)
    n_valid = nv_ref[i]

    def weight_copies(e, slot):
        return (pltpu.make_async_copy(wg_hbm.at[l, e], wgf.at[slot], sems.at[slot, 0]),
                pltpu.make_async_copy(wu_hbm.at[l, e], wuf.at[slot], sems.at[slot, 1]),
                pltpu.make_async_copy(wd_hbm.at[l, e], wdf.at[slot], sems.at[slot, 2]))

    @pl.when(first_ref[i] == 1)
    def _():
        k = ord_ref[i]
        slot = k % W_SLOTS

        @pl.when(k == 0)
        def _():
            for cp in weight_copies(e_ref[i], 0):
                cp.start(priority=1)

            @pl.when(nxt_ref[i] >= 0)
            def _():
                for cp in weight_copies(nxt_ref[i], 1):
                    cp.start(priority=1)

        for cp in weight_copies(e_ref[i], slot):
            cp.wait()

        @pl.when(nxt2_ref[i] >= 0)
        def _():
            for cp in weight_copies(nxt2_ref[i], (k + 2) % W_SLOTS):
                cp.start(priority=1)

        wgb[...] = wgf[slot].astype(BF16)
        wub[...] = wuf[slot].astype(BF16)
        wdb[...] = wdf[slot].astype(BF16)

    @pl.when(n_valid > 0)
    def _():
        row = lax.broadcasted_iota(I32, (TE, 1), 0)
        x = jnp.where(row < n_valid, xs_ref[...], jnp.zeros((), BF16))
        gate = jnp.dot(x, wgb[...], preferred_element_type=F32)
        up = jnp.dot(x, wub[...], preferred_element_type=F32)
        act = (gate * _sigmoid(gate) * up).astype(BF16)
        y_ref[...] = jnp.dot(act, wdb[...], preferred_element_type=F32).astype(BF16)


def _experts(l, tile_map, xs, w_gate, w_up, w_down):
    n_pre = len(tile_map)
    blk = lambda i, rb, *_: (rb[i], 0)
    return pl.pallas_call(
        functools.partial(_moe_kernel, l),
        grid_spec=pltpu.PrefetchScalarGridSpec(
            num_scalar_prefetch=n_pre,
            grid=(MOE_TILES,),
            in_specs=[
                pl.BlockSpec((TE, D), blk),
                pl.BlockSpec(memory_space=pl.ANY),
                pl.BlockSpec(memory_space=pl.ANY),
                pl.BlockSpec(memory_space=pl.ANY),
            ],
            out_specs=pl.BlockSpec((TE, D), blk),
            scratch_shapes=[pltpu.VMEM((W_SLOTS, D, D_FF), F32), pltpu.VMEM((W_SLOTS, D, D_FF), F32),
                            pltpu.VMEM((W_SLOTS, D_FF, D), F32),
                            pltpu.VMEM((D, D_FF), BF16), pltpu.VMEM((D, D_FF), BF16),
                            pltpu.VMEM((D_FF, D), BF16),
                            pltpu.SemaphoreType.DMA((W_SLOTS, 3))],
        ),
        out_shape=_hbm_out((N_EXPERTS * EXPERT_CAP, D), BF16),
        compiler_params=_cparams(("arbitrary",), 48),
        name="moe_experts",
    )(*tile_map, _hbm(xs), w_gate, w_up, w_down)


def _combine_kernel(split_out, nch_ref, loff_ref, gb_ref, x_ref, m_ref, lpt_ref, w_ref, y_ref, *refs):
    *o_refs, ybuf, sems = refs
    i = pl.program_id(0)
    slot = i % 2

    def copies_of(s, sub):
        def make_copy(local_row, global_row, rows):
            return pltpu.make_async_copy(y_ref.at[pl.ds(global_row, rows)],
                                         ybuf.at[s, sub, pl.ds(local_row, rows)], sems.at[s])
        return make_copy

    @pl.when(i == 0)
    def _():
        ybuf[...] = jnp.zeros_like(ybuf)
        for sub in range(MOVE_SUBS):
            _start_run_copies(sub, nch_ref, loff_ref, gb_ref, copies_of(slot, sub))

    @pl.when(i + 1 < N_TILES // MOVE_SUBS)
    def _():
        for sub in range(MOVE_SUBS):
            _start_run_copies((i + 1) * MOVE_SUBS + sub, nch_ref, loff_ref, gb_ref, copies_of(1 - slot, sub))

    for sub in range(MOVE_SUBS):
        _wait_run_copies(i * MOVE_SUBS + sub, nch_ref, copies_of(slot, sub))

    col = lax.broadcasted_iota(I32, (TM, SORTED_ROWS), 1)
    parts = []
    for sub in range(MOVE_SUBS):
        tok = slice(sub * TM, (sub + 1) * TM)
        pick = jnp.where(col == lpt_ref[tok, 0:1], w_ref[tok, 0:1],
                         jnp.where(col == lpt_ref[tok, 1:2], w_ref[tok, 1:2], 0.0)).astype(BF16)
        parts.append(jnp.dot(pick, ybuf[slot, sub], preferred_element_type=F32))
    res = x_ref[...] + m_ref[0, 0, 5:6, :] * jnp.concatenate(parts, axis=0)
    if split_out:
        @pl.when(i < T_CTX // (MOVE_SUBS * TM))
        def _():
            o_refs[0][...] = res

        @pl.when(i >= T_CTX // (MOVE_SUBS * TM))
        def _():
            o_refs[1][...] = res
    else:
        o_refs[0][...] = res


def _combine(l, layout, x, mod, lpt, wcol, y, split_out):
    rows = MOVE_SUBS * TM
    n_ctx = T_CTX // rows
    if split_out:
        out_specs = [pl.BlockSpec((rows, D), lambda i, *_: (jnp.minimum(i, n_ctx - 1), 0)),
                     pl.BlockSpec((rows, D), lambda i, *_: (jnp.maximum(i - n_ctx, 0), 0))]
        out_shape = [jax.ShapeDtypeStruct((T_CTX, D), F32), jax.ShapeDtypeStruct((T_LAT, D), F32)]
        aliases = {}
    else:
        out_specs = pl.BlockSpec((rows, D), lambda i, *_: (i, 0))
        out_shape = _hbm_out((T, D), F32)
        aliases = {3: 0}
    return pl.pallas_call(
        functools.partial(_combine_kernel, split_out),
        grid_spec=pltpu.PrefetchScalarGridSpec(
            num_scalar_prefetch=3,
            grid=(N_TILES // MOVE_SUBS,),
            in_specs=[
                pl.BlockSpec((rows, D), lambda i, *_: (i, 0)),
                pl.BlockSpec((1, 1, N_MOD, D), lambda i, *_: (l, _cond_of_tile(i * MOVE_SUBS), 0, 0)),
                pl.BlockSpec((rows, 8), lambda i, *_: (i, 0)),
                pl.BlockSpec((rows, 8), lambda i, *_: (i, 0)),
                pl.BlockSpec(memory_space=pl.ANY),
            ],
            out_specs=out_specs,
            scratch_shapes=[pltpu.VMEM((2, MOVE_SUBS, SORTED_ROWS, D), BF16),
                            pltpu.SemaphoreType.DMA((2,))],
        ),
        out_shape=out_shape,
        input_output_aliases=aliases,
        compiler_params=_cparams(("arbitrary",), 48),
        name="moe_combine",
    )(*layout, _hbm(x), mod, lpt, wcol, _hbm(y))


def _tile_map(cnt):
    cnt = cnt[:, 0]
    ids = jnp.arange(N_EXPERTS, dtype=I32)
    n_tiles = (cnt + TE - 1) // TE
    cum = jnp.cumsum(n_tiles)
    total = cum[-1]
    i = jnp.arange(MOE_TILES, dtype=I32)
    ii = jnp.minimum(i, total - 1)
    e = jnp.sum((ii[:, None] >= cum[None, :]).astype(I32), axis=1)
    onehot = (e[:, None] == ids[None, :]).astype(I32)
    pick = lambda v: jnp.sum(onehot * v[None, :], axis=1)
    j = ii - pick(cum - n_tiles)
    active = i < total
    n_valid = jnp.where(active, jnp.clip(pick(cnt) - j * TE, 0, TE), 0)
    first = (active & (j == 0)).astype(I32)
    nonempty = n_tiles > 0
    ordinal = jnp.cumsum(nonempty.astype(I32)) - 1
    later = jnp.where(nonempty[None, :] & (ids[None, :] > ids[:, None]), ids[None, :], N_EXPERTS)
    nxt = jnp.min(later, axis=1)
    nxt_of = lambda v: jnp.sum((v[:, None] == ids[None, :]).astype(I32) * nxt[None, :], axis=1)
    nxt2 = jnp.where(nxt == N_EXPERTS, N_EXPERTS, nxt_of(nxt))
    none = lambda v: jnp.where(v == N_EXPERTS, -1, v)
    as_i32 = lambda v: v.astype(I32)
    return tuple(map(as_i32, (e * CAP_TILES + j, e, n_valid, first, pick(ordinal),
                              pick(none(nxt)), pick(none(nxt2)))))


def _moe(l, x, tail, mod, w_gate, w_up, w_down, split_out):
    h2, lp, rf, nch, loff, gb, cnt = tail
    layout = tuple(a[:, :, 0].reshape(N_TILES * N_EXPERTS) for a in (nch, loff, gb))
    xs = _dispatch(layout, h2, lp)
    y = _experts(l, _tile_map(cnt), xs, w_gate, w_up, w_down)
    return _combine(l, layout, x, mod, lp.T, rf.T, y, split_out)


def _rope_tables():
    t = jnp.arange(LAT_S)
    row = (t // GRID_W).astype(F32)
    col = (t % GRID_W).astype(F32)
    nf = HEAD_DIM // 4
    inv = ROPE_BASE ** (-jnp.arange(nf, dtype=F32) / nf)
    ar = row[:, None] * inv[None, :]
    ac = col[:, None] * inv[None, :]
    cos = jnp.concatenate([jnp.cos(ar), jnp.cos(ar), jnp.cos(ac), jnp.cos(ac)], axis=1)
    sin = jnp.concatenate([-jnp.sin(ar), jnp.sin(ar), -jnp.sin(ac), jnp.sin(ac)], axis=1)
    reps = QK_W // HEAD_DIM
    return jnp.tile(cos, (1, reps)), jnp.tile(sin, (1, reps))


def kernel(x_prompt, x_sample, cache_k, cache_v, c, c_ctx, norm1_g, norm2_g, w_ada, b_ada, w_in, w_out,
           q_norm_g, k_norm_g, lam_q1, lam_k1, lam_q2, lam_k2, subln_g, sg_w, sg_b, pool_w, pool_scale,
           router_w, router_b, w_gate, w_up, w_down):
    x = (x_prompt.reshape(T_CTX, D), x_sample.reshape(T_LAT, D))
    cond = jnp.concatenate([c_ctx[None, :], c], axis=0)
    mod = _modulation(cond, w_ada, b_ada)

    seg_id = np.arange(QK_W) // HEAD_DIM
    seg = jnp.asarray((seg_id[:, None] == seg_id[None, :]) / HEAD_DIM, dtype=BF16)
    tri = jnp.asarray(np.arange(TM)[:, None] < np.arange(TM)[None, :], dtype=BF16)
    rope_tabs = _rope_tables()
    rwt = router_w.T
    rbb = jnp.broadcast_to(router_b[:, None], (N_EXPERTS, TM))
    states = None

    for l in range(DEPTH):
        if l % 2 == 0:
            e = l // 2
            qg = jnp.tile(q_norm_g[e], QK_W // HEAD_DIM)[None, :]
            kg = jnp.tile(k_norm_g[e], QK_W // HEAD_DIM)[None, :]
            sgb = jnp.broadcast_to(sg_b[:, :, :, None], (DEPTH // 2, SG_GROUPS, SG_CHUNK, 128))
            x_c, x_l = x if isinstance(x, tuple) else (x, x)
            common = (mod, norm1_g, w_in, qg, kg, seg, sg_w, sgb)
            q_c, k_c, v_c, sg_c, *states = _front(False, l, x_c, *common, None, states)
            q_l, k_l, v_l, sg_l = _front(True, l, x_l, *common, rope_tabs, None)
            lam = (lam_q1, lam_k1, lam_q2, lam_k2, subln_g)
            att_c = _attention(False, l, q_c, k_c, v_c, None, None, *lam)
            att_l = _attention(True, l, q_l, k_l, v_l, cache_k, cache_v, *lam)
            x, *tail = _out_proj(l, x, mod, att_c, att_l, sg_c, sg_l, w_out, norm2_g, rwt, rbb, tri)
        else:
            x, *tail = _pool_mixer(l, x, mod, norm1_g, pool_w, pool_scale, norm2_g, rwt, rbb, tri)
        x = _moe(l, x, tail, mod, w_gate, w_up, w_down, split_out=(l == DEPTH - 1))
    state_k, state_v = states

    y_prompt = x[0].reshape(N_CTX_B, CTX_S, D)
    y_sample = x[1].reshape(N_LAT_B, LAT_S, D)
    state_k = state_k.reshape(N_CTX_B, DEPTH // 2, CTX_S, N_HEADS, 2, HEAD_DIM)
    return y_prompt, y_sample, state_k, state_v
```

```python
import functools
import math

import jax
import jax.numpy as jnp
import numpy as np
from jax import lax
from jax.experimental import pallas as pl
from jax.experimental.pallas import tpu as pltpu

F32 = jnp.float32
BF16 = jnp.bfloat16
I32 = jnp.int32

D = 1024
N_CTX_B, CTX_S = 16, 256
N_LAT_B, LAT_S = 2, 1024
PAST = 512
DEPTH = 4
T_CTX = N_CTX_B * CTX_S
T_LAT = N_LAT_B * LAT_S
T = T_CTX + T_LAT
GRID_W = 64
N_HEADS = 4
HEAD_DIM = 64
VAL_DIM = 128
QK_W = 512
MIX_A = 512
MIX_B = 512
IN_W = 2560
SG_GROUPS = 4
SG_CHUNK = 128
POOL_WINDOWS = (2, 4, 8, 16)
POOL_DIM = 256
POOL_HALO = 8
N_EXPERTS = 16
N_GROUPS = 4
PER_GROUP = 4
D_FF = 512
N_MOD = 6
N_COND = 1 + N_LAT_B
EPS = 1e-6
ROPE_BASE = 10000.0
LOG2_E = math.log2(math.e)

TM = 256
N_TILES = T // TM
N_CTX_TILES = T_CTX // TM
LAT_TILES_PER_SEQ = LAT_S // TM
ROW_CHUNK = 16
PAD_PER_RUN = ROW_CHUNK - 1
SORTED_ROWS = -(-(2 * TM + N_EXPERTS * PAD_PER_RUN) // 128) * 128
TE = 512
W_SLOTS = 3
ATTN_CTX_SEQS = 2
ATTN_LAT_HEADS = 2
MOVE_SUBS = 2
TF = 512
TO = 512
TP = LAT_S
POOL_RB = 128
CAP_TILES = -(-(T + N_TILES * PAD_PER_RUN) // TE)
EXPERT_CAP = CAP_TILES * TE
MOE_TILES = (2 * T + N_EXPERTS * N_TILES * PAD_PER_RUN) // TE + N_EXPERTS
MOD_TN = 2048
MIB = 1024 * 1024

NT_DIMS = (((1,), (1,)), ((), ()))


def _cparams(sem, vmem_mib):
    return pltpu.CompilerParams(dimension_semantics=sem, vmem_limit_bytes=vmem_mib * MIB)


def _hbm(x):
    try:
        return pltpu.with_memory_space_constraint(x, pltpu.HBM)
    except ValueError:
        return x


def _hbm_out(shape, dtype):
    return pltpu.HBM(shape, dtype)


def _sigmoid(x):
    return 1.0 / (1.0 + jnp.exp(-x))


def _gelu_tanh(x):
    c = math.sqrt(2.0 / math.pi)
    return x * (0.5 * (1.0 + jnp.tanh(c * (x + 0.044715 * (x * x * x)))))


def _modnorm(x, g, scale, shift):
    ms = jnp.mean(x * x, axis=-1, keepdims=True)
    return (x * lax.rsqrt(ms + EPS)) * (g * (1.0 + scale)) + shift


def _cond_of_tile(i):
    return jnp.where(i < N_CTX_TILES, 0, 1 + (i - N_CTX_TILES) // LAT_TILES_PER_SEQ)


def _mod_kernel(c_ref, w_ref, b_ref, o_ref):
    c = c_ref[...]
    s = (c * _sigmoid(c)).astype(BF16)
    o_ref[0] = jnp.dot(s, w_ref[0].astype(BF16), preferred_element_type=F32) + b_ref[0]


def _modulation(cond, w_ada, b_ada):
    cond8 = jnp.concatenate([cond, jnp.zeros((8 - N_COND, D), F32)], axis=0)
    out = pl.pallas_call(
        _mod_kernel,
        grid=(DEPTH, N_MOD * D // MOD_TN),
        in_specs=[
            pl.BlockSpec((8, D), lambda l, n: (0, 0)),
            pl.BlockSpec((1, D, MOD_TN), lambda l, n: (l, 0, n)),
            pl.BlockSpec((1, 1, MOD_TN), lambda l, n: (l, 0, n)),
        ],
        out_specs=pl.BlockSpec((1, 8, MOD_TN), lambda l, n: (l, 0, n)),
        out_shape=jax.ShapeDtypeStruct((DEPTH, 8, N_MOD * D), F32),
        compiler_params=_cparams(("arbitrary", "arbitrary"), 40),
        name="adaln_mod",
    )(cond8, w_ada, b_ada.reshape(DEPTH, 1, N_MOD * D))
    return out.reshape(DEPTH, 8, N_MOD, D)


def _front_kernel(latent, n_state_in, *refs):
    (x_ref, m_ref, n1_ref, w_ref, qg_ref, kg_ref, seg_ref, sgw_ref, sgb_ref), refs = refs[:9], refs[9:]
    if latent:
        cos_ref, sin_ref, q_o, k_o, v_o, sg_o, wbf = refs
    else:
        q_o, k_o, v_o, sg_o, sk_o, sv_o, wbf = refs[n_state_in:]

    @pl.when(pl.program_id(0) == 0)
    def _():
        for c0 in range(0, IN_W, 512):
            wbf[:, c0:c0 + 512] = w_ref[0, :, c0:c0 + 512].astype(BF16)

    h = _modnorm(x_ref[...], n1_ref[0], m_ref[0, 0, 1:2, :], m_ref[0, 0, 0:1, :])
    proj = jnp.dot(h.astype(BF16), wbf[...], preferred_element_type=F32)

    def qk_norm(z, g):
        ms = jnp.dot((z * z).astype(BF16), seg_ref[...], preferred_element_type=F32)
        return z * lax.rsqrt(ms + EPS) * g

    q = qk_norm(proj[:, 0:QK_W], qg_ref[...])
    k = qk_norm(proj[:, QK_W:2 * QK_W], kg_ref[...])
    v = proj[:, 2 * QK_W:2 * QK_W + MIX_A]

    if latent:
        lane = lax.broadcasted_iota(I32, (1, QK_W), 1)
        first = (lane % 32) < 16

        def rope(z):
            partner = jnp.where(first, pltpu.roll(z, QK_W - 16, 1), pltpu.roll(z, 16, 1))
            return z * cos_ref[...] + partner * sin_ref[...]

        q = rope(q)
        k = rope(k)
    else:
        for b in range(TF // CTX_S):
            sk_o[b, 0] = k[b * CTX_S:(b + 1) * CTX_S, :]
            for hd in range(N_HEADS):
                sv_o[b, 0, :, hd, :] = v[b * CTX_S:(b + 1) * CTX_S, hd * VAL_DIM:(hd + 1) * VAL_DIM]

    q_o[...] = (q * (HEAD_DIM ** -0.5 * LOG2_E)).astype(BF16)
    k_o[...] = k.astype(BF16)
    v_o[...] = v.astype(BF16)

    gu0 = 2 * QK_W + MIX_A
    gv0 = gu0 + MIX_B
    for g in range(SG_GROUPS):
        wg = sgw_ref[0, g].astype(BF16)
        for n in range(TF // SG_CHUNK):
            rows = slice(n * SG_CHUNK, (n + 1) * SG_CHUNK)
            cu = slice(gu0 + g * 128, gu0 + (g + 1) * 128)
            cv = slice(gv0 + g * 128, gv0 + (g + 1) * 128)
            gv = _gelu_tanh(proj[rows, cv]).astype(BF16)
            mixed = jnp.dot(wg, gv, preferred_element_type=F32) + sgb_ref[0, g]
            sg_o[rows, g * 128:(g + 1) * 128] = (_gelu_tanh(proj[rows, cu]) * mixed).astype(BF16)


def _front(latent, l, x, mod, norm1_g, w_in, qg, kg, seg, sg_w, sgb, rope_tabs, states):
    e = l // 2
    rows = T_LAT if latent else T_CTX
    n_tiles = rows // TF
    off = T_CTX // TF if (latent and x.shape[0] == T) else 0
    tiles_per_seq = LAT_S // TF
    if latent:
        cond_map = lambda i: (l, 1 + i // tiles_per_seq, 0, 0)
    else:
        cond_map = lambda i: (l, 0, 0, 0)
    in_specs = [
        pl.BlockSpec((TF, D), lambda i: (i + off, 0)),
        pl.BlockSpec((1, 1, N_MOD, D), cond_map),
        pl.BlockSpec((1, 1, D), lambda i: (l, 0, 0)),
        pl.BlockSpec((1, D, IN_W), lambda i: (e, 0, 0)),
        pl.BlockSpec((1, QK_W), lambda i: (0, 0)),
        pl.BlockSpec((1, QK_W), lambda i: (0, 0)),
        pl.BlockSpec((QK_W, QK_W), lambda i: (0, 0)),
        pl.BlockSpec((1, SG_GROUPS, SG_CHUNK, SG_CHUNK), lambda i: (e, 0, 0, 0)),
        pl.BlockSpec((1, SG_GROUPS, SG_CHUNK, 128), lambda i: (e, 0, 0, 0)),
    ]
    args = [_hbm(x), mod, norm1_g.reshape(DEPTH, 1, D), w_in, qg, kg, seg, sg_w, sgb]
    tok = lambda: pl.BlockSpec((TF, QK_W), lambda i: (i, 0))
    out_specs = [tok(), tok(), tok(), tok()]
    out_shape = [_hbm_out((rows, QK_W), BF16)] * 4
    aliases = {}
    if latent:
        cos_t, sin_t = rope_tabs
        in_specs += [pl.BlockSpec((TF, QK_W), lambda i: (i % tiles_per_seq, 0))] * 2
        args += [cos_t, sin_t]
    else:
        if states:
            in_specs += [pl.BlockSpec(memory_space=pl.ANY)] * 2
            args += list(states)
            aliases = {9: 4, 10: 5}
        out_specs += [pl.BlockSpec((TF // CTX_S, 1, CTX_S, QK_W), lambda i: (i, e, 0, 0)),
                      pl.BlockSpec((TF // CTX_S, 1, CTX_S, N_HEADS, VAL_DIM), lambda i: (i, e, 0, 0, 0))]
        out_shape += [_hbm_out((N_CTX_B, DEPTH // 2, CTX_S, QK_W), F32),
                      _hbm_out((N_CTX_B, DEPTH // 2, CTX_S, N_HEADS, VAL_DIM), F32)]
    return pl.pallas_call(
        functools.partial(_front_kernel, latent, len(aliases)),
        grid=(n_tiles,),
        in_specs=in_specs,
        out_specs=out_specs,
        out_shape=out_shape,
        scratch_shapes=[pltpu.VMEM((D, IN_W), BF16)],
        input_output_aliases=aliases,
        compiler_params=_cparams(("arbitrary",), 48),
        name="front_lat" if latent else "front_ctx",
    )(*args)


def _attn_kernel(has_cache, seqs, heads, lam0, *refs):
    if has_cache:
        (q_ref, k_ref, v_ref, ck_ref, cv_ref, lq1, lk1, lq2, lk2, sub_ref, o_ref) = refs
    else:
        (q_ref, k_ref, v_ref, lq1, lk1, lq2, lk2, sub_ref, o_ref) = refs

    lane = lax.broadcasted_iota(I32, (1, VAL_DIM), 1)
    lam = (jnp.exp(jnp.sum(lq1[0] * lk1[0], axis=-1, keepdims=True))
           - jnp.exp(jnp.sum(lq2[0] * lk2[0], axis=-1, keepdims=True)) + lam0)
    q_rows = q_ref.shape[0] // seqs
    k_rows = k_ref.shape[0] // seqs

    for sq, h in [(sq, h) for sq in range(seqs) for h in range(heads)]:
        hs = slice(h * VAL_DIM, (h + 1) * VAL_DIM)
        qr = slice(sq * q_rows, (sq + 1) * q_rows)
        kr = slice(sq * k_rows, (sq + 1) * k_rows)
        qf = q_ref[qr, hs].astype(F32)
        qc = [jnp.where(lane < HEAD_DIM, qf, 0.0).astype(BF16),
              jnp.where(lane >= HEAD_DIM, qf, 0.0).astype(BF16)]
        keys = [k_ref[kr, hs]]
        vals = [v_ref[kr, hs]]
        if has_cache:
            keys.append(ck_ref[0, 0, :, hs].astype(BF16))
            vals.append(cv_ref[0, 0, :, hs].astype(BF16))

        probs = []
        for c in range(2):
            s = [lax.dot_general(qc[c], kk, NT_DIMS, preferred_element_type=F32) for kk in keys]
            m = s[0].max(axis=-1, keepdims=True)
            for sx in s[1:]:
                m = jnp.maximum(m, sx.max(axis=-1, keepdims=True))
            p = [jnp.exp2(sx - m) for sx in s]
            den = p[0].sum(axis=-1, keepdims=True)
            for px in p[1:]:
                den = den + px.sum(axis=-1, keepdims=True)
            probs.append((p, den))

        r0 = 1.0 / probs[0][1]
        r1 = lam / probs[1][1]
        o = None
        for idx in range(len(keys)):
            a = (probs[0][0][idx] * r0 - probs[1][0][idx] * r1).astype(BF16)
            t = jnp.dot(a, vals[idx], preferred_element_type=F32)
            o = t if o is None else o + t
        ms = jnp.mean(o * o, axis=-1, keepdims=True)
        o_ref[qr, hs] = ((o * lax.rsqrt(ms + EPS) * sub_ref[0]) * (1.0 - lam0)).astype(BF16)


def _attention(has_cache, l, q, k, v, cache_k, cache_v, lam_q1, lam_k1, lam_q2, lam_k2, subln_g):
    e = l // 2
    lam0 = 0.8 - 0.6 * math.exp(-0.3 * l)
    n_b, s_len = (N_LAT_B, LAT_S) if has_cache else (N_CTX_B, CTX_S)
    heads = ATTN_LAT_HEADS if has_cache else N_HEADS
    seqs = 1 if has_cache else ATTN_CTX_SEQS
    width = heads * VAL_DIM
    nq = s_len // TM
    kv = lambda: pl.BlockSpec((seqs * s_len, width), lambda b, h, qi: (b, h))
    par = lambda: pl.BlockSpec((1, 1, HEAD_DIM), lambda b, h, qi: (e, 0, 0))
    in_specs = [pl.BlockSpec((seqs * TM, width), lambda b, h, qi: (b * nq + qi, h)), kv(), kv()]
    args = [_hbm(q), _hbm(k), _hbm(v)]
    if has_cache:
        cs = lambda: pl.BlockSpec((1, 1, PAST, width), lambda b, h, qi: (b, e, 0, h))
        in_specs += [cs(), cs()]
        args += [cache_k.reshape(N_LAT_B, DEPTH // 2, PAST, QK_W),
                 cache_v.reshape(N_LAT_B, DEPTH // 2, PAST, MIX_A)]
    in_specs += [par(), par(), par(), par(),
                 pl.BlockSpec((1, 1, VAL_DIM), lambda b, h, qi: (e, 0, 0))]
    r3 = lambda a: a.reshape(DEPTH // 2, 1, -1)
    args += [r3(lam_q1), r3(lam_k1), r3(lam_q2), r3(lam_k2), r3(subln_g)]
    return pl.pallas_call(
        functools.partial(_attn_kernel, has_cache, seqs, heads, lam0),
        grid=(n_b // seqs, N_HEADS // heads, nq),
        in_specs=in_specs,
        out_specs=pl.BlockSpec((seqs * TM, width), lambda b, h, qi: (b * nq + qi, h)),
        out_shape=_hbm_out((n_b * s_len, MIX_A), BF16),
        compiler_params=_cparams(("arbitrary", "arbitrary", "arbitrary"), 32),
        name="attn_lat" if has_cache else "attn_ctx",
    )(*args)


def _route_tail(sub_tile, x_new, m_ref, n2_ref, rwt_ref, rbb_ref, tri_ref,
                h2_o, lp_o, rf_o, nch_o, loff_o, gb_o, cnt_o, carry):
    tok = slice(sub_tile * TM, (sub_tile + 1) * TM)
    if sub_tile == 0:
        @pl.when(pl.program_id(0) == 0)
        def _():
            carry[...] = jnp.zeros_like(carry)

    h2 = _modnorm(x_new, n2_ref[0], m_ref[0, 0, 4:5, :], m_ref[0, 0, 3:4, :])

    hh = h2.astype(BF16)
    h2_o[tok, :] = hh
    hl = (h2 - hh.astype(F32)).astype(BF16)
    rw = rwt_ref[...]
    wh = rw.astype(BF16)
    wl = (rw - wh.astype(F32)).astype(BF16)
    dg = lambda a, b: lax.dot_general(a, b, NT_DIMS, preferred_element_type=F32)
    logits = dg(wh, hh) + dg(wh, hl) + dg(wl, hh)
    score = _sigmoid(logits)
    sel = score + rbb_ref[...]

    neg_inf = jnp.full((1, TM), -jnp.inf, F32)
    grp = []
    for g in range(N_GROUPS):
        s_rows = [sel[PER_GROUP * g + j:PER_GROUP * g + j + 1, :] for j in range(PER_GROUP)]
        c_rows = [score[PER_GROUP * g + j:PER_GROUP * g + j + 1, :] for j in range(PER_GROUP)]
        f_idx = jnp.zeros((1, TM), I32)
        f_val, f_sc = s_rows[0], c_rows[0]
        for j in range(1, PER_GROUP):
            better = s_rows[j] > f_val
            f_idx = jnp.where(better, j, f_idx)
            f_val = jnp.where(better, s_rows[j], f_val)
            f_sc = jnp.where(better, c_rows[j], f_sc)
        s_idx = jnp.zeros((1, TM), I32)
        s_val, s_sc = neg_inf, jnp.zeros((1, TM), F32)
        for j in range(PER_GROUP):
            better = (f_idx != j) & (s_rows[j] > s_val)
            s_idx = jnp.where(better, j, s_idx)
            s_val = jnp.where(better, s_rows[j], s_val)
            s_sc = jnp.where(better, c_rows[j], s_sc)
        grp.append((f_val + s_val, PER_GROUP * g + f_idx, PER_GROUP * g + s_idx, f_sc, s_sc))

    best, e0, e1, c0, c1 = grp[0]
    for g in range(1, N_GROUPS):
        better = grp[g][0] > best
        best = jnp.where(better, grp[g][0], best)
        e0 = jnp.where(better, grp[g][1], e0)
        e1 = jnp.where(better, grp[g][2], e1)
        c0 = jnp.where(better, grp[g][3], c0)
        c1 = jnp.where(better, grp[g][4], c1)
    den = c0 + c1
    w0 = c0 / den
    w1 = c1 / den

    sub = lax.broadcasted_iota(I32, (N_EXPERTS, TM), 0)
    hit0 = sub == jnp.broadcast_to(e0, (N_EXPERTS, TM))
    hit1 = sub == jnp.broadcast_to(e1, (N_EXPERTS, TM))
    onehot = jnp.where(hit0 | hit1, 1.0, 0.0)
    prefix = jnp.dot(onehot.astype(BF16), tri_ref[...], preferred_element_type=F32)
    count = jnp.sum(onehot, axis=1, keepdims=True)
    padded = jnp.floor((count + (ROW_CHUNK - 1)) * (1.0 / ROW_CHUNK)) * ROW_CHUNK

    sub1 = lax.broadcasted_iota(I32, (N_EXPERTS, 1), 0)
    run = jnp.zeros((1, 1), F32)
    local_off = jnp.zeros((N_EXPERTS, 1), F32)
    for e in range(N_EXPERTS):
        local_off = jnp.where(sub1 == e, run, local_off)
        run = run + padded[e:e + 1, :]
    place = prefix + local_off
    lpos0 = jnp.sum(jnp.where(hit0, place, 0.0), axis=0, keepdims=True).astype(I32)
    lpos1 = jnp.sum(jnp.where(hit1, place, 0.0), axis=0, keepdims=True).astype(I32)

    lanes = lambda col: jnp.broadcast_to(col, (N_EXPERTS, 128)).astype(I32)
    nch_o[sub_tile] = lanes(padded * (1.0 / ROW_CHUNK))
    loff_o[sub_tile] = lanes(local_off)
    gb_o[sub_tile] = carry[...].astype(I32)
    carry[...] = carry[...] + padded
    cnt_o[...] = carry[...].astype(I32)

    sub8 = lax.broadcasted_iota(I32, (8, TM), 0)
    bc = lambda r: jnp.broadcast_to(r, (8, TM))
    lp_o[:, tok] = jnp.where(sub8 == 0, bc(lpos0), jnp.where(sub8 == 1, bc(lpos1), 0))
    rf_o[:, tok] = jnp.where(sub8 == 0, bc(w0), jnp.where(sub8 == 1, bc(w1), 0.0))


def _tail_specs(l, sub_tiles):
    rows = sub_tiles * TM
    in_specs = [
        pl.BlockSpec((1, 1, D), lambda i: (l, 0, 0)),
        pl.BlockSpec((N_EXPERTS, D), lambda i: (0, 0)),
        pl.BlockSpec((N_EXPERTS, TM), lambda i: (0, 0)),
        pl.BlockSpec((TM, TM), lambda i: (0, 0)),
    ]
    per_tile = lambda: pl.BlockSpec((sub_tiles, N_EXPERTS, 128), lambda i: (i, 0, 0))
    out_specs = [
        pl.BlockSpec((rows, D), lambda i: (i, 0)),
        pl.BlockSpec((8, rows), lambda i: (0, i)),
        pl.BlockSpec((8, rows), lambda i: (0, i)),
        per_tile(), per_tile(), per_tile(),
        pl.BlockSpec((N_EXPERTS, 128), lambda i: (0, 0)),
    ]
    out_shape = [
        _hbm_out((T, D), BF16),
        jax.ShapeDtypeStruct((8, T), I32),
        jax.ShapeDtypeStruct((8, T), F32),
        jax.ShapeDtypeStruct((N_TILES, N_EXPERTS, 128), I32),
        jax.ShapeDtypeStruct((N_TILES, N_EXPERTS, 128), I32),
        jax.ShapeDtypeStruct((N_TILES, N_EXPERTS, 128), I32),
        jax.ShapeDtypeStruct((N_EXPERTS, 128), I32),
    ]
    scratch = [pltpu.VMEM((N_EXPERTS, 128), F32)]
    return in_specs, out_specs, out_shape, scratch


def _out_kernel(split_x, *refs):
    x_refs, refs = refs[:1 + split_x], refs[1 + split_x:]
    (m_ref, ac_ref, al_ref, sc_ref, sl_ref, w_ref, n2_ref, rwt_ref, rbb_ref, tri_ref,
     x_o, *tail_refs) = refs
    *tail_outs, wbf, carry = tail_refs
    i = pl.program_id(0)

    @pl.when(i == 0)
    def _():
        wbf[...] = w_ref[0].astype(BF16)

    is_ctx = i < T_CTX // TO
    x = jnp.where(is_ctx, x_refs[0][...], x_refs[1][...]) if split_x else x_refs[0][...]
    att = jnp.where(is_ctx, ac_ref[...], al_ref[...])
    sgo = jnp.where(is_ctx, sc_ref[...], sl_ref[...])
    out = (jnp.dot(att, wbf[0:MIX_A, :], preferred_element_type=F32)
           + jnp.dot(sgo, wbf[MIX_A:MIX_A + MIX_B, :], preferred_element_type=F32))
    x_new = x + m_ref[0, 0, 2:3, :] * out
    x_o[...] = x_new
    for sub_tile in range(TO // TM):
        _route_tail(sub_tile, x_new[sub_tile * TM:(sub_tile + 1) * TM, :],
                    m_ref, n2_ref, rwt_ref, rbb_ref, tri_ref, *tail_outs, carry)


def _out_proj(l, x, mod, att_c, att_l, sg_c, sg_l, w_out, norm2_g, rwt, rbb, tri):
    e = l // 2
    t_in, t_out, t_shape, t_scratch = _tail_specs(l, TO // TM)
    n_ctx = T_CTX // TO
    ctx = lambda w: pl.BlockSpec((TO, w), lambda i: (jnp.minimum(i, n_ctx - 1), 0))
    lat = lambda w: pl.BlockSpec((TO, w), lambda i: (jnp.maximum(i - n_ctx, 0), 0))
    split_x = isinstance(x, tuple)
    if split_x:
        x_specs, x_args, aliases = [ctx(D), lat(D)], [_hbm(x[0]), _hbm(x[1])], {}
    else:
        x_specs, x_args, aliases = [pl.BlockSpec((TO, D), lambda i: (i, 0))], [_hbm(x)], {0: 0}
    in_specs = x_specs + [
        pl.BlockSpec((1, 1, N_MOD, D), lambda i: (l, _cond_of_tile(i * (TO // TM)), 0, 0)),
        ctx(MIX_A), lat(MIX_A), ctx(MIX_B), lat(MIX_B),
        pl.BlockSpec((1, D, D), lambda i: (e, 0, 0)),
    ] + t_in
    return pl.pallas_call(
        functools.partial(_out_kernel, split_x),
        grid=(T // TO,),
        in_specs=in_specs,
        out_specs=[pl.BlockSpec((TO, D), lambda i: (i, 0))] + t_out,
        out_shape=[_hbm_out((T, D), F32)] + t_shape,
        scratch_shapes=[pltpu.VMEM((D, D), BF16)] + t_scratch,
        input_output_aliases=aliases,
        compiler_params=_cparams(("arbitrary",), 40),
        name="out_proj_route",
    )(*x_args, mod, _hbm(att_c), _hbm(att_l), _hbm(sg_c), _hbm(sg_l), w_out,
      norm2_g.reshape(DEPTH, 1, D), rwt, rbb, tri)


def _pool_tables():
    t = np.arange(TP)[:, None]
    s = np.arange(TP + POOL_RB)[None, :]
    p = s - POOL_HALO
    bands, sizes = [], []
    for seq_len in (CTX_S, LAT_S):
        same = (p >= 0) & (p < TP) & (p // seq_len == t // seq_len)
        per_w, size_w = [], []
        for w in POOL_WINDOWS:
            band = (p >= t - w // 2) & (p < t - w // 2 + w) & same
            size_w.append(band.sum(axis=1, keepdims=True))
            per_w.append(np.stack([band[rb * POOL_RB:(rb + 1) * POOL_RB, rb * POOL_RB:(rb + 2) * POOL_RB]
                                   for rb in range(TP // POOL_RB)]))
        bands.append(np.stack(per_w))
        sizes.append(np.stack(size_w))
    return jnp.asarray(np.stack(bands), dtype=BF16), jnp.asarray(np.stack(sizes), dtype=F32)


def _pool_kernel(x_ref, m_ref, n1_ref, band_ref, size_ref, pw_ref, ps_ref,
                 n2_ref, rwt_ref, rbb_ref, tri_ref,
                 x_o, *tail_refs):
    *tail_outs, carry = tail_refs
    x = x_ref[...]
    h = _modnorm(x, n1_ref[0], m_ref[0, 0, 1:2, :], m_ref[0, 0, 0:1, :])
    hcat = jnp.concatenate([jnp.zeros((POOL_HALO, D), F32), h,
                            jnp.zeros((POOL_RB - POOL_HALO, D), F32)], axis=0)
    hi = hcat.astype(BF16)
    lo = (hcat - hi.astype(F32)).astype(BF16)
    dot = lambda a, b: jnp.dot(a, b, preferred_element_type=F32)
    outs = []
    for g in range(len(POOL_WINDOWS)):
        cols = slice(g * POOL_DIM, (g + 1) * POOL_DIM)
        sums = []
        for rb in range(TP // POOL_RB):
            near = slice(rb * POOL_RB, (rb + 2) * POOL_RB)
            band = band_ref[0, g, rb]
            sums.append(dot(band, hi[near, cols]) + dot(band, lo[near, cols]))
        mean = jnp.concatenate(sums, axis=0) / size_ref[0, g]
        dlt = (mean - h[:, cols]).astype(BF16)
        outs.append(dot(dlt, pw_ref[0, g].astype(BF16)))
    out = jnp.concatenate(outs, axis=1) * ps_ref[0]
    x_new = x + m_ref[0, 0, 2:3, :] * out
    x_o[...] = x_new
    for sub_tile in range(TP // TM):
        _route_tail(sub_tile, x_new[sub_tile * TM:(sub_tile + 1) * TM, :],
                    m_ref, n2_ref, rwt_ref, rbb_ref, tri_ref, *tail_outs, carry)


def _pool_mixer(l, x, mod, norm1_g, pool_w, pool_scale, norm2_g, rwt, rbb, tri):
    o = l // 2
    t_in, t_out, t_shape, t_scratch = _tail_specs(l, TP // TM)
    bands, sizes = _pool_tables()
    n_ctx = T_CTX // TP
    kind = lambda i: jnp.where(i < n_ctx, 0, 1)
    n_w, n_rb = len(POOL_WINDOWS), TP // POOL_RB
    in_specs = [
        pl.BlockSpec((TP, D), lambda i: (i, 0)),
        pl.BlockSpec((1, 1, N_MOD, D), lambda i: (l, _cond_of_tile(i * (TP // TM)), 0, 0)),
        pl.BlockSpec((1, 1, D), lambda i: (l, 0, 0)),
        pl.BlockSpec((1, n_w, n_rb, POOL_RB, 2 * POOL_RB), lambda i: (kind(i), 0, 0, 0, 0)),
        pl.BlockSpec((1, n_w, TP, 1), lambda i: (kind(i), 0, 0, 0)),
        pl.BlockSpec((1, n_w, POOL_DIM, POOL_DIM), lambda i: (o, 0, 0, 0)),
        pl.BlockSpec((1, 1, D), lambda i: (o, 0, 0)),
    ] + t_in
    return pl.pallas_call(
        _pool_kernel,
        grid=(T // TP,),
        in_specs=in_specs,
        out_specs=[pl.BlockSpec((TP, D), lambda i: (i, 0))] + t_out,
        out_shape=[_hbm_out((T, D), F32)] + t_shape,
        scratch_shapes=t_scratch,
        compiler_params=_cparams(("arbitrary",), 52),
        name="pool_route",
    )(_hbm(x), mod, norm1_g.reshape(DEPTH, 1, D), bands, sizes, pool_w,
      pool_scale.reshape(DEPTH // 2, 1, D),
      norm2_g.reshape(DEPTH, 1, D), rwt, rbb, tri)


def _start_run_copies(tile, nch_ref, loff_ref, gb_ref, make_copy):
    for e in range(N_EXPERTS):
        n = nch_ref[tile * N_EXPERTS + e]
        lo = loff_ref[tile * N_EXPERTS + e]
        gb = gb_ref[tile * N_EXPERTS + e] + e * EXPERT_CAP

        def issue(c, carry, lo=lo, gb=gb):
            make_copy(pl.multiple_of(lo + c * (2 * ROW_CHUNK), ROW_CHUNK),
                      pl.multiple_of(gb + c * (2 * ROW_CHUNK), ROW_CHUNK), 2 * ROW_CHUNK).start()
            return carry

        lax.fori_loop(0, lax.shift_right_logical(n, 1), issue, 0)

        @pl.when((n & 1) == 1)
        def _(n=n, lo=lo, gb=gb):
            make_copy(pl.multiple_of(lo + (n - 1) * ROW_CHUNK, ROW_CHUNK),
                      pl.multiple_of(gb + (n - 1) * ROW_CHUNK, ROW_CHUNK), ROW_CHUNK).start()


def _wait_run_copies(tile, nch_ref, make_copy):
    doubles, singles = 0, 0
    for e in range(N_EXPERTS):
        n = nch_ref[tile * N_EXPERTS + e]
        doubles = doubles + lax.shift_right_logical(n, 1)
        singles = singles + (n & 1)

    def drain(rows):
        def body(c, carry):
            make_copy(0, 0, rows).wait()
            return carry
        return body

    lax.fori_loop(0, doubles, drain(2 * ROW_CHUNK), 0)
    lax.fori_loop(0, singles, drain(ROW_CHUNK), 0)


def _dispatch_kernel(nch_ref, loff_ref, gb_ref, h_ref, lp_ref, xs_ref, sbuf, sems):
    i = pl.program_id(0)
    slot = i % 2
    r = lax.broadcasted_iota(I32, (SORTED_ROWS, TM), 0)
    for sub in range(MOVE_SUBS):
        tok = slice(sub * TM, (sub + 1) * TM)
        hit = (r == lp_ref[0:1, tok]) | (r == lp_ref[1:2, tok])
        perm = jnp.where(hit, 1.0, 0.0).astype(BF16)
        sbuf[slot, sub] = jnp.dot(perm, h_ref[tok, :], preferred_element_type=F32).astype(BF16)

    def copies_of(s, sub):
        def make_copy(local_row, global_row, rows):
            return pltpu.make_async_copy(sbuf.at[s, sub, pl.ds(local_row, rows)],
                                         xs_ref.at[pl.ds(global_row, rows)], sems.at[s])
        return make_copy

    @pl.when(i > 0)
    def _():
        for sub in range(MOVE_SUBS):
            _wait_run_copies((i - 1) * MOVE_SUBS + sub, nch_ref, copies_of(1 - slot, sub))

    for sub in range(MOVE_SUBS):
        _start_run_copies(i * MOVE_SUBS + sub, nch_ref, loff_ref, gb_ref, copies_of(slot, sub))

    @pl.when(i == N_TILES // MOVE_SUBS - 1)
    def _():
        for sub in range(MOVE_SUBS):
            _wait_run_copies(i * MOVE_SUBS + sub, nch_ref, copies_of(slot, sub))


def _dispatch(layout, h2, lp):
    rows = MOVE_SUBS * TM
    return pl.pallas_call(
        _dispatch_kernel,
        grid_spec=pltpu.PrefetchScalarGridSpec(
            num_scalar_prefetch=3,
            grid=(N_TILES // MOVE_SUBS,),
            in_specs=[pl.BlockSpec((rows, D), lambda i, *_: (i, 0)),
                      pl.BlockSpec((8, rows), lambda i, *_: (0, i))],
            out_specs=pl.BlockSpec(memory_space=pl.ANY),
            scratch_shapes=[pltpu.VMEM((2, MOVE_SUBS, SORTED_ROWS, D), BF16),
                            pltpu.SemaphoreType.DMA((2,))],
        ),
        out_shape=_hbm_out((N_EXPERTS * EXPERT_CAP, D), BF16),
        compiler_params=_cparams(("arbitrary",), 32),
        name="moe_dispatch",
    )(*layout, _hbm(h2), lp)


def _moe_kernel(l, rb_ref, e_ref, nv_ref, first_ref, ord_ref, nxt_ref, nxt2_ref,
                xs_ref, wg_hbm, wu_hbm, wd_hbm, y_ref, wgf, wuf, wdf, wgb, wub, wdb, sems):
    i = pl.program_id(0)
    n_valid = nv_ref[i]

    def weight_copies(e, slot):
        return (pltpu.make_async_copy(wg_hbm.at[l, e], wgf.at[slot], sems.at[slot, 0]),
                pltpu.make_async_copy(wu_hbm.at[l, e], wuf.at[slot], sems.at[slot, 1]),
                pltpu.make_async_copy(wd_hbm.at[l, e], wdf.at[slot], sems.at[slot, 2]))

    @pl.when(first_ref[i] == 1)
    def _():
        k = ord_ref[i]
        slot = k % W_SLOTS

        @pl.when(k == 0)
        def _():
            for cp in weight_copies(e_ref[i], 0):
                cp.start(priority=1)

            @pl.when(nxt_ref[i] >= 0)
            def _():
                for cp in weight_copies(nxt_ref[i], 1):
                    cp.start(priority=1)

        for cp in weight_copies(e_ref[i], slot):
            cp.wait()

        @pl.when(nxt2_ref[i] >= 0)
        def _():
            for cp in weight_copies(nxt2_ref[i], (k + 2) % W_SLOTS):
                cp.start(priority=1)

        wgb[...] = wgf[slot].astype(BF16)
        wub[...] = wuf[slot].astype(BF16)
        wdb[...] = wdf[slot].astype(BF16)

    @pl.when(n_valid > 0)
    def _():
        row = lax.broadcasted_iota(I32, (TE, 1), 0)
        x = jnp.where(row < n_valid, xs_ref[...], jnp.zeros((), BF16))
        gate = jnp.dot(x, wgb[...], preferred_element_type=F32)
        up = jnp.dot(x, wub[...], preferred_element_type=F32)
        act = (gate * _sigmoid(gate) * up).astype(BF16)
        y_ref[...] = jnp.dot(act, wdb[...], preferred_element_type=F32).astype(BF16)


def _experts(l, tile_map, xs, w_gate, w_up, w_down):
    n_pre = len(tile_map)
    blk = lambda i, rb, *_: (rb[i], 0)
    return pl.pallas_call(
        functools.partial(_moe_kernel, l),
        grid_spec=pltpu.PrefetchScalarGridSpec(
            num_scalar_prefetch=n_pre,
            grid=(MOE_TILES,),
            in_specs=[
                pl.BlockSpec((TE, D), blk),
                pl.BlockSpec(memory_space=pl.ANY),
                pl.BlockSpec(memory_space=pl.ANY),
                pl.BlockSpec(memory_space=pl.ANY),
            ],
            out_specs=pl.BlockSpec((TE, D), blk),
            scratch_shapes=[pltpu.VMEM((W_SLOTS, D, D_FF), F32), pltpu.VMEM((W_SLOTS, D, D_FF), F32),
                            pltpu.VMEM((W_SLOTS, D_FF, D), F32),
                            pltpu.VMEM((D, D_FF), BF16), pltpu.VMEM((D, D_FF), BF16),
                            pltpu.VMEM((D_FF, D), BF16),
                            pltpu.SemaphoreType.DMA((W_SLOTS, 3))],
        ),
        out_shape=_hbm_out((N_EXPERTS * EXPERT_CAP, D), BF16),
        compiler_params=_cparams(("arbitrary",), 48),
        name="moe_experts",
    )(*tile_map, _hbm(xs), w_gate, w_up, w_down)


def _combine_kernel(split_out, nch_ref, loff_ref, gb_ref, x_ref, m_ref, lpt_ref, w_ref, y_ref, *refs):
    *o_refs, ybuf, sems = refs
    i = pl.program_id(0)
    slot = i % 2

    def copies_of(s, sub):
        def make_copy(local_row, global_row, rows):
            return pltpu.make_async_copy(y_ref.at[pl.ds(global_row, rows)],
                                         ybuf.at[s, sub, pl.ds(local_row, rows)], sems.at[s])
        return make_copy

    @pl.when(i == 0)
    def _():
        ybuf[...] = jnp.zeros_like(ybuf)
        for sub in range(MOVE_SUBS):
            _start_run_copies(sub, nch_ref, loff_ref, gb_ref, copies_of(slot, sub))

    @pl.when(i + 1 < N_TILES // MOVE_SUBS)
    def _():
        for sub in range(MOVE_SUBS):
            _start_run_copies((i + 1) * MOVE_SUBS + sub, nch_ref, loff_ref, gb_ref, copies_of(1 - slot, sub))

    for sub in range(MOVE_SUBS):
        _wait_run_copies(i * MOVE_SUBS + sub, nch_ref, copies_of(slot, sub))

    col = lax.broadcasted_iota(I32, (TM, SORTED_ROWS), 1)
    parts = []
    for sub in range(MOVE_SUBS):
        tok = slice(sub * TM, (sub + 1) * TM)
        pick = jnp.where(col == lpt_ref[tok, 0:1], w_ref[tok, 0:1],
                         jnp.where(col == lpt_ref[tok, 1:2], w_ref[tok, 1:2], 0.0)).astype(BF16)
        parts.append(jnp.dot(pick, ybuf[slot, sub], preferred_element_type=F32))
    res = x_ref[...] + m_ref[0, 0, 5:6, :] * jnp.concatenate(parts, axis=0)
    if split_out:
        @pl.when(i < T_CTX // (MOVE_SUBS * TM))
        def _():
            o_refs[0][...] = res

        @pl.when(i >= T_CTX // (MOVE_SUBS * TM))
        def _():
            o_refs[1][...] = res
    else:
        o_refs[0][...] = res


def _combine(l, layout, x, mod, lpt, wcol, y, split_out):
    rows = MOVE_SUBS * TM
    n_ctx = T_CTX // rows
    if split_out:
        out_specs = [pl.BlockSpec((rows, D), lambda i, *_: (jnp.minimum(i, n_ctx - 1), 0)),
                     pl.BlockSpec((rows, D), lambda i, *_: (jnp.maximum(i - n_ctx, 0), 0))]
        out_shape = [jax.ShapeDtypeStruct((T_CTX, D), F32), jax.ShapeDtypeStruct((T_LAT, D), F32)]
        aliases = {}
    else:
        out_specs = pl.BlockSpec((rows, D), lambda i, *_: (i, 0))
        out_shape = _hbm_out((T, D), F32)
        aliases = {3: 0}
    return pl.pallas_call(
        functools.partial(_combine_kernel, split_out),
        grid_spec=pltpu.PrefetchScalarGridSpec(
            num_scalar_prefetch=3,
            grid=(N_TILES // MOVE_SUBS,),
            in_specs=[
                pl.BlockSpec((rows, D), lambda i, *_: (i, 0)),
                pl.BlockSpec((1, 1, N_MOD, D), lambda i, *_: (l, _cond_of_tile(i * MOVE_SUBS), 0, 0)),
                pl.BlockSpec((rows, 8), lambda i, *_: (i, 0)),
                pl.BlockSpec((rows, 8), lambda i, *_: (i, 0)),
                pl.BlockSpec(memory_space=pl.ANY),
            ],
            out_specs=out_specs,
            scratch_shapes=[pltpu.VMEM((2, MOVE_SUBS, SORTED_ROWS, D), BF16),
                            pltpu.SemaphoreType.DMA((2,))],
        ),
        out_shape=out_shape,
        input_output_aliases=aliases,
        compiler_params=_cparams(("arbitrary",), 48),
        name="moe_combine",
    )(*layout, _hbm(x), mod, lpt, wcol, _hbm(y))


def _tile_map(cnt):
    cnt = cnt[:, 0]
    ids = jnp.arange(N_EXPERTS, dtype=I32)
    n_tiles = (cnt + TE - 1) // TE
    cum = jnp.cumsum(n_tiles)
    total = cum[-1]
    i = jnp.arange(MOE_TILES, dtype=I32)
    ii = jnp.minimum(i, total - 1)
    e = jnp.sum((ii[:, None] >= cum[None, :]).astype(I32), axis=1)
    onehot = (e[:, None] == ids[None, :]).astype(I32)
    pick = lambda v: jnp.sum(onehot * v[None, :], axis=1)
    j = ii - pick(cum - n_tiles)
    active = i < total
    n_valid = jnp.where(active, jnp.clip(pick(cnt) - j * TE, 0, TE), 0)
    first = (active & (j == 0)).astype(I32)
    nonempty = n_tiles > 0
    ordinal = jnp.cumsum(nonempty.astype(I32)) - 1
    later = jnp.where(nonempty[None, :] & (ids[None, :] > ids[:, None]), ids[None, :], N_EXPERTS)
    nxt = jnp.min(later, axis=1)
    nxt_of = lambda v: jnp.sum((v[:, None] == ids[None, :]).astype(I32) * nxt[None, :], axis=1)
    nxt2 = jnp.where(nxt == N_EXPERTS, N_EXPERTS, nxt_of(nxt))
    none = lambda v: jnp.where(v == N_EXPERTS, -1, v)
    as_i32 = lambda v: v.astype(I32)
    return tuple(map(as_i32, (e * CAP_TILES + j, e, n_valid, first, pick(ordinal),
                              pick(none(nxt)), pick(none(nxt2)))))


def _moe(l, x, tail, mod, w_gate, w_up, w_down, split_out):
    h2, lp, rf, nch, loff, gb, cnt = tail
    layout = tuple(a[:, :, 0].reshape(N_TILES * N_EXPERTS) for a in (nch, loff, gb))
    xs = _dispatch(layout, h2, lp)
    y = _experts(l, _tile_map(cnt), xs, w_gate, w_up, w_down)
    return _combine(l, layout, x, mod, lp.T, rf.T, y, split_out)


def _rope_tables():
    t = jnp.arange(LAT_S)
    row = (t // GRID_W).astype(F32)
    col = (t % GRID_W).astype(F32)
    nf = HEAD_DIM // 4
    inv = ROPE_BASE ** (-jnp.arange(nf, dtype=F32) / nf)
    ar = row[:, None] * inv[None, :]
    ac = col[:, None] * inv[None, :]
    cos = jnp.concatenate([jnp.cos(ar), jnp.cos(ar), jnp.cos(ac), jnp.cos(ac)], axis=1)
    sin = jnp.concatenate([-jnp.sin(ar), jnp.sin(ar), -jnp.sin(ac), jnp.sin(ac)], axis=1)
    reps = QK_W // HEAD_DIM
    return jnp.tile(cos, (1, reps)), jnp.tile(sin, (1, reps))


def kernel(x_prompt, x_sample, cache_k, cache_v, c, c_ctx, norm1_g, norm2_g, w_ada, b_ada, w_in, w_out,
           q_norm_g, k_norm_g, lam_q1, lam_k1, lam_q2, lam_k2, subln_g, sg_w, sg_b, pool_w, pool_scale,
           router_w, router_b, w_gate, w_up, w_down):
    x = (x_prompt.reshape(T_CTX, D), x_sample.reshape(T_LAT, D))
    cond = jnp.concatenate([c_ctx[None, :], c], axis=0)
    mod = _modulation(cond, w_ada, b_ada)

    seg_id = np.arange(QK_W) // HEAD_DIM
    seg = jnp.asarray((seg_id[:, None] == seg_id[None, :]) / HEAD_DIM, dtype=BF16)
    tri = jnp.asarray(np.arange(TM)[:, None] < np.arange(TM)[None, :], dtype=BF16)
    rope_tabs = _rope_tables()
    rwt = router_w.T
    rbb = jnp.broadcast_to(router_b[:, None], (N_EXPERTS, TM))
    states = None

    for l in range(DEPTH):
        if l % 2 == 0:
            e = l // 2
            qg = jnp.tile(q_norm_g[e], QK_W // HEAD_DIM)[None, :]
            kg = jnp.tile(k_norm_g[e], QK_W // HEAD_DIM)[None, :]
            sgb = jnp.broadcast_to(sg_b[:, :, :, None], (DEPTH // 2, SG_GROUPS, SG_CHUNK, 128))
            x_c, x_l = x if isinstance(x, tuple) else (x, x)
            common = (mod, norm1_g, w_in, qg, kg, seg, sg_w, sgb)
            q_c, k_c, v_c, sg_c, *states = _front(False, l, x_c, *common, None, states)
            q_l, k_l, v_l, sg_l = _front(True, l, x_l, *common, rope_tabs, None)
            lam = (lam_q1, lam_k1, lam_q2, lam_k2, subln_g)
            att_c = _attention(False, l, q_c, k_c, v_c, None, None, *lam)
            att_l = _attention(True, l, q_l, k_l, v_l, cache_k, cache_v, *lam)
            x, *tail = _out_proj(l, x, mod, att_c, att_l, sg_c, sg_l, w_out, norm2_g, rwt, rbb, tri)
        else:
            x, *tail = _pool_mixer(l, x, mod, norm1_g, pool_w, pool_scale, norm2_g, rwt, rbb, tri)
        x = _moe(l, x, tail, mod, w_gate, w_up, w_down, split_out=(l == DEPTH - 1))
    state_k, state_v = states

    y_prompt = x[0].reshape(N_CTX_B, CTX_S, D)
    y_sample = x[1].reshape(N_LAT_B, LAT_S, D)
    state_k = state_k.reshape(N_CTX_B, DEPTH // 2, CTX_S, N_HEADS, 2, HEAD_DIM)
    return y_prompt, y_sample, state_k, state_v
```
